```python
import math
import jax, jax.numpy as jnp
from jax import lax
import numpy as np

D_MODEL = 1024
BATCH = 16
SEQ = 2048
DEPTH = 2

CHUNK = 64
EPS = 1e-6
N_EVEN = (DEPTH + 1) // 2
N_ODD = DEPTH // 2

A_WIDTH = 512
A_BLOCKS = 8
A_BLOCK = A_WIDTH // A_BLOCKS
CONV_WIDTH = 4
RG_C = 8.0
B_HEADS = 8
B_HEAD_DIM = 64
B_WIDTH = B_HEADS * B_HEAD_DIM
LEFT_CHUNKS = 8
BAND = LEFT_CHUNKS + 1
REL_CLIP = 256
AB_IN = 2 * A_WIDTH + 3 * B_WIDTH
AB_MIX = A_WIDTH + B_WIDTH
C_HEADS = 8
C_DK = 128
C_DV = 128
C_WIDTH = C_HEADS * C_DK
C_IN = 2 * C_HEADS * C_DK + 2 * C_HEADS * C_DV
N_GROUPS = 4
EXPERTS_PER_GROUP = 8
N_EXPERTS = N_GROUPS * EXPERTS_PER_GROUP
TOP_K = 2
D_EXPERT = 512
MOE_BLOCK = 128

kernel_name = "hybrid_rglru_chunkattn_hgrn2_hmoe"


def rms_norm(x, g):
    xf = x.astype(jnp.float32)
    y = xf * lax.rsqrt(jnp.mean(xf * xf, axis=-1, keepdims=True) + EPS)
    return (y * g.astype(jnp.float32)).astype(x.dtype)


def causal_depthwise_conv(x, w, b):
    S = x.shape[1]
    xp = jnp.pad(x, ((0, 0), (CONV_WIDTH - 1, 0), (0, 0)))
    y = b
    for j in range(CONV_WIDTH):
        y = y + w[j] * xp[:, j:j + S]
    return y


def rg_lru(x, w_a, b_a, w_x, b_x, lam):
    B_, S, _ = x.shape
    xb = x.reshape(B_, S, A_BLOCKS, A_BLOCK)
    r = jax.nn.sigmoid((jnp.einsum('bsnc,ncd->bsnd', xb, w_a).reshape(B_, S, A_WIDTH) + b_a).astype(jnp.float32))
    i = jax.nn.sigmoid((jnp.einsum('bsnc,ncd->bsnd', xb, w_x).reshape(B_, S, A_WIDTH) + b_x).astype(jnp.float32))
    log_a = RG_C * r * jax.nn.log_sigmoid(lam.astype(jnp.float32))
    a = jnp.exp(log_a)
    u = jnp.sqrt(-jnp.expm1(2.0 * log_a)) * (i * x.astype(jnp.float32))

    def combine(left, right):
        a1, b1 = left
        a2, b2 = right
        return a1 * a2, a2 * b1 + b2

    _, h = lax.associative_scan(combine, (a, u), axis=1)
    return h.astype(x.dtype)


def chunked_rel_attention(q, k, v, rel_bias):
    B_, S = q.shape[:2]
    NC = S // CHUNK
    qc = q.reshape(B_, NC, CHUNK, B_HEADS, B_HEAD_DIM)
    band_idx = jnp.arange(NC)[:, None] + jnp.arange(BAND)[None, :]

    def gather_band(t):
        tc = t.reshape(B_, NC, CHUNK, B_HEADS, B_HEAD_DIM)
        tp = jnp.pad(tc, ((0, 0), (LEFT_CHUNKS, 0), (0, 0), (0, 0), (0, 0)))
        return tp[:, band_idx].reshape(B_, NC, BAND * CHUNK, B_HEADS, B_HEAD_DIM)

    kb = gather_band(k)
    vb = gather_band(v)
    qpos = jnp.arange(CHUNK)
    kpos = ((jnp.arange(BAND)[:, None] - LEFT_CHUNKS) * CHUNK + jnp.arange(CHUNK)[None, :]).reshape(-1)
    rel = jnp.clip(qpos[:, None] - kpos[None, :], -REL_CLIP, REL_CLIP) + REL_CLIP
    bias = rel_bias[:, rel].astype(jnp.float32)
    valid = (jnp.arange(NC)[:, None] - LEFT_CHUNKS + jnp.arange(BAND)[None, :]) >= 0
    valid = jnp.repeat(valid, CHUNK, axis=1)
    s = jnp.einsum('bcihd,bckhd->bchik', qc, kb, preferred_element_type=jnp.float32) * (B_HEAD_DIM ** -0.5)
    s = s + bias[None, None]
    s = jnp.where(valid[None, :, None, None, :], s, -1e30)
    p = jax.nn.softmax(s, axis=-1)
    o = jnp.einsum('bchik,bckhd->bcihd', p.astype(v.dtype), vb)
    return o.reshape(B_, S, B_WIDTH)


def mixer_ab(h, w_in, conv_w, conv_b, w_a, b_a, w_x, b_x, lam, rel_bias, w_out):
    B_, S = h.shape[:2]
    z = h @ w_in
    xa, ga, q, k, v = jnp.split(z, [A_WIDTH, 2 * A_WIDTH, 2 * A_WIDTH + B_WIDTH, 2 * A_WIDTH + 2 * B_WIDTH], axis=-1)
    ya = rg_lru(causal_depthwise_conv(xa, conv_w, conv_b), w_a, b_a, w_x, b_x, lam) * jax.nn.gelu(ga)
    hs = (B_, S, B_HEADS, B_HEAD_DIM)
    yb = chunked_rel_attention(q.reshape(hs), k.reshape(hs), v.reshape(hs), rel_bias)
    return jnp.concatenate([ya, yb], axis=-1) @ w_out


def hgrn2_mixer(h, w_in, lb, norm_g, w_out):
    B_, S = h.shape[:2]
    NC = S // CHUNK
    f32 = jnp.float32
    z = h @ w_in
    q, fz, i, g = jnp.split(z, [C_HEADS * C_DK, 2 * C_HEADS * C_DK, 2 * C_HEADS * C_DK + C_HEADS * C_DV], axis=-1)
    lbf = lb.astype(f32)
    f = lbf + (1.0 - lbf) * jax.nn.sigmoid(fz.astype(f32))
    log_f = jnp.log(f)
    kk = 1.0 - f
    q = jax.nn.silu(q.astype(f32))

    def heads(t, d):
        return t.astype(f32).reshape(B_, NC, CHUNK, C_HEADS, d).transpose(1, 0, 3, 2, 4)

    qc, kc, lfc, vc = heads(q, C_DK), heads(kk, C_DK), heads(log_f, C_DK), heads(i, C_DV)
    causal = jnp.tril(jnp.ones((CHUNK, CHUNK), dtype=bool))

    def step(state, inp):
        qj, kj, lfj, vj = inp
        b = jnp.cumsum(lfj, axis=-2)
        diff = b[..., :, None, :] - b[..., None, :, :]
        decay = jnp.exp(jnp.where(causal[:, :, None], diff, -jnp.inf))
        attn = jnp.einsum('bhtk,bhsk,bhtsk->bhts', qj, kj, decay)
        o = jnp.einsum('bhts,bhsv->bhtv', attn, vj) + jnp.einsum('bhtk,bhkv->bhtv', qj * jnp.exp(b), state)
        b_last = b[..., -1:, :]
        new_state = jnp.exp(b_last[..., 0, :])[..., None] * state + jnp.einsum('bhsk,bhsv->bhkv', kj * jnp.exp(b_last - b), vj)
        return new_state, o

    s0 = jnp.zeros((B_, C_HEADS, C_DK, C_DV), f32)
    _, o = lax.scan(step, s0, (qc, kc, lfc, vc))
    o = o.transpose(1, 0, 3, 2, 4).reshape(B_, S, C_HEADS, C_DV)
    o = o * lax.rsqrt(jnp.mean(o * o, axis=-1, keepdims=True) + EPS)
    o = o.reshape(B_, S, C_HEADS * C_DV) * norm_g.astype(f32) * jax.nn.silu(g.astype(f32))
    return o.astype(h.dtype) @ w_out


def hier_moe(h, router_group, router_expert, w_gate, w_up, w_down):
    B_, S, D = h.shape
    T = B_ * S
    f32 = jnp.float32
    xt = h.reshape(T, D)
    g_prob = jax.nn.softmax(jnp.dot(xt, router_group, preferred_element_type=f32), axis=-1)
    g_gate, g_idx = lax.top_k(g_prob, 1)
    e_logits = jnp.dot(xt, router_expert, preferred_element_type=f32).reshape(T, N_GROUPS, EXPERTS_PER_GROUP)
    e_sel = e_logits[jnp.arange(T), g_idx[:, 0]]
    e_val, e_idx = lax.top_k(e_sel, TOP_K)
    gate = g_gate * jax.nn.softmax(e_val, axis=-1)
    expert = g_idx * EXPERTS_PER_GROUP + e_idx
    A = T * TOP_K
    e_flat = expert.reshape(A)
    w_flat = gate.reshape(A)
    tok_flat = jnp.repeat(jnp.arange(T, dtype=jnp.int32), TOP_K)
    order = jnp.argsort(e_flat)
    e_sorted = e_flat[order]
    counts = jnp.bincount(e_flat, length=N_EXPERTS)
    padded = (counts + MOE_BLOCK - 1) // MOE_BLOCK * MOE_BLOCK
    pad_end = jnp.cumsum(padded)
    pad_start = pad_end - padded
    start = jnp.cumsum(counts) - counts
    dest = pad_start[e_sorted] + (jnp.arange(A) - start[e_sorted])
    n_blocks = -(-(A + N_EXPERTS * (MOE_BLOCK - 1)) // MOE_BLOCK)
    P = n_blocks * MOE_BLOCK
    slot_tok = jnp.full((P,), T, dtype=jnp.int32).at[dest].set(tok_flat[order])
    slot_w = jnp.zeros((P,), f32).at[dest].set(w_flat[order])
    block_expert = jnp.minimum(jnp.searchsorted(pad_end, jnp.arange(n_blocks) * MOE_BLOCK, side='right'), N_EXPERTS - 1)
    x_pad = jnp.concatenate([xt, jnp.zeros((1, D), xt.dtype)], axis=0)

    def expert_block(args):
        toks, e = args
        xb = x_pad[toks]
        hb = jax.nn.silu(xb @ w_gate[e]) * (xb @ w_up[e])
        return hb @ w_down[e]

    y_blocks = lax.map(expert_block, (slot_tok.reshape(n_blocks, MOE_BLOCK), block_expert))
    y = jax.ops.segment_sum(y_blocks.reshape(P, D).astype(f32) * slot_w[:, None], slot_tok, num_segments=T + 1)[:T]
    return y.astype(h.dtype).reshape(B_, S, D)


def setup_inputs(seed: int = 0) -> dict:
    key = jax.random.key(seed)
    ks = jax.random.split(key, 24)

    def nrm(k, shape, scale):
        return jax.random.normal(k, shape, jnp.float32) * scale

    u = jax.random.uniform(ks[11], (N_EVEN, A_WIDTH), jnp.float32, minval=0.9, maxval=0.999)
    log_a = jnp.log(u) / RG_C
    rg_lambda = log_a - jnp.log(-jnp.expm1(log_a))
    return {
        "x": nrm(ks[0], (BATCH, SEQ, D_MODEL), 1.0),
        "norm_mix_g": 1.0 + nrm(ks[1], (DEPTH, D_MODEL), 0.02),
        "norm_ffn_g": 1.0 + nrm(ks[2], (DEPTH, D_MODEL), 0.02),
        "norm_final_g": 1.0 + nrm(ks[3], (D_MODEL,), 0.02),
        "ab_w_in": nrm(ks[4], (N_EVEN, D_MODEL, AB_IN), D_MODEL ** -0.5),
        "ab_conv_w": nrm(ks[5], (N_EVEN, CONV_WIDTH, A_WIDTH), CONV_WIDTH ** -0.5),
        "ab_conv_b": nrm(ks[6], (N_EVEN, A_WIDTH), 0.01),
        "rg_w_a": nrm(ks[7], (N_EVEN, A_BLOCKS, A_BLOCK, A_BLOCK), A_BLOCK ** -0.5),
        "rg_b_a": nrm(ks[8], (N_EVEN, A_WIDTH), 0.01),
        "rg_w_x": nrm(ks[9], (N_EVEN, A_BLOCKS, A_BLOCK, A_BLOCK), A_BLOCK ** -0.5),
        "rg_b_x": nrm(ks[10], (N_EVEN, A_WIDTH), 0.01),
        "rg_lambda": rg_lambda,
        "attn_rel_bias": nrm(ks[12], (N_EVEN, B_HEADS, 2 * REL_CLIP + 1), 0.1),
        "ab_w_out": nrm(ks[13], (N_EVEN, AB_MIX, D_MODEL), AB_MIX ** -0.5),
        "c_w_in": nrm(ks[14], (N_ODD, D_MODEL, C_IN), D_MODEL ** -0.5),
        "c_lb_logits": nrm(ks[15], (N_ODD + 1, C_WIDTH), 0.5),
        "c_norm_g": 1.0 + nrm(ks[16], (N_ODD, C_HEADS * C_DV), 0.02),
        "c_w_out": nrm(ks[17], (N_ODD, C_HEADS * C_DV, D_MODEL), (C_HEADS * C_DV) ** -0.5),
        "moe_router_group": nrm(ks[18], (DEPTH, D_MODEL, N_GROUPS), D_MODEL ** -0.5),
        "moe_router_expert": nrm(ks[19], (DEPTH, D_MODEL, N_EXPERTS), D_MODEL ** -0.5),
        "moe_w_gate": nrm(ks[20], (DEPTH, N_EXPERTS, D_MODEL, D_EXPERT), D_MODEL ** -0.5),
        "moe_w_up": nrm(ks[21], (DEPTH, N_EXPERTS, D_MODEL, D_EXPERT), D_MODEL ** -0.5),
        "moe_w_down": nrm(ks[22], (DEPTH, N_EXPERTS, D_EXPERT, D_MODEL), D_EXPERT ** -0.5),
    }


def reference(x, norm_mix_g, norm_ffn_g, norm_final_g, ab_w_in, ab_conv_w, ab_conv_b, rg_w_a, rg_b_a, rg_w_x, rg_b_x, rg_lambda, attn_rel_bias, ab_w_out, c_w_in, c_lb_logits, c_norm_g, c_w_out, moe_router_group, moe_router_expert, moe_w_gate, moe_w_up, moe_w_down):
    lb_all = jnp.cumsum(jax.nn.softmax(c_lb_logits.astype(jnp.float32), axis=0), axis=0)
    h = x
    for layer in range(DEPTH):
        j = layer // 2
        hn = rms_norm(h, norm_mix_g[layer])
        if layer % 2 == 0:
            mix = mixer_ab(hn, ab_w_in[j], ab_conv_w[j], ab_conv_b[j], rg_w_a[j], rg_b_a[j], rg_w_x[j], rg_b_x[j], rg_lambda[j], attn_rel_bias[j], ab_w_out[j])
        else:
            mix = hgrn2_mixer(hn, c_w_in[j], lb_all[j], c_norm_g[j], c_w_out[j])
        h = h + mix
        h = h + hier_moe(rms_norm(h, norm_ffn_g[layer]), moe_router_group[layer], moe_router_expert[layer], moe_w_gate[layer], moe_w_up[layer], moe_w_down[layer])
    return rms_norm(h, norm_final_g)
```

```python
import functools

import jax
import jax.numpy as jnp
from jax import lax
from jax.experimental import pallas as pl
from jax.experimental.pallas import tpu as pltpu

F32 = jnp.float32
BF16 = jnp.bfloat16

EPS = 1e-6
RG_C = 8.0
CHUNK = 64
LEFT_CHUNKS = 8
REL_CLIP = 256
N_GROUPS = 4
EXPERTS_PER_GROUP = 8
N_EXPERTS = N_GROUPS * EXPERTS_PER_GROUP
TOP_K = 2

LANES = 128
SUBLANES = 8
SUB = 16

TM_PROJ = 512
TS_LRU = 512
TQ = 256
TM_ROUTE = 256
TB_EXPERT = 256
TM_COMB = 256
TH = 256
VMEM_MB = 48


def _params(n_axes, vmem_mb=VMEM_MB):
    return pltpu.CompilerParams(dimension_semantics=("arbitrary",) * n_axes,
                                vmem_limit_bytes=vmem_mb * 1024 * 1024)


def _const_spec(shape):
    nd = len(shape)
    return pl.BlockSpec(shape, lambda *_: (0,) * nd)


def _rms(x, g):
    return x * lax.rsqrt(jnp.mean(x * x, axis=-1, keepdims=True) + EPS) * g


def _dot(a, b):
    return jnp.dot(a, b, preferred_element_type=F32)


def _dot_nt(a, b):
    return lax.dot_general(a, b, (((1,), (1,)), ((), ())), preferred_element_type=F32)


def _dot_tn(a, b):
    return lax.dot_general(a, b, (((0,), (0,)), ((), ())), preferred_element_type=F32)


def _split_bf16(x):
    hi = x.astype(BF16)
    lo = (x - hi.astype(F32)).astype(BF16)
    return hi, lo


def _to_slab(ref, val):
    m = val.shape[0]
    for j in range(val.shape[1] // LANES):
        ref[pl.ds(j, m, stride=SUBLANES), :] = val[:, LANES * j:LANES * (j + 1)]


def _from_slab(ref, m):
    return jnp.concatenate([ref[pl.ds(j, m, stride=SUBLANES), :] for j in range(SUBLANES)], axis=1)


def _ln_proj0_kernel(x_ref, g_ref, w_ref, xg_ref, qkv_ref):
    hn = _rms(x_ref[...], g_ref[...]).astype(BF16)
    z = _dot(hn, w_ref[...])
    aw = xg_ref.shape[1]
    bw = (z.shape[1] - aw) // 3
    xg_ref[...] = z[:, :aw]
    qkv_ref[:, :bw] = (z[:, aw:aw + bw] * (CHUNK ** -0.5)).astype(BF16)
    qkv_ref[:, bw:] = z[:, aw + bw:].astype(BF16)


def _ln_proj0(x, g, w, aw):
    t, d = x.shape
    n = w.shape[1]
    tm = min(TM_PROJ, t)
    return pl.pallas_call(
        _ln_proj0_kernel,
        grid=(t // tm,),
        in_specs=[pl.BlockSpec((tm, d), lambda i: (i, 0)), _const_spec(g.shape), _const_spec(w.shape)],
        out_specs=[pl.BlockSpec((tm, aw), lambda i: (i, 0)), pl.BlockSpec((tm, n - aw), lambda i: (i, 0))],
        out_shape=[jax.ShapeDtypeStruct((t, aw), F32), jax.ShapeDtypeStruct((t, n - aw), BF16)],
        compiler_params=_params(1),
        name="ln_proj0",
    )(x, g, w)


def _rglru_kernel(xg_ref, cw_ref, cb_ref, wa_ref, wx_ref, ba_ref, bx_ref, lam_ref, ya_ref,
                  halo_ref, h_ref, a_s, u_s):
    ts = xg_ref.shape[0]
    w = ya_ref.shape[1]

    @pl.when(pl.program_id(1) == 0)
    def _():
        halo_ref[...] = jnp.zeros_like(halo_ref)
        h_ref[...] = jnp.zeros_like(h_ref)

    xa = xg_ref[:, :w]
    halo = halo_ref[...]
    row8 = lax.broadcasted_iota(jnp.int32, (SUBLANES, w), 0)
    nk = cw_ref.shape[0]
    y = cb_ref[...] + cw_ref[nk - 1:nk, :] * xa
    for k in range(1, nk):
        r = pltpu.roll(xa, k, 0)
        top = jnp.where(row8 < k, pltpu.roll(halo, k, 0), r[:SUBLANES, :])
        y = y + cw_ref[nk - 1 - k:nk - k, :] * jnp.concatenate([top, r[SUBLANES:, :]], axis=0)
    halo_ref[...] = xa[ts - SUBLANES:, :]

    yb = y.astype(BF16)
    rg = jax.nn.sigmoid(_dot(yb, wa_ref[...]) + ba_ref[...])
    ig = jax.nn.sigmoid(_dot(yb, wx_ref[...]) + bx_ref[...])
    lam = lam_ref[...]
    log_sig = jnp.minimum(lam, 0.0) - jnp.log1p(jnp.exp(-jnp.abs(lam)))
    log_a = RG_C * rg * log_sig
    a = jnp.exp(log_a)
    u = jnp.sqrt(-jnp.tanh(log_a) * (a * a + 1.0)) * (ig * y)

    rowm = lax.broadcasted_iota(jnp.int32, (ts, w), 0) & (SUBLANES - 1)
    for s in (1, 2, 4):
        keep = rowm >= s
        a_sh = jnp.where(keep, pltpu.roll(a, s, 0), 1.0)
        u_sh = jnp.where(keep, pltpu.roll(u, s, 0), 0.0)
        u = a * u_sh + u
        a = a * a_sh
    a_s[...] = a
    u_s[...] = u

    def group(gi, h):
        off = pl.multiple_of(gi * SUBLANES, SUBLANES)
        hg = a_s[pl.ds(off, SUBLANES), :] * h + u_s[pl.ds(off, SUBLANES), :]
        u_s[pl.ds(off, SUBLANES), :] = hg
        return jnp.broadcast_to(hg[SUBLANES - 1:SUBLANES, :], hg.shape)

    h_ref[...] = lax.fori_loop(0, ts // SUBLANES, group, h_ref[...])

    ga = xg_ref[:, w:]
    gelu = 0.5 * ga * (1.0 + jnp.tanh(0.7978845608028654 * (ga + 0.044715 * (ga * ga * ga))))
    ya_ref[...] = (u_s[...] * gelu).astype(BF16)


def _rglru(xg, cw, cb, wa, wx, ba, bx, lam, batch):
    t, w2 = xg.shape
    w = w2 // 2
    s = t // batch
    ts = min(TS_LRU, s)
    xg3 = xg.reshape(batch, s, w2)
    small = [cw, cb, wa, wx, ba, bx, lam]
    out = pl.pallas_call(
        _rglru_kernel,
        grid=(batch, s // ts),
        in_specs=[pl.BlockSpec((None, ts, w2), lambda b, j: (b, j, 0))] + [_const_spec(a.shape) for a in small],
        out_specs=pl.BlockSpec((None, ts, w), lambda b, j: (b, j, 0)),
        out_shape=jax.ShapeDtypeStruct((batch, s, w), BF16),
        scratch_shapes=[pltpu.VMEM((SUBLANES, w), F32), pltpu.VMEM((SUBLANES, w), F32),
                        pltpu.VMEM((ts, w), F32), pltpu.VMEM((ts, w), F32)],
        compiler_params=_params(2),
        name="rglru",
    )(xg3, *small)
    return out.reshape(t, w)


def _attn_bias_kernel(base_ref, o_ref):
    tq, tk = o_ref.shape
    row = base_ref[...]
    full = pltpu.roll(jnp.broadcast_to(row, (tq, tq + tk)), 0, 1, stride=1, stride_axis=0)
    qc = lax.broadcasted_iota(jnp.int32, (tq, tk), 0) // CHUNK
    kc = lax.broadcasted_iota(jnp.int32, (tq, tk), 1) // CHUNK
    valid = (kc >= qc) & (kc <= qc + LEFT_CHUNKS)
    o_ref[...] = jnp.where(valid, full[:, tq:], -1e30)


def _attn_bias(base):
    h = base.shape[0]
    tk = TQ + LEFT_CHUNKS * CHUNK
    return pl.pallas_call(
        _attn_bias_kernel,
        grid=(h,),
        in_specs=[pl.BlockSpec((None, 1, TQ + tk), lambda i: (i, 0, 0))],
        out_specs=pl.BlockSpec((None, TQ, tk), lambda i: (i, 0, 0)),
        out_shape=jax.ShapeDtypeStruct((h, TQ, tk), F32),
        compiler_params=_params(1),
        name="attn_bias",
    )(base)


def _attn_kernel(q_ref, k0_ref, k1_ref, k2_ref, v0_ref, v1_ref, v2_ref, bias_ref, o_ref):
    tq = q_ref.shape[0]
    i = pl.program_id(1)
    nkb = 3
    col = lax.broadcasted_iota(jnp.int32, (tq, nkb * tq), 1)
    col_ok = col >= jnp.maximum(nkb - 1 - i, 0) * tq
    lo = lax.broadcasted_iota(jnp.int32, (tq, LANES), 1) < (LANES // 2)
    k_refs = (k0_ref, k1_ref, k2_ref)
    v_refs = (v0_ref, v1_ref, v2_ref)
    for j in range(q_ref.shape[1] // LANES):
        sl = slice(LANES * j, LANES * (j + 1))
        q2 = q_ref[:, sl]
        kcat = jnp.concatenate([r[:, sl] for r in k_refs], axis=0)
        vcat = jnp.concatenate([r[:, sl] for r in v_refs], axis=0)
        outs = []
        for half in range(2):
            qm = jnp.where(lo if half == 0 else jnp.logical_not(lo), q2, jnp.zeros_like(q2))
            s = _dot_nt(qm, kcat) + bias_ref[2 * j + half]
            s = jnp.where(col_ok, s, -1e30)
            p = jnp.exp(s - jnp.max(s, axis=-1, keepdims=True))
            den = jnp.sum(p, axis=-1, keepdims=True)
            outs.append(_dot(p.astype(BF16), vcat) / den)
        o_ref[:, sl] = jnp.where(lo, outs[0], outs[1]).astype(BF16)


def _attention(qkv, bias, batch):
    t, w3 = qkv.shape
    w = w3 // 3
    s = t // batch
    qkv3 = qkv.reshape(batch, s, w3)

    def kv_spec(colblk, back):
        return pl.BlockSpec((None, TQ, w), lambda b, i: (b, jnp.maximum(i - back, 0), colblk))

    out = pl.pallas_call(
        _attn_kernel,
        grid=(batch, s // TQ),
        in_specs=[pl.BlockSpec((None, TQ, w), lambda b, i: (b, i, 0)),
                  kv_spec(1, 2), kv_spec(1, 1), kv_spec(1, 0),
                  kv_spec(2, 2), kv_spec(2, 1), kv_spec(2, 0),
                  _const_spec(bias.shape)],
        out_specs=pl.BlockSpec((None, TQ, w), lambda b, i: (b, i, 0)),
        out_shape=jax.ShapeDtypeStruct((batch, s, w), BF16),
        compiler_params=_params(2),
        name="attention",
    )(qkv3, qkv3, qkv3, qkv3, qkv3, qkv3, qkv3, bias)
    return out.reshape(t, w)


def _proj_route_kernel(a_ref, b_ref, r_ref, wa_ref, wb_ref, g_ref, wrh_ref, wrl_ref,
                       h_ref, hn_ref, meta_ref, cnt_ref, carry_ref):
    tm = a_ref.shape[0]

    @pl.when(pl.program_id(0) == 0)
    def _():
        carry_ref[...] = jnp.zeros_like(carry_ref)

    h = r_ref[...] + (_dot(a_ref[...], wa_ref[...]) + _dot(b_ref[...], wb_ref[...]))
    h_ref[...] = h
    hn = _rms(h, g_ref[...])
    _to_slab(hn_ref, hn)

    hi, lo = _split_bf16(hn)
    wrh = wrh_ref[...]
    logits = _dot(hi, wrh) + (_dot(hi, wrl_ref[...]) + _dot(lo, wrh))
    lane = lax.broadcasted_iota(jnp.int32, (tm, LANES), 1).astype(F32)
    neg = -jnp.inf
    far = float(LANES)

    def first_max(mask):
        v = jnp.max(jnp.where(mask, logits, neg), axis=-1, keepdims=True)
        idx = jnp.min(jnp.where(mask & (logits == v), lane, far), axis=-1, keepdims=True)
        return v, idx

    gmask = lane < N_GROUPS
    gmax, gidx = first_max(gmask)
    g_gate = 1.0 / jnp.sum(jnp.where(gmask, jnp.exp(logits - gmax), 0.0), axis=-1, keepdims=True)
    e_lo = N_GROUPS + EXPERTS_PER_GROUP * gidx
    emask = (lane >= e_lo) & (lane < e_lo + EXPERTS_PER_GROUP)
    v1, i1 = first_max(emask)
    v2, i2 = first_max(emask & (lane != i1))
    tt = jnp.exp(v2 - v1)
    w1 = g_gate / (1.0 + tt)
    w2 = g_gate * tt / (1.0 + tt)

    sel1 = lane == i1
    sel2 = lane == i2
    onehot = jnp.where(sel1 | sel2, 1.0, 0.0)
    rr = lax.broadcasted_iota(jnp.int32, (tm, tm), 0)
    cc = lax.broadcasted_iota(jnp.int32, (tm, tm), 1)
    ltri = jnp.where(cc < rr, 1.0, 0.0).astype(BF16)
    carry = carry_ref[0:1, :]
    before = _dot(ltri, onehot.astype(BF16)) + carry
    rank1 = jnp.sum(jnp.where(sel1, before, 0.0), axis=-1, keepdims=True)
    rank2 = jnp.sum(jnp.where(sel2, before, 0.0), axis=-1, keepdims=True)
    carry = carry + jnp.sum(onehot, axis=0, keepdims=True)
    carry_ref[...] = jnp.broadcast_to(carry, carry_ref.shape)
    cnt_ref[...] = jnp.broadcast_to(carry, cnt_ref.shape)

    cols = (i1 - N_GROUPS, i2 - N_GROUPS, rank1, rank2, w1, w2)
    meta = jnp.zeros((tm, LANES), F32)
    for c, val in enumerate(cols):
        meta = jnp.where(lane == c, val, meta)
    meta_ref[...] = meta


def _proj_route(a, acol, b, bcol, resid, wa, wb, g, wrh, wrl):
    t, d = resid.shape
    kw = wa.shape[0]
    tm = min(TM_ROUTE, t)
    return pl.pallas_call(
        _proj_route_kernel,
        grid=(t // tm,),
        in_specs=[pl.BlockSpec((tm, kw), lambda i: (i, acol)), pl.BlockSpec((tm, kw), lambda i: (i, bcol)),
                  pl.BlockSpec((tm, d), lambda i: (i, 0)),
                  _const_spec(wa.shape), _const_spec(wb.shape), _const_spec(g.shape),
                  _const_spec(wrh.shape), _const_spec(wrl.shape)],
        out_specs=[pl.BlockSpec((tm, d), lambda i: (i, 0)),
                   pl.BlockSpec((tm * SUBLANES, LANES), lambda i: (i, 0)),
                   pl.BlockSpec((tm, LANES), lambda i: (i, 0)),
                   pl.BlockSpec((SUBLANES, LANES), lambda i: (0, 0))],
        out_shape=[jax.ShapeDtypeStruct((t, d), F32),
                   jax.ShapeDtypeStruct((t * SUBLANES, LANES), F32),
                   jax.ShapeDtypeStruct((t, LANES), F32),
                   jax.ShapeDtypeStruct((SUBLANES, LANES), F32)],
        scratch_shapes=[pltpu.VMEM((SUBLANES, LANES), F32)],
        compiler_params=_params(1),
        name="proj_route",
    )(a, b, resid, wa, wb, g, wrh, wrl)


def _plan(meta, cnt, tb):
    t = meta.shape[0]
    expert = meta[:, 0:TOP_K].astype(jnp.int32)
    rank = meta[:, TOP_K:2 * TOP_K].astype(jnp.int32)
    counts = cnt[0, N_GROUPS:N_GROUPS + N_EXPERTS].astype(jnp.int32)
    padded = (counts + tb - 1) // tb * tb
    pad_end = jnp.cumsum(padded)
    pad_start = pad_end - padded
    dest = (pad_start[expert] + rank).reshape(-1)
    nblk = -(-(t * TOP_K + N_EXPERTS * (tb - 1)) // tb)
    blk_start = jnp.arange(nblk, dtype=jnp.int32) * tb
    blk_expert = jnp.minimum(jnp.sum((pad_end[None, :] <= blk_start[:, None]).astype(jnp.int32), axis=1),
                             N_EXPERTS - 1)
    nact = (pad_end[-1:] // tb).astype(jnp.int32)
    return dest, blk_expert, nact, nblk


def _dispatch_kernel(dest_ref, hn_ref, xs_in_ref, xs_ref, sem):
    del xs_in_ref
    tm = hn_ref.shape[0] // SUBLANES
    base = pl.program_id(0) * (TOP_K * tm)

    def copy(r, k):
        d = dest_ref[base + TOP_K * r + k]
        return pltpu.make_async_copy(
            hn_ref.at[pl.ds(pl.multiple_of(r * SUBLANES, SUBLANES), SUBLANES), :],
            xs_ref.at[pl.ds(pl.multiple_of(d * SUBLANES, SUBLANES), SUBLANES), :], sem)

    def start(r, c):
        for k in range(TOP_K):
            copy(r, k).start()
        return c

    def wait(r, c):
        for k in range(TOP_K):
            copy(r, k).wait()
        return c

    lax.fori_loop(0, tm, start, 0)
    lax.fori_loop(0, tm, wait, 0)


def _dispatch(dest, hn_slab, nblk, tb):
    t = hn_slab.shape[0] // SUBLANES
    tm = min(TM_COMB, t)
    xs0 = jnp.zeros((nblk * tb * SUBLANES, LANES), F32)
    return pl.pallas_call(
        _dispatch_kernel,
        grid_spec=pltpu.PrefetchScalarGridSpec(
            num_scalar_prefetch=1,
            grid=(t // tm,),
            in_specs=[pl.BlockSpec((tm * SUBLANES, LANES), lambda i, d: (i, 0)),
                      pl.BlockSpec(memory_space=pl.ANY)],
            out_specs=pl.BlockSpec(memory_space=pl.ANY),
            scratch_shapes=[pltpu.SemaphoreType.DMA(())]),
        out_shape=jax.ShapeDtypeStruct(xs0.shape, F32),
        input_output_aliases={2: 0},
        compiler_params=_params(1),
        name="dispatch",
    )(dest, hn_slab, xs0)


def _expert_kernel(be_ref, nact_ref, xs_ref, wg_ref, wu_ref, wd_ref, ys_ref, wg_s, wu_s, wd_s):
    i = pl.program_id(0)
    tb = xs_ref.shape[0] // SUBLANES
    active = i < nact_ref[0]
    changed = (i == 0) | (be_ref[i] != be_ref[jnp.maximum(i - 1, 0)])

    @pl.when(active & changed)
    def _():
        wg_s[...] = wg_ref[...].astype(BF16)
        wu_s[...] = wu_ref[...].astype(BF16)
        wd_s[...] = wd_ref[...].astype(BF16)

    @pl.when(active)
    def _():
        x = _from_slab(xs_ref, tb).astype(BF16)
        gate = _dot(x, wg_s[...])
        up = _dot(x, wu_s[...])
        hid = (gate * jax.nn.sigmoid(gate) * up).astype(BF16)
        _to_slab(ys_ref, _dot(hid, wd_s[...]))

    @pl.when(jnp.logical_not(active))
    def _():
        ys_ref[...] = jnp.zeros_like(ys_ref)


def _experts(blk_expert, nact, xs, wg, wu, wd, tb):
    nblk = blk_expert.shape[0]
    _, d, de = wg.shape

    def row_map(i, be, na):
        return (jnp.minimum(i, na[0] - 1), 0)

    def w_map(i, be, na):
        return (be[jnp.minimum(i, na[0] - 1)], 0, 0)

    return pl.pallas_call(
        _expert_kernel,
        grid_spec=pltpu.PrefetchScalarGridSpec(
            num_scalar_prefetch=2,
            grid=(nblk,),
            in_specs=[pl.BlockSpec((tb * SUBLANES, LANES), row_map),
                      pl.BlockSpec((None, d, de), w_map), pl.BlockSpec((None, d, de), w_map),
                      pl.BlockSpec((None, de, d), w_map)],
            out_specs=pl.BlockSpec((tb * SUBLANES, LANES), lambda i, be, na: (i, 0)),
            scratch_shapes=[pltpu.VMEM((d, de), BF16), pltpu.VMEM((d, de), BF16), pltpu.VMEM((de, d), BF16)]),
        out_shape=jax.ShapeDtypeStruct(xs.shape, F32),
        compiler_params=_params(1),
        name="experts",
    )(blk_expert, nact, xs, wg, wu, wd)


def _combine(dest_ref, ys_ref, meta_ref, buf_ref, sem_ref, tm):
    i = pl.program_id(0)
    n = pl.num_programs(0)
    slot = i % 2

    def copy(tile, sl, r, k):
        d = dest_ref[tile * (TOP_K * tm) + TOP_K * r + k]
        return pltpu.make_async_copy(
            ys_ref.at[pl.ds(pl.multiple_of(d * SUBLANES, SUBLANES), SUBLANES), :],
            buf_ref.at[sl, k, pl.ds(pl.multiple_of(r * SUBLANES, SUBLANES), SUBLANES), :],
            sem_ref.at[sl])

    def start_tile(tile, sl):
        def body(r, c):
            for k in range(TOP_K):
                copy(tile, sl, r, k).start()
            return c
        lax.fori_loop(0, tm, body, 0)

    def wait_tile(tile, sl):
        def body(r, c):
            for k in range(TOP_K):
                copy(tile, sl, r, k).wait()
            return c
        lax.fori_loop(0, tm, body, 0)

    @pl.when(i == 0)
    def _():
        start_tile(0, 0)

    @pl.when(i + 1 < n)
    def _():
        start_tile(i + 1, 1 - slot)

    wait_tile(i, slot)
    y = None
    for k in range(TOP_K):
        gate = meta_ref[:, 2 * TOP_K + k:2 * TOP_K + k + 1]
        term = _from_slab(buf_ref.at[slot, k], tm) * gate
        y = term if y is None else y + term
    return y


def _combine_scratch(tm):
    return [pltpu.VMEM((2, TOP_K, tm * SUBLANES, LANES), F32), pltpu.SemaphoreType.DMA((2,))]


def _comb_proj1_kernel(dest_ref, h_ref, meta_ref, ys_ref, g_ref, w_ref, lbl_ref,
                       h2_ref, qs_ref, lf_ref, iv_ref, gs_ref, buf_ref, sem_ref, *, lb_rows):
    tm = h_ref.shape[0]
    h2 = h_ref[...] + _combine(dest_ref, ys_ref, meta_ref, buf_ref, sem_ref, tm)
    h2_ref[...] = h2
    z = _dot(_rms(h2, g_ref[...]).astype(BF16), w_ref[...])
    d = h2.shape[1]
    q, fz, iv, gz = (z[:, d * c:d * (c + 1)] for c in range(4))
    lbl = lbl_ref[...]
    ex = jnp.exp(lbl - jnp.max(lbl, axis=0, keepdims=True))
    lb = jnp.sum(ex[:lb_rows], axis=0, keepdims=True) / jnp.sum(ex, axis=0, keepdims=True)
    f = lb + (1.0 - lb) * jax.nn.sigmoid(fz)
    qs_ref[...] = (q * jax.nn.sigmoid(q)).astype(BF16)
    lf_ref[...] = jnp.log(f)
    iv_ref[...] = iv.astype(BF16)
    gs_ref[...] = (gz * jax.nn.sigmoid(gz)).astype(BF16)


def _comb_proj1(dest, h, meta, ys, g, w, lb_logits, lb_rows):
    t, d = h.shape
    tm = min(TM_COMB, t)
    row = lambda i, dd: (i, 0)
    const = lambda shape: pl.BlockSpec(shape, lambda i, dd: (0,) * len(shape))
    return pl.pallas_call(
        functools.partial(_comb_proj1_kernel, lb_rows=lb_rows),
        grid_spec=pltpu.PrefetchScalarGridSpec(
            num_scalar_prefetch=1,
            grid=(t // tm,),
            in_specs=[pl.BlockSpec((tm, d), row), pl.BlockSpec((tm, LANES), row),
                      pl.BlockSpec(memory_space=pl.ANY),
                      const(g.shape), const(w.shape), const(lb_logits.shape)],
            out_specs=[pl.BlockSpec((tm, d), row)] * 5,
            scratch_shapes=_combine_scratch(tm)),
        out_shape=[jax.ShapeDtypeStruct((t, d), F32), jax.ShapeDtypeStruct((t, d), BF16),
                   jax.ShapeDtypeStruct((t, d), F32), jax.ShapeDtypeStruct((t, d), BF16),
                   jax.ShapeDtypeStruct((t, d), BF16)],
        compiler_params=_params(1),
        name="comb_proj1",
    )(dest, h, meta, ys, g, w, lb_logits)


def _comb_final_kernel(dest_ref, h_ref, meta_ref, ys_ref, g_ref, o_ref, buf_ref, sem_ref):
    tm = h_ref.shape[0]
    h = h_ref[...] + _combine(dest_ref, ys_ref, meta_ref, buf_ref, sem_ref, tm)
    o_ref[...] = _rms(h, g_ref[...])


def _comb_final(dest, h, meta, ys, g):
    t, d = h.shape
    tm = min(TM_COMB, t)
    row = lambda i, dd: (i, 0)
    return pl.pallas_call(
        _comb_final_kernel,
        grid_spec=pltpu.PrefetchScalarGridSpec(
            num_scalar_prefetch=1,
            grid=(t // tm,),
            in_specs=[pl.BlockSpec((tm, d), row), pl.BlockSpec((tm, LANES), row),
                      pl.BlockSpec(memory_space=pl.ANY),
                      pl.BlockSpec(g.shape, lambda i, dd: (0, 0))],
            out_specs=pl.BlockSpec((tm, d), row),
            scratch_shapes=_combine_scratch(tm)),
        out_shape=jax.ShapeDtypeStruct((t, d), F32),
        compiler_params=_params(1),
        name="comb_final",
    )(dest, h, meta, ys, g)


def _hgrn_kernel(qs_ref, lf_ref, iv_ref, gs_ref, ng_ref, o_ref, st_ref, d1_s, t16_s, *, dk):
    th = qs_ref.shape[0]
    nsub = CHUNK // SUB

    @pl.when(pl.program_id(1) == 0)
    def _():
        st_ref[...] = jnp.zeros_like(st_ref)

    rr = lax.broadcasted_iota(jnp.int32, (th, th), 0)
    cc = lax.broadcasted_iota(jnp.int32, (th, th), 1)
    same_sub = (rr // SUB) == (cc // SUB)
    same_chunk = (rr // CHUNK) == (cc // CHUNK)
    m_diag = same_sub & (cc <= rr)
    dsub = rr // SUB - cc // SUB
    m_off = [same_chunk & (dsub == dd) for dd in range(1, nsub)]

    hi, lo = _split_bf16(lf_ref[...])
    tri = jnp.where(m_diag, 1.0, 0.0).astype(BF16)
    ones = jnp.where(same_sub, 1.0, 0.0).astype(BF16)
    d1_s[...] = _dot(tri, hi) + _dot(tri, lo)
    t16_s[...] = _dot(ones, hi) + _dot(ones, lo)
    rsub = (lax.broadcasted_iota(jnp.int32, (th, dk), 0) // SUB) % nsub

    def head(hd, carry):
        ls = pl.ds(pl.multiple_of(hd * dk, dk), dk)
        d1 = d1_s[:, ls]
        t16 = t16_s[:, ls]
        q = qs_ref[:, ls].astype(F32)
        v = iv_ref[:, ls]
        kt = 1.0 - jnp.exp(lf_ref[:, ls])
        suf = t16 - d1
        kx = (kt * jnp.exp(suf)).astype(BF16)
        qv = [q * jnp.exp(-suf)]
        acc = d1
        tail = suf
        for dd in range(1, nsub):
            qv.append(q * jnp.exp(acc))
            acc = acc + jnp.where(rsub >= dd, pltpu.roll(t16, SUB * dd, 0), 0.0)
            tail = tail + jnp.where(rsub < nsub - dd, pltpu.roll(t16, th - SUB * dd, 0), 0.0)
        b = acc
        qb = (q * jnp.exp(b)).astype(BF16)
        kend = (kt * jnp.exp(tail)).astype(BF16)

        a4 = _dot_nt(jnp.concatenate(qv, axis=0).astype(BF16), kx)
        att = jnp.where(m_diag, a4[:th], 0.0)
        for dd in range(1, nsub):
            att = jnp.where(m_off[dd - 1], a4[dd * th:(dd + 1) * th], att)
        o_intra = _dot(att.astype(BF16), v)

        st = st_ref[hd]
        outs = []
        for c in range(th // CHUNK):
            rows = slice(CHUNK * c, CHUNK * (c + 1))
            outs.append(o_intra[rows] + _dot_nt(qb[rows], st.astype(BF16)))
            dec = jnp.exp(b[CHUNK * (c + 1) - 1:CHUNK * (c + 1), :])
            st = st * dec + _dot_tn(v[rows], kend[rows])
        st_ref[hd] = st
        oh = jnp.concatenate(outs, axis=0)
        on = oh * lax.rsqrt(jnp.mean(oh * oh, axis=-1, keepdims=True) + EPS)
        o_ref[:, ls] = (on * ng_ref[:, ls] * gs_ref[:, ls].astype(F32)).astype(BF16)
        return carry

    lax.fori_loop(0, qs_ref.shape[1] // dk, head, 0)


def _hgrn(qs, lf, iv, gs, ng, batch, dk):
    t, d = qs.shape
    s = t // batch
    th = min(TH, s)
    blk = pl.BlockSpec((None, th, d), lambda b, j: (b, j, 0))
    r3 = lambda a: a.reshape(batch, s, d)
    out = pl.pallas_call(
        functools.partial(_hgrn_kernel, dk=dk),
        grid=(batch, s // th),
        in_specs=[blk, blk, blk, blk, _const_spec(ng.shape)],
        out_specs=blk,
        out_shape=jax.ShapeDtypeStruct((batch, s, d), BF16),
        scratch_shapes=[pltpu.VMEM((d // dk, dk, dk), F32), pltpu.VMEM((th, d), F32), pltpu.VMEM((th, d), F32)],
        compiler_params=_params(2),
        name="hgrn2",
    )(r3(qs), r3(lf), r3(iv), r3(gs), ng)
    return out.reshape(t, d)


def _block_diag(w):
    n, c, dd = w.shape
    eye = jnp.eye(n, dtype=w.dtype)
    return (eye[:, None, :, None] * w[:, :, None, :]).reshape(n * c, n * dd)


def _router_weights(wg, we):
    d = wg.shape[0]
    w = jnp.concatenate([wg, we, jnp.zeros((d, LANES - wg.shape[1] - we.shape[1]), F32)], axis=1)
    hi = w.astype(BF16)
    return hi, (w - hi.astype(F32)).astype(BF16)


def _moe(layer, dest_plan_inputs, moe_w_gate, moe_w_up, moe_w_down):
    meta, cnt, hn_slab = dest_plan_inputs
    dest, blk_expert, nact, nblk = _plan(meta, cnt, TB_EXPERT)
    xs = _dispatch(dest, hn_slab, nblk, TB_EXPERT)
    ys = _experts(blk_expert, nact, xs, moe_w_gate[layer], moe_w_up[layer], moe_w_down[layer], TB_EXPERT)
    return dest, ys


def kernel(x, norm_mix_g, norm_ffn_g, norm_final_g, ab_w_in, ab_conv_w, ab_conv_b, rg_w_a, rg_b_a, rg_w_x, rg_b_x, rg_lambda, attn_rel_bias, ab_w_out, c_w_in, c_lb_logits, c_norm_g, c_w_out, moe_router_group, moe_router_expert, moe_w_gate, moe_w_up, moe_w_down):
    batch, seq, d = x.shape
    t = batch * seq
    xt = x.reshape(t, d)
    row = lambda v: v.reshape(1, -1)

    aw = ab_conv_w.shape[2]
    xg, qkv = _ln_proj0(xt, row(norm_mix_g[0]), ab_w_in[0].astype(BF16), 2 * aw)
    ya = _rglru(xg, ab_conv_w[0], row(ab_conv_b[0]),
                _block_diag(rg_w_a[0]).astype(BF16), _block_diag(rg_w_x[0]).astype(BF16),
                row(rg_b_a[0]), row(rg_b_x[0]), row(rg_lambda[0]), batch)
    tk = TQ + LEFT_CHUNKS * CHUNK
    dist = jnp.clip(tk - jnp.arange(TQ + tk), -REL_CLIP, REL_CLIP) + REL_CLIP
    bias = _attn_bias(attn_rel_bias[0][:, None, dist])
    yb = _attention(qkv, bias, batch)
    wo = ab_w_out[0].astype(BF16)
    wrh, wrl = _router_weights(moe_router_group[0], moe_router_expert[0])
    h1, hn1, meta1, cnt1 = _proj_route(ya, 0, yb, 0, xt, wo[:aw], wo[aw:], row(norm_ffn_g[0]), wrh, wrl)
    dest1, ys1 = _moe(0, (meta1, cnt1, hn1), moe_w_gate, moe_w_up, moe_w_down)

    dk = c_norm_g.shape[1] // 8
    h2, qs, lf, iv, gs = _comb_proj1(dest1, h1, meta1, ys1, row(norm_mix_g[1]), c_w_in[0].astype(BF16),
                                     c_lb_logits, 1)
    om = _hgrn(qs, lf, iv, gs, row(c_norm_g[0]), batch, dk)
    wo = c_w_out[0].astype(BF16)
    half = wo.shape[0] // 2
    wrh, wrl = _router_weights(moe_router_group[1], moe_router_expert[1])
    h3, hn3, meta3, cnt3 = _proj_route(om, 0, om, 1, h2, wo[:half], wo[half:], row(norm_ffn_g[1]), wrh, wrl)
    dest3, ys3 = _moe(1, (meta3, cnt3, hn3), moe_w_gate, moe_w_up, moe_w_down)

    out = _comb_final(dest3, h3, meta3, ys3, row(norm_final_g))
    return out.reshape(batch, seq, d)
```

```python
import functools

import jax
import jax.numpy as jnp
from jax import lax
from jax.experimental import pallas as pl
from jax.experimental.pallas import tpu as pltpu

F32 = jnp.float32
BF16 = jnp.bfloat16

EPS = 1e-6
RG_C = 8.0
CHUNK = 64
LEFT_CHUNKS = 8
REL_CLIP = 256
N_GROUPS = 4
EXPERTS_PER_GROUP = 8
N_EXPERTS = N_GROUPS * EXPERTS_PER_GROUP
TOP_K = 2

LANES = 128
SUBLANES = 8
SUB = 16

TM_PROJ = 512
TS_LRU = 512
TQ = 256
TM_ROUTE = 256
TB_EXPERT = 256
TM_COMB = 256
TH = 256
DMA_UNROLL = 8
VMEM_MB = 48


def _params(n_axes, vmem_mb=VMEM_MB):
    return pltpu.CompilerParams(dimension_semantics=("arbitrary",) * n_axes,
                                vmem_limit_bytes=vmem_mb * 1024 * 1024)


def _const_spec(shape):
    nd = len(shape)
    return pl.BlockSpec(shape, lambda *_: (0,) * nd)


def _rms(x, g):
    return x * lax.rsqrt(jnp.mean(x * x, axis=-1, keepdims=True) + EPS) * g


def _dot(a, b):
    return jnp.dot(a, b, preferred_element_type=F32)


def _dot_nt(a, b):
    return lax.dot_general(a, b, (((1,), (1,)), ((), ())), preferred_element_type=F32)


def _dot_tn(a, b):
    return lax.dot_general(a, b, (((0,), (0,)), ((), ())), preferred_element_type=F32)


def _split_bf16(x):
    hi = x.astype(BF16)
    lo = (x - hi.astype(F32)).astype(BF16)
    return hi, lo


def _to_slab(ref, val):
    m = val.shape[0]
    for j in range(val.shape[1] // LANES):
        ref[pl.ds(j, m, stride=SUBLANES), :] = val[:, LANES * j:LANES * (j + 1)]


def _from_slab(ref, m):
    return jnp.concatenate([ref[pl.ds(j, m, stride=SUBLANES), :] for j in range(SUBLANES)], axis=1)


def _ln_proj0_kernel(x_ref, g_ref, w_ref, xg_ref, qkv_ref):
    hn = _rms(x_ref[...], g_ref[...]).astype(BF16)
    z = _dot(hn, w_ref[...])
    aw = xg_ref.shape[1]
    bw = (z.shape[1] - aw) // 3
    xg_ref[...] = z[:, :aw]
    qkv_ref[:, :bw] = (z[:, aw:aw + bw] * (CHUNK ** -0.5)).astype(BF16)
    qkv_ref[:, bw:] = z[:, aw + bw:].astype(BF16)


def _ln_proj0(x, g, w, aw):
    t, d = x.shape
    n = w.shape[1]
    tm = min(TM_PROJ, t)
    return pl.pallas_call(
        _ln_proj0_kernel,
        grid=(t // tm,),
        in_specs=[pl.BlockSpec((tm, d), lambda i: (i, 0)), _const_spec(g.shape), _const_spec(w.shape)],
        out_specs=[pl.BlockSpec((tm, aw), lambda i: (i, 0)), pl.BlockSpec((tm, n - aw), lambda i: (i, 0))],
        out_shape=[jax.ShapeDtypeStruct((t, aw), F32), jax.ShapeDtypeStruct((t, n - aw), BF16)],
        compiler_params=_params(1),
        name="ln_proj0",
    )(x, g, w)


def _rglru_kernel(xg_ref, cw_ref, cb_ref, wa_ref, wx_ref, ba_ref, bx_ref, lam_ref, ya_ref,
                  halo_ref, h_ref, a_s, u_s):
    ts = xg_ref.shape[0]
    w = ya_ref.shape[1]

    @pl.when(pl.program_id(1) == 0)
    def _():
        halo_ref[...] = jnp.zeros_like(halo_ref)
        h_ref[...] = jnp.zeros_like(h_ref)

    xa = xg_ref[:, :w]
    halo = halo_ref[...]
    row8 = lax.broadcasted_iota(jnp.int32, (SUBLANES, w), 0)
    nk = cw_ref.shape[0]
    y = cb_ref[...] + cw_ref[nk - 1:nk, :] * xa
    for k in range(1, nk):
        r = pltpu.roll(xa, k, 0)
        top = jnp.where(row8 < k, pltpu.roll(halo, k, 0), r[:SUBLANES, :])
        y = y + cw_ref[nk - 1 - k:nk - k, :] * jnp.concatenate([top, r[SUBLANES:, :]], axis=0)
    halo_ref[...] = xa[ts - SUBLANES:, :]

    yb = y.astype(BF16)
    rg = jax.nn.sigmoid(_dot(yb, wa_ref[...]) + ba_ref[...])
    ig = jax.nn.sigmoid(_dot(yb, wx_ref[...]) + bx_ref[...])
    lam = lam_ref[...]
    log_sig = jnp.minimum(lam, 0.0) - jnp.log1p(jnp.exp(-jnp.abs(lam)))
    log_a = RG_C * rg * log_sig
    a = jnp.exp(log_a)
    u = jnp.sqrt(-jnp.tanh(log_a) * (a * a + 1.0)) * (ig * y)

    rowm = lax.broadcasted_iota(jnp.int32, (ts, w), 0) & (SUBLANES - 1)
    for s in (1, 2, 4):
        keep = rowm >= s
        a_sh = jnp.where(keep, pltpu.roll(a, s, 0), 1.0)
        u_sh = jnp.where(keep, pltpu.roll(u, s, 0), 0.0)
        u = a * u_sh + u
        a = a * a_sh
    a_s[...] = a
    u_s[...] = u

    def group(gi, h):
        off = pl.multiple_of(gi * SUBLANES, SUBLANES)
        hg = a_s[pl.ds(off, SUBLANES), :] * h + u_s[pl.ds(off, SUBLANES), :]
        u_s[pl.ds(off, SUBLANES), :] = hg
        return jnp.broadcast_to(hg[SUBLANES - 1:SUBLANES, :], hg.shape)

    h_ref[...] = lax.fori_loop(0, ts // SUBLANES, group, h_ref[...])

    ga = xg_ref[:, w:]
    gelu = 0.5 * ga * (1.0 + jnp.tanh(0.7978845608028654 * (ga + 0.044715 * (ga * ga * ga))))
    ya_ref[...] = (u_s[...] * gelu).astype(BF16)


def _rglru(xg, cw, cb, wa, wx, ba, bx, lam, batch):
    t, w2 = xg.shape
    w = w2 // 2
    s = t // batch
    ts = min(TS_LRU, s)
    xg3 = xg.reshape(batch, s, w2)
    small = [cw, cb, wa, wx, ba, bx, lam]
    out = pl.pallas_call(
        _rglru_kernel,
        grid=(batch, s // ts),
        in_specs=[pl.BlockSpec((None, ts, w2), lambda b, j: (b, j, 0))] + [_const_spec(a.shape) for a in small],
        out_specs=pl.BlockSpec((None, ts, w), lambda b, j: (b, j, 0)),
        out_shape=jax.ShapeDtypeStruct((batch, s, w), BF16),
        scratch_shapes=[pltpu.VMEM((SUBLANES, w), F32), pltpu.VMEM((SUBLANES, w), F32),
                        pltpu.VMEM((ts, w), F32), pltpu.VMEM((ts, w), F32)],
        compiler_params=_params(2),
        name="rglru",
    )(xg3, *small)
    return out.reshape(t, w)


def _attn_bias_kernel(base_ref, o_ref):
    tq, tk = o_ref.shape
    row = base_ref[...]
    full = pltpu.roll(jnp.broadcast_to(row, (tq, tq + tk)), 0, 1, stride=1, stride_axis=0)
    qc = lax.broadcasted_iota(jnp.int32, (tq, tk), 0) // CHUNK
    kc = lax.broadcasted_iota(jnp.int32, (tq, tk), 1) // CHUNK
    valid = (kc >= qc) & (kc <= qc + LEFT_CHUNKS)
    o_ref[...] = jnp.where(valid, full[:, tq:], -1e30)


def _attn_bias(base):
    h = base.shape[0]
    tk = TQ + LEFT_CHUNKS * CHUNK
    return pl.pallas_call(
        _attn_bias_kernel,
        grid=(h,),
        in_specs=[pl.BlockSpec((None, 1, TQ + tk), lambda i: (i, 0, 0))],
        out_specs=pl.BlockSpec((None, TQ, tk), lambda i: (i, 0, 0)),
        out_shape=jax.ShapeDtypeStruct((h, TQ, tk), F32),
        compiler_params=_params(1),
        name="attn_bias",
    )(base)


def _attn_kernel(q_ref, k0_ref, k1_ref, k2_ref, v0_ref, v1_ref, v2_ref, bias_ref, o_ref):
    tq = q_ref.shape[0]
    i = pl.program_id(1)
    nkb = 3
    col = lax.broadcasted_iota(jnp.int32, (tq, nkb * tq), 1)
    col_ok = col >= jnp.maximum(nkb - 1 - i, 0) * tq
    lo = lax.broadcasted_iota(jnp.int32, (tq, LANES), 1) < (LANES // 2)
    k_refs = (k0_ref, k1_ref, k2_ref)
    v_refs = (v0_ref, v1_ref, v2_ref)
    for j in range(q_ref.shape[1] // LANES):
        sl = slice(LANES * j, LANES * (j + 1))
        q2 = q_ref[:, sl]
        kcat = jnp.concatenate([r[:, sl] for r in k_refs], axis=0)
        vcat = jnp.concatenate([r[:, sl] for r in v_refs], axis=0)
        outs = []
        for half in range(2):
            qm = jnp.where(lo if half == 0 else jnp.logical_not(lo), q2, jnp.zeros_like(q2))
            s = _dot_nt(qm, kcat) + bias_ref[2 * j + half]
            s = jnp.where(col_ok, s, -1e30)
            p = jnp.exp(s - jnp.max(s, axis=-1, keepdims=True))
            den = jnp.sum(p, axis=-1, keepdims=True)
            outs.append(_dot(p.astype(BF16), vcat) / den)
        o_ref[:, sl] = jnp.where(lo, outs[0], outs[1]).astype(BF16)


def _attention(qkv, bias, batch):
    t, w3 = qkv.shape
    w = w3 // 3
    s = t // batch
    qkv3 = qkv.reshape(batch, s, w3)

    def kv_spec(colblk, back):
        return pl.BlockSpec((None, TQ, w), lambda b, i: (b, jnp.maximum(i - back, 0), colblk))

    out = pl.pallas_call(
        _attn_kernel,
        grid=(batch, s // TQ),
        in_specs=[pl.BlockSpec((None, TQ, w), lambda b, i: (b, i, 0)),
                  kv_spec(1, 2), kv_spec(1, 1), kv_spec(1, 0),
                  kv_spec(2, 2), kv_spec(2, 1), kv_spec(2, 0),
                  _const_spec(bias.shape)],
        out_specs=pl.BlockSpec((None, TQ, w), lambda b, i: (b, i, 0)),
        out_shape=jax.ShapeDtypeStruct((batch, s, w), BF16),
        compiler_params=_params(2),
        name="attention",
    )(qkv3, qkv3, qkv3, qkv3, qkv3, qkv3, qkv3, bias)
    return out.reshape(t, w)


def _proj_route_kernel(a_ref, b_ref, r_ref, wa_ref, wb_ref, g_ref, wrh_ref, wrl_ref,
                       h_ref, hn_ref, meta_ref, cnt_ref, carry_ref):
    tm = a_ref.shape[0]

    @pl.when(pl.program_id(0) == 0)
    def _():
        carry_ref[...] = jnp.zeros_like(carry_ref)

    h = r_ref[...] + (_dot(a_ref[...], wa_ref[...]) + _dot(b_ref[...], wb_ref[...]))
    h_ref[...] = h
    hn = _rms(h, g_ref[...])
    _to_slab(hn_ref, hn)

    hi, lo = _split_bf16(hn)
    wrh = wrh_ref[...]
    logits = _dot(hi, wrh) + (_dot(hi, wrl_ref[...]) + _dot(lo, wrh))
    lane = lax.broadcasted_iota(jnp.int32, (tm, LANES), 1).astype(F32)
    neg = -jnp.inf
    far = float(LANES)

    def first_max(mask):
        v = jnp.max(jnp.where(mask, logits, neg), axis=-1, keepdims=True)
        idx = jnp.min(jnp.where(mask & (logits == v), lane, far), axis=-1, keepdims=True)
        return v, idx

    gmask = lane < N_GROUPS
    gmax, gidx = first_max(gmask)
    g_gate = 1.0 / jnp.sum(jnp.where(gmask, jnp.exp(logits - gmax), 0.0), axis=-1, keepdims=True)
    e_lo = N_GROUPS + EXPERTS_PER_GROUP * gidx
    emask = (lane >= e_lo) & (lane < e_lo + EXPERTS_PER_GROUP)
    v1, i1 = first_max(emask)
    v2, i2 = first_max(emask & (lane != i1))
    tt = jnp.exp(v2 - v1)
    w1 = g_gate / (1.0 + tt)
    w2 = g_gate * tt / (1.0 + tt)

    sel1 = lane == i1
    sel2 = lane == i2
    onehot = jnp.where(sel1 | sel2, 1.0, 0.0)
    rr = lax.broadcasted_iota(jnp.int32, (tm, tm), 0)
    cc = lax.broadcasted_iota(jnp.int32, (tm, tm), 1)
    ltri = jnp.where(cc < rr, 1.0, 0.0).astype(BF16)
    carry = carry_ref[0:1, :]
    before = _dot(ltri, onehot.astype(BF16)) + carry
    rank1 = jnp.sum(jnp.where(sel1, before, 0.0), axis=-1, keepdims=True)
    rank2 = jnp.sum(jnp.where(sel2, before, 0.0), axis=-1, keepdims=True)
    carry = carry + jnp.sum(onehot, axis=0, keepdims=True)
    carry_ref[...] = jnp.broadcast_to(carry, carry_ref.shape)
    cnt_ref[...] = jnp.broadcast_to(carry, cnt_ref.shape)

    cols = (i1 - N_GROUPS, i2 - N_GROUPS, rank1, rank2, w1, w2)
    meta = jnp.zeros((tm, LANES), F32)
    for c, val in enumerate(cols):
        meta = jnp.where(lane == c, val, meta)
    meta_ref[...] = meta


def _proj_route(a, acol, b, bcol, resid, wa, wb, g, wrh, wrl):
    t, d = resid.shape
    kw = wa.shape[0]
    tm = min(TM_ROUTE, t)
    return pl.pallas_call(
        _proj_route_kernel,
        grid=(t // tm,),
        in_specs=[pl.BlockSpec((tm, kw), lambda i: (i, acol)), pl.BlockSpec((tm, kw), lambda i: (i, bcol)),
                  pl.BlockSpec((tm, d), lambda i: (i, 0)),
                  _const_spec(wa.shape), _const_spec(wb.shape), _const_spec(g.shape),
                  _const_spec(wrh.shape), _const_spec(wrl.shape)],
        out_specs=[pl.BlockSpec((tm, d), lambda i: (i, 0)),
                   pl.BlockSpec((tm * SUBLANES, LANES), lambda i: (i, 0)),
                   pl.BlockSpec((tm, LANES), lambda i: (i, 0)),
                   pl.BlockSpec((SUBLANES, LANES), lambda i: (0, 0))],
        out_shape=[jax.ShapeDtypeStruct((t, d), F32),
                   jax.ShapeDtypeStruct((t * SUBLANES, LANES), F32),
                   jax.ShapeDtypeStruct((t, LANES), F32),
                   jax.ShapeDtypeStruct((SUBLANES, LANES), F32)],
        scratch_shapes=[pltpu.VMEM((SUBLANES, LANES), F32)],
        compiler_params=_params(1),
        name="proj_route",
    )(a, b, resid, wa, wb, g, wrh, wrl)


def _plan(meta, cnt, tb):
    t = meta.shape[0]
    expert = meta[:, 0:TOP_K].astype(jnp.int32)
    rank = meta[:, TOP_K:2 * TOP_K].astype(jnp.int32)
    counts = cnt[0, N_GROUPS:N_GROUPS + N_EXPERTS].astype(jnp.int32)
    padded = (counts + tb - 1) // tb * tb
    pad_end = jnp.cumsum(padded)
    pad_start = pad_end - padded
    dest = (pad_start[expert] + rank).reshape(-1)
    nblk = -(-(t * TOP_K + N_EXPERTS * (tb - 1)) // tb)
    blk_start = jnp.arange(nblk, dtype=jnp.int32) * tb
    blk_expert = jnp.minimum(jnp.sum((pad_end[None, :] <= blk_start[:, None]).astype(jnp.int32), axis=1),
                             N_EXPERTS - 1)
    nact = (pad_end[-1:] // tb).astype(jnp.int32)
    return dest, blk_expert, nact, nblk


def _dispatch_kernel(dest_ref, hn_ref, xs_in_ref, xs_ref, sem):
    del xs_in_ref
    tm = hn_ref.shape[0] // SUBLANES
    base = pl.program_id(0) * (TOP_K * tm)

    def copy(r, k):
        d = dest_ref[base + TOP_K * r + k]
        return pltpu.make_async_copy(
            hn_ref.at[pl.ds(pl.multiple_of(r * SUBLANES, SUBLANES), SUBLANES), :],
            xs_ref.at[pl.ds(pl.multiple_of(d * SUBLANES, SUBLANES), SUBLANES), :], sem)

    def start(r, c):
        for k in range(TOP_K):
            copy(r, k).start()
        return c

    lax.fori_loop(0, tm, start, 0, unroll=DMA_UNROLL)
    for k in range(TOP_K):
        pltpu.make_async_copy(hn_ref, xs_ref.at[pl.ds(0, tm * SUBLANES), :], sem).wait()


def _dispatch(dest, hn_slab, nblk, tb):
    t = hn_slab.shape[0] // SUBLANES
    tm = min(TM_COMB, t)
    xs0 = jnp.zeros((nblk * tb * SUBLANES, LANES), F32)
    return pl.pallas_call(
        _dispatch_kernel,
        grid_spec=pltpu.PrefetchScalarGridSpec(
            num_scalar_prefetch=1,
            grid=(t // tm,),
            in_specs=[pl.BlockSpec((tm * SUBLANES, LANES), lambda i, d: (i, 0)),
                      pl.BlockSpec(memory_space=pl.ANY)],
            out_specs=pl.BlockSpec(memory_space=pl.ANY),
            scratch_shapes=[pltpu.SemaphoreType.DMA(())]),
        out_shape=jax.ShapeDtypeStruct(xs0.shape, F32),
        input_output_aliases={2: 0},
        compiler_params=_params(1),
        name="dispatch",
    )(dest, hn_slab, xs0)


def _expert_kernel(be_ref, nact_ref, xs_ref, wg_ref, wu_ref, wd_ref, ys_ref, wg_s, wu_s, wd_s):
    i = pl.program_id(0)
    tb = xs_ref.shape[0] // SUBLANES
    active = i < nact_ref[0]
    changed = (i == 0) | (be_ref[i] != be_ref[jnp.maximum(i - 1, 0)])

    @pl.when(active & changed)
    def _():
        wg_s[...] = wg_ref[...].astype(BF16)
        wu_s[...] = wu_ref[...].astype(BF16)
        wd_s[...] = wd_ref[...].astype(BF16)

    @pl.when(active)
    def _():
        x = _from_slab(xs_ref, tb).astype(BF16)
        gate = _dot(x, wg_s[...])
        up = _dot(x, wu_s[...])
        hid = (gate * jax.nn.sigmoid(gate) * up).astype(BF16)
        _to_slab(ys_ref, _dot(hid, wd_s[...]))

    @pl.when(jnp.logical_not(active))
    def _():
        ys_ref[...] = jnp.zeros_like(ys_ref)


def _experts(blk_expert, nact, xs, wg, wu, wd, layer, tb):
    nblk = blk_expert.shape[0]
    _, _, d, de = wg.shape

    def row_map(i, be, na):
        return (jnp.minimum(i, na[0] - 1), 0)

    def w_map(i, be, na):
        return (layer, be[jnp.minimum(i, na[0] - 1)], 0, 0)

    return pl.pallas_call(
        _expert_kernel,
        grid_spec=pltpu.PrefetchScalarGridSpec(
            num_scalar_prefetch=2,
            grid=(nblk,),
            in_specs=[pl.BlockSpec((tb * SUBLANES, LANES), row_map),
                      pl.BlockSpec((None, None, d, de), w_map), pl.BlockSpec((None, None, d, de), w_map),
                      pl.BlockSpec((None, None, de, d), w_map)],
            out_specs=pl.BlockSpec((tb * SUBLANES, LANES), lambda i, be, na: (i, 0)),
            scratch_shapes=[pltpu.VMEM((d, de), BF16), pltpu.VMEM((d, de), BF16), pltpu.VMEM((de, d), BF16)]),
        out_shape=jax.ShapeDtypeStruct(xs.shape, F32),
        compiler_params=_params(1),
        name="experts",
    )(blk_expert, nact, xs, wg, wu, wd)


def _combine(dest_ref, ys_ref, meta_ref, buf_ref, sem_ref, tm):
    i = pl.program_id(0)
    n = pl.num_programs(0)
    slot = i % 2

    def copy(tile, sl, r, k):
        d = dest_ref[tile * (TOP_K * tm) + TOP_K * r + k]
        return pltpu.make_async_copy(
            ys_ref.at[pl.ds(pl.multiple_of(d * SUBLANES, SUBLANES), SUBLANES), :],
            buf_ref.at[sl, k, pl.ds(pl.multiple_of(r * SUBLANES, SUBLANES), SUBLANES), :],
            sem_ref.at[sl])

    def start_tile(tile, sl):
        def body(r, c):
            for k in range(TOP_K):
                copy(tile, sl, r, k).start()
            return c
        lax.fori_loop(0, tm, body, 0, unroll=DMA_UNROLL)

    def wait_tile(sl):
        for k in range(TOP_K):
            pltpu.make_async_copy(ys_ref.at[pl.ds(0, tm * SUBLANES), :], buf_ref.at[sl, k], sem_ref.at[sl]).wait()

    @pl.when(i == 0)
    def _():
        start_tile(0, 0)

    @pl.when(i + 1 < n)
    def _():
        start_tile(i + 1, 1 - slot)

    wait_tile(slot)
    y = None
    for k in range(TOP_K):
        gate = meta_ref[:, 2 * TOP_K + k:2 * TOP_K + k + 1]
        term = _from_slab(buf_ref.at[slot, k], tm) * gate
        y = term if y is None else y + term
    return y


def _combine_scratch(tm):
    return [pltpu.VMEM((2, TOP_K, tm * SUBLANES, LANES), F32), pltpu.SemaphoreType.DMA((2,))]


def _comb_proj1_kernel(dest_ref, h_ref, meta_ref, ys_ref, g_ref, w_ref, lbl_ref,
                       h2_ref, qs_ref, lf_ref, iv_ref, gs_ref, buf_ref, sem_ref, *, lb_rows):
    tm = h_ref.shape[0]
    h2 = h_ref[...] + _combine(dest_ref, ys_ref, meta_ref, buf_ref, sem_ref, tm)
    h2_ref[...] = h2
    z = _dot(_rms(h2, g_ref[...]).astype(BF16), w_ref[...])
    d = h2.shape[1]
    q, fz, iv, gz = (z[:, d * c:d * (c + 1)] for c in range(4))
    lbl = lbl_ref[...]
    ex = jnp.exp(lbl - jnp.max(lbl, axis=0, keepdims=True))
    lb = jnp.sum(ex[:lb_rows], axis=0, keepdims=True) / jnp.sum(ex, axis=0, keepdims=True)
    f = lb + (1.0 - lb) * jax.nn.sigmoid(fz)
    qs_ref[...] = (q * jax.nn.sigmoid(q)).astype(BF16)
    lf_ref[...] = jnp.log(f)
    iv_ref[...] = iv.astype(BF16)
    gs_ref[...] = (gz * jax.nn.sigmoid(gz)).astype(BF16)


def _comb_proj1(dest, h, meta, ys, g, w, lb_logits, lb_rows):
    t, d = h.shape
    tm = min(TM_COMB, t)
    row = lambda i, dd: (i, 0)
    const = lambda shape: pl.BlockSpec(shape, lambda i, dd: (0,) * len(shape))
    return pl.pallas_call(
        functools.partial(_comb_proj1_kernel, lb_rows=lb_rows),
        grid_spec=pltpu.PrefetchScalarGridSpec(
            num_scalar_prefetch=1,
            grid=(t // tm,),
            in_specs=[pl.BlockSpec((tm, d), row), pl.BlockSpec((tm, LANES), row),
                      pl.BlockSpec(memory_space=pl.ANY),
                      const(g.shape), const(w.shape), const(lb_logits.shape)],
            out_specs=[pl.BlockSpec((tm, d), row)] * 5,
            scratch_shapes=_combine_scratch(tm)),
        out_shape=[jax.ShapeDtypeStruct((t, d), F32), jax.ShapeDtypeStruct((t, d), BF16),
                   jax.ShapeDtypeStruct((t, d), F32), jax.ShapeDtypeStruct((t, d), BF16),
                   jax.ShapeDtypeStruct((t, d), BF16)],
        compiler_params=_params(1),
        name="comb_proj1",
    )(dest, h, meta, ys, g, w, lb_logits)


def _comb_final_kernel(dest_ref, h_ref, meta_ref, ys_ref, g_ref, o_ref, buf_ref, sem_ref):
    tm = h_ref.shape[0]
    h = h_ref[...] + _combine(dest_ref, ys_ref, meta_ref, buf_ref, sem_ref, tm)
    o_ref[...] = _rms(h, g_ref[...])


def _comb_final(dest, h, meta, ys, g):
    t, d = h.shape
    tm = min(TM_COMB, t)
    row = lambda i, dd: (i, 0)
    return pl.pallas_call(
        _comb_final_kernel,
        grid_spec=pltpu.PrefetchScalarGridSpec(
            num_scalar_prefetch=1,
            grid=(t // tm,),
            in_specs=[pl.BlockSpec((tm, d), row), pl.BlockSpec((tm, LANES), row),
                      pl.BlockSpec(memory_space=pl.ANY),
                      pl.BlockSpec(g.shape, lambda i, dd: (0, 0))],
            out_specs=pl.BlockSpec((tm, d), row),
            scratch_shapes=_combine_scratch(tm)),
        out_shape=jax.ShapeDtypeStruct((t, d), F32),
        compiler_params=_params(1),
        name="comb_final",
    )(dest, h, meta, ys, g)


def _hgrn_kernel(qs_ref, lf_ref, iv_ref, gs_ref, ng_ref, o_ref, st_ref, d1_s, t16_s, *, dk):
    th = qs_ref.shape[0]
    nsub = CHUNK // SUB

    @pl.when(pl.program_id(1) == 0)
    def _():
        st_ref[...] = jnp.zeros_like(st_ref)

    rr = lax.broadcasted_iota(jnp.int32, (th, th), 0)
    cc = lax.broadcasted_iota(jnp.int32, (th, th), 1)
    same_sub = (rr // SUB) == (cc // SUB)
    same_chunk = (rr // CHUNK) == (cc // CHUNK)
    m_diag = same_sub & (cc <= rr)
    dsub = rr // SUB - cc // SUB
    m_off = [same_chunk & (dsub == dd) for dd in range(1, nsub)]

    hi, lo = _split_bf16(lf_ref[...])
    tri = jnp.where(m_diag, 1.0, 0.0).astype(BF16)
    ones = jnp.where(same_sub, 1.0, 0.0).astype(BF16)
    d1_s[...] = _dot(tri, hi) + _dot(tri, lo)
    t16_s[...] = _dot(ones, hi) + _dot(ones, lo)
    rrow = lax.broadcasted_iota(jnp.int32, (th, dk), 0)
    rsub = (rrow // SUB) % nsub
    rchunk = rrow // CHUNK
    nch = th // CHUNK

    def head(hd, carry):
        ls = pl.ds(pl.multiple_of(hd * dk, dk), dk)
        d1 = d1_s[:, ls]
        t16 = t16_s[:, ls]
        q = qs_ref[:, ls].astype(F32)
        v = iv_ref[:, ls]
        kt = 1.0 - jnp.exp(lf_ref[:, ls])
        suf = t16 - d1
        kx = (kt * jnp.exp(suf)).astype(BF16)
        qv = [q * jnp.exp(-suf)]
        acc = d1
        tail = suf
        for dd in range(1, nsub):
            qv.append(q * jnp.exp(acc))
            acc = acc + jnp.where(rsub >= dd, pltpu.roll(t16, SUB * dd, 0), 0.0)
            tail = tail + jnp.where(rsub < nsub - dd, pltpu.roll(t16, th - SUB * dd, 0), 0.0)
        b = acc
        qb = q * jnp.exp(b)
        kend = kt * jnp.exp(tail)

        a4 = _dot_nt(jnp.concatenate(qv, axis=0).astype(BF16), kx)
        att = jnp.where(m_diag, a4[:th], 0.0)
        for dd in range(1, nsub):
            att = jnp.where(m_off[dd - 1], a4[dd * th:(dd + 1) * th], att)
        o_intra = _dot(att.astype(BF16), v)

        zero = jnp.zeros_like(q)
        kend_x = jnp.concatenate([jnp.where(rchunk == c, kend, zero) for c in range(nch)], axis=1).astype(BF16)
        qb_x = jnp.concatenate([jnp.where(rchunk == c, qb, zero) for c in range(nch)], axis=1).astype(BF16)
        inc = _dot_tn(v, kend_x)
        st = st_ref[hd]
        starts = []
        for c in range(nch):
            starts.append(st)
            dec = jnp.exp(b[CHUNK * (c + 1) - 1:CHUNK * (c + 1), :])
            st = st * dec + inc[:, dk * c:dk * (c + 1)]
        st_ref[hd] = st
        oh = o_intra + _dot_nt(qb_x, jnp.concatenate(starts, axis=1).astype(BF16))
        on = oh * lax.rsqrt(jnp.mean(oh * oh, axis=-1, keepdims=True) + EPS)
        o_ref[:, ls] = (on * ng_ref[:, ls] * gs_ref[:, ls].astype(F32)).astype(BF16)
        return carry

    lax.fori_loop(0, qs_ref.shape[1] // dk, head, 0, unroll=4)


def _hgrn(qs, lf, iv, gs, ng, batch, dk):
    t, d = qs.shape
    s = t // batch
    th = min(TH, s)
    blk = pl.BlockSpec((None, th, d), lambda b, j: (b, j, 0))
    r3 = lambda a: a.reshape(batch, s, d)
    out = pl.pallas_call(
        functools.partial(_hgrn_kernel, dk=dk),
        grid=(batch, s // th),
        in_specs=[blk, blk, blk, blk, _const_spec(ng.shape)],
        out_specs=blk,
        out_shape=jax.ShapeDtypeStruct((batch, s, d), BF16),
        scratch_shapes=[pltpu.VMEM((d // dk, dk, dk), F32), pltpu.VMEM((th, d), F32), pltpu.VMEM((th, d), F32)],
        compiler_params=_params(2),
        name="hgrn2",
    )(r3(qs), r3(lf), r3(iv), r3(gs), ng)
    return out.reshape(t, d)


def _block_diag(w):
    n, c, dd = w.shape
    eye = jnp.eye(n, dtype=w.dtype)
    return (eye[:, None, :, None] * w[:, :, None, :]).reshape(n * c, n * dd)


def _router_weights(wg, we):
    d = wg.shape[0]
    w = jnp.concatenate([wg, we, jnp.zeros((d, LANES - wg.shape[1] - we.shape[1]), F32)], axis=1)
    hi = w.astype(BF16)
    return hi, (w - hi.astype(F32)).astype(BF16)


def _moe(layer, meta, cnt, hn_slab, moe_w_gate, moe_w_up, moe_w_down):
    dest, blk_expert, nact, nblk = _plan(meta, cnt, TB_EXPERT)
    xs = _dispatch(dest, hn_slab, nblk, TB_EXPERT)
    ys = _experts(blk_expert, nact, xs, moe_w_gate, moe_w_up, moe_w_down, layer, TB_EXPERT)
    return dest, ys


def kernel(x, norm_mix_g, norm_ffn_g, norm_final_g, ab_w_in, ab_conv_w, ab_conv_b, rg_w_a, rg_b_a, rg_w_x, rg_b_x, rg_lambda, attn_rel_bias, ab_w_out, c_w_in, c_lb_logits, c_norm_g, c_w_out, moe_router_group, moe_router_expert, moe_w_gate, moe_w_up, moe_w_down):
    batch, seq, d = x.shape
    t = batch * seq
    xt = x.reshape(t, d)
    row = lambda v: v.reshape(1, -1)

    aw = ab_conv_w.shape[2]
    xg, qkv = _ln_proj0(xt, row(norm_mix_g[0]), ab_w_in[0].astype(BF16), 2 * aw)
    ya = _rglru(xg, ab_conv_w[0], row(ab_conv_b[0]),
                _block_diag(rg_w_a[0]).astype(BF16), _block_diag(rg_w_x[0]).astype(BF16),
                row(rg_b_a[0]), row(rg_b_x[0]), row(rg_lambda[0]), batch)
    tk = TQ + LEFT_CHUNKS * CHUNK
    dist = jnp.clip(tk - jnp.arange(TQ + tk), -REL_CLIP, REL_CLIP) + REL_CLIP
    bias = _attn_bias(attn_rel_bias[0][:, None, dist])
    yb = _attention(qkv, bias, batch)
    wo = ab_w_out[0].astype(BF16)
    wrh, wrl = _router_weights(moe_router_group[0], moe_router_expert[0])
    h1, hn1, meta1, cnt1 = _proj_route(ya, 0, yb, 0, xt, wo[:aw], wo[aw:], row(norm_ffn_g[0]), wrh, wrl)
    dest1, ys1 = _moe(0, meta1, cnt1, hn1, moe_w_gate, moe_w_up, moe_w_down)

    dk = c_norm_g.shape[1] // 8
    h2, qs, lf, iv, gs = _comb_proj1(dest1, h1, meta1, ys1, row(norm_mix_g[1]), c_w_in[0].astype(BF16),
                                     c_lb_logits, 1)
    om = _hgrn(qs, lf, iv, gs, row(c_norm_g[0]), batch, dk)
    wo = c_w_out[0].astype(BF16)
    half = wo.shape[0] // 2
    wrh, wrl = _router_weights(moe_router_group[1], moe_router_expert[1])
    h3, hn3, meta3, cnt3 = _proj_route(om, 0, om, 1, h2, wo[:half], wo[half:], row(norm_ffn_g[1]), wrh, wrl)
    dest3, ys3 = _moe(1, meta3, cnt3, hn3, moe_w_gate, moe_w_up, moe_w_down)

    out = _comb_final(dest3, h3, meta3, ys3, row(norm_final_g))
    return out.reshape(batch, seq, d)
```

```python
import functools

import jax
import jax.numpy as jnp
from jax import lax
from jax.experimental import pallas as pl
from jax.experimental.pallas import tpu as pltpu

F32 = jnp.float32
BF16 = jnp.bfloat16

EPS = 1e-6
LOG2E = 1.4426950408889634
RG_C = 8.0
CHUNK = 64
LEFT_CHUNKS = 8
REL_CLIP = 256
N_GROUPS = 4
EXPERTS_PER_GROUP = 8
N_EXPERTS = N_GROUPS * EXPERTS_PER_GROUP
TOP_K = 2

LANES = 128
SUBLANES = 8
SUB = 16

TM_PROJ = 512
TS_LRU = 512
TQ = 256
TM_ROUTE = 256
TB_EXPERT = 512
TM_COMB = 256
TH = 256
DMA_UNROLL = 8
VMEM_MB = 48


def _params(n_axes, vmem_mb=VMEM_MB):
    return pltpu.CompilerParams(dimension_semantics=("arbitrary",) * n_axes,
                                vmem_limit_bytes=vmem_mb * 1024 * 1024)


def _const_spec(shape):
    nd = len(shape)
    return pl.BlockSpec(shape, lambda *_: (0,) * nd)


def _rms(x, g):
    return x * lax.rsqrt(jnp.mean(x * x, axis=-1, keepdims=True) + EPS) * g


def _dot(a, b):
    return jnp.dot(a, b, preferred_element_type=F32)


def _dot_nt(a, b):
    return lax.dot_general(a, b, (((1,), (1,)), ((), ())), preferred_element_type=F32)


def _dot_tn(a, b):
    return lax.dot_general(a, b, (((0,), (0,)), ((), ())), preferred_element_type=F32)


def _split_bf16(x):
    hi = x.astype(BF16)
    lo = (x - hi.astype(F32)).astype(BF16)
    return hi, lo


def _to_slab(ref, val):
    m = val.shape[0]
    for j in range(val.shape[1] // LANES):
        ref[pl.ds(j, m, stride=SUBLANES), :] = val[:, LANES * j:LANES * (j + 1)]


def _from_slab(ref, m):
    return jnp.concatenate([ref[pl.ds(j, m, stride=SUBLANES), :] for j in range(SUBLANES)], axis=1)


def _ln_proj0_kernel(x_ref, g_ref, w_ref, xg_ref, qkv_ref):
    hn = _rms(x_ref[...], g_ref[...]).astype(BF16)
    z = _dot(hn, w_ref[...])
    aw = xg_ref.shape[1]
    bw = (z.shape[1] - aw) // 3
    xg_ref[...] = z[:, :aw]
    qkv_ref[:, :bw] = (z[:, aw:aw + bw] * (CHUNK ** -0.5 * LOG2E)).astype(BF16)
    qkv_ref[:, bw:] = z[:, aw + bw:].astype(BF16)


def _ln_proj0(x, g, w, aw):
    t, d = x.shape
    n = w.shape[1]
    tm = min(TM_PROJ, t)
    return pl.pallas_call(
        _ln_proj0_kernel,
        grid=(t // tm,),
        in_specs=[pl.BlockSpec((tm, d), lambda i: (i, 0)), _const_spec(g.shape), _const_spec(w.shape)],
        out_specs=[pl.BlockSpec((tm, aw), lambda i: (i, 0)), pl.BlockSpec((tm, n - aw), lambda i: (i, 0))],
        out_shape=[jax.ShapeDtypeStruct((t, aw), F32), jax.ShapeDtypeStruct((t, n - aw), BF16)],
        compiler_params=_params(1),
        name="ln_proj0",
    )(x, g, w)


def _rglru_kernel(xg_ref, cw_ref, cb_ref, wa_ref, wx_ref, ba_ref, bx_ref, lam_ref, ya_ref,
                  halo_ref, h_ref, a_s, u_s):
    ts = xg_ref.shape[0]
    w = ya_ref.shape[1]

    @pl.when(pl.program_id(1) == 0)
    def _():
        halo_ref[...] = jnp.zeros_like(halo_ref)
        h_ref[...] = jnp.zeros_like(h_ref)

    xa = xg_ref[:, :w]
    halo = halo_ref[...]
    row8 = lax.broadcasted_iota(jnp.int32, (SUBLANES, w), 0)
    nk = cw_ref.shape[0]
    y = cb_ref[...] + cw_ref[nk - 1:nk, :] * xa
    for k in range(1, nk):
        r = pltpu.roll(xa, k, 0)
        top = jnp.where(row8 < k, pltpu.roll(halo, k, 0), r[:SUBLANES, :])
        y = y + cw_ref[nk - 1 - k:nk - k, :] * jnp.concatenate([top, r[SUBLANES:, :]], axis=0)
    halo_ref[...] = xa[ts - SUBLANES:, :]

    yb = y.astype(BF16)
    rg = jax.nn.sigmoid(_dot(yb, wa_ref[...]) + ba_ref[...])
    ig = jax.nn.sigmoid(_dot(yb, wx_ref[...]) + bx_ref[...])
    lam = lam_ref[...]
    log_sig = jnp.minimum(lam, 0.0) - jnp.log1p(jnp.exp(-jnp.abs(lam)))
    log_a = RG_C * rg * log_sig
    a = jnp.exp(log_a)
    u = jnp.sqrt(-jnp.tanh(log_a) * (a * a + 1.0)) * (ig * y)

    rowm = lax.broadcasted_iota(jnp.int32, (ts, w), 0) & (SUBLANES - 1)
    for s in (1, 2, 4):
        keep = rowm >= s
        a_sh = jnp.where(keep, pltpu.roll(a, s, 0), 1.0)
        u_sh = jnp.where(keep, pltpu.roll(u, s, 0), 0.0)
        u = a * u_sh + u
        a = a * a_sh
    a_s[...] = a
    u_s[...] = u

    def group(gi, h):
        off = pl.multiple_of(gi * SUBLANES, SUBLANES)
        hg = a_s[pl.ds(off, SUBLANES), :] * h + u_s[pl.ds(off, SUBLANES), :]
        u_s[pl.ds(off, SUBLANES), :] = hg
        return jnp.broadcast_to(hg[SUBLANES - 1:SUBLANES, :], hg.shape)

    h_ref[...] = lax.fori_loop(0, ts // SUBLANES, group, h_ref[...])

    ga = xg_ref[:, w:]
    gelu = 0.5 * ga * (1.0 + jnp.tanh(0.7978845608028654 * (ga + 0.044715 * (ga * ga * ga))))
    ya_ref[...] = (u_s[...] * gelu).astype(BF16)


def _rglru(xg, cw, cb, wa, wx, ba, bx, lam, batch):
    t, w2 = xg.shape
    w = w2 // 2
    s = t // batch
    ts = min(TS_LRU, s)
    xg3 = xg.reshape(batch, s, w2)
    small = [cw, cb, wa, wx, ba, bx, lam]
    out = pl.pallas_call(
        _rglru_kernel,
        grid=(batch, s // ts),
        in_specs=[pl.BlockSpec((None, ts, w2), lambda b, j: (b, j, 0))] + [_const_spec(a.shape) for a in small],
        out_specs=pl.BlockSpec((None, ts, w), lambda b, j: (b, j, 0)),
        out_shape=jax.ShapeDtypeStruct((batch, s, w), BF16),
        scratch_shapes=[pltpu.VMEM((SUBLANES, w), F32), pltpu.VMEM((SUBLANES, w), F32),
                        pltpu.VMEM((ts, w), F32), pltpu.VMEM((ts, w), F32)],
        compiler_params=_params(2),
        name="rglru",
    )(xg3, *small)
    return out.reshape(t, w)


def _attn_bias_kernel(base_ref, o_ref):
    tq, tk = o_ref.shape
    nkb = tk // tq
    row = base_ref[...]
    full = pltpu.roll(jnp.broadcast_to(row, (tq, tq + tk)), 0, 1, stride=1, stride_axis=0)
    qc = lax.broadcasted_iota(jnp.int32, (tq, tk), 0) // CHUNK
    col = lax.broadcasted_iota(jnp.int32, (tq, tk), 1)
    kc = col // CHUNK
    first_ok = (nkb - 1 - pl.program_id(0)) * tq
    valid = (kc >= qc) & (kc <= qc + LEFT_CHUNKS) & (col >= first_ok)
    o_ref[...] = jnp.where(valid, full[:, tq:] * LOG2E, -1e30)


def _attn_bias(base):
    h = base.shape[0]
    tk = TQ + LEFT_CHUNKS * CHUNK
    return pl.pallas_call(
        _attn_bias_kernel,
        grid=(tk // TQ, h),
        in_specs=[pl.BlockSpec((None, 1, TQ + tk), lambda v, i: (i, 0, 0))],
        out_specs=pl.BlockSpec((None, None, TQ, tk), lambda v, i: (v, i, 0, 0)),
        out_shape=jax.ShapeDtypeStruct((tk // TQ, h, TQ, tk), F32),
        compiler_params=_params(2),
        name="attn_bias",
    )(base)


def _attn_kernel(q_ref, k0_ref, k1_ref, k2_ref, v0_ref, v1_ref, v2_ref, bias_ref, o_ref):
    tq = q_ref.shape[0]
    half_w = LANES // 2
    lo = lax.broadcasted_iota(jnp.int32, (tq, LANES), 1) < half_w
    lo_k = lax.broadcasted_iota(jnp.int32, (3 * tq, LANES), 1) < half_w
    k_refs = (k0_ref, k1_ref, k2_ref)
    v_refs = (v0_ref, v1_ref, v2_ref)
    for j in range(q_ref.shape[1] // LANES):
        sl = slice(LANES * j, LANES * (j + 1))
        q2 = q_ref[:, sl]
        kcat = jnp.concatenate([r[:, sl] for r in k_refs], axis=0)
        vcat = jnp.concatenate([r[:, sl] for r in v_refs], axis=0)
        one = jnp.ones_like(vcat)
        v_half = (jnp.where(lo_k, vcat, one), jnp.where(lo_k, one, vcat))
        pv = []
        for half in range(2):
            qm = jnp.where(lo if half == 0 else jnp.logical_not(lo), q2, jnp.zeros_like(q2))
            s = _dot_nt(qm, kcat) + bias_ref[2 * j + half]
            p = jnp.exp2(s - jnp.max(s, axis=-1, keepdims=True))
            pv.append(_dot(p.astype(BF16), v_half[half]))
        num = jnp.where(lo, pv[0], pv[1])
        den = pltpu.roll(jnp.where(lo, pv[1], pv[0]), half_w, 1)
        o_ref[:, sl] = (num / den).astype(BF16)


def _attention(qkv, bias, batch):
    t, w3 = qkv.shape
    w = w3 // 3
    s = t // batch
    qkv3 = qkv.reshape(batch, s, w3)
    nvar = bias.shape[0]

    def kv_spec(colblk, back):
        return pl.BlockSpec((None, TQ, w), lambda i, b: (b, jnp.maximum(i - back, 0), colblk))

    out = pl.pallas_call(
        _attn_kernel,
        grid=(s // TQ, batch),
        in_specs=[pl.BlockSpec((None, TQ, w), lambda i, b: (b, i, 0)),
                  kv_spec(1, 2), kv_spec(1, 1), kv_spec(1, 0),
                  kv_spec(2, 2), kv_spec(2, 1), kv_spec(2, 0),
                  pl.BlockSpec((None,) + bias.shape[1:], lambda i, b: (jnp.minimum(i, nvar - 1), 0, 0, 0))],
        out_specs=pl.BlockSpec((None, TQ, w), lambda i, b: (b, i, 0)),
        out_shape=jax.ShapeDtypeStruct((batch, s, w), BF16),
        compiler_params=_params(2),
        name="attention",
    )(qkv3, qkv3, qkv3, qkv3, qkv3, qkv3, qkv3, bias)
    return out.reshape(t, w)


def _proj_route_kernel(a_ref, b_ref, r_ref, wa_ref, wb_ref, g_ref, wrh_ref, wrl_ref,
                       h_ref, hn_ref, meta_ref, cnt_ref, carry_ref):
    tm = a_ref.shape[0]

    @pl.when(pl.program_id(0) == 0)
    def _():
        carry_ref[...] = jnp.zeros_like(carry_ref)

    h = r_ref[...] + (_dot(a_ref[...], wa_ref[...]) + _dot(b_ref[...], wb_ref[...]))
    h_ref[...] = h
    hn = _rms(h, g_ref[...])
    _to_slab(hn_ref, hn)

    hi, lo = _split_bf16(hn)
    wrh = wrh_ref[...]
    logits = _dot(hi, wrh) + (_dot(hi, wrl_ref[...]) + _dot(lo, wrh))
    lane = lax.broadcasted_iota(jnp.int32, (tm, LANES), 1).astype(F32)
    neg = -jnp.inf
    far = float(LANES)

    def first_max(mask):
        v = jnp.max(jnp.where(mask, logits, neg), axis=-1, keepdims=True)
        idx = jnp.min(jnp.where(mask & (logits == v), lane, far), axis=-1, keepdims=True)
        return v, idx

    gmask = lane < N_GROUPS
    gmax, gidx = first_max(gmask)
    g_gate = 1.0 / jnp.sum(jnp.where(gmask, jnp.exp(logits - gmax), 0.0), axis=-1, keepdims=True)
    e_lo = N_GROUPS + EXPERTS_PER_GROUP * gidx
    emask = (lane >= e_lo) & (lane < e_lo + EXPERTS_PER_GROUP)
    v1, i1 = first_max(emask)
    v2, i2 = first_max(emask & (lane != i1))
    tt = jnp.exp(v2 - v1)
    w1 = g_gate / (1.0 + tt)
    w2 = g_gate * tt / (1.0 + tt)

    sel1 = lane == i1
    sel2 = lane == i2
    onehot = jnp.where(sel1 | sel2, 1.0, 0.0)
    rr = lax.broadcasted_iota(jnp.int32, (tm, tm), 0)
    cc = lax.broadcasted_iota(jnp.int32, (tm, tm), 1)
    ltri = jnp.where(cc < rr, 1.0, 0.0).astype(BF16)
    carry = carry_ref[0:1, :]
    before = _dot(ltri, onehot.astype(BF16)) + carry
    rank1 = jnp.sum(jnp.where(sel1, before, 0.0), axis=-1, keepdims=True)
    rank2 = jnp.sum(jnp.where(sel2, before, 0.0), axis=-1, keepdims=True)
    carry = carry + jnp.sum(onehot, axis=0, keepdims=True)
    carry_ref[...] = jnp.broadcast_to(carry, carry_ref.shape)
    cnt_ref[...] = jnp.broadcast_to(carry, cnt_ref.shape)

    cols = (i1 - N_GROUPS, i2 - N_GROUPS, rank1, rank2, w1, w2)
    meta = jnp.zeros((tm, LANES), F32)
    for c, val in enumerate(cols):
        meta = jnp.where(lane == c, val, meta)
    meta_ref[...] = meta


def _proj_route(a, acol, b, bcol, resid, wa, wb, g, wrh, wrl):
    t, d = resid.shape
    kw = wa.shape[0]
    tm = min(TM_ROUTE, t)
    return pl.pallas_call(
        _proj_route_kernel,
        grid=(t // tm,),
        in_specs=[pl.BlockSpec((tm, kw), lambda i: (i, acol)), pl.BlockSpec((tm, kw), lambda i: (i, bcol)),
                  pl.BlockSpec((tm, d), lambda i: (i, 0)),
                  _const_spec(wa.shape), _const_spec(wb.shape), _const_spec(g.shape),
                  _const_spec(wrh.shape), _const_spec(wrl.shape)],
        out_specs=[pl.BlockSpec((tm, d), lambda i: (i, 0)),
                   pl.BlockSpec((tm * SUBLANES, LANES), lambda i: (i, 0)),
                   pl.BlockSpec((tm, LANES), lambda i: (i, 0)),
                   pl.BlockSpec((SUBLANES, LANES), lambda i: (0, 0))],
        out_shape=[jax.ShapeDtypeStruct((t, d), F32),
                   jax.ShapeDtypeStruct((t * SUBLANES, LANES), F32),
                   jax.ShapeDtypeStruct((t, LANES), F32),
                   jax.ShapeDtypeStruct((SUBLANES, LANES), F32)],
        scratch_shapes=[pltpu.VMEM((SUBLANES, LANES), F32)],
        compiler_params=_params(1),
        name="proj_route",
    )(a, b, resid, wa, wb, g, wrh, wrl)


def _plan(meta, cnt, tb):
    t = meta.shape[0]
    counts = cnt[0, N_GROUPS:N_GROUPS + N_EXPERTS].astype(jnp.int32)
    padded = (counts + tb - 1) // tb * tb
    pad_end = jnp.cumsum(padded)
    pad_start = pad_end - padded
    dest = []
    for k in range(TOP_K):
        expert = meta[:, k].astype(jnp.int32)
        start = jnp.zeros((t,), jnp.int32)
        for e in range(N_EXPERTS):
            start = jnp.where(expert == e, pad_start[e], start)
        dest.append(start + meta[:, TOP_K + k].astype(jnp.int32))
    dest = jnp.concatenate(dest)
    nblk = -(-(t * TOP_K + N_EXPERTS * (tb - 1)) // tb)
    blk_start = jnp.arange(nblk, dtype=jnp.int32) * tb
    blk_expert = jnp.minimum(jnp.sum((pad_end[None, :] <= blk_start[:, None]).astype(jnp.int32), axis=1),
                             N_EXPERTS - 1)
    nact = (pad_end[-1:] // tb).astype(jnp.int32)
    return dest, pad_start + counts, pad_end, blk_expert, nact, nblk


def _dispatch_kernel(dest_ref, plo_ref, phi_ref, nact_ref, hn_ref, xs_ref, zero_ref, sem, zsem):
    tm = hn_ref.shape[0] // SUBLANES
    t = dest_ref.shape[0] // TOP_K
    tb = zero_ref.shape[0] // SUBLANES
    nblk = xs_ref.shape[0] // zero_ref.shape[0]
    i = pl.program_id(0)

    def zero_copy(row, nrows):
        return pltpu.make_async_copy(
            zero_ref.at[pl.ds(0, nrows * SUBLANES), :],
            xs_ref.at[pl.ds(pl.multiple_of(row * SUBLANES, SUBLANES), nrows * SUBLANES), :], zsem)

    def zero_fill(start):
        def go(cp):
            cp.start() if start else cp.wait()

        def per_expert(e, c):
            off = plo_ref[e]
            n = phi_ref[e] - off
            bit = tb // 2
            while bit:
                pl.when((n & bit) != 0)(functools.partial(lambda o, b: go(zero_copy(o, b)), off, bit))
                off = off + (n & bit)
                bit //= 2
            return c

        def per_block(b, c):
            go(zero_copy(b * tb, tb))
            return c

        lax.fori_loop(0, N_EXPERTS, per_expert, 0)
        lax.fori_loop(nact_ref[0], nblk, per_block, 0)

    @pl.when(i == 0)
    def _():
        zero_ref[...] = jnp.zeros_like(zero_ref)
        zero_fill(True)

    def copy(r, k):
        d = dest_ref[k * t + i * tm + r]
        return pltpu.make_async_copy(
            hn_ref.at[pl.ds(pl.multiple_of(r * SUBLANES, SUBLANES), SUBLANES), :],
            xs_ref.at[pl.ds(pl.multiple_of(d * SUBLANES, SUBLANES), SUBLANES), :], sem)

    def start(r, c):
        for k in range(TOP_K):
            copy(r, k).start(priority=k)
        return c

    lax.fori_loop(0, tm, start, 0, unroll=DMA_UNROLL)
    for k in range(TOP_K):
        pltpu.make_async_copy(hn_ref, xs_ref.at[pl.ds(0, tm * SUBLANES), :], sem).wait()

    @pl.when(i == 0)
    def _():
        zero_fill(False)


def _dispatch(dest, pad_lo, pad_hi, nact, hn_slab, nblk, tb):
    t = hn_slab.shape[0] // SUBLANES
    tm = min(TM_COMB, t)
    return pl.pallas_call(
        _dispatch_kernel,
        grid_spec=pltpu.PrefetchScalarGridSpec(
            num_scalar_prefetch=4,
            grid=(t // tm,),
            in_specs=[pl.BlockSpec((tm * SUBLANES, LANES), lambda i, *_: (i, 0))],
            out_specs=pl.BlockSpec(memory_space=pl.ANY),
            scratch_shapes=[pltpu.VMEM((tb * SUBLANES, LANES), F32),
                            pltpu.SemaphoreType.DMA(()), pltpu.SemaphoreType.DMA(())]),
        out_shape=jax.ShapeDtypeStruct((nblk * tb * SUBLANES, LANES), F32),
        compiler_params=_params(1),
        name="dispatch",
    )(dest, pad_lo, pad_hi, nact, hn_slab)


def _expert_kernel(be_ref, nact_ref, xs_ref, wg_ref, wu_ref, wd_ref, ys_ref, wg_s, wu_s, wd_s):
    i = pl.program_id(0)
    tb = xs_ref.shape[0] // SUBLANES
    active = i < nact_ref[0]
    changed = (i == 0) | (be_ref[i] != be_ref[jnp.maximum(i - 1, 0)])

    @pl.when(active & changed)
    def _():
        wg_s[...] = wg_ref[...].astype(BF16)
        wu_s[...] = wu_ref[...].astype(BF16)
        wd_s[...] = wd_ref[...].astype(BF16)

    @pl.when(active)
    def _():
        x = _from_slab(xs_ref, tb).astype(BF16)
        gate = _dot(x, wg_s[...])
        up = _dot(x, wu_s[...])
        hid = (gate * jax.nn.sigmoid(gate) * up).astype(BF16)
        _to_slab(ys_ref, _dot(hid, wd_s[...]))

    @pl.when(jnp.logical_not(active))
    def _():
        ys_ref[...] = jnp.zeros_like(ys_ref)


def _experts(blk_expert, nact, xs, wg, wu, wd, layer, tb):
    nblk = blk_expert.shape[0]
    _, _, d, de = wg.shape

    def row_map(i, be, na):
        return (jnp.minimum(i, na[0] - 1), 0)

    def w_map(i, be, na):
        return (layer, be[jnp.minimum(i, na[0] - 1)], 0, 0)

    return pl.pallas_call(
        _expert_kernel,
        grid_spec=pltpu.PrefetchScalarGridSpec(
            num_scalar_prefetch=2,
            grid=(nblk,),
            in_specs=[pl.BlockSpec((tb * SUBLANES, LANES), row_map),
                      pl.BlockSpec((None, None, d, de), w_map), pl.BlockSpec((None, None, d, de), w_map),
                      pl.BlockSpec((None, None, de, d), w_map)],
            out_specs=pl.BlockSpec((tb * SUBLANES, LANES), lambda i, be, na: (i, 0)),
            scratch_shapes=[pltpu.VMEM((d, de), BF16), pltpu.VMEM((d, de), BF16), pltpu.VMEM((de, d), BF16)]),
        out_shape=jax.ShapeDtypeStruct(xs.shape, F32),
        compiler_params=_params(1),
        name="experts",
    )(blk_expert, nact, xs, wg, wu, wd)


def _combine(dest_ref, ys_ref, meta_ref, buf_ref, sem_ref, tm):
    i = pl.program_id(0)
    n = pl.num_programs(0)
    slot = i % 2

    t = dest_ref.shape[0] // TOP_K

    def copy(tile, sl, r, k):
        d = dest_ref[k * t + tile * tm + r]
        return pltpu.make_async_copy(
            ys_ref.at[pl.ds(pl.multiple_of(d * SUBLANES, SUBLANES), SUBLANES), :],
            buf_ref.at[sl, k, pl.ds(pl.multiple_of(r * SUBLANES, SUBLANES), SUBLANES), :],
            sem_ref.at[sl])

    def start_tile(tile, sl):
        def body(r, c):
            for k in range(TOP_K):
                copy(tile, sl, r, k).start(priority=k)
            return c
        lax.fori_loop(0, tm, body, 0, unroll=DMA_UNROLL)

    def wait_tile(sl):
        for k in range(TOP_K):
            pltpu.make_async_copy(ys_ref.at[pl.ds(0, tm * SUBLANES), :], buf_ref.at[sl, k], sem_ref.at[sl]).wait()

    @pl.when(i == 0)
    def _():
        start_tile(0, 0)

    @pl.when(i + 1 < n)
    def _():
        start_tile(i + 1, 1 - slot)

    wait_tile(slot)
    y = None
    for k in range(TOP_K):
        gate = meta_ref[:, 2 * TOP_K + k:2 * TOP_K + k + 1]
        term = _from_slab(buf_ref.at[slot, k], tm) * gate
        y = term if y is None else y + term
    return y


def _combine_scratch(tm):
    return [pltpu.VMEM((2, TOP_K, tm * SUBLANES, LANES), F32), pltpu.SemaphoreType.DMA((2,))]


def _comb_proj1_kernel(dest_ref, h_ref, meta_ref, ys_ref, g_ref, w_ref, lbl_ref,
                       h2_ref, qs_ref, lf_ref, iv_ref, gs_ref, buf_ref, sem_ref, *, lb_rows):
    tm = h_ref.shape[0]
    h2 = h_ref[...] + _combine(dest_ref, ys_ref, meta_ref, buf_ref, sem_ref, tm)
    h2_ref[...] = h2
    z = _dot(_rms(h2, g_ref[...]).astype(BF16), w_ref[...])
    d = h2.shape[1]
    q, fz, iv, gz = (z[:, d * c:d * (c + 1)] for c in range(4))
    lbl = lbl_ref[...]
    ex = jnp.exp(lbl - jnp.max(lbl, axis=0, keepdims=True))
    lb = jnp.sum(ex[:lb_rows], axis=0, keepdims=True) / jnp.sum(ex, axis=0, keepdims=True)
    f = lb + (1.0 - lb) * jax.nn.sigmoid(fz)
    qs_ref[...] = (q * jax.nn.sigmoid(q)).astype(BF16)
    lf_ref[...] = jnp.log(f)
    iv_ref[...] = iv.astype(BF16)
    gs_ref[...] = (gz * jax.nn.sigmoid(gz)).astype(BF16)


def _comb_proj1(dest, h, meta, ys, g, w, lb_logits, lb_rows):
    t, d = h.shape
    tm = min(TM_COMB, t)
    row = lambda i, dd: (i, 0)
    const = lambda shape: pl.BlockSpec(shape, lambda i, dd: (0,) * len(shape))
    return pl.pallas_call(
        functools.partial(_comb_proj1_kernel, lb_rows=lb_rows),
        grid_spec=pltpu.PrefetchScalarGridSpec(
            num_scalar_prefetch=1,
            grid=(t // tm,),
            in_specs=[pl.BlockSpec((tm, d), row), pl.BlockSpec((tm, LANES), row),
                      pl.BlockSpec(memory_space=pl.ANY),
                      const(g.shape), const(w.shape), const(lb_logits.shape)],
            out_specs=[pl.BlockSpec((tm, d), row)] * 5,
            scratch_shapes=_combine_scratch(tm)),
        out_shape=[jax.ShapeDtypeStruct((t, d), F32), jax.ShapeDtypeStruct((t, d), BF16),
                   jax.ShapeDtypeStruct((t, d), F32), jax.ShapeDtypeStruct((t, d), BF16),
                   jax.ShapeDtypeStruct((t, d), BF16)],
        compiler_params=_params(1),
        name="comb_proj1",
    )(dest, h, meta, ys, g, w, lb_logits)


def _comb_final_kernel(dest_ref, h_ref, meta_ref, ys_ref, g_ref, o_ref, buf_ref, sem_ref):
    tm = h_ref.shape[0]
    h = h_ref[...] + _combine(dest_ref, ys_ref, meta_ref, buf_ref, sem_ref, tm)
    o_ref[...] = _rms(h, g_ref[...])


def _comb_final(dest, h, meta, ys, g):
    t, d = h.shape
    tm = min(TM_COMB, t)
    row = lambda i, dd: (i, 0)
    return pl.pallas_call(
        _comb_final_kernel,
        grid_spec=pltpu.PrefetchScalarGridSpec(
            num_scalar_prefetch=1,
            grid=(t // tm,),
            in_specs=[pl.BlockSpec((tm, d), row), pl.BlockSpec((tm, LANES), row),
                      pl.BlockSpec(memory_space=pl.ANY),
                      pl.BlockSpec(g.shape, lambda i, dd: (0, 0))],
            out_specs=pl.BlockSpec((tm, d), row),
            scratch_shapes=_combine_scratch(tm)),
        out_shape=jax.ShapeDtypeStruct((t, d), F32),
        compiler_params=_params(1),
        name="comb_final",
    )(dest, h, meta, ys, g)


def _hgrn_kernel(qs_ref, lf_ref, iv_ref, gs_ref, ng_ref, o_ref, st_ref, d1_s, t16_s, *, dk):
    th = qs_ref.shape[0]
    nsub = CHUNK // SUB

    @pl.when(pl.program_id(1) == 0)
    def _():
        st_ref[...] = jnp.zeros_like(st_ref)

    rr = lax.broadcasted_iota(jnp.int32, (th, th), 0)
    cc = lax.broadcasted_iota(jnp.int32, (th, th), 1)
    same_sub = (rr // SUB) == (cc // SUB)
    same_chunk = (rr // CHUNK) == (cc // CHUNK)
    m_diag = same_sub & (cc <= rr)
    dsub = rr // SUB - cc // SUB
    m_off = [same_chunk & (dsub == dd) for dd in range(1, nsub)]

    hi, lo = _split_bf16(lf_ref[...])
    tri = jnp.where(m_diag, 1.0, 0.0).astype(BF16)
    ones = jnp.where(same_sub, 1.0, 0.0).astype(BF16)
    d1_s[...] = _dot(tri, hi) + _dot(tri, lo)
    t16_s[...] = _dot(ones, hi) + _dot(ones, lo)
    rrow = lax.broadcasted_iota(jnp.int32, (th, dk), 0)
    rsub = (rrow // SUB) % nsub
    rchunk = rrow // CHUNK
    nch = th // CHUNK

    def head(hd, carry):
        ls = pl.ds(pl.multiple_of(hd * dk, dk), dk)
        d1 = d1_s[:, ls]
        t16 = t16_s[:, ls]
        q = qs_ref[:, ls].astype(F32)
        v = iv_ref[:, ls]
        kt = 1.0 - jnp.exp(lf_ref[:, ls])
        suf = t16 - d1
        kx = (kt * jnp.exp(suf)).astype(BF16)
        qv = [q * jnp.exp(-suf)]
        acc = d1
        tail = suf
        for dd in range(1, nsub):
            qv.append(q * jnp.exp(acc))
            acc = acc + jnp.where(rsub >= dd, pltpu.roll(t16, SUB * dd, 0), 0.0)
            tail = tail + jnp.where(rsub < nsub - dd, pltpu.roll(t16, th - SUB * dd, 0), 0.0)
        b = acc
        qb = q * jnp.exp(b)
        kend = kt * jnp.exp(tail)

        a4 = _dot_nt(jnp.concatenate(qv, axis=0).astype(BF16), kx)
        att = jnp.where(m_diag, a4[:th], 0.0)
        for dd in range(1, nsub):
            att = jnp.where(m_off[dd - 1], a4[dd * th:(dd + 1) * th], att)
        o_intra = _dot(att.astype(BF16), v)

        zero = jnp.zeros_like(q)
        kend_x = jnp.concatenate([jnp.where(rchunk == c, kend, zero) for c in range(nch)], axis=1).astype(BF16)
        qb_x = jnp.concatenate([jnp.where(rchunk == c, qb, zero) for c in range(nch)], axis=1).astype(BF16)
        inc = _dot_tn(v, kend_x)
        st = st_ref[hd]
        starts = []
        for c in range(nch):
            starts.append(st)
            dec = jnp.exp(b[CHUNK * (c + 1) - 1:CHUNK * (c + 1), :])
            st = st * dec + inc[:, dk * c:dk * (c + 1)]
        st_ref[hd] = st
        oh = o_intra + _dot_nt(qb_x, jnp.concatenate(starts, axis=1).astype(BF16))
        on = oh * lax.rsqrt(jnp.mean(oh * oh, axis=-1, keepdims=True) + EPS)
        o_ref[:, ls] = (on * ng_ref[:, ls] * gs_ref[:, ls].astype(F32)).astype(BF16)
        return carry

    lax.fori_loop(0, qs_ref.shape[1] // dk, head, 0, unroll=4)


def _hgrn(qs, lf, iv, gs, ng, batch, dk):
    t, d = qs.shape
    s = t // batch
    th = min(TH, s)
    blk = pl.BlockSpec((None, th, d), lambda b, j: (b, j, 0))
    r3 = lambda a: a.reshape(batch, s, d)
    out = pl.pallas_call(
        functools.partial(_hgrn_kernel, dk=dk),
        grid=(batch, s // th),
        in_specs=[blk, blk, blk, blk, _const_spec(ng.shape)],
        out_specs=blk,
        out_shape=jax.ShapeDtypeStruct((batch, s, d), BF16),
        scratch_shapes=[pltpu.VMEM((d // dk, dk, dk), F32), pltpu.VMEM((th, d), F32), pltpu.VMEM((th, d), F32)],
        compiler_params=_params(2),
        name="hgrn2",
    )(r3(qs), r3(lf), r3(iv), r3(gs), ng)
    return out.reshape(t, d)


def _block_diag(w):
    n, c, dd = w.shape
    eye = jnp.eye(n, dtype=w.dtype)
    return (eye[:, None, :, None] * w[:, :, None, :]).reshape(n * c, n * dd)


def _router_weights(wg, we):
    d = wg.shape[0]
    w = jnp.concatenate([wg, we, jnp.zeros((d, LANES - wg.shape[1] - we.shape[1]), F32)], axis=1)
    hi = w.astype(BF16)
    return hi, (w - hi.astype(F32)).astype(BF16)


def _moe(layer, meta, cnt, hn_slab, moe_w_gate, moe_w_up, moe_w_down):
    dest, pad_lo, pad_hi, blk_expert, nact, nblk = _plan(meta, cnt, TB_EXPERT)
    xs = _dispatch(dest, pad_lo, pad_hi, nact, hn_slab, nblk, TB_EXPERT)
    ys = _experts(blk_expert, nact, xs, moe_w_gate, moe_w_up, moe_w_down, layer, TB_EXPERT)
    return dest, ys


def kernel(x, norm_mix_g, norm_ffn_g, norm_final_g, ab_w_in, ab_conv_w, ab_conv_b, rg_w_a, rg_b_a, rg_w_x, rg_b_x, rg_lambda, attn_rel_bias, ab_w_out, c_w_in, c_lb_logits, c_norm_g, c_w_out, moe_router_group, moe_router_expert, moe_w_gate, moe_w_up, moe_w_down):
    batch, seq, d = x.shape
    t = batch * seq
    xt = x.reshape(t, d)
    row = lambda v: v.reshape(1, -1)

    aw = ab_conv_w.shape[2]
    xg, qkv = _ln_proj0(xt, row(norm_mix_g[0]), ab_w_in[0].astype(BF16), 2 * aw)
    ya = _rglru(xg, ab_conv_w[0], row(ab_conv_b[0]),
                _block_diag(rg_w_a[0]).astype(BF16), _block_diag(rg_w_x[0]).astype(BF16),
                row(rg_b_a[0]), row(rg_b_x[0]), row(rg_lambda[0]), batch)
    tk = TQ + LEFT_CHUNKS * CHUNK
    dist = jnp.clip(tk - jnp.arange(TQ + tk), -REL_CLIP, REL_CLIP) + REL_CLIP
    bias = _attn_bias(attn_rel_bias[0][:, None, dist])
    yb = _attention(qkv, bias, batch)
    wo = ab_w_out[0].astype(BF16)
    wrh, wrl = _router_weights(moe_router_group[0], moe_router_expert[0])
    h1, hn1, meta1, cnt1 = _proj_route(ya, 0, yb, 0, xt, wo[:aw], wo[aw:], row(norm_ffn_g[0]), wrh, wrl)
    dest1, ys1 = _moe(0, meta1, cnt1, hn1, moe_w_gate, moe_w_up, moe_w_down)

    dk = c_norm_g.shape[1] // 8
    h2, qs, lf, iv, gs = _comb_proj1(dest1, h1, meta1, ys1, row(norm_mix_g[1]), c_w_in[0].astype(BF16),
                                     c_lb_logits, 1)
    om = _hgrn(qs, lf, iv, gs, row(c_norm_g[0]), batch, dk)
    wo = c_w_out[0].astype(BF16)
    half = wo.shape[0] // 2
    wrh, wrl = _router_weights(moe_router_group[1], moe_router_expert[1])
    h3, hn3, meta3, cnt3 = _proj_route(om, 0, om, 1, h2, wo[:half], wo[half:], row(norm_ffn_g[1]), wrh, wrl)
    dest3, ys3 = _moe(1, meta3, cnt3, hn3, moe_w_gate, moe_w_up, moe_w_down)

    out = _comb_final(dest3, h3, meta3, ys3, row(norm_final_g))
    return out.reshape(batch, seq, d)
```

```python
import functools

import jax
import jax.numpy as jnp
from jax import lax
from jax.experimental import pallas as pl
from jax.experimental.pallas import tpu as pltpu

F32 = jnp.float32
BF16 = jnp.bfloat16

EPS = 1e-6
LOG2E = 1.4426950408889634
RG_C = 8.0
CHUNK = 64
LEFT_CHUNKS = 8
REL_CLIP = 256
N_GROUPS = 4
EXPERTS_PER_GROUP = 8
N_EXPERTS = N_GROUPS * EXPERTS_PER_GROUP
TOP_K = 2
ROUTE_ROWS = 40

LANES = 128
SUBLANES = 8
SUB = 16

TM_PROJ = 512
TS_LRU = 512
TQ = 256
ATTN_GROUP = 4
TM_ROUTE = 1024
TM_ROUTE_SUB = 256
TB_EXPERT = 512
TM_COMB = 256
TH = 256
HGRN_GROUP = 4
DMA_UNROLL = 8
VMEM_MB = 48


def _params(n_axes, vmem_mb=VMEM_MB):
    return pltpu.CompilerParams(dimension_semantics=("arbitrary",) * n_axes,
                                vmem_limit_bytes=vmem_mb * 1024 * 1024)


def _const_spec(shape):
    nd = len(shape)
    return pl.BlockSpec(shape, lambda *_: (0,) * nd)


def _rms(x, g):
    return x * lax.rsqrt(jnp.mean(x * x, axis=-1, keepdims=True) + EPS) * g


def _dot(a, b):
    return jnp.dot(a, b, preferred_element_type=F32)


def _dot_nt(a, b):
    return lax.dot_general(a, b, (((1,), (1,)), ((), ())), preferred_element_type=F32)


def _dot_tn(a, b):
    return lax.dot_general(a, b, (((0,), (0,)), ((), ())), preferred_element_type=F32)


def _split_bf16(x):
    hi = x.astype(BF16)
    lo = (x - hi.astype(F32)).astype(BF16)
    return hi, lo


def _to_slab(ref, val):
    m = val.shape[0]
    for j in range(val.shape[1] // LANES):
        ref[pl.ds(j, m, stride=SUBLANES), :] = val[:, LANES * j:LANES * (j + 1)]


def _from_slab(ref, m):
    return jnp.concatenate([ref[pl.ds(j, m, stride=SUBLANES), :] for j in range(SUBLANES)], axis=1)


def _ln_proj0_kernel(x_ref, g_ref, w_ref, xg_ref, qkv_ref):
    hn = _rms(x_ref[...], g_ref[...]).astype(BF16)
    z = _dot(hn, w_ref[...])
    aw = xg_ref.shape[1]
    bw = (z.shape[1] - aw) // 3
    xg_ref[...] = z[:, :aw]
    qkv_ref[:, :bw] = (z[:, aw:aw + bw] * (CHUNK ** -0.5 * LOG2E)).astype(BF16)
    qkv_ref[:, bw:] = z[:, aw + bw:].astype(BF16)


def _ln_proj0(x, g, w, aw):
    t, d = x.shape
    n = w.shape[1]
    tm = min(TM_PROJ, t)
    return pl.pallas_call(
        _ln_proj0_kernel,
        grid=(t // tm,),
        in_specs=[pl.BlockSpec((tm, d), lambda i: (i, 0)), _const_spec(g.shape), _const_spec(w.shape)],
        out_specs=[pl.BlockSpec((tm, aw), lambda i: (i, 0)), pl.BlockSpec((tm, n - aw), lambda i: (i, 0))],
        out_shape=[jax.ShapeDtypeStruct((t, aw), F32), jax.ShapeDtypeStruct((t, n - aw), BF16)],
        compiler_params=_params(1),
        name="ln_proj0",
    )(x, g, w)


def _rglru_kernel(xg_ref, cw_ref, cb_ref, wa_ref, wx_ref, ba_ref, bx_ref, lam_ref, ya_ref,
                  halo_ref, h_ref, a_s, u_s):
    ts = xg_ref.shape[0]
    w = ya_ref.shape[1]

    @pl.when(pl.program_id(1) == 0)
    def _():
        halo_ref[...] = jnp.zeros_like(halo_ref)
        h_ref[...] = jnp.zeros_like(h_ref)

    xa = xg_ref[:, :w]
    halo = halo_ref[...]
    row8 = lax.broadcasted_iota(jnp.int32, (SUBLANES, w), 0)
    nk = cw_ref.shape[0]
    y = cb_ref[...] + cw_ref[nk - 1:nk, :] * xa
    for k in range(1, nk):
        r = pltpu.roll(xa, k, 0)
        top = jnp.where(row8 < k, pltpu.roll(halo, k, 0), r[:SUBLANES, :])
        y = y + cw_ref[nk - 1 - k:nk - k, :] * jnp.concatenate([top, r[SUBLANES:, :]], axis=0)
    halo_ref[...] = xa[ts - SUBLANES:, :]

    yb = y.astype(BF16)
    rg = jax.nn.sigmoid(_dot(yb, wa_ref[...]) + ba_ref[...])
    ig = jax.nn.sigmoid(_dot(yb, wx_ref[...]) + bx_ref[...])
    lam = lam_ref[...]
    log_sig = jnp.minimum(lam, 0.0) - jnp.log1p(jnp.exp(-jnp.abs(lam)))
    log_a = RG_C * rg * log_sig
    a = jnp.exp(log_a)
    u = jnp.sqrt(-jnp.tanh(log_a) * (a * a + 1.0)) * (ig * y)

    rowm = lax.broadcasted_iota(jnp.int32, (ts, w), 0) & (SUBLANES - 1)
    for s in (1, 2, 4):
        keep = rowm >= s
        a_sh = jnp.where(keep, pltpu.roll(a, s, 0), 1.0)
        u_sh = jnp.where(keep, pltpu.roll(u, s, 0), 0.0)
        u = a * u_sh + u
        a = a * a_sh
    a_s[...] = a
    u_s[...] = u

    def group(gi, h):
        off = pl.multiple_of(gi * SUBLANES, SUBLANES)
        hg = a_s[pl.ds(off, SUBLANES), :] * h + u_s[pl.ds(off, SUBLANES), :]
        u_s[pl.ds(off, SUBLANES), :] = hg
        return jnp.broadcast_to(hg[SUBLANES - 1:SUBLANES, :], hg.shape)

    h_ref[...] = lax.fori_loop(0, ts // SUBLANES, group, h_ref[...])

    ga = xg_ref[:, w:]
    gelu = 0.5 * ga * (1.0 + jnp.tanh(0.7978845608028654 * (ga + 0.044715 * (ga * ga * ga))))
    ya_ref[...] = (u_s[...] * gelu).astype(BF16)


def _rglru(xg, cw, cb, wa, wx, ba, bx, lam, batch):
    t, w2 = xg.shape
    w = w2 // 2
    s = t // batch
    ts = min(TS_LRU, s)
    xg3 = xg.reshape(batch, s, w2)
    small = [cw, cb, wa, wx, ba, bx, lam]
    out = pl.pallas_call(
        _rglru_kernel,
        grid=(batch, s // ts),
        in_specs=[pl.BlockSpec((None, ts, w2), lambda b, j: (b, j, 0))] + [_const_spec(a.shape) for a in small],
        out_specs=pl.BlockSpec((None, ts, w), lambda b, j: (b, j, 0)),
        out_shape=jax.ShapeDtypeStruct((batch, s, w), BF16),
        scratch_shapes=[pltpu.VMEM((SUBLANES, w), F32), pltpu.VMEM((SUBLANES, w), F32),
                        pltpu.VMEM((ts, w), F32), pltpu.VMEM((ts, w), F32)],
        compiler_params=_params(2),
        name="rglru",
    )(xg3, *small)
    return out.reshape(t, w)


def _attn_bias_kernel(base_ref, o_ref):
    tq, tk = o_ref.shape
    nkb = tk // tq
    row = base_ref[...]
    full = pltpu.roll(jnp.broadcast_to(row, (tq, tq + tk)), 0, 1, stride=1, stride_axis=0)
    qc = lax.broadcasted_iota(jnp.int32, (tq, tk), 0) // CHUNK
    col = lax.broadcasted_iota(jnp.int32, (tq, tk), 1)
    kc = col // CHUNK
    first_ok = (nkb - 1 - pl.program_id(0)) * tq
    valid = (kc >= qc) & (kc <= qc + LEFT_CHUNKS) & (col >= first_ok)
    o_ref[...] = jnp.where(valid, full[:, tq:] * LOG2E, -1e30)


def _attn_bias(base):
    h = base.shape[0]
    tk = TQ + LEFT_CHUNKS * CHUNK
    return pl.pallas_call(
        _attn_bias_kernel,
        grid=(tk // TQ, h),
        in_specs=[pl.BlockSpec((None, 1, TQ + tk), lambda v, i: (i, 0, 0))],
        out_specs=pl.BlockSpec((None, None, TQ, tk), lambda v, i: (v, i, 0, 0)),
        out_shape=jax.ShapeDtypeStruct((tk // TQ, h, TQ, tk), F32),
        compiler_params=_params(2),
        name="attn_bias",
    )(base)


def _attn_kernel(q_ref, k0_ref, k1_ref, k2_ref, v0_ref, v1_ref, v2_ref, bias_ref, o_ref):
    tq = q_ref.shape[0]
    half_w = LANES // 2
    lo = lax.broadcasted_iota(jnp.int32, (tq, LANES), 1) < half_w
    lo_k = lax.broadcasted_iota(jnp.int32, (3 * tq, LANES), 1) < half_w
    k_refs = (k0_ref, k1_ref, k2_ref)
    v_refs = (v0_ref, v1_ref, v2_ref)
    npair = q_ref.shape[1] // LANES
    for j0 in range(0, npair, ATTN_GROUP):
        pairs = range(j0, min(j0 + ATTN_GROUP, npair))
        qk, vh = {}, {}
        for j in pairs:
            sl = slice(LANES * j, LANES * (j + 1))
            q2 = q_ref[:, sl]
            kcat = jnp.concatenate([r[:, sl] for r in k_refs], axis=0)
            vcat = jnp.concatenate([r[:, sl] for r in v_refs], axis=0)
            one = jnp.ones_like(vcat)
            zero = jnp.zeros_like(q2)
            vh[j] = (jnp.where(lo_k, vcat, one), jnp.where(lo_k, one, vcat))
            qk[j] = [_dot_nt(jnp.where(lo if half == 0 else jnp.logical_not(lo), q2, zero), kcat)
                     for half in range(2)]
        ps = {}
        for j in pairs:
            ps[j] = []
            for half in range(2):
                s = qk[j][half] + bias_ref[2 * j + half]
                ps[j].append(jnp.exp2(s - jnp.max(s, axis=-1, keepdims=True)).astype(BF16))
        for j in pairs:
            pv = [_dot(ps[j][half], vh[j][half]) for half in range(2)]
            num = jnp.where(lo, pv[0], pv[1])
            den = pltpu.roll(jnp.where(lo, pv[1], pv[0]), half_w, 1)
            o_ref[:, LANES * j:LANES * (j + 1)] = (num / den).astype(BF16)


def _attention(qkv, bias, batch):
    t, w3 = qkv.shape
    w = w3 // 3
    s = t // batch
    qkv3 = qkv.reshape(batch, s, w3)
    nvar = bias.shape[0]

    def kv_spec(colblk, back):
        return pl.BlockSpec((None, TQ, w), lambda i, b: (b, jnp.maximum(i - back, 0), colblk))

    out = pl.pallas_call(
        _attn_kernel,
        grid=(s // TQ, batch),
        in_specs=[pl.BlockSpec((None, TQ, w), lambda i, b: (b, i, 0)),
                  kv_spec(1, 2), kv_spec(1, 1), kv_spec(1, 0),
                  kv_spec(2, 2), kv_spec(2, 1), kv_spec(2, 0),
                  pl.BlockSpec((None,) + bias.shape[1:], lambda i, b: (jnp.minimum(i, nvar - 1), 0, 0, 0))],
        out_specs=pl.BlockSpec((None, TQ, w), lambda i, b: (b, i, 0)),
        out_shape=jax.ShapeDtypeStruct((batch, s, w), BF16),
        compiler_params=_params(2),
        name="attention",
    )(qkv3, qkv3, qkv3, qkv3, qkv3, qkv3, qkv3, bias)
    return out.reshape(t, w)


def _proj_route_kernel(a_ref, b_ref, r_ref, wa_ref, wb_ref, g_ref, wt_ref,
                       h_ref, hn_ref, meta_ref, metat_ref, cnt_ref, carry_ref):
    tm = a_ref.shape[0]
    ts = min(TM_ROUTE_SUB, tm)

    @pl.when(pl.program_id(0) == 0)
    def _():
        carry_ref[...] = jnp.zeros_like(carry_ref)

    wt = wt_ref[...]
    ridx = lax.broadcasted_iota(jnp.int32, (ROUTE_ROWS, ts), 0).astype(F32)
    r8 = lax.broadcasted_iota(jnp.int32, (SUBLANES, ts), 0)
    rr = lax.broadcasted_iota(jnp.int32, (ts, ts), 0)
    cc = lax.broadcasted_iota(jnp.int32, (ts, ts), 1)
    utri = jnp.where(rr < cc, 1.0, 0.0).astype(BF16)
    neg = -jnp.inf
    far = float(LANES)
    carry = carry_ref[...]

    all_logits = []
    for sub in range(tm // ts):
        rows = pl.ds(sub * ts, ts)
        h = r_ref[rows, :] + (_dot(a_ref[rows, :], wa_ref[...]) + _dot(b_ref[rows, :], wb_ref[...]))
        h_ref[rows, :] = h
        hn = _rms(h, g_ref[...])
        _to_slab(hn_ref.at[pl.ds(sub * ts * SUBLANES, ts * SUBLANES), :], hn)

        hi, lo = _split_bf16(hn)
        p_hi = _dot_nt(wt, hi)
        p_lo = _dot_nt(wt[:LANES], lo)
        all_logits.append((p_hi[:ROUTE_ROWS] + p_hi[LANES:LANES + ROUTE_ROWS]) + p_lo[:ROUTE_ROWS])

    for sub, logits in enumerate(all_logits):
        rows = pl.ds(sub * ts, ts)

        def first_max(mask, logits=logits):
            v = jnp.max(jnp.where(mask, logits, neg), axis=0, keepdims=True)
            idx = jnp.min(jnp.where(mask & (logits == v), ridx, far), axis=0, keepdims=True)
            return v, idx

        gmask = ridx < N_GROUPS
        gmax, gidx = first_max(gmask)
        g_gate = 1.0 / jnp.sum(jnp.where(gmask, jnp.exp(logits - gmax), 0.0), axis=0, keepdims=True)
        e_lo = N_GROUPS + EXPERTS_PER_GROUP * gidx
        emask = (ridx >= e_lo) & (ridx < e_lo + EXPERTS_PER_GROUP)
        v1, i1 = first_max(emask)
        v2, i2 = first_max(emask & (ridx != i1))
        tt = jnp.exp(v2 - v1)
        w1 = g_gate / (1.0 + tt)
        w2 = g_gate * tt / (1.0 + tt)

        sel1 = ridx == i1
        sel2 = ridx == i2
        onehot = jnp.where(sel1 | sel2, 1.0, 0.0)
        before = _dot(onehot.astype(BF16), utri) + carry[:, 0:1]
        rank1 = jnp.sum(jnp.where(sel1, before, 0.0), axis=0, keepdims=True)
        rank2 = jnp.sum(jnp.where(sel2, before, 0.0), axis=0, keepdims=True)
        carry = carry + jnp.sum(onehot, axis=1, keepdims=True)

        mt = jnp.zeros((SUBLANES, ts), F32)
        for c, val in enumerate((i1 - N_GROUPS, i2 - N_GROUPS, rank1, rank2, w1, w2)):
            mt = jnp.where(r8 == c, val, mt)
        metat_ref[:, rows] = mt
        meta_ref[rows, :] = jnp.concatenate([mt, jnp.zeros((LANES - SUBLANES, ts), F32)], axis=0).T

    carry_ref[...] = carry
    cnt_ref[...] = carry


def _proj_route(a, acol, b, bcol, resid, wa, wb, g, wt):
    t, d = resid.shape
    kw = wa.shape[0]
    tm = min(TM_ROUTE, t)
    return pl.pallas_call(
        _proj_route_kernel,
        grid=(t // tm,),
        in_specs=[pl.BlockSpec((tm, kw), lambda i: (i, acol)), pl.BlockSpec((tm, kw), lambda i: (i, bcol)),
                  pl.BlockSpec((tm, d), lambda i: (i, 0)),
                  _const_spec(wa.shape), _const_spec(wb.shape), _const_spec(g.shape), _const_spec(wt.shape)],
        out_specs=[pl.BlockSpec((tm, d), lambda i: (i, 0)),
                   pl.BlockSpec((tm * SUBLANES, LANES), lambda i: (i, 0)),
                   pl.BlockSpec((tm, LANES), lambda i: (i, 0)),
                   pl.BlockSpec((SUBLANES, tm), lambda i: (0, i)),
                   pl.BlockSpec((ROUTE_ROWS, LANES), lambda i: (0, 0))],
        out_shape=[jax.ShapeDtypeStruct((t, d), F32),
                   jax.ShapeDtypeStruct((t * SUBLANES, LANES), F32),
                   jax.ShapeDtypeStruct((t, LANES), F32),
                   jax.ShapeDtypeStruct((SUBLANES, t), F32),
                   jax.ShapeDtypeStruct((ROUTE_ROWS, LANES), F32)],
        scratch_shapes=[pltpu.VMEM((ROUTE_ROWS, LANES), F32)],
        compiler_params=_params(1),
        name="proj_route",
    )(a, b, resid, wa, wb, g, wt)


def _plan(meta_t, cnt, tb):
    t = meta_t.shape[1]
    counts = cnt[N_GROUPS:N_GROUPS + N_EXPERTS, 0].astype(jnp.int32)
    padded = (counts + tb - 1) // tb * tb
    pad_end = jnp.cumsum(padded)
    pad_start = pad_end - padded
    dest = []
    for k in range(TOP_K):
        expert = meta_t[k].astype(jnp.int32)
        start = jnp.zeros((t,), jnp.int32)
        for e in range(N_EXPERTS):
            start = jnp.where(expert == e, pad_start[e], start)
        dest.append(start + meta_t[TOP_K + k].astype(jnp.int32))
    dest = jnp.concatenate(dest)
    nblk = -(-(t * TOP_K + N_EXPERTS * (tb - 1)) // tb)
    blk_start = jnp.arange(nblk, dtype=jnp.int32) * tb
    blk_expert = jnp.minimum(jnp.sum((pad_end[None, :] <= blk_start[:, None]).astype(jnp.int32), axis=1),
                             N_EXPERTS - 1)
    nact = (pad_end[-1:] // tb).astype(jnp.int32)
    return dest, pad_start + counts, pad_end, blk_expert, nact, nblk


def _dispatch_kernel(dest_ref, plo_ref, phi_ref, nact_ref, hn_ref, xs_ref, zero_ref, sem, zsem):
    tm = hn_ref.shape[0] // SUBLANES
    t = dest_ref.shape[0] // TOP_K
    tb = zero_ref.shape[0] // SUBLANES
    nblk = xs_ref.shape[0] // zero_ref.shape[0]
    i = pl.program_id(0)

    def zero_copy(row, nrows):
        return pltpu.make_async_copy(
            zero_ref.at[pl.ds(0, nrows * SUBLANES), :],
            xs_ref.at[pl.ds(pl.multiple_of(row * SUBLANES, SUBLANES), nrows * SUBLANES), :], zsem)

    def zero_fill(start):
        def go(cp):
            cp.start() if start else cp.wait()

        def per_expert(e, c):
            off = plo_ref[e]
            n = phi_ref[e] - off
            bit = tb // 2
            while bit:
                pl.when((n & bit) != 0)(functools.partial(lambda o, b: go(zero_copy(o, b)), off, bit))
                off = off + (n & bit)
                bit //= 2
            return c

        def per_block(b, c):
            go(zero_copy(b * tb, tb))
            return c

        lax.fori_loop(0, N_EXPERTS, per_expert, 0)
        lax.fori_loop(nact_ref[0], nblk, per_block, 0)

    @pl.when(i == 0)
    def _():
        zero_ref[...] = jnp.zeros_like(zero_ref)
        zero_fill(True)

    def copy(r, k):
        d = dest_ref[k * t + i * tm + r]
        return pltpu.make_async_copy(
            hn_ref.at[pl.ds(pl.multiple_of(r * SUBLANES, SUBLANES), SUBLANES), :],
            xs_ref.at[pl.ds(pl.multiple_of(d * SUBLANES, SUBLANES), SUBLANES), :], sem)

    def start(r, c):
        for k in range(TOP_K):
            copy(r, k).start(priority=k)
        return c

    lax.fori_loop(0, tm, start, 0, unroll=DMA_UNROLL)
    for k in range(TOP_K):
        pltpu.make_async_copy(hn_ref, xs_ref.at[pl.ds(0, tm * SUBLANES), :], sem).wait()

    @pl.when(i == 0)
    def _():
        zero_fill(False)


def _dispatch(dest, pad_lo, pad_hi, nact, hn_slab, nblk, tb):
    t = hn_slab.shape[0] // SUBLANES
    tm = min(TM_COMB, t)
    return pl.pallas_call(
        _dispatch_kernel,
        grid_spec=pltpu.PrefetchScalarGridSpec(
            num_scalar_prefetch=4,
            grid=(t // tm,),
            in_specs=[pl.BlockSpec((tm * SUBLANES, LANES), lambda i, *_: (i, 0))],
            out_specs=pl.BlockSpec(memory_space=pl.ANY),
            scratch_shapes=[pltpu.VMEM((tb * SUBLANES, LANES), F32),
                            pltpu.SemaphoreType.DMA(()), pltpu.SemaphoreType.DMA(())]),
        out_shape=jax.ShapeDtypeStruct((nblk * tb * SUBLANES, LANES), F32),
        compiler_params=_params(1),
        name="dispatch",
    )(dest, pad_lo, pad_hi, nact, hn_slab)


def _expert_kernel(be_ref, nact_ref, xs_ref, wg_ref, wu_ref, wd_ref, ys_ref, wg_s, wu_s, wd_s):
    i = pl.program_id(0)
    tb = xs_ref.shape[0] // SUBLANES
    active = i < nact_ref[0]
    changed = (i == 0) | (be_ref[i] != be_ref[jnp.maximum(i - 1, 0)])

    @pl.when(active & changed)
    def _():
        wg_s[...] = wg_ref[...].astype(BF16)
        wu_s[...] = wu_ref[...].astype(BF16)
        wd_s[...] = wd_ref[...].astype(BF16)

    @pl.when(active)
    def _():
        x = _from_slab(xs_ref, tb).astype(BF16)
        gate = _dot(x, wg_s[...])
        up = _dot(x, wu_s[...])
        hid = (gate * jax.nn.sigmoid(gate) * up).astype(BF16)
        _to_slab(ys_ref, _dot(hid, wd_s[...]))

    @pl.when(jnp.logical_not(active))
    def _():
        ys_ref[...] = jnp.zeros_like(ys_ref)


def _experts(blk_expert, nact, xs, wg, wu, wd, layer, tb):
    nblk = blk_expert.shape[0]
    _, _, d, de = wg.shape

    def row_map(i, be, na):
        return (jnp.minimum(i, na[0] - 1), 0)

    def w_map(i, be, na):
        return (layer, be[jnp.minimum(i, na[0] - 1)], 0, 0)

    return pl.pallas_call(
        _expert_kernel,
        grid_spec=pltpu.PrefetchScalarGridSpec(
            num_scalar_prefetch=2,
            grid=(nblk,),
            in_specs=[pl.BlockSpec((tb * SUBLANES, LANES), row_map),
                      pl.BlockSpec((None, None, d, de), w_map), pl.BlockSpec((None, None, d, de), w_map),
                      pl.BlockSpec((None, None, de, d), w_map)],
            out_specs=pl.BlockSpec((tb * SUBLANES, LANES), lambda i, be, na: (i, 0)),
            scratch_shapes=[pltpu.VMEM((d, de), BF16), pltpu.VMEM((d, de), BF16), pltpu.VMEM((de, d), BF16)]),
        out_shape=jax.ShapeDtypeStruct(xs.shape, F32),
        compiler_params=_params(1),
        name="experts",
    )(blk_expert, nact, xs, wg, wu, wd)


def _combine(dest_ref, ys_ref, meta_ref, buf_ref, sem_ref, tm):
    i = pl.program_id(0)
    n = pl.num_programs(0)
    slot = i % 2

    t = dest_ref.shape[0] // TOP_K

    def copy(tile, sl, r, k):
        d = dest_ref[k * t + tile * tm + r]
        row = r * SUBLANES if isinstance(r, int) else pl.multiple_of(r * SUBLANES, SUBLANES)
        return pltpu.make_async_copy(
            ys_ref.at[pl.ds(pl.multiple_of(d * SUBLANES, SUBLANES), SUBLANES), :],
            buf_ref.at[sl, k, pl.ds(row, SUBLANES), :],
            sem_ref.at[sl])

    def start_tile(tile, sl):
        def body(r, c):
            for k in range(TOP_K):
                copy(tile, sl, r, k).start(priority=k)
            return c
        lax.fori_loop(0, tm, body, 0, unroll=DMA_UNROLL)

    def wait_tile(sl):
        for k in range(TOP_K):
            pltpu.make_async_copy(ys_ref.at[pl.ds(0, tm * SUBLANES), :], buf_ref.at[sl, k], sem_ref.at[sl]).wait()

    @pl.when(i == 0)
    def _():
        start_tile(0, 0)

    nxt = jnp.minimum(i + 1, n - 1)
    for r in range(tm):
        for k in range(TOP_K):
            copy(nxt, 1 - slot, r, k).start(priority=k)

    wait_tile(slot)

    @pl.when(i == n - 1)
    def _():
        wait_tile(1 - slot)

    y = None
    for k in range(TOP_K):
        gate = meta_ref[:, 2 * TOP_K + k:2 * TOP_K + k + 1]
        term = _from_slab(buf_ref.at[slot, k], tm) * gate
        y = term if y is None else y + term
    return y


def _combine_scratch(tm):
    return [pltpu.VMEM((2, TOP_K, tm * SUBLANES, LANES), F32), pltpu.SemaphoreType.DMA((2,))]


def _comb_proj1_kernel(dest_ref, h_ref, meta_ref, ys_ref, g_ref, w_ref, lbl_ref,
                       h2_ref, qs_ref, lf_ref, iv_ref, gs_ref, buf_ref, sem_ref, *, lb_rows):
    tm = h_ref.shape[0]
    h2 = h_ref[...] + _combine(dest_ref, ys_ref, meta_ref, buf_ref, sem_ref, tm)
    h2_ref[...] = h2
    z = _dot(_rms(h2, g_ref[...]).astype(BF16), w_ref[...])
    d = h2.shape[1]
    q, fz, iv, gz = (z[:, d * c:d * (c + 1)] for c in range(4))
    lbl = lbl_ref[...]
    ex = jnp.exp(lbl - jnp.max(lbl, axis=0, keepdims=True))
    lb = jnp.sum(ex[:lb_rows], axis=0, keepdims=True) / jnp.sum(ex, axis=0, keepdims=True)
    f = lb + (1.0 - lb) * jax.nn.sigmoid(fz)
    qs_ref[...] = (q * jax.nn.sigmoid(q)).astype(BF16)
    lf_ref[...] = jnp.log(f)
    iv_ref[...] = iv.astype(BF16)
    gs_ref[...] = (gz * jax.nn.sigmoid(gz)).astype(BF16)


def _comb_proj1(dest, h, meta, ys, g, w, lb_logits, lb_rows):
    t, d = h.shape
    tm = min(TM_COMB, t)
    row = lambda i, dd: (i, 0)
    const = lambda shape: pl.BlockSpec(shape, lambda i, dd: (0,) * len(shape))
    return pl.pallas_call(
        functools.partial(_comb_proj1_kernel, lb_rows=lb_rows),
        grid_spec=pltpu.PrefetchScalarGridSpec(
            num_scalar_prefetch=1,
            grid=(t // tm,),
            in_specs=[pl.BlockSpec((tm, d), row), pl.BlockSpec((tm, LANES), row),
                      pl.BlockSpec(memory_space=pl.ANY),
                      const(g.shape), const(w.shape), const(lb_logits.shape)],
            out_specs=[pl.BlockSpec((tm, d), row)] * 5,
            scratch_shapes=_combine_scratch(tm)),
        out_shape=[jax.ShapeDtypeStruct((t, d), F32), jax.ShapeDtypeStruct((t, d), BF16),
                   jax.ShapeDtypeStruct((t, d), F32), jax.ShapeDtypeStruct((t, d), BF16),
                   jax.ShapeDtypeStruct((t, d), BF16)],
        compiler_params=_params(1),
        name="comb_proj1",
    )(dest, h, meta, ys, g, w, lb_logits)


def _comb_final_kernel(dest_ref, h_ref, meta_ref, ys_ref, g_ref, o_ref, buf_ref, sem_ref):
    tm = h_ref.shape[0]
    h = h_ref[...] + _combine(dest_ref, ys_ref, meta_ref, buf_ref, sem_ref, tm)
    o_ref[...] = _rms(h, g_ref[...])


def _comb_final(dest, h, meta, ys, g):
    t, d = h.shape
    tm = min(TM_COMB, t)
    row = lambda i, dd: (i, 0)
    return pl.pallas_call(
        _comb_final_kernel,
        grid_spec=pltpu.PrefetchScalarGridSpec(
            num_scalar_prefetch=1,
            grid=(t // tm,),
            in_specs=[pl.BlockSpec((tm, d), row), pl.BlockSpec((tm, LANES), row),
                      pl.BlockSpec(memory_space=pl.ANY),
                      pl.BlockSpec(g.shape, lambda i, dd: (0, 0))],
            out_specs=pl.BlockSpec((tm, d), row),
            scratch_shapes=_combine_scratch(tm)),
        out_shape=jax.ShapeDtypeStruct((t, d), F32),
        compiler_params=_params(1),
        name="comb_final",
    )(dest, h, meta, ys, g)


def _hgrn_kernel(qs_ref, lf_ref, iv_ref, gs_ref, ng_ref, o_ref, st_ref, d1_s, t16_s, *, dk):
    th = qs_ref.shape[0]
    nsub = CHUNK // SUB

    @pl.when(pl.program_id(1) == 0)
    def _():
        st_ref[...] = jnp.zeros_like(st_ref)

    rr = lax.broadcasted_iota(jnp.int32, (th, th), 0)
    cc = lax.broadcasted_iota(jnp.int32, (th, th), 1)
    same_sub = (rr // SUB) == (cc // SUB)
    same_chunk = (rr // CHUNK) == (cc // CHUNK)
    m_diag = same_sub & (cc <= rr)
    dsub = rr // SUB - cc // SUB
    m_off = [same_chunk & (dsub == dd) for dd in range(1, nsub)]

    hi, lo = _split_bf16(lf_ref[...])
    tri = jnp.where(m_diag, 1.0, 0.0).astype(BF16)
    ones = jnp.where(same_sub, 1.0, 0.0).astype(BF16)
    d1_s[...] = _dot(tri, hi) + _dot(tri, lo)
    t16_s[...] = _dot(ones, hi) + _dot(ones, lo)
    rrow = lax.broadcasted_iota(jnp.int32, (th, dk), 0)
    rsub = (rrow // SUB) % nsub
    rchunk = rrow // CHUNK
    nch = th // CHUNK

    def prepare(hd):
        ls = pl.ds(pl.multiple_of(hd * dk, dk), dk)
        d1 = d1_s[:, ls]
        t16 = t16_s[:, ls]
        q = qs_ref[:, ls].astype(F32)
        kt = 1.0 - jnp.exp(lf_ref[:, ls])
        suf = t16 - d1
        kx = (kt * jnp.exp(suf)).astype(BF16)
        qv = [q * jnp.exp(-suf)]
        acc = d1
        tail = suf
        for dd in range(1, nsub):
            qv.append(q * jnp.exp(acc))
            acc = acc + jnp.where(rsub >= dd, pltpu.roll(t16, SUB * dd, 0), 0.0)
            tail = tail + jnp.where(rsub < nsub - dd, pltpu.roll(t16, th - SUB * dd, 0), 0.0)
        b = acc
        qb = q * jnp.exp(b)
        kend = kt * jnp.exp(tail)
        zero = jnp.zeros_like(q)
        kend_x = jnp.concatenate([jnp.where(rchunk == c, kend, zero) for c in range(nch)], axis=1).astype(BF16)
        qb_x = jnp.concatenate([jnp.where(rchunk == c, qb, zero) for c in range(nch)], axis=1).astype(BF16)
        decs = [jnp.exp(b[CHUNK * (c + 1) - 1:CHUNK * (c + 1), :]) for c in range(nch)]
        return ls, jnp.concatenate(qv, axis=0).astype(BF16), kx, kend_x, qb_x, decs

    def group(gi, carry):
        heads = [gi * HGRN_GROUP + u for u in range(HGRN_GROUP)]
        prep = [prepare(hd) for hd in heads]
        a4s = [_dot_nt(p[1], p[2]) for p in prep]
        incs = [_dot_tn(iv_ref[:, p[0]], p[3]) for p in prep]
        o_intras = []
        for p, a4 in zip(prep, a4s):
            att = jnp.where(m_diag, a4[:th], 0.0)
            for dd in range(1, nsub):
                att = jnp.where(m_off[dd - 1], a4[dd * th:(dd + 1) * th], att)
            o_intras.append(_dot(att.astype(BF16), iv_ref[:, p[0]]))
        for hd, p, inc, o_intra in zip(heads, prep, incs, o_intras):
            ls, decs = p[0], p[5]
            st = st_ref[hd]
            starts = []
            for c in range(nch):
                starts.append(st)
                st = st * decs[c] + inc[:, dk * c:dk * (c + 1)]
            st_ref[hd] = st
            oh = o_intra + _dot_nt(p[4], jnp.concatenate(starts, axis=1).astype(BF16))
            on = oh * lax.rsqrt(jnp.mean(oh * oh, axis=-1, keepdims=True) + EPS)
            o_ref[:, ls] = (on * ng_ref[:, ls] * gs_ref[:, ls].astype(F32)).astype(BF16)
        return carry

    lax.fori_loop(0, qs_ref.shape[1] // dk // HGRN_GROUP, group, 0)


def _hgrn(qs, lf, iv, gs, ng, batch, dk):
    t, d = qs.shape
    s = t // batch
    th = min(TH, s)
    blk = pl.BlockSpec((None, th, d), lambda b, j: (b, j, 0))
    r3 = lambda a: a.reshape(batch, s, d)
    out = pl.pallas_call(
        functools.partial(_hgrn_kernel, dk=dk),
        grid=(batch, s // th),
        in_specs=[blk, blk, blk, blk, _const_spec(ng.shape)],
        out_specs=blk,
        out_shape=jax.ShapeDtypeStruct((batch, s, d), BF16),
        scratch_shapes=[pltpu.VMEM((d // dk, dk, dk), F32), pltpu.VMEM((th, d), F32), pltpu.VMEM((th, d), F32)],
        compiler_params=_params(2),
        name="hgrn2",
    )(r3(qs), r3(lf), r3(iv), r3(gs), ng)
    return out.reshape(t, d)


def _block_diag(w):
    n, c, dd = w.shape
    eye = jnp.eye(n, dtype=w.dtype)
    return (eye[:, None, :, None] * w[:, :, None, :]).reshape(n * c, n * dd)


def _router_weights(wg, we):
    d = wg.shape[0]
    w = jnp.concatenate([wg, we, jnp.zeros((d, LANES - wg.shape[1] - we.shape[1]), F32)], axis=1).T
    hi = w.astype(BF16)
    return jnp.concatenate([hi, (w - hi.astype(F32)).astype(BF16)], axis=0)


def _moe(layer, meta_t, cnt, hn_slab, moe_w_gate, moe_w_up, moe_w_down):
    dest, pad_lo, pad_hi, blk_expert, nact, nblk = _plan(meta_t, cnt, TB_EXPERT)
    xs = _dispatch(dest, pad_lo, pad_hi, nact, hn_slab, nblk, TB_EXPERT)
    ys = _experts(blk_expert, nact, xs, moe_w_gate, moe_w_up, moe_w_down, layer, TB_EXPERT)
    return dest, ys


def kernel(x, norm_mix_g, norm_ffn_g, norm_final_g, ab_w_in, ab_conv_w, ab_conv_b, rg_w_a, rg_b_a, rg_w_x, rg_b_x, rg_lambda, attn_rel_bias, ab_w_out, c_w_in, c_lb_logits, c_norm_g, c_w_out, moe_router_group, moe_router_expert, moe_w_gate, moe_w_up, moe_w_down):
    batch, seq, d = x.shape
    t = batch * seq
    xt = x.reshape(t, d)
    row = lambda v: v.reshape(1, -1)

    aw = ab_conv_w.shape[2]
    xg, qkv = _ln_proj0(xt, row(norm_mix_g[0]), ab_w_in[0].astype(BF16), 2 * aw)
    ya = _rglru(xg, ab_conv_w[0], row(ab_conv_b[0]),
                _block_diag(rg_w_a[0]).astype(BF16), _block_diag(rg_w_x[0]).astype(BF16),
                row(rg_b_a[0]), row(rg_b_x[0]), row(rg_lambda[0]), batch)
    tk = TQ + LEFT_CHUNKS * CHUNK
    dist = jnp.clip(tk - jnp.arange(TQ + tk), -REL_CLIP, REL_CLIP) + REL_CLIP
    bias = _attn_bias(attn_rel_bias[0][:, None, dist])
    yb = _attention(qkv, bias, batch)
    wo = ab_w_out[0].astype(BF16)
    wt = _router_weights(moe_router_group[0], moe_router_expert[0])
    h1, hn1, meta1, metat1, cnt1 = _proj_route(ya, 0, yb, 0, xt, wo[:aw], wo[aw:], row(norm_ffn_g[0]), wt)
    dest1, ys1 = _moe(0, metat1, cnt1, hn1, moe_w_gate, moe_w_up, moe_w_down)

    dk = c_norm_g.shape[1] // 8
    h2, qs, lf, iv, gs = _comb_proj1(dest1, h1, meta1, ys1, row(norm_mix_g[1]), c_w_in[0].astype(BF16),
                                     c_lb_logits, 1)
    om = _hgrn(qs, lf, iv, gs, row(c_norm_g[0]), batch, dk)
    wo = c_w_out[0].astype(BF16)
    half = wo.shape[0] // 2
    wt = _router_weights(moe_router_group[1], moe_router_expert[1])
    h3, hn3, meta3, metat3, cnt3 = _proj_route(om, 0, om, 1, h2, wo[:half], wo[half:], row(norm_ffn_g[1]), wt)
    dest3, ys3 = _moe(1, metat3, cnt3, hn3, moe_w_gate, moe_w_up, moe_w_down)

    out = _comb_final(dest3, h3, meta3, ys3, row(norm_final_g))
    return out.reshape(batch, seq, d)
```

```python
import functools

import jax
import jax.numpy as jnp
from jax import lax
from jax.experimental import pallas as pl
from jax.experimental.pallas import tpu as pltpu

F32 = jnp.float32
BF16 = jnp.bfloat16

EPS = 1e-6
LOG2E = 1.4426950408889634
RG_C = 8.0
CHUNK = 64
LEFT_CHUNKS = 8
REL_CLIP = 256
N_GROUPS = 4
EXPERTS_PER_GROUP = 8
N_EXPERTS = N_GROUPS * EXPERTS_PER_GROUP
TOP_K = 2
ROUTE_ROWS = 40

LANES = 128
SUBLANES = 8
SUB = 16

TM_PROJ = 512
TS_LRU = 512
TQ = 256
ATTN_GROUP = 4
TM_ROUTE = 1024
TM_ROUTE_SUB = 256
TB_EXPERT = 512
TM_COMB = 256
TH = 256
HGRN_GROUP = 4
DMA_UNROLL = 8
COMB_SLOTS = 3
VMEM_MB = 48


def _params(n_axes, vmem_mb=VMEM_MB):
    return pltpu.CompilerParams(dimension_semantics=("arbitrary",) * n_axes,
                                vmem_limit_bytes=vmem_mb * 1024 * 1024)


def _const_spec(shape):
    nd = len(shape)
    return pl.BlockSpec(shape, lambda *_: (0,) * nd)


def _rms(x, g):
    return x * lax.rsqrt(jnp.mean(x * x, axis=-1, keepdims=True) + EPS) * g


def _dot(a, b):
    return jnp.dot(a, b, preferred_element_type=F32)


def _dot_nt(a, b):
    return lax.dot_general(a, b, (((1,), (1,)), ((), ())), preferred_element_type=F32)


def _dot_tn(a, b):
    return lax.dot_general(a, b, (((0,), (0,)), ((), ())), preferred_element_type=F32)


def _split_bf16(x):
    hi = x.astype(BF16)
    lo = (x - hi.astype(F32)).astype(BF16)
    return hi, lo


def _to_slab(ref, val):
    m = val.shape[0]
    for j in range(val.shape[1] // LANES):
        ref[pl.ds(j, m, stride=SUBLANES), :] = val[:, LANES * j:LANES * (j + 1)]


def _from_slab(ref, m):
    return jnp.concatenate([ref[pl.ds(j, m, stride=SUBLANES), :] for j in range(SUBLANES)], axis=1)


def _ln_proj0_kernel(x_ref, g_ref, w_ref, xg_ref, qkv_ref):
    hn = _rms(x_ref[...], g_ref[...]).astype(BF16)
    z = _dot(hn, w_ref[...])
    aw = xg_ref.shape[1]
    bw = (z.shape[1] - aw) // 3
    xg_ref[...] = z[:, :aw]
    qkv_ref[:, :bw] = (z[:, aw:aw + bw] * (CHUNK ** -0.5 * LOG2E)).astype(BF16)
    qkv_ref[:, bw:] = z[:, aw + bw:].astype(BF16)


def _ln_proj0(x, g, w, aw):
    t, d = x.shape
    n = w.shape[1]
    tm = min(TM_PROJ, t)
    return pl.pallas_call(
        _ln_proj0_kernel,
        grid=(t // tm,),
        in_specs=[pl.BlockSpec((tm, d), lambda i: (i, 0)), _const_spec(g.shape), _const_spec(w.shape)],
        out_specs=[pl.BlockSpec((tm, aw), lambda i: (i, 0)), pl.BlockSpec((tm, n - aw), lambda i: (i, 0))],
        out_shape=[jax.ShapeDtypeStruct((t, aw), F32), jax.ShapeDtypeStruct((t, n - aw), BF16)],
        compiler_params=_params(1),
        name="ln_proj0",
    )(x, g, w)


def _rglru_kernel(xg_ref, cw_ref, cb_ref, wa_ref, wx_ref, ba_ref, bx_ref, lam_ref, ya_ref,
                  xbuf, h_ref, a_s, u_s):
    ts = xg_ref.shape[0]
    w = ya_ref.shape[1]

    @pl.when(pl.program_id(1) == 0)
    def _():
        xbuf[pl.ds(0, SUBLANES), :] = jnp.zeros((SUBLANES, w), F32)
        h_ref[...] = jnp.zeros_like(h_ref)

    xbuf[pl.ds(SUBLANES, ts), :] = xg_ref[:, :w]
    nk = cw_ref.shape[0]
    y = cb_ref[...]
    for k in range(nk):
        y = y + cw_ref[nk - 1 - k:nk - k, :] * xbuf[pl.ds(SUBLANES - k, ts), :]
    xbuf[pl.ds(0, SUBLANES), :] = xbuf[pl.ds(ts, SUBLANES), :]

    yb = y.astype(BF16)
    rg = jax.nn.sigmoid(_dot(yb, wa_ref[...]) + ba_ref[...])
    ig = jax.nn.sigmoid(_dot(yb, wx_ref[...]) + bx_ref[...])
    lam = lam_ref[...]
    log_sig = jnp.minimum(lam, 0.0) - jnp.log1p(jnp.exp(-jnp.abs(lam)))
    log_a = RG_C * rg * log_sig
    a = jnp.exp(log_a)
    m = 1.0 - a * a
    u = jnp.where(m > 0.0, m * lax.rsqrt(m), 0.0) * (ig * y)

    grp = (ts // SUBLANES, SUBLANES, w)
    a = a.reshape(grp)
    u = u.reshape(grp)
    rowm = lax.broadcasted_iota(jnp.int32, grp, 1)
    for s in (1, 2, 4):
        keep = rowm >= s
        a_sh = jnp.where(keep, pltpu.roll(a, s, 1), 1.0)
        u_sh = jnp.where(keep, pltpu.roll(u, s, 1), 0.0)
        u = a * u_sh + u
        a = a * a_sh
    a_s[...] = a.reshape(ts, w)
    u_s[...] = u.reshape(ts, w)

    def group(gi, h):
        off = pl.multiple_of(gi * SUBLANES, SUBLANES)
        hg = a_s[pl.ds(off, SUBLANES), :] * h + u_s[pl.ds(off, SUBLANES), :]
        u_s[pl.ds(off, SUBLANES), :] = hg
        return jnp.broadcast_to(hg[SUBLANES - 1:SUBLANES, :], hg.shape)

    h_ref[...] = lax.fori_loop(0, ts // SUBLANES, group, h_ref[...])

    ga = xg_ref[:, w:]
    gelu = 0.5 * ga * (1.0 + jnp.tanh(0.7978845608028654 * (ga + 0.044715 * (ga * ga * ga))))
    ya_ref[...] = (u_s[...] * gelu).astype(BF16)


def _rglru(xg, cw, cb, wa, wx, ba, bx, lam, batch):
    t, w2 = xg.shape
    w = w2 // 2
    s = t // batch
    ts = min(TS_LRU, s)
    xg3 = xg.reshape(batch, s, w2)
    small = [cw, cb, wa, wx, ba, bx, lam]
    out = pl.pallas_call(
        _rglru_kernel,
        grid=(batch, s // ts),
        in_specs=[pl.BlockSpec((None, ts, w2), lambda b, j: (b, j, 0))] + [_const_spec(a.shape) for a in small],
        out_specs=pl.BlockSpec((None, ts, w), lambda b, j: (b, j, 0)),
        out_shape=jax.ShapeDtypeStruct((batch, s, w), BF16),
        scratch_shapes=[pltpu.VMEM((ts + SUBLANES, w), F32), pltpu.VMEM((SUBLANES, w), F32),
                        pltpu.VMEM((ts, w), F32), pltpu.VMEM((ts, w), F32)],
        compiler_params=_params(2),
        name="rglru",
    )(xg3, *small)
    return out.reshape(t, w)


def _attn_bias_kernel(base_ref, o_ref):
    tq, tk = o_ref.shape
    nkb = tk // tq
    row = base_ref[...]
    full = pltpu.roll(jnp.broadcast_to(row, (tq, tq + tk)), 0, 1, stride=1, stride_axis=0)
    qc = lax.broadcasted_iota(jnp.int32, (tq, tk), 0) // CHUNK
    col = lax.broadcasted_iota(jnp.int32, (tq, tk), 1)
    kc = col // CHUNK
    first_ok = (nkb - 1 - pl.program_id(0)) * tq
    valid = (kc >= qc) & (kc <= qc + LEFT_CHUNKS) & (col >= first_ok)
    o_ref[...] = jnp.where(valid, full[:, tq:] * LOG2E, -1e30)


def _attn_bias(base):
    h = base.shape[0]
    tk = TQ + LEFT_CHUNKS * CHUNK
    return pl.pallas_call(
        _attn_bias_kernel,
        grid=(tk // TQ, h),
        in_specs=[pl.BlockSpec((None, 1, TQ + tk), lambda v, i: (i, 0, 0))],
        out_specs=pl.BlockSpec((None, None, TQ, tk), lambda v, i: (v, i, 0, 0)),
        out_shape=jax.ShapeDtypeStruct((tk // TQ, h, TQ, tk), F32),
        compiler_params=_params(2),
        name="attn_bias",
    )(base)


def _attn_kernel(q_ref, k0_ref, k1_ref, k2_ref, v0_ref, v1_ref, v2_ref, bias_ref, o_ref):
    tq = q_ref.shape[0]
    half_w = LANES // 2
    lo = lax.broadcasted_iota(jnp.int32, (tq, LANES), 1) < half_w
    lo_k = lax.broadcasted_iota(jnp.int32, (3 * tq, LANES), 1) < half_w
    k_refs = (k0_ref, k1_ref, k2_ref)
    v_refs = (v0_ref, v1_ref, v2_ref)
    npair = q_ref.shape[1] // LANES
    for j0 in range(0, npair, ATTN_GROUP):
        pairs = range(j0, min(j0 + ATTN_GROUP, npair))
        qk, vh = {}, {}
        for j in pairs:
            sl = slice(LANES * j, LANES * (j + 1))
            q2 = q_ref[:, sl]
            kcat = jnp.concatenate([r[:, sl] for r in k_refs], axis=0)
            vcat = jnp.concatenate([r[:, sl] for r in v_refs], axis=0)
            one = jnp.ones_like(vcat)
            zero = jnp.zeros_like(q2)
            vh[j] = (jnp.where(lo_k, vcat, one), jnp.where(lo_k, one, vcat))
            qk[j] = [_dot_nt(jnp.where(lo if half == 0 else jnp.logical_not(lo), q2, zero), kcat)
                     for half in range(2)]
        ps = {}
        for j in pairs:
            ps[j] = []
            for half in range(2):
                s = qk[j][half] + bias_ref[2 * j + half]
                ps[j].append(jnp.exp2(s - jnp.max(s, axis=-1, keepdims=True)).astype(BF16))
        for j in pairs:
            pv = [_dot(ps[j][half], vh[j][half]) for half in range(2)]
            num = jnp.where(lo, pv[0], pv[1])
            den = pltpu.roll(jnp.where(lo, pv[1], pv[0]), half_w, 1)
            o_ref[:, LANES * j:LANES * (j + 1)] = (num / den).astype(BF16)


def _attention(qkv, bias, batch):
    t, w3 = qkv.shape
    w = w3 // 3
    s = t // batch
    qkv3 = qkv.reshape(batch, s, w3)
    nvar = bias.shape[0]

    def kv_spec(colblk, back):
        return pl.BlockSpec((None, TQ, w), lambda i, b: (b, jnp.maximum(i - back, 0), colblk))

    out = pl.pallas_call(
        _attn_kernel,
        grid=(s // TQ, batch),
        in_specs=[pl.BlockSpec((None, TQ, w), lambda i, b: (b, i, 0)),
                  kv_spec(1, 2), kv_spec(1, 1), kv_spec(1, 0),
                  kv_spec(2, 2), kv_spec(2, 1), kv_spec(2, 0),
                  pl.BlockSpec((None,) + bias.shape[1:], lambda i, b: (jnp.minimum(i, nvar - 1), 0, 0, 0))],
        out_specs=pl.BlockSpec((None, TQ, w), lambda i, b: (b, i, 0)),
        out_shape=jax.ShapeDtypeStruct((batch, s, w), BF16),
        compiler_params=_params(2),
        name="attention",
    )(qkv3, qkv3, qkv3, qkv3, qkv3, qkv3, qkv3, bias)
    return out.reshape(t, w)


def _proj_route_kernel(a_ref, b_ref, r_ref, wa_ref, wb_ref, g_ref, wt_ref,
                       h_ref, hn_ref, meta_ref, metat_ref, cnt_ref, carry_ref):
    tm = a_ref.shape[0]
    ts = min(TM_ROUTE_SUB, tm)

    @pl.when(pl.program_id(0) == 0)
    def _():
        carry_ref[...] = jnp.zeros_like(carry_ref)

    wt = wt_ref[...]
    ridx = lax.broadcasted_iota(jnp.int32, (ROUTE_ROWS, ts), 0).astype(F32)
    r8 = lax.broadcasted_iota(jnp.int32, (SUBLANES, ts), 0)
    rr = lax.broadcasted_iota(jnp.int32, (ts, ts), 0)
    cc = lax.broadcasted_iota(jnp.int32, (ts, ts), 1)
    utri = jnp.where(rr < cc, 1.0, 0.0).astype(BF16)
    neg = -jnp.inf
    far = float(LANES)
    carry = carry_ref[...]

    all_logits = []
    for sub in range(tm // ts):
        rows = pl.ds(sub * ts, ts)
        h = r_ref[rows, :] + (_dot(a_ref[rows, :], wa_ref[...]) + _dot(b_ref[rows, :], wb_ref[...]))
        h_ref[rows, :] = h
        hn = _rms(h, g_ref[...])
        _to_slab(hn_ref.at[pl.ds(sub * ts * SUBLANES, ts * SUBLANES), :], hn)

        hi, lo = _split_bf16(hn)
        p_hi = _dot_nt(wt, hi)
        p_lo = _dot_nt(wt[:LANES], lo)
        all_logits.append((p_hi[:ROUTE_ROWS] + p_hi[LANES:LANES + ROUTE_ROWS]) + p_lo[:ROUTE_ROWS])

    for sub, logits in enumerate(all_logits):
        rows = pl.ds(sub * ts, ts)

        def first_max(mask, logits=logits):
            v = jnp.max(jnp.where(mask, logits, neg), axis=0, keepdims=True)
            idx = jnp.min(jnp.where(mask & (logits == v), ridx, far), axis=0, keepdims=True)
            return v, idx

        gmask = ridx < N_GROUPS
        gmax, gidx = first_max(gmask)
        g_gate = 1.0 / jnp.sum(jnp.where(gmask, jnp.exp(logits - gmax), 0.0), axis=0, keepdims=True)
        e_lo = N_GROUPS + EXPERTS_PER_GROUP * gidx
        emask = (ridx >= e_lo) & (ridx < e_lo + EXPERTS_PER_GROUP)
        v1, i1 = first_max(emask)
        v2, i2 = first_max(emask & (ridx != i1))
        tt = jnp.exp(v2 - v1)
        w1 = g_gate / (1.0 + tt)
        w2 = g_gate * tt / (1.0 + tt)

        sel1 = ridx == i1
        sel2 = ridx == i2
        onehot = jnp.where(sel1 | sel2, 1.0, 0.0)
        before = _dot(onehot.astype(BF16), utri) + carry[:, 0:1]
        rank1 = jnp.sum(jnp.where(sel1, before, 0.0), axis=0, keepdims=True)
        rank2 = jnp.sum(jnp.where(sel2, before, 0.0), axis=0, keepdims=True)
        carry = carry + jnp.sum(onehot, axis=1, keepdims=True)

        mt = jnp.zeros((SUBLANES, ts), F32)
        for c, val in enumerate((i1 - N_GROUPS, i2 - N_GROUPS, rank1, rank2, w1, w2)):
            mt = jnp.where(r8 == c, val, mt)
        metat_ref[:, rows] = mt
        meta_ref[rows, :] = jnp.concatenate([mt, jnp.zeros((LANES - SUBLANES, ts), F32)], axis=0).T

    carry_ref[...] = carry
    cnt_ref[...] = carry


def _proj_route(a, acol, b, bcol, resid, wa, wb, g, wt):
    t, d = resid.shape
    kw = wa.shape[0]
    tm = min(TM_ROUTE, t)
    return pl.pallas_call(
        _proj_route_kernel,
        grid=(t // tm,),
        in_specs=[pl.BlockSpec((tm, kw), lambda i: (i, acol)), pl.BlockSpec((tm, kw), lambda i: (i, bcol)),
                  pl.BlockSpec((tm, d), lambda i: (i, 0)),
                  _const_spec(wa.shape), _const_spec(wb.shape), _const_spec(g.shape), _const_spec(wt.shape)],
        out_specs=[pl.BlockSpec((tm, d), lambda i: (i, 0)),
                   pl.BlockSpec((tm * SUBLANES, LANES), lambda i: (i, 0)),
                   pl.BlockSpec((tm, LANES), lambda i: (i, 0)),
                   pl.BlockSpec((SUBLANES, tm), lambda i: (0, i)),
                   pl.BlockSpec((ROUTE_ROWS, LANES), lambda i: (0, 0))],
        out_shape=[jax.ShapeDtypeStruct((t, d), F32),
                   jax.ShapeDtypeStruct((t * SUBLANES, LANES), F32),
                   jax.ShapeDtypeStruct((t, LANES), F32),
                   jax.ShapeDtypeStruct((SUBLANES, t), F32),
                   jax.ShapeDtypeStruct((ROUTE_ROWS, LANES), F32)],
        scratch_shapes=[pltpu.VMEM((ROUTE_ROWS, LANES), F32)],
        compiler_params=_params(1),
        name="proj_route",
    )(a, b, resid, wa, wb, g, wt)


def _plan(meta_t, cnt, tb):
    t = meta_t.shape[1]
    counts = cnt[N_GROUPS:N_GROUPS + N_EXPERTS, 0].astype(jnp.int32)
    padded = (counts + tb - 1) // tb * tb
    pad_end = jnp.cumsum(padded)
    pad_start = pad_end - padded
    dest = []
    for k in range(TOP_K):
        expert = meta_t[k].astype(jnp.int32)
        start = jnp.zeros((t,), jnp.int32)
        for e in range(N_EXPERTS):
            start = jnp.where(expert == e, pad_start[e], start)
        dest.append(start + meta_t[TOP_K + k].astype(jnp.int32))
    dest = jnp.concatenate(dest)
    nblk = -(-(t * TOP_K + N_EXPERTS * (tb - 1)) // tb)
    blk_start = jnp.arange(nblk, dtype=jnp.int32) * tb
    blk_expert = jnp.minimum(jnp.sum((pad_end[None, :] <= blk_start[:, None]).astype(jnp.int32), axis=1),
                             N_EXPERTS - 1)
    nact = (pad_end[-1:] // tb).astype(jnp.int32)
    return dest, pad_start + counts, pad_end, blk_expert, nact, nblk


def _dispatch_kernel(dest_ref, plo_ref, phi_ref, nact_ref, hn_ref, xs_ref, zero_ref, sem, zsem):
    tm = hn_ref.shape[0] // SUBLANES
    t = dest_ref.shape[0] // TOP_K
    tb = zero_ref.shape[0] // SUBLANES
    nblk = xs_ref.shape[0] // zero_ref.shape[0]
    i = pl.program_id(0)

    def zero_copy(row, nrows):
        return pltpu.make_async_copy(
            zero_ref.at[pl.ds(0, nrows * SUBLANES), :],
            xs_ref.at[pl.ds(pl.multiple_of(row * SUBLANES, SUBLANES), nrows * SUBLANES), :], zsem)

    def zero_fill(start):
        def go(cp):
            cp.start() if start else cp.wait()

        def per_expert(e, c):
            off = plo_ref[e]
            n = phi_ref[e] - off
            bit = tb // 2
            while bit:
                pl.when((n & bit) != 0)(functools.partial(lambda o, b: go(zero_copy(o, b)), off, bit))
                off = off + (n & bit)
                bit //= 2
            return c

        def per_block(b, c):
            go(zero_copy(b * tb, tb))
            return c

        lax.fori_loop(0, N_EXPERTS, per_expert, 0)
        lax.fori_loop(nact_ref[0], nblk, per_block, 0)

    @pl.when(i == 0)
    def _():
        zero_ref[...] = jnp.zeros_like(zero_ref)
        zero_fill(True)

    def copy(r, k):
        d = dest_ref[k * t + i * tm + r]
        return pltpu.make_async_copy(
            hn_ref.at[pl.ds(pl.multiple_of(r * SUBLANES, SUBLANES), SUBLANES), :],
            xs_ref.at[pl.ds(pl.multiple_of(d * SUBLANES, SUBLANES), SUBLANES), :], sem)

    def start(r, c):
        for k in range(TOP_K):
            copy(r, k).start(priority=k)
        return c

    lax.fori_loop(0, tm, start, 0, unroll=DMA_UNROLL)
    for k in range(TOP_K):
        pltpu.make_async_copy(hn_ref, xs_ref.at[pl.ds(0, tm * SUBLANES), :], sem).wait()

    @pl.when(i == 0)
    def _():
        zero_fill(False)


def _dispatch(dest, pad_lo, pad_hi, nact, hn_slab, nblk, tb):
    t = hn_slab.shape[0] // SUBLANES
    tm = min(TM_COMB, t)
    return pl.pallas_call(
        _dispatch_kernel,
        grid_spec=pltpu.PrefetchScalarGridSpec(
            num_scalar_prefetch=4,
            grid=(t // tm,),
            in_specs=[pl.BlockSpec((tm * SUBLANES, LANES), lambda i, *_: (i, 0))],
            out_specs=pl.BlockSpec(memory_space=pl.ANY),
            scratch_shapes=[pltpu.VMEM((tb * SUBLANES, LANES), F32),
                            pltpu.SemaphoreType.DMA(()), pltpu.SemaphoreType.DMA(())]),
        out_shape=jax.ShapeDtypeStruct((nblk * tb * SUBLANES, LANES), F32),
        compiler_params=_params(1),
        name="dispatch",
    )(dest, pad_lo, pad_hi, nact, hn_slab)


def _expert_kernel(be_ref, nact_ref, xs_ref, wg_ref, wu_ref, wd_ref, ys_ref, wg_s, wu_s, wd_s):
    i = pl.program_id(0)
    tb = xs_ref.shape[0] // SUBLANES
    active = i < nact_ref[0]
    changed = (i == 0) | (be_ref[i] != be_ref[jnp.maximum(i - 1, 0)])

    @pl.when(active & changed)
    def _():
        wg_s[...] = wg_ref[...].astype(BF16)
        wu_s[...] = wu_ref[...].astype(BF16)
        wd_s[...] = wd_ref[...].astype(BF16)

    @pl.when(active)
    def _():
        x = _from_slab(xs_ref, tb).astype(BF16)
        gate = _dot(x, wg_s[...])
        up = _dot(x, wu_s[...])
        hid = (gate * jax.nn.sigmoid(gate) * up).astype(BF16)
        _to_slab(ys_ref, _dot(hid, wd_s[...]))

    @pl.when(jnp.logical_not(active))
    def _():
        ys_ref[...] = jnp.zeros_like(ys_ref)


def _experts(blk_expert, nact, xs, wg, wu, wd, layer, tb):
    nblk = blk_expert.shape[0]
    _, _, d, de = wg.shape

    def row_map(i, be, na):
        return (jnp.minimum(i, na[0] - 1), 0)

    def w_map(i, be, na):
        return (layer, be[jnp.minimum(i, na[0] - 1)], 0, 0)

    return pl.pallas_call(
        _expert_kernel,
        grid_spec=pltpu.PrefetchScalarGridSpec(
            num_scalar_prefetch=2,
            grid=(nblk,),
            in_specs=[pl.BlockSpec((tb * SUBLANES, LANES), row_map),
                      pl.BlockSpec((None, None, d, de), w_map), pl.BlockSpec((None, None, d, de), w_map),
                      pl.BlockSpec((None, None, de, d), w_map)],
            out_specs=pl.BlockSpec((tb * SUBLANES, LANES), lambda i, be, na: (i, 0)),
            scratch_shapes=[pltpu.VMEM((d, de), BF16), pltpu.VMEM((d, de), BF16), pltpu.VMEM((de, d), BF16)]),
        out_shape=jax.ShapeDtypeStruct(xs.shape, F32),
        compiler_params=_params(1),
        name="experts",
    )(blk_expert, nact, xs, wg, wu, wd)


def _combine(dest_ref, ys_ref, meta_ref, buf_ref, sem_ref, tm):
    i = pl.program_id(0)
    n = pl.num_programs(0)
    slot = i % COMB_SLOTS
    ahead = COMB_SLOTS - 1
    t = dest_ref.shape[0] // TOP_K

    def copy(tile, sl, r, k):
        d = dest_ref[k * t + tile * tm + r]
        row = r * SUBLANES if isinstance(r, int) else pl.multiple_of(r * SUBLANES, SUBLANES)
        return pltpu.make_async_copy(
            ys_ref.at[pl.ds(pl.multiple_of(d * SUBLANES, SUBLANES), SUBLANES), :],
            buf_ref.at[sl, k, pl.ds(row, SUBLANES), :],
            sem_ref.at[sl])

    def start_tile(tile, sl):
        def body(r, c):
            for k in range(TOP_K):
                copy(tile, sl, r, k).start(priority=k)
            return c
        lax.fori_loop(0, tm, body, 0, unroll=DMA_UNROLL)

    def wait_tile(sl):
        for k in range(TOP_K):
            pltpu.make_async_copy(ys_ref.at[pl.ds(0, tm * SUBLANES), :], buf_ref.at[sl, k], sem_ref.at[sl]).wait()

    @pl.when(i == 0)
    def _():
        for a in range(ahead):
            start_tile(jnp.minimum(a, n - 1), a)

    wait_tile(slot)

    nxt = jnp.minimum(i + ahead, n - 1)
    nslot = (i + ahead) % COMB_SLOTS

    def issue(part, nparts):
        rows = tm // nparts
        for r in range(part * rows, (part + 1) * rows):
            for k in range(TOP_K):
                copy(nxt, nslot, r, k).start(priority=k)

    @pl.when(i == n - 1)
    def _():
        for a in range(1, ahead):
            wait_tile((i + a) % COMB_SLOTS)

    y = None
    for k in range(TOP_K):
        gate = meta_ref[:, 2 * TOP_K + k:2 * TOP_K + k + 1]
        term = _from_slab(buf_ref.at[slot, k], tm) * gate
        y = term if y is None else y + term
    return y, issue


def _combine_drain(buf_ref, sem_ref, ys_ref, tm):
    i = pl.program_id(0)

    @pl.when(i == pl.num_programs(0) - 1)
    def _():
        sl = (i + COMB_SLOTS - 1) % COMB_SLOTS
        for k in range(TOP_K):
            pltpu.make_async_copy(ys_ref.at[pl.ds(0, tm * SUBLANES), :], buf_ref.at[sl, k], sem_ref.at[sl]).wait()


def _combine_scratch(tm):
    return [pltpu.VMEM((COMB_SLOTS, TOP_K, tm * SUBLANES, LANES), F32), pltpu.SemaphoreType.DMA((COMB_SLOTS,))]


def _comb_proj1_kernel(dest_ref, h_ref, meta_ref, ys_ref, g_ref, w_ref, lbl_ref,
                       h2_ref, qs_ref, lf_ref, iv_ref, gs_ref, buf_ref, sem_ref, *, lb_rows):
    tm = h_ref.shape[0]
    y, issue = _combine(dest_ref, ys_ref, meta_ref, buf_ref, sem_ref, tm)
    h2 = h_ref[...] + y
    h2_ref[...] = h2
    hn = _rms(h2, g_ref[...]).astype(BF16)
    d = h2.shape[1]
    lbl = lbl_ref[...]
    ex = jnp.exp(lbl - jnp.max(lbl, axis=0, keepdims=True))
    lb = jnp.sum(ex[:lb_rows], axis=0, keepdims=True) / jnp.sum(ex, axis=0, keepdims=True)

    def col(c):
        return _dot(hn, w_ref[:, d * c:d * (c + 1)])

    q = col(0)
    issue(0, 4)
    fz = col(1)
    issue(1, 4)
    iv = col(2)
    issue(2, 4)
    gz = col(3)
    issue(3, 4)
    qs_ref[...] = (q * jax.nn.sigmoid(q)).astype(BF16)
    lf_ref[...] = jnp.log(lb + (1.0 - lb) * jax.nn.sigmoid(fz))
    iv_ref[...] = iv.astype(BF16)
    gs_ref[...] = (gz * jax.nn.sigmoid(gz)).astype(BF16)
    _combine_drain(buf_ref, sem_ref, ys_ref, tm)


def _comb_proj1(dest, h, meta, ys, g, w, lb_logits, lb_rows):
    t, d = h.shape
    tm = min(TM_COMB, t)
    row = lambda i, dd: (i, 0)
    const = lambda shape: pl.BlockSpec(shape, lambda i, dd: (0,) * len(shape))
    return pl.pallas_call(
        functools.partial(_comb_proj1_kernel, lb_rows=lb_rows),
        grid_spec=pltpu.PrefetchScalarGridSpec(
            num_scalar_prefetch=1,
            grid=(t // tm,),
            in_specs=[pl.BlockSpec((tm, d), row), pl.BlockSpec((tm, LANES), row),
                      pl.BlockSpec(memory_space=pl.ANY),
                      const(g.shape), const(w.shape), const(lb_logits.shape)],
            out_specs=[pl.BlockSpec((tm, d), row)] * 5,
            scratch_shapes=_combine_scratch(tm)),
        out_shape=[jax.ShapeDtypeStruct((t, d), F32), jax.ShapeDtypeStruct((t, d), BF16),
                   jax.ShapeDtypeStruct((t, d), F32), jax.ShapeDtypeStruct((t, d), BF16),
                   jax.ShapeDtypeStruct((t, d), BF16)],
        compiler_params=_params(1),
        name="comb_proj1",
    )(dest, h, meta, ys, g, w, lb_logits)


def _comb_final_kernel(dest_ref, h_ref, meta_ref, ys_ref, g_ref, o_ref, buf_ref, sem_ref):
    tm = h_ref.shape[0]
    y, issue = _combine(dest_ref, ys_ref, meta_ref, buf_ref, sem_ref, tm)
    issue(0, 1)
    o_ref[...] = _rms(h_ref[...] + y, g_ref[...])
    _combine_drain(buf_ref, sem_ref, ys_ref, tm)


def _comb_final(dest, h, meta, ys, g):
    t, d = h.shape
    tm = min(TM_COMB, t)
    row = lambda i, dd: (i, 0)
    return pl.pallas_call(
        _comb_final_kernel,
        grid_spec=pltpu.PrefetchScalarGridSpec(
            num_scalar_prefetch=1,
            grid=(t // tm,),
            in_specs=[pl.BlockSpec((tm, d), row), pl.BlockSpec((tm, LANES), row),
                      pl.BlockSpec(memory_space=pl.ANY),
                      pl.BlockSpec(g.shape, lambda i, dd: (0, 0))],
            out_specs=pl.BlockSpec((tm, d), row),
            scratch_shapes=_combine_scratch(tm)),
        out_shape=jax.ShapeDtypeStruct((t, d), F32),
        compiler_params=_params(1),
        name="comb_final",
    )(dest, h, meta, ys, g)


def _hgrn_kernel(qs_ref, lf_ref, iv_ref, gs_ref, ng_ref, o_ref, st_ref, d1_s, t16_s, *, dk):
    th = qs_ref.shape[0]
    nsub = CHUNK // SUB

    @pl.when(pl.program_id(1) == 0)
    def _():
        st_ref[...] = jnp.zeros_like(st_ref)

    rr = lax.broadcasted_iota(jnp.int32, (th, th), 0)
    cc = lax.broadcasted_iota(jnp.int32, (th, th), 1)
    same_sub = (rr // SUB) == (cc // SUB)
    same_chunk = (rr // CHUNK) == (cc // CHUNK)
    m_diag = same_sub & (cc <= rr)
    dsub = rr // SUB - cc // SUB
    m_off = [same_chunk & (dsub == dd) for dd in range(1, nsub)]

    hi, lo = _split_bf16(lf_ref[...])
    tri = jnp.where(m_diag, 1.0, 0.0).astype(BF16)
    ones = jnp.where(same_sub, 1.0, 0.0).astype(BF16)
    d1_s[...] = _dot(tri, hi) + _dot(tri, lo)
    t16_s[...] = _dot(ones, hi) + _dot(ones, lo)
    rrow = lax.broadcasted_iota(jnp.int32, (th, dk), 0)
    rsub = (rrow // SUB) % nsub
    rchunk = rrow // CHUNK
    nch = th // CHUNK

    def prepare(hd):
        ls = pl.ds(pl.multiple_of(hd * dk, dk), dk)
        d1 = d1_s[:, ls]
        t16 = t16_s[:, ls]
        q = qs_ref[:, ls].astype(F32)
        kt = 1.0 - jnp.exp(lf_ref[:, ls])
        suf = t16 - d1
        kx = (kt * jnp.exp(suf)).astype(BF16)
        qv = [q * jnp.exp(-suf)]
        acc = d1
        tail = suf
        for dd in range(1, nsub):
            qv.append(q * jnp.exp(acc))
            acc = acc + jnp.where(rsub >= dd, pltpu.roll(t16, SUB * dd, 0), 0.0)
            tail = tail + jnp.where(rsub < nsub - dd, pltpu.roll(t16, th - SUB * dd, 0), 0.0)
        b = acc
        qb = q * jnp.exp(b)
        kend = kt * jnp.exp(tail)
        zero = jnp.zeros_like(q)
        kend_x = jnp.concatenate([jnp.where(rchunk == c, kend, zero) for c in range(nch)], axis=1).astype(BF16)
        qb_x = jnp.concatenate([jnp.where(rchunk == c, qb, zero) for c in range(nch)], axis=1).astype(BF16)
        decs = [jnp.exp(b[CHUNK * (c + 1) - 1:CHUNK * (c + 1), :]) for c in range(nch)]
        return ls, jnp.concatenate(qv, axis=0).astype(BF16), kx, kend_x, qb_x, decs

    def group(gi, carry):
        heads = [gi * HGRN_GROUP + u for u in range(HGRN_GROUP)]
        prep = [prepare(hd) for hd in heads]
        a4s = [_dot_nt(p[1], p[2]) for p in prep]
        incs = [_dot_tn(iv_ref[:, p[0]], p[3]) for p in prep]
        o_intras = []
        for p, a4 in zip(prep, a4s):
            att = jnp.where(m_diag, a4[:th], 0.0)
            for dd in range(1, nsub):
                att = jnp.where(m_off[dd - 1], a4[dd * th:(dd + 1) * th], att)
            o_intras.append(_dot(att.astype(BF16), iv_ref[:, p[0]]))
        for hd, p, inc, o_intra in zip(heads, prep, incs, o_intras):
            ls, decs = p[0], p[5]
            st = st_ref[hd]
            starts = []
            for c in range(nch):
                starts.append(st)
                st = st * decs[c] + inc[:, dk * c:dk * (c + 1)]
            st_ref[hd] = st
            oh = o_intra + _dot_nt(p[4], jnp.concatenate(starts, axis=1).astype(BF16))
            on = oh * lax.rsqrt(jnp.mean(oh * oh, axis=-1, keepdims=True) + EPS)
            o_ref[:, ls] = (on * ng_ref[:, ls] * gs_ref[:, ls].astype(F32)).astype(BF16)
        return carry

    lax.fori_loop(0, qs_ref.shape[1] // dk // HGRN_GROUP, group, 0)


def _hgrn(qs, lf, iv, gs, ng, batch, dk):
    t, d = qs.shape
    s = t // batch
    th = min(TH, s)
    blk = pl.BlockSpec((None, th, d), lambda b, j: (b, j, 0))
    r3 = lambda a: a.reshape(batch, s, d)
    out = pl.pallas_call(
        functools.partial(_hgrn_kernel, dk=dk),
        grid=(batch, s // th),
        in_specs=[blk, blk, blk, blk, _const_spec(ng.shape)],
        out_specs=blk,
        out_shape=jax.ShapeDtypeStruct((batch, s, d), BF16),
        scratch_shapes=[pltpu.VMEM((d // dk, dk, dk), F32), pltpu.VMEM((th, d), F32), pltpu.VMEM((th, d), F32)],
        compiler_params=_params(2),
        name="hgrn2",
    )(r3(qs), r3(lf), r3(iv), r3(gs), ng)
    return out.reshape(t, d)


def _block_diag(w):
    n, c, dd = w.shape
    eye = jnp.eye(n, dtype=w.dtype)
    return (eye[:, None, :, None] * w[:, :, None, :]).reshape(n * c, n * dd)


def _router_weights(wg, we):
    d = wg.shape[0]
    w = jnp.concatenate([wg, we, jnp.zeros((d, LANES - wg.shape[1] - we.shape[1]), F32)], axis=1).T
    hi = w.astype(BF16)
    return jnp.concatenate([hi, (w - hi.astype(F32)).astype(BF16)], axis=0)


def _moe(layer, meta_t, cnt, hn_slab, moe_w_gate, moe_w_up, moe_w_down):
    dest, pad_lo, pad_hi, blk_expert, nact, nblk = _plan(meta_t, cnt, TB_EXPERT)
    xs = _dispatch(dest, pad_lo, pad_hi, nact, hn_slab, nblk, TB_EXPERT)
    ys = _experts(blk_expert, nact, xs, moe_w_gate, moe_w_up, moe_w_down, layer, TB_EXPERT)
    return dest, ys


def kernel(x, norm_mix_g, norm_ffn_g, norm_final_g, ab_w_in, ab_conv_w, ab_conv_b, rg_w_a, rg_b_a, rg_w_x, rg_b_x, rg_lambda, attn_rel_bias, ab_w_out, c_w_in, c_lb_logits, c_norm_g, c_w_out, moe_router_group, moe_router_expert, moe_w_gate, moe_w_up, moe_w_down):
    batch, seq, d = x.shape
    t = batch * seq
    xt = x.reshape(t, d)
    row = lambda v: v.reshape(1, -1)

    aw = ab_conv_w.shape[2]
    xg, qkv = _ln_proj0(xt, row(norm_mix_g[0]), ab_w_in[0].astype(BF16), 2 * aw)
    ya = _rglru(xg, ab_conv_w[0], row(ab_conv_b[0]),
                _block_diag(rg_w_a[0]).astype(BF16), _block_diag(rg_w_x[0]).astype(BF16),
                row(rg_b_a[0]), row(rg_b_x[0]), row(rg_lambda[0]), batch)
    tk = TQ + LEFT_CHUNKS * CHUNK
    dist = jnp.clip(tk - jnp.arange(TQ + tk), -REL_CLIP, REL_CLIP) + REL_CLIP
    bias = _attn_bias(attn_rel_bias[0][:, None, dist])
    yb = _attention(qkv, bias, batch)
    wo = ab_w_out[0].astype(BF16)
    wt = _router_weights(moe_router_group[0], moe_router_expert[0])
    h1, hn1, meta1, metat1, cnt1 = _proj_route(ya, 0, yb, 0, xt, wo[:aw], wo[aw:], row(norm_ffn_g[0]), wt)
    dest1, ys1 = _moe(0, metat1, cnt1, hn1, moe_w_gate, moe_w_up, moe_w_down)

    dk = c_norm_g.shape[1] // 8
    h2, qs, lf, iv, gs = _comb_proj1(dest1, h1, meta1, ys1, row(norm_mix_g[1]), c_w_in[0].astype(BF16),
                                     c_lb_logits, 1)
    om = _hgrn(qs, lf, iv, gs, row(c_norm_g[0]), batch, dk)
    wo = c_w_out[0].astype(BF16)
    half = wo.shape[0] // 2
    wt = _router_weights(moe_router_group[1], moe_router_expert[1])
    h3, hn3, meta3, metat3, cnt3 = _proj_route(om, 0, om, 1, h2, wo[:half], wo[half:], row(norm_ffn_g[1]), wt)
    dest3, ys3 = _moe(1, metat3, cnt3, hn3, moe_w_gate, moe_w_up, moe_w_down)

    out = _comb_final(dest3, h3, meta3, ys3, row(norm_final_g))
    return out.reshape(batch, seq, d)
```

```python
import functools

import jax
import jax.numpy as jnp
from jax import lax
from jax.experimental import pallas as pl
from jax.experimental.pallas import tpu as pltpu

F32 = jnp.float32
BF16 = jnp.bfloat16

EPS = 1e-6
LOG2E = 1.4426950408889634
RG_C = 8.0
CHUNK = 64
LEFT_CHUNKS = 8
REL_CLIP = 256
N_GROUPS = 4
EXPERTS_PER_GROUP = 8
N_EXPERTS = N_GROUPS * EXPERTS_PER_GROUP
TOP_K = 2
ROUTE_ROWS = 40

LANES = 128
SUBLANES = 8
SUB = 16

TM_PROJ = 512
TS_LRU = 512
TQ = 256
ATTN_GROUP = 4
TM_ROUTE = 1024
TM_ROUTE_SUB = 256
TB_EXPERT = 512
TB_SUB = 256
TM_COMB = 256
TH = 256
HGRN_GROUP = 4
DMA_UNROLL = 8
COMB_SLOTS = 3
VMEM_MB = 48


def _params(n_axes, vmem_mb=VMEM_MB):
    return pltpu.CompilerParams(dimension_semantics=("arbitrary",) * n_axes,
                                vmem_limit_bytes=vmem_mb * 1024 * 1024)


def _const_spec(shape):
    nd = len(shape)
    return pl.BlockSpec(shape, lambda *_: (0,) * nd)


def _rms(x, g):
    return x * lax.rsqrt(jnp.mean(x * x, axis=-1, keepdims=True) + EPS) * g


def _dot(a, b):
    return jnp.dot(a, b, preferred_element_type=F32)


def _dot_nt(a, b):
    return lax.dot_general(a, b, (((1,), (1,)), ((), ())), preferred_element_type=F32)


def _dot_tn(a, b):
    return lax.dot_general(a, b, (((0,), (0,)), ((), ())), preferred_element_type=F32)


def _split_bf16(x):
    hi = x.astype(BF16)
    lo = (x - hi.astype(F32)).astype(BF16)
    return hi, lo


def _to_slab(ref, val):
    m = val.shape[0]
    for j in range(val.shape[1] // LANES):
        ref[pl.ds(j, m, stride=SUBLANES), :] = val[:, LANES * j:LANES * (j + 1)]


def _from_slab(ref, m):
    return jnp.concatenate([ref[pl.ds(j, m, stride=SUBLANES), :] for j in range(SUBLANES)], axis=1)


def _ln_proj0_kernel(x_ref, g_ref, w_ref, xg_ref, qkv_ref):
    hn = _rms(x_ref[...], g_ref[...]).astype(BF16)
    z = _dot(hn, w_ref[...])
    aw = xg_ref.shape[1]
    bw = (z.shape[1] - aw) // 3
    xg_ref[...] = z[:, :aw]
    qkv_ref[:, :bw] = (z[:, aw:aw + bw] * (CHUNK ** -0.5 * LOG2E)).astype(BF16)
    qkv_ref[:, bw:] = z[:, aw + bw:].astype(BF16)


def _ln_proj0(x, g, w, aw):
    t, d = x.shape
    n = w.shape[1]
    tm = min(TM_PROJ, t)
    return pl.pallas_call(
        _ln_proj0_kernel,
        grid=(t // tm,),
        in_specs=[pl.BlockSpec((tm, d), lambda i: (i, 0)), _const_spec(g.shape), _const_spec(w.shape)],
        out_specs=[pl.BlockSpec((tm, aw), lambda i: (i, 0)), pl.BlockSpec((tm, n - aw), lambda i: (i, 0))],
        out_shape=[jax.ShapeDtypeStruct((t, aw), F32), jax.ShapeDtypeStruct((t, n - aw), BF16)],
        compiler_params=_params(1),
        name="ln_proj0",
    )(x, g, w)


def _rglru_kernel(xg_ref, cw_ref, cb_ref, wa_ref, wx_ref, ba_ref, bx_ref, lam_ref, ya_ref,
                  xbuf, h_ref, a_s, u_s):
    ts = xg_ref.shape[0]
    w = ya_ref.shape[1]

    @pl.when(pl.program_id(1) == 0)
    def _():
        xbuf[pl.ds(0, SUBLANES), :] = jnp.zeros((SUBLANES, w), F32)
        h_ref[...] = jnp.zeros_like(h_ref)

    xbuf[pl.ds(SUBLANES, ts), :] = xg_ref[:, :w]
    nk = cw_ref.shape[0]
    y = cb_ref[...]
    for k in range(nk):
        y = y + cw_ref[nk - 1 - k:nk - k, :] * xbuf[pl.ds(SUBLANES - k, ts), :]
    xbuf[pl.ds(0, SUBLANES), :] = xbuf[pl.ds(ts, SUBLANES), :]

    yb = y.astype(BF16)
    rg = jax.nn.sigmoid(_dot(yb, wa_ref[...]) + ba_ref[...])
    ig = jax.nn.sigmoid(_dot(yb, wx_ref[...]) + bx_ref[...])
    lam = lam_ref[...]
    log_sig = jnp.minimum(lam, 0.0) - jnp.log1p(jnp.exp(-jnp.abs(lam)))
    log_a = RG_C * rg * log_sig
    a = jnp.exp(log_a)
    m = 1.0 - a * a
    u = jnp.where(m > 0.0, m * lax.rsqrt(m), 0.0) * (ig * y)

    grp = (ts // SUBLANES, SUBLANES, w)
    a = a.reshape(grp)
    u = u.reshape(grp)
    rowm = lax.broadcasted_iota(jnp.int32, grp, 1)
    for s in (1, 2, 4):
        keep = rowm >= s
        a_sh = jnp.where(keep, pltpu.roll(a, s, 1), 1.0)
        u_sh = jnp.where(keep, pltpu.roll(u, s, 1), 0.0)
        u = a * u_sh + u
        a = a * a_sh
    a_s[...] = a.reshape(ts, w)
    u_s[...] = u.reshape(ts, w)

    def group(gi, h):
        off = pl.multiple_of(gi * SUBLANES, SUBLANES)
        hg = a_s[pl.ds(off, SUBLANES), :] * h + u_s[pl.ds(off, SUBLANES), :]
        u_s[pl.ds(off, SUBLANES), :] = hg
        return jnp.broadcast_to(hg[SUBLANES - 1:SUBLANES, :], hg.shape)

    h_ref[...] = lax.fori_loop(0, ts // SUBLANES, group, h_ref[...])

    ga = xg_ref[:, w:]
    gelu = 0.5 * ga * (1.0 + jnp.tanh(0.7978845608028654 * (ga + 0.044715 * (ga * ga * ga))))
    ya_ref[...] = (u_s[...] * gelu).astype(BF16)


def _rglru(xg, cw, cb, wa, wx, ba, bx, lam, batch):
    t, w2 = xg.shape
    w = w2 // 2
    s = t // batch
    ts = min(TS_LRU, s)
    xg3 = xg.reshape(batch, s, w2)
    small = [cw, cb, wa, wx, ba, bx, lam]
    out = pl.pallas_call(
        _rglru_kernel,
        grid=(batch, s // ts),
        in_specs=[pl.BlockSpec((None, ts, w2), lambda b, j: (b, j, 0))] + [_const_spec(a.shape) for a in small],
        out_specs=pl.BlockSpec((None, ts, w), lambda b, j: (b, j, 0)),
        out_shape=jax.ShapeDtypeStruct((batch, s, w), BF16),
        scratch_shapes=[pltpu.VMEM((ts + SUBLANES, w), F32), pltpu.VMEM((SUBLANES, w), F32),
                        pltpu.VMEM((ts, w), F32), pltpu.VMEM((ts, w), F32)],
        compiler_params=_params(2),
        name="rglru",
    )(xg3, *small)
    return out.reshape(t, w)


def _attn_bias_kernel(base_ref, o_ref):
    tq, tk = o_ref.shape
    nkb = tk // tq
    row = base_ref[...]
    full = pltpu.roll(jnp.broadcast_to(row, (tq, tq + tk)), 0, 1, stride=1, stride_axis=0)
    qc = lax.broadcasted_iota(jnp.int32, (tq, tk), 0) // CHUNK
    col = lax.broadcasted_iota(jnp.int32, (tq, tk), 1)
    kc = col // CHUNK
    first_ok = (nkb - 1 - pl.program_id(0)) * tq
    valid = (kc >= qc) & (kc <= qc + LEFT_CHUNKS) & (col >= first_ok)
    o_ref[...] = jnp.where(valid, full[:, tq:] * LOG2E, -1e30)


def _attn_bias(base):
    h = base.shape[0]
    tk = TQ + LEFT_CHUNKS * CHUNK
    return pl.pallas_call(
        _attn_bias_kernel,
        grid=(tk // TQ, h),
        in_specs=[pl.BlockSpec((None, 1, TQ + tk), lambda v, i: (i, 0, 0))],
        out_specs=pl.BlockSpec((None, None, TQ, tk), lambda v, i: (v, i, 0, 0)),
        out_shape=jax.ShapeDtypeStruct((tk // TQ, h, TQ, tk), F32),
        compiler_params=_params(2),
        name="attn_bias",
    )(base)


def _attn_kernel(q_ref, k0_ref, k1_ref, k2_ref, v0_ref, v1_ref, v2_ref, bias_ref, o_ref):
    tq = q_ref.shape[0]
    half_w = LANES // 2
    lo = lax.broadcasted_iota(jnp.int32, (tq, LANES), 1) < half_w
    lo_k = lax.broadcasted_iota(jnp.int32, (3 * tq, LANES), 1) < half_w
    k_refs = (k0_ref, k1_ref, k2_ref)
    v_refs = (v0_ref, v1_ref, v2_ref)
    npair = q_ref.shape[1] // LANES
    for j0 in range(0, npair, ATTN_GROUP):
        pairs = range(j0, min(j0 + ATTN_GROUP, npair))
        qk, vh = {}, {}
        for j in pairs:
            sl = slice(LANES * j, LANES * (j + 1))
            q2 = q_ref[:, sl]
            kcat = jnp.concatenate([r[:, sl] for r in k_refs], axis=0)
            vcat = jnp.concatenate([r[:, sl] for r in v_refs], axis=0)
            one = jnp.ones_like(vcat)
            zero = jnp.zeros_like(q2)
            vh[j] = (jnp.where(lo_k, vcat, one), jnp.where(lo_k, one, vcat))
            qk[j] = [_dot_nt(jnp.where(lo if half == 0 else jnp.logical_not(lo), q2, zero), kcat)
                     for half in range(2)]
        ps = {}
        for j in pairs:
            ps[j] = []
            for half in range(2):
                s = qk[j][half] + bias_ref[2 * j + half]
                ps[j].append(jnp.exp2(s - jnp.max(s, axis=-1, keepdims=True)).astype(BF16))
        for j in pairs:
            pv = [_dot(ps[j][half], vh[j][half]) for half in range(2)]
            num = jnp.where(lo, pv[0], pv[1])
            den = pltpu.roll(jnp.where(lo, pv[1], pv[0]), half_w, 1)
            o_ref[:, LANES * j:LANES * (j + 1)] = (num / den).astype(BF16)


def _attention(qkv, bias, batch):
    t, w3 = qkv.shape
    w = w3 // 3
    s = t // batch
    qkv3 = qkv.reshape(batch, s, w3)
    nvar = bias.shape[0]

    def kv_spec(colblk, back):
        return pl.BlockSpec((None, TQ, w), lambda i, b: (b, jnp.maximum(i - back, 0), colblk))

    out = pl.pallas_call(
        _attn_kernel,
        grid=(s // TQ, batch),
        in_specs=[pl.BlockSpec((None, TQ, w), lambda i, b: (b, i, 0)),
                  kv_spec(1, 2), kv_spec(1, 1), kv_spec(1, 0),
                  kv_spec(2, 2), kv_spec(2, 1), kv_spec(2, 0),
                  pl.BlockSpec((None,) + bias.shape[1:], lambda i, b: (jnp.minimum(i, nvar - 1), 0, 0, 0))],
        out_specs=pl.BlockSpec((None, TQ, w), lambda i, b: (b, i, 0)),
        out_shape=jax.ShapeDtypeStruct((batch, s, w), BF16),
        compiler_params=_params(2),
        name="attention",
    )(qkv3, qkv3, qkv3, qkv3, qkv3, qkv3, qkv3, bias)
    return out.reshape(t, w)


def _proj_route_kernel(a_ref, b_ref, r_ref, wa_ref, wb_ref, g_ref, wt_ref,
                       h_ref, hn_ref, meta_ref, metat_ref, cnt_ref, carry_ref):
    tm = a_ref.shape[0]
    ts = min(TM_ROUTE_SUB, tm)

    @pl.when(pl.program_id(0) == 0)
    def _():
        carry_ref[...] = jnp.zeros_like(carry_ref)

    wt = wt_ref[...]
    ridx = lax.broadcasted_iota(jnp.int32, (ROUTE_ROWS, ts), 0).astype(F32)
    r8 = lax.broadcasted_iota(jnp.int32, (SUBLANES, ts), 0)
    rr = lax.broadcasted_iota(jnp.int32, (ts, ts), 0)
    cc = lax.broadcasted_iota(jnp.int32, (ts, ts), 1)
    utri = jnp.where(rr < cc, 1.0, 0.0).astype(BF16)
    neg = -jnp.inf
    far = float(LANES)
    carry = carry_ref[...]

    all_logits = []
    for sub in range(tm // ts):
        rows = pl.ds(sub * ts, ts)
        h = r_ref[rows, :] + (_dot(a_ref[rows, :], wa_ref[...]) + _dot(b_ref[rows, :], wb_ref[...]))
        h_ref[rows, :] = h
        hn = _rms(h, g_ref[...])
        _to_slab(hn_ref.at[pl.ds(sub * ts * SUBLANES, ts * SUBLANES), :], hn)

        hi, lo = _split_bf16(hn)
        p_hi = _dot_nt(wt, hi)
        p_lo = _dot_nt(wt[:LANES], lo)
        all_logits.append((p_hi[:ROUTE_ROWS] + p_hi[LANES:LANES + ROUTE_ROWS]) + p_lo[:ROUTE_ROWS])

    for sub, logits in enumerate(all_logits):
        rows = pl.ds(sub * ts, ts)

        def first_max(mask, logits=logits):
            v = jnp.max(jnp.where(mask, logits, neg), axis=0, keepdims=True)
            idx = jnp.min(jnp.where(mask & (logits == v), ridx, far), axis=0, keepdims=True)
            return v, idx

        gmask = ridx < N_GROUPS
        gmax, gidx = first_max(gmask)
        g_gate = 1.0 / jnp.sum(jnp.where(gmask, jnp.exp(logits - gmax), 0.0), axis=0, keepdims=True)
        e_lo = N_GROUPS + EXPERTS_PER_GROUP * gidx
        emask = (ridx >= e_lo) & (ridx < e_lo + EXPERTS_PER_GROUP)
        v1, i1 = first_max(emask)
        v2, i2 = first_max(emask & (ridx != i1))
        tt = jnp.exp(v2 - v1)
        w1 = g_gate / (1.0 + tt)
        w2 = g_gate * tt / (1.0 + tt)

        sel1 = ridx == i1
        sel2 = ridx == i2
        onehot = jnp.where(sel1 | sel2, 1.0, 0.0)
        before = _dot(onehot.astype(BF16), utri) + carry[:, 0:1]
        rank1 = jnp.sum(jnp.where(sel1, before, 0.0), axis=0, keepdims=True)
        rank2 = jnp.sum(jnp.where(sel2, before, 0.0), axis=0, keepdims=True)
        carry = carry + jnp.sum(onehot, axis=1, keepdims=True)

        mt = jnp.zeros((SUBLANES, ts), F32)
        for c, val in enumerate((i1 - N_GROUPS, i2 - N_GROUPS, rank1, rank2, w1, w2)):
            mt = jnp.where(r8 == c, val, mt)
        metat_ref[:, rows] = mt
        meta_ref[rows, :] = jnp.concatenate([mt, jnp.zeros((LANES - SUBLANES, ts), F32)], axis=0).T

    carry_ref[...] = carry
    cnt_ref[...] = carry


def _proj_route(a, acol, b, bcol, resid, wa, wb, g, wt):
    t, d = resid.shape
    kw = wa.shape[0]
    tm = min(TM_ROUTE, t)
    return pl.pallas_call(
        _proj_route_kernel,
        grid=(t // tm,),
        in_specs=[pl.BlockSpec((tm, kw), lambda i: (i, acol)), pl.BlockSpec((tm, kw), lambda i: (i, bcol)),
                  pl.BlockSpec((tm, d), lambda i: (i, 0)),
                  _const_spec(wa.shape), _const_spec(wb.shape), _const_spec(g.shape), _const_spec(wt.shape)],
        out_specs=[pl.BlockSpec((tm, d), lambda i: (i, 0)),
                   pl.BlockSpec((tm * SUBLANES, LANES), lambda i: (i, 0)),
                   pl.BlockSpec((tm, LANES), lambda i: (i, 0)),
                   pl.BlockSpec((SUBLANES, tm), lambda i: (0, i)),
                   pl.BlockSpec((ROUTE_ROWS, LANES), lambda i: (0, 0))],
        out_shape=[jax.ShapeDtypeStruct((t, d), F32),
                   jax.ShapeDtypeStruct((t * SUBLANES, LANES), F32),
                   jax.ShapeDtypeStruct((t, LANES), F32),
                   jax.ShapeDtypeStruct((SUBLANES, t), F32),
                   jax.ShapeDtypeStruct((ROUTE_ROWS, LANES), F32)],
        scratch_shapes=[pltpu.VMEM((ROUTE_ROWS, LANES), F32)],
        compiler_params=_params(1),
        name="proj_route",
    )(a, b, resid, wa, wb, g, wt)


def _plan(meta_t, cnt, tb):
    t = meta_t.shape[1]
    counts = cnt[N_GROUPS:N_GROUPS + N_EXPERTS, 0].astype(jnp.int32)
    padded = (counts + tb - 1) // tb * tb
    pad_end = jnp.cumsum(padded)
    pad_start = pad_end - padded
    dest = []
    for k in range(TOP_K):
        expert = meta_t[k].astype(jnp.int32)
        start = jnp.zeros((t,), jnp.int32)
        for e in range(N_EXPERTS):
            start = jnp.where(expert == e, pad_start[e], start)
        dest.append(start + meta_t[TOP_K + k].astype(jnp.int32))
    dest = jnp.concatenate(dest)
    nblk = -(-(t * TOP_K + N_EXPERTS * (tb - 1)) // tb)
    blk_start = jnp.arange(nblk, dtype=jnp.int32) * tb
    blk_expert = jnp.minimum(jnp.sum((pad_end[None, :] <= blk_start[:, None]).astype(jnp.int32), axis=1),
                             N_EXPERTS - 1)
    nact = (pad_end[-1:] // tb).astype(jnp.int32)
    return dest, pad_start + counts, pad_end, blk_expert, nact, nblk


def _dispatch_kernel(dest_ref, plo_ref, phi_ref, nact_ref, hn_ref, xs_ref, zero_ref, sem, zsem):
    tm = hn_ref.shape[0] // SUBLANES
    t = dest_ref.shape[0] // TOP_K
    tb = zero_ref.shape[0] // SUBLANES
    nblk = xs_ref.shape[0] // zero_ref.shape[0]
    i = pl.program_id(0)

    def zero_copy(row, nrows):
        return pltpu.make_async_copy(
            zero_ref.at[pl.ds(0, nrows * SUBLANES), :],
            xs_ref.at[pl.ds(pl.multiple_of(row * SUBLANES, SUBLANES), nrows * SUBLANES), :], zsem)

    def zero_fill(start):
        def go(cp):
            cp.start() if start else cp.wait()

        def per_expert(e, c):
            off = plo_ref[e]
            n = phi_ref[e] - off
            bit = tb // 2
            while bit:
                pl.when((n & bit) != 0)(functools.partial(lambda o, b: go(zero_copy(o, b)), off, bit))
                off = off + (n & bit)
                bit //= 2
            return c

        def per_block(b, c):
            go(zero_copy(b * tb, tb))
            return c

        lax.fori_loop(0, N_EXPERTS, per_expert, 0)
        lax.fori_loop(nact_ref[0], nblk, per_block, 0)

    @pl.when(i == 0)
    def _():
        zero_ref[...] = jnp.zeros_like(zero_ref)
        zero_fill(True)

    def copy(r, k):
        d = dest_ref[k * t + i * tm + r]
        return pltpu.make_async_copy(
            hn_ref.at[pl.ds(pl.multiple_of(r * SUBLANES, SUBLANES), SUBLANES), :],
            xs_ref.at[pl.ds(pl.multiple_of(d * SUBLANES, SUBLANES), SUBLANES), :], sem)

    def start(r, c):
        for k in range(TOP_K):
            copy(r, k).start(priority=k)
        return c

    lax.fori_loop(0, tm, start, 0, unroll=DMA_UNROLL)
    for k in range(TOP_K):
        pltpu.make_async_copy(hn_ref, xs_ref.at[pl.ds(0, tm * SUBLANES), :], sem).wait()

    @pl.when(i == 0)
    def _():
        zero_fill(False)


def _dispatch(dest, pad_lo, pad_hi, nact, hn_slab, nblk, tb):
    t = hn_slab.shape[0] // SUBLANES
    tm = min(TM_COMB, t)
    return pl.pallas_call(
        _dispatch_kernel,
        grid_spec=pltpu.PrefetchScalarGridSpec(
            num_scalar_prefetch=4,
            grid=(t // tm,),
            in_specs=[pl.BlockSpec((tm * SUBLANES, LANES), lambda i, *_: (i, 0))],
            out_specs=pl.BlockSpec(memory_space=pl.ANY),
            scratch_shapes=[pltpu.VMEM((tb * SUBLANES, LANES), F32),
                            pltpu.SemaphoreType.DMA(()), pltpu.SemaphoreType.DMA(())]),
        out_shape=jax.ShapeDtypeStruct((nblk * tb * SUBLANES, LANES), F32),
        compiler_params=_params(1),
        name="dispatch",
    )(dest, pad_lo, pad_hi, nact, hn_slab)


def _expert_kernel(be_ref, nact_ref, xs_ref, wg_ref, wu_ref, wd_ref, ys_ref, wg_s, wu_s, wd_s):
    i = pl.program_id(0)
    tb = xs_ref.shape[0] // SUBLANES
    active = i < nact_ref[0]
    changed = (i == 0) | (be_ref[i] != be_ref[jnp.maximum(i - 1, 0)])

    @pl.when(active & changed)
    def _():
        wg_s[...] = wg_ref[...].astype(BF16)
        wu_s[...] = wu_ref[...].astype(BF16)
        wd_s[...] = wd_ref[...].astype(BF16)

    @pl.when(active)
    def _():
        ts = min(TB_SUB, tb)
        subs = [pl.ds(s * ts * SUBLANES, ts * SUBLANES) for s in range(tb // ts)]
        xs = [_from_slab(xs_ref.at[sl, :], ts).astype(BF16) for sl in subs]
        gu = [(_dot(x, wg_s[...]), _dot(x, wu_s[...])) for x in xs]
        ys = []
        for gate, up in gu:
            hid = (gate * jax.nn.sigmoid(gate) * up).astype(BF16)
            ys.append(_dot(hid, wd_s[...]))
        for sl, y in zip(subs, ys):
            _to_slab(ys_ref.at[sl, :], y)

    @pl.when(jnp.logical_not(active))
    def _():
        ys_ref[...] = jnp.zeros_like(ys_ref)


def _experts(blk_expert, nact, xs, wg, wu, wd, layer, tb):
    nblk = blk_expert.shape[0]
    _, _, d, de = wg.shape

    def row_map(i, be, na):
        return (jnp.minimum(i, na[0] - 1), 0)

    def w_map(i, be, na):
        return (layer, be[jnp.minimum(i, na[0] - 1)], 0, 0)

    return pl.pallas_call(
        _expert_kernel,
        grid_spec=pltpu.PrefetchScalarGridSpec(
            num_scalar_prefetch=2,
            grid=(nblk,),
            in_specs=[pl.BlockSpec((tb * SUBLANES, LANES), row_map),
                      pl.BlockSpec((None, None, d, de), w_map), pl.BlockSpec((None, None, d, de), w_map),
                      pl.BlockSpec((None, None, de, d), w_map)],
            out_specs=pl.BlockSpec((tb * SUBLANES, LANES), lambda i, be, na: (i, 0)),
            scratch_shapes=[pltpu.VMEM((d, de), BF16), pltpu.VMEM((d, de), BF16), pltpu.VMEM((de, d), BF16)]),
        out_shape=jax.ShapeDtypeStruct(xs.shape, F32),
        compiler_params=_params(1),
        name="experts",
    )(blk_expert, nact, xs, wg, wu, wd)


def _combine(dest_ref, ys_ref, meta_ref, buf_ref, sem_ref, tm):
    i = pl.program_id(0)
    n = pl.num_programs(0)
    slot = i % COMB_SLOTS
    ahead = COMB_SLOTS - 1
    t = dest_ref.shape[0] // TOP_K

    def copy(tile, sl, r, k):
        d = dest_ref[k * t + tile * tm + r]
        row = r * SUBLANES if isinstance(r, int) else pl.multiple_of(r * SUBLANES, SUBLANES)
        return pltpu.make_async_copy(
            ys_ref.at[pl.ds(pl.multiple_of(d * SUBLANES, SUBLANES), SUBLANES), :],
            buf_ref.at[sl, k, pl.ds(row, SUBLANES), :],
            sem_ref.at[sl])

    def start_tile(tile, sl):
        def body(r, c):
            for k in range(TOP_K):
                copy(tile, sl, r, k).start(priority=k)
            return c
        lax.fori_loop(0, tm, body, 0, unroll=DMA_UNROLL)

    def wait_tile(sl):
        for k in range(TOP_K):
            pltpu.make_async_copy(ys_ref.at[pl.ds(0, tm * SUBLANES), :], buf_ref.at[sl, k], sem_ref.at[sl]).wait()

    @pl.when(i == 0)
    def _():
        for a in range(ahead):
            start_tile(jnp.minimum(a, n - 1), a)

    wait_tile(slot)

    nxt = jnp.minimum(i + ahead, n - 1)
    nslot = (i + ahead) % COMB_SLOTS

    def issue(part, nparts):
        rows = tm // nparts
        for r in range(part * rows, (part + 1) * rows):
            for k in range(TOP_K):
                copy(nxt, nslot, r, k).start(priority=k)

    @pl.when(i == n - 1)
    def _():
        for a in range(1, ahead):
            wait_tile((i + a) % COMB_SLOTS)

    y = None
    for k in range(TOP_K):
        gate = meta_ref[:, 2 * TOP_K + k:2 * TOP_K + k + 1]
        term = _from_slab(buf_ref.at[slot, k], tm) * gate
        y = term if y is None else y + term
    return y, issue


def _combine_drain(buf_ref, sem_ref, ys_ref, tm):
    i = pl.program_id(0)

    @pl.when(i == pl.num_programs(0) - 1)
    def _():
        sl = (i + COMB_SLOTS - 1) % COMB_SLOTS
        for k in range(TOP_K):
            pltpu.make_async_copy(ys_ref.at[pl.ds(0, tm * SUBLANES), :], buf_ref.at[sl, k], sem_ref.at[sl]).wait()


def _combine_scratch(tm):
    return [pltpu.VMEM((COMB_SLOTS, TOP_K, tm * SUBLANES, LANES), F32), pltpu.SemaphoreType.DMA((COMB_SLOTS,))]


def _comb_proj1_kernel(dest_ref, h_ref, meta_ref, ys_ref, g_ref, w_ref, lbl_ref,
                       h2_ref, qs_ref, lf_ref, iv_ref, gs_ref, buf_ref, sem_ref, *, lb_rows):
    tm = h_ref.shape[0]
    y, issue = _combine(dest_ref, ys_ref, meta_ref, buf_ref, sem_ref, tm)
    h2 = h_ref[...] + y
    h2_ref[...] = h2
    hn = _rms(h2, g_ref[...]).astype(BF16)
    d = h2.shape[1]
    lbl = lbl_ref[...]
    ex = jnp.exp(lbl - jnp.max(lbl, axis=0, keepdims=True))
    lb = jnp.sum(ex[:lb_rows], axis=0, keepdims=True) / jnp.sum(ex, axis=0, keepdims=True)

    def col(c):
        return _dot(hn, w_ref[:, d * c:d * (c + 1)])

    q = col(0)
    issue(0, 4)
    fz = col(1)
    issue(1, 4)
    iv = col(2)
    issue(2, 4)
    gz = col(3)
    issue(3, 4)
    qs_ref[...] = (q * jax.nn.sigmoid(q)).astype(BF16)
    lf_ref[...] = jnp.log(lb + (1.0 - lb) * jax.nn.sigmoid(fz))
    iv_ref[...] = iv.astype(BF16)
    gs_ref[...] = (gz * jax.nn.sigmoid(gz)).astype(BF16)
    _combine_drain(buf_ref, sem_ref, ys_ref, tm)


def _comb_proj1(dest, h, meta, ys, g, w, lb_logits, lb_rows):
    t, d = h.shape
    tm = min(TM_COMB, t)
    row = lambda i, dd: (i, 0)
    const = lambda shape: pl.BlockSpec(shape, lambda i, dd: (0,) * len(shape))
    return pl.pallas_call(
        functools.partial(_comb_proj1_kernel, lb_rows=lb_rows),
        grid_spec=pltpu.PrefetchScalarGridSpec(
            num_scalar_prefetch=1,
            grid=(t // tm,),
            in_specs=[pl.BlockSpec((tm, d), row), pl.BlockSpec((tm, LANES), row),
                      pl.BlockSpec(memory_space=pl.ANY),
                      const(g.shape), const(w.shape), const(lb_logits.shape)],
            out_specs=[pl.BlockSpec((tm, d), row)] * 5,
            scratch_shapes=_combine_scratch(tm)),
        out_shape=[jax.ShapeDtypeStruct((t, d), F32), jax.ShapeDtypeStruct((t, d), BF16),
                   jax.ShapeDtypeStruct((t, d), F32), jax.ShapeDtypeStruct((t, d), BF16),
                   jax.ShapeDtypeStruct((t, d), BF16)],
        compiler_params=_params(1),
        name="comb_proj1",
    )(dest, h, meta, ys, g, w, lb_logits)


def _comb_final_kernel(dest_ref, h_ref, meta_ref, ys_ref, g_ref, o_ref, buf_ref, sem_ref):
    tm = h_ref.shape[0]
    y, issue = _combine(dest_ref, ys_ref, meta_ref, buf_ref, sem_ref, tm)
    issue(0, 1)
    o_ref[...] = _rms(h_ref[...] + y, g_ref[...])
    _combine_drain(buf_ref, sem_ref, ys_ref, tm)


def _comb_final(dest, h, meta, ys, g):
    t, d = h.shape
    tm = min(TM_COMB, t)
    row = lambda i, dd: (i, 0)
    return pl.pallas_call(
        _comb_final_kernel,
        grid_spec=pltpu.PrefetchScalarGridSpec(
            num_scalar_prefetch=1,
            grid=(t // tm,),
            in_specs=[pl.BlockSpec((tm, d), row), pl.BlockSpec((tm, LANES), row),
                      pl.BlockSpec(memory_space=pl.ANY),
                      pl.BlockSpec(g.shape, lambda i, dd: (0, 0))],
            out_specs=pl.BlockSpec((tm, d), row),
            scratch_shapes=_combine_scratch(tm)),
        out_shape=jax.ShapeDtypeStruct((t, d), F32),
        compiler_params=_params(1),
        name="comb_final",
    )(dest, h, meta, ys, g)


def _hgrn_kernel(qs_ref, lf_ref, iv_ref, gs_ref, ng_ref, o_ref, st_ref, d1_s, t16_s, *, dk):
    th = qs_ref.shape[0]
    nsub = CHUNK // SUB

    @pl.when(pl.program_id(1) == 0)
    def _():
        st_ref[...] = jnp.zeros_like(st_ref)

    rr = lax.broadcasted_iota(jnp.int32, (th, th), 0)
    cc = lax.broadcasted_iota(jnp.int32, (th, th), 1)
    same_sub = (rr // SUB) == (cc // SUB)
    same_chunk = (rr // CHUNK) == (cc // CHUNK)
    m_diag = same_sub & (cc <= rr)
    dsub = rr // SUB - cc // SUB
    m_off = [same_chunk & (dsub == dd) for dd in range(1, nsub)]

    hi, lo = _split_bf16(lf_ref[...])
    tri = jnp.where(m_diag, 1.0, 0.0).astype(BF16)
    d1 = _dot(tri, hi) + _dot(tri, lo)
    d1_s[...] = d1
    sub3 = (th // SUB, SUB, d1.shape[1])
    t16_s[...] = jnp.broadcast_to(d1.reshape(sub3)[:, SUB - 1:SUB, :], sub3).reshape(th, d1.shape[1])
    rrow = lax.broadcasted_iota(jnp.int32, (th, dk), 0)
    rsub = (rrow // SUB) % nsub
    rchunk = rrow // CHUNK
    nch = th // CHUNK

    def prepare(hd):
        ls = pl.ds(hd * dk, dk)
        d1 = d1_s[:, ls]
        t16 = t16_s[:, ls]
        q = qs_ref[:, ls].astype(F32)
        kt = 1.0 - jnp.exp(lf_ref[:, ls])
        suf = t16 - d1
        kx = (kt * jnp.exp(suf)).astype(BF16)
        qv = [q * jnp.exp(-suf)]
        acc = d1
        tail = suf
        for dd in range(1, nsub):
            qv.append(q * jnp.exp(acc))
            acc = acc + jnp.where(rsub >= dd, pltpu.roll(t16, SUB * dd, 0), 0.0)
            tail = tail + jnp.where(rsub < nsub - dd, pltpu.roll(t16, th - SUB * dd, 0), 0.0)
        b = acc
        qb = q * jnp.exp(b)
        kend = kt * jnp.exp(tail)
        zero = jnp.zeros_like(q)
        kend_x = jnp.concatenate([jnp.where(rchunk == c, kend, zero) for c in range(nch)], axis=1).astype(BF16)
        qb_x = jnp.concatenate([jnp.where(rchunk == c, qb, zero) for c in range(nch)], axis=1).astype(BF16)
        decs = [jnp.exp(b[CHUNK * (c + 1) - 1:CHUNK * (c + 1), :]) for c in range(nch)]
        return ls, jnp.concatenate(qv, axis=0).astype(BF16), kx, kend_x, qb_x, decs

    def group(gi, carry):
        heads = [gi * HGRN_GROUP + u for u in range(HGRN_GROUP)]
        prep = [prepare(hd) for hd in heads]
        a4s = [_dot_nt(p[1], p[2]) for p in prep]
        incs = [_dot_tn(iv_ref[:, p[0]], p[3]) for p in prep]
        o_intras = []
        for p, a4 in zip(prep, a4s):
            att = jnp.where(m_diag, a4[:th], 0.0)
            for dd in range(1, nsub):
                att = jnp.where(m_off[dd - 1], a4[dd * th:(dd + 1) * th], att)
            o_intras.append(_dot(att.astype(BF16), iv_ref[:, p[0]]))
        for hd, p, inc, o_intra in zip(heads, prep, incs, o_intras):
            ls, decs = p[0], p[5]
            st = st_ref[hd]
            starts = []
            for c in range(nch):
                starts.append(st)
                st = st * decs[c] + inc[:, dk * c:dk * (c + 1)]
            st_ref[hd] = st
            oh = o_intra + _dot_nt(p[4], jnp.concatenate(starts, axis=1).astype(BF16))
            on = oh * lax.rsqrt(jnp.mean(oh * oh, axis=-1, keepdims=True) + EPS)
            o_ref[:, ls] = (on * ng_ref[:, ls] * gs_ref[:, ls].astype(F32)).astype(BF16)
        return carry

    for gi in range(qs_ref.shape[1] // dk // HGRN_GROUP):
        group(gi, 0)


def _hgrn(qs, lf, iv, gs, ng, batch, dk):
    t, d = qs.shape
    s = t // batch
    th = min(TH, s)
    blk = pl.BlockSpec((None, th, d), lambda b, j: (b, j, 0))
    r3 = lambda a: a.reshape(batch, s, d)
    out = pl.pallas_call(
        functools.partial(_hgrn_kernel, dk=dk),
        grid=(batch, s // th),
        in_specs=[blk, blk, blk, blk, _const_spec(ng.shape)],
        out_specs=blk,
        out_shape=jax.ShapeDtypeStruct((batch, s, d), BF16),
        scratch_shapes=[pltpu.VMEM((d // dk, dk, dk), F32), pltpu.VMEM((th, d), F32), pltpu.VMEM((th, d), F32)],
        compiler_params=_params(2),
        name="hgrn2",
    )(r3(qs), r3(lf), r3(iv), r3(gs), ng)
    return out.reshape(t, d)


def _block_diag(w):
    n, c, dd = w.shape
    eye = jnp.eye(n, dtype=w.dtype)
    return (eye[:, None, :, None] * w[:, :, None, :]).reshape(n * c, n * dd)


def _router_weights(wg, we):
    d = wg.shape[0]
    w = jnp.concatenate([wg, we, jnp.zeros((d, LANES - wg.shape[1] - we.shape[1]), F32)], axis=1).T
    hi = w.astype(BF16)
    return jnp.concatenate([hi, (w - hi.astype(F32)).astype(BF16)], axis=0)


def _moe(layer, meta_t, cnt, hn_slab, moe_w_gate, moe_w_up, moe_w_down):
    dest, pad_lo, pad_hi, blk_expert, nact, nblk = _plan(meta_t, cnt, TB_EXPERT)
    xs = _dispatch(dest, pad_lo, pad_hi, nact, hn_slab, nblk, TB_EXPERT)
    ys = _experts(blk_expert, nact, xs, moe_w_gate, moe_w_up, moe_w_down, layer, TB_EXPERT)
    return dest, ys


def kernel(x, norm_mix_g, norm_ffn_g, norm_final_g, ab_w_in, ab_conv_w, ab_conv_b, rg_w_a, rg_b_a, rg_w_x, rg_b_x, rg_lambda, attn_rel_bias, ab_w_out, c_w_in, c_lb_logits, c_norm_g, c_w_out, moe_router_group, moe_router_expert, moe_w_gate, moe_w_up, moe_w_down):
    batch, seq, d = x.shape
    t = batch * seq
    xt = x.reshape(t, d)
    row = lambda v: v.reshape(1, -1)

    aw = ab_conv_w.shape[2]
    xg, qkv = _ln_proj0(xt, row(norm_mix_g[0]), ab_w_in[0].astype(BF16), 2 * aw)
    ya = _rglru(xg, ab_conv_w[0], row(ab_conv_b[0]),
                _block_diag(rg_w_a[0]).astype(BF16), _block_diag(rg_w_x[0]).astype(BF16),
                row(rg_b_a[0]), row(rg_b_x[0]), row(rg_lambda[0]), batch)
    tk = TQ + LEFT_CHUNKS * CHUNK
    dist = jnp.clip(tk - jnp.arange(TQ + tk), -REL_CLIP, REL_CLIP) + REL_CLIP
    bias = _attn_bias(attn_rel_bias[0][:, None, dist])
    yb = _attention(qkv, bias, batch)
    wo = ab_w_out[0].astype(BF16)
    wt = _router_weights(moe_router_group[0], moe_router_expert[0])
    h1, hn1, meta1, metat1, cnt1 = _proj_route(ya, 0, yb, 0, xt, wo[:aw], wo[aw:], row(norm_ffn_g[0]), wt)
    dest1, ys1 = _moe(0, metat1, cnt1, hn1, moe_w_gate, moe_w_up, moe_w_down)

    dk = c_norm_g.shape[1] // 8
    h2, qs, lf, iv, gs = _comb_proj1(dest1, h1, meta1, ys1, row(norm_mix_g[1]), c_w_in[0].astype(BF16),
                                     c_lb_logits, 1)
    om = _hgrn(qs, lf, iv, gs, row(c_norm_g[0]), batch, dk)
    wo = c_w_out[0].astype(BF16)
    half = wo.shape[0] // 2
    wt = _router_weights(moe_router_group[1], moe_router_expert[1])
    h3, hn3, meta3, metat3, cnt3 = _proj_route(om, 0, om, 1, h2, wo[:half], wo[half:], row(norm_ffn_g[1]), wt)
    dest3, ys3 = _moe(1, metat3, cnt3, hn3, moe_w_gate, moe_w_up, moe_w_down)

    out = _comb_final(dest3, h3, meta3, ys3, row(norm_final_g))
    return out.reshape(batch, seq, d)
```

```python
import functools

import jax
import jax.numpy as jnp
from jax import lax
from jax.experimental import pallas as pl
from jax.experimental.pallas import tpu as pltpu

F32 = jnp.float32
BF16 = jnp.bfloat16

EPS = 1e-6
LOG2E = 1.4426950408889634
RG_C = 8.0
CHUNK = 64
LEFT_CHUNKS = 8
REL_CLIP = 256
N_GROUPS = 4
EXPERTS_PER_GROUP = 8
N_EXPERTS = N_GROUPS * EXPERTS_PER_GROUP
TOP_K = 2
ROUTE_ROWS = 40

LANES = 128
SUBLANES = 8
SUB = 16

TM_PROJ = 512
TS_LRU = 512
TQ = 256
ATTN_GROUP = 4
TM_ROUTE = 1024
TM_ROUTE_SUB = 256
TB_EXPERT = 512
X_SLOTS = 3
TM_COMB = 256
TH = 256
HGRN_GROUP = 4
DMA_UNROLL = 8
COMB_SLOTS = 3
VMEM_MB = 48


def _params(n_axes, vmem_mb=VMEM_MB):
    return pltpu.CompilerParams(dimension_semantics=("arbitrary",) * n_axes,
                                vmem_limit_bytes=vmem_mb * 1024 * 1024)


def _const_spec(shape):
    nd = len(shape)
    return pl.BlockSpec(shape, lambda *_: (0,) * nd)


def _rms(x, g):
    return x * lax.rsqrt(jnp.mean(x * x, axis=-1, keepdims=True) + EPS) * g


def _dot(a, b):
    return jnp.dot(a, b, preferred_element_type=F32)


def _dot_nt(a, b):
    return lax.dot_general(a, b, (((1,), (1,)), ((), ())), preferred_element_type=F32)


def _dot_tn(a, b):
    return lax.dot_general(a, b, (((0,), (0,)), ((), ())), preferred_element_type=F32)


def _split_bf16(x):
    hi = x.astype(BF16)
    lo = (x - hi.astype(F32)).astype(BF16)
    return hi, lo


def _to_slab(ref, val):
    m = val.shape[0]
    for j in range(val.shape[1] // LANES):
        ref[pl.ds(j, m, stride=SUBLANES), :] = val[:, LANES * j:LANES * (j + 1)]


def _from_slab(ref, m):
    return jnp.concatenate([ref[pl.ds(j, m, stride=SUBLANES), :] for j in range(SUBLANES)], axis=1)


def _ln_proj0_kernel(x_ref, g_ref, w_ref, xg_ref, qkv_ref):
    hn = _rms(x_ref[...], g_ref[...]).astype(BF16)
    z = _dot(hn, w_ref[...])
    aw = xg_ref.shape[1]
    bw = (z.shape[1] - aw) // 3
    xg_ref[...] = z[:, :aw]
    qkv_ref[:, :bw] = (z[:, aw:aw + bw] * (CHUNK ** -0.5 * LOG2E)).astype(BF16)
    qkv_ref[:, bw:] = z[:, aw + bw:].astype(BF16)


def _ln_proj0(x, g, w, aw):
    t, d = x.shape
    n = w.shape[1]
    tm = min(TM_PROJ, t)
    return pl.pallas_call(
        _ln_proj0_kernel,
        grid=(t // tm,),
        in_specs=[pl.BlockSpec((tm, d), lambda i: (i, 0)), _const_spec(g.shape), _const_spec(w.shape)],
        out_specs=[pl.BlockSpec((tm, aw), lambda i: (i, 0)), pl.BlockSpec((tm, n - aw), lambda i: (i, 0))],
        out_shape=[jax.ShapeDtypeStruct((t, aw), F32), jax.ShapeDtypeStruct((t, n - aw), BF16)],
        compiler_params=_params(1),
        name="ln_proj0",
    )(x, g, w)


def _rglru_kernel(xg_ref, cw_ref, cb_ref, wa_ref, wx_ref, ba_ref, bx_ref, lam_ref, ya_ref,
                  xbuf, h_ref, a_s, u_s):
    ts = xg_ref.shape[0]
    w = ya_ref.shape[1]

    @pl.when(pl.program_id(1) == 0)
    def _():
        xbuf[pl.ds(0, SUBLANES), :] = jnp.zeros((SUBLANES, w), F32)
        h_ref[...] = jnp.zeros_like(h_ref)

    xbuf[pl.ds(SUBLANES, ts), :] = xg_ref[:, :w]
    nk = cw_ref.shape[0]
    y = cb_ref[...]
    for k in range(nk):
        y = y + cw_ref[nk - 1 - k:nk - k, :] * xbuf[pl.ds(SUBLANES - k, ts), :]
    xbuf[pl.ds(0, SUBLANES), :] = xbuf[pl.ds(ts, SUBLANES), :]

    yb = y.astype(BF16)
    rg = jax.nn.sigmoid(_dot(yb, wa_ref[...]) + ba_ref[...])
    ig = jax.nn.sigmoid(_dot(yb, wx_ref[...]) + bx_ref[...])
    lam = lam_ref[...]
    log_sig = jnp.minimum(lam, 0.0) - jnp.log1p(jnp.exp(-jnp.abs(lam)))
    log_a = RG_C * rg * log_sig
    a = jnp.exp(log_a)
    m = 1.0 - a * a
    u = jnp.where(m > 0.0, m * lax.rsqrt(m), 0.0) * (ig * y)

    grp = (ts // SUBLANES, SUBLANES, w)
    a = a.reshape(grp)
    u = u.reshape(grp)
    rowm = lax.broadcasted_iota(jnp.int32, grp, 1)
    for s in (1, 2, 4):
        keep = rowm >= s
        a_sh = jnp.where(keep, pltpu.roll(a, s, 1), 1.0)
        u_sh = jnp.where(keep, pltpu.roll(u, s, 1), 0.0)
        u = a * u_sh + u
        a = a * a_sh
    a_s[...] = a.reshape(ts, w)
    u_s[...] = u.reshape(ts, w)

    def group(gi, h):
        off = pl.multiple_of(gi * SUBLANES, SUBLANES)
        hg = a_s[pl.ds(off, SUBLANES), :] * h + u_s[pl.ds(off, SUBLANES), :]
        u_s[pl.ds(off, SUBLANES), :] = hg
        return jnp.broadcast_to(hg[SUBLANES - 1:SUBLANES, :], hg.shape)

    h_ref[...] = lax.fori_loop(0, ts // SUBLANES, group, h_ref[...])

    ga = xg_ref[:, w:]
    gelu = 0.5 * ga * (1.0 + jnp.tanh(0.7978845608028654 * (ga + 0.044715 * (ga * ga * ga))))
    ya_ref[...] = (u_s[...] * gelu).astype(BF16)


def _rglru(xg, cw, cb, wa, wx, ba, bx, lam, batch):
    t, w2 = xg.shape
    w = w2 // 2
    s = t // batch
    ts = min(TS_LRU, s)
    xg3 = xg.reshape(batch, s, w2)
    small = [cw, cb, wa, wx, ba, bx, lam]
    out = pl.pallas_call(
        _rglru_kernel,
        grid=(batch, s // ts),
        in_specs=[pl.BlockSpec((None, ts, w2), lambda b, j: (b, j, 0))] + [_const_spec(a.shape) for a in small],
        out_specs=pl.BlockSpec((None, ts, w), lambda b, j: (b, j, 0)),
        out_shape=jax.ShapeDtypeStruct((batch, s, w), BF16),
        scratch_shapes=[pltpu.VMEM((ts + SUBLANES, w), F32), pltpu.VMEM((SUBLANES, w), F32),
                        pltpu.VMEM((ts, w), F32), pltpu.VMEM((ts, w), F32)],
        compiler_params=_params(2),
        name="rglru",
    )(xg3, *small)
    return out.reshape(t, w)


def _attn_bias_kernel(base_ref, o_ref):
    tq, tk = o_ref.shape
    nkb = tk // tq
    row = base_ref[...]
    full = pltpu.roll(jnp.broadcast_to(row, (tq, tq + tk)), 0, 1, stride=1, stride_axis=0)
    qc = lax.broadcasted_iota(jnp.int32, (tq, tk), 0) // CHUNK
    col = lax.broadcasted_iota(jnp.int32, (tq, tk), 1)
    kc = col // CHUNK
    first_ok = (nkb - 1 - pl.program_id(0)) * tq
    valid = (kc >= qc) & (kc <= qc + LEFT_CHUNKS) & (col >= first_ok)
    o_ref[...] = jnp.where(valid, full[:, tq:] * LOG2E, -1e30)


def _attn_bias(base):
    h = base.shape[0]
    tk = TQ + LEFT_CHUNKS * CHUNK
    return pl.pallas_call(
        _attn_bias_kernel,
        grid=(tk // TQ, h),
        in_specs=[pl.BlockSpec((None, 1, TQ + tk), lambda v, i: (i, 0, 0))],
        out_specs=pl.BlockSpec((None, None, TQ, tk), lambda v, i: (v, i, 0, 0)),
        out_shape=jax.ShapeDtypeStruct((tk // TQ, h, TQ, tk), F32),
        compiler_params=_params(2),
        name="attn_bias",
    )(base)


def _attn_kernel(q_ref, k0_ref, k1_ref, k2_ref, v0_ref, v1_ref, v2_ref, bias_ref, o_ref):
    tq = q_ref.shape[0]
    half_w = LANES // 2
    lo = lax.broadcasted_iota(jnp.int32, (tq, LANES), 1) < half_w
    lo_k = lax.broadcasted_iota(jnp.int32, (3 * tq, LANES), 1) < half_w
    k_refs = (k0_ref, k1_ref, k2_ref)
    v_refs = (v0_ref, v1_ref, v2_ref)
    npair = q_ref.shape[1] // LANES
    for j0 in range(0, npair, ATTN_GROUP):
        pairs = range(j0, min(j0 + ATTN_GROUP, npair))
        qk, vh = {}, {}
        for j in pairs:
            sl = slice(LANES * j, LANES * (j + 1))
            q2 = q_ref[:, sl]
            kcat = jnp.concatenate([r[:, sl] for r in k_refs], axis=0)
            vcat = jnp.concatenate([r[:, sl] for r in v_refs], axis=0)
            one = jnp.ones_like(vcat)
            zero = jnp.zeros_like(q2)
            vh[j] = (jnp.where(lo_k, vcat, one), jnp.where(lo_k, one, vcat))
            qk[j] = [_dot_nt(jnp.where(lo if half == 0 else jnp.logical_not(lo), q2, zero), kcat)
                     for half in range(2)]
        ps = {}
        for j in pairs:
            ps[j] = []
            for half in range(2):
                s = qk[j][half] + bias_ref[2 * j + half]
                ps[j].append(jnp.exp2(s - jnp.max(s, axis=-1, keepdims=True)).astype(BF16))
        for j in pairs:
            pv = [_dot(ps[j][half], vh[j][half]) for half in range(2)]
            num = jnp.where(lo, pv[0], pv[1])
            den = pltpu.roll(jnp.where(lo, pv[1], pv[0]), half_w, 1)
            o_ref[:, LANES * j:LANES * (j + 1)] = (num / den).astype(BF16)


def _attention(qkv, bias, batch):
    t, w3 = qkv.shape
    w = w3 // 3
    s = t // batch
    qkv3 = qkv.reshape(batch, s, w3)
    nvar = bias.shape[0]

    def kv_spec(colblk, back):
        return pl.BlockSpec((None, TQ, w), lambda i, b: (b, jnp.maximum(i - back, 0), colblk))

    out = pl.pallas_call(
        _attn_kernel,
        grid=(s // TQ, batch),
        in_specs=[pl.BlockSpec((None, TQ, w), lambda i, b: (b, i, 0)),
                  kv_spec(1, 2), kv_spec(1, 1), kv_spec(1, 0),
                  kv_spec(2, 2), kv_spec(2, 1), kv_spec(2, 0),
                  pl.BlockSpec((None,) + bias.shape[1:], lambda i, b: (jnp.minimum(i, nvar - 1), 0, 0, 0))],
        out_specs=pl.BlockSpec((None, TQ, w), lambda i, b: (b, i, 0)),
        out_shape=jax.ShapeDtypeStruct((batch, s, w), BF16),
        compiler_params=_params(2),
        name="attention",
    )(qkv3, qkv3, qkv3, qkv3, qkv3, qkv3, qkv3, bias)
    return out.reshape(t, w)


def _proj_route_kernel(a_ref, b_ref, r_ref, wa_ref, wb_ref, g_ref, wt_ref,
                       h_ref, hn_ref, meta_ref, metat_ref, cnt_ref, carry_ref):
    tm = a_ref.shape[0]
    ts = min(TM_ROUTE_SUB, tm)

    @pl.when(pl.program_id(0) == 0)
    def _():
        carry_ref[...] = jnp.zeros_like(carry_ref)

    wt = wt_ref[...]
    ridx = lax.broadcasted_iota(jnp.int32, (ROUTE_ROWS, ts), 0).astype(F32)
    r8 = lax.broadcasted_iota(jnp.int32, (SUBLANES, ts), 0)
    rr = lax.broadcasted_iota(jnp.int32, (ts, ts), 0)
    cc = lax.broadcasted_iota(jnp.int32, (ts, ts), 1)
    utri = jnp.where(rr < cc, 1.0, 0.0).astype(BF16)
    neg = -jnp.inf
    far = float(LANES)
    carry = carry_ref[...]

    all_logits = []
    for sub in range(tm // ts):
        rows = pl.ds(sub * ts, ts)
        h = r_ref[rows, :] + (_dot(a_ref[rows, :], wa_ref[...]) + _dot(b_ref[rows, :], wb_ref[...]))
        h_ref[rows, :] = h
        hn = _rms(h, g_ref[...])
        _to_slab(hn_ref.at[pl.ds(sub * ts * SUBLANES, ts * SUBLANES), :], hn)

        hi, lo = _split_bf16(hn)
        p_hi = _dot_nt(wt, hi)
        p_lo = _dot_nt(wt[:LANES], lo)
        all_logits.append((p_hi[:ROUTE_ROWS] + p_hi[LANES:LANES + ROUTE_ROWS]) + p_lo[:ROUTE_ROWS])

    for sub, logits in enumerate(all_logits):
        rows = pl.ds(sub * ts, ts)

        def first_max(mask, logits=logits):
            v = jnp.max(jnp.where(mask, logits, neg), axis=0, keepdims=True)
            idx = jnp.min(jnp.where(mask & (logits == v), ridx, far), axis=0, keepdims=True)
            return v, idx

        gmask = ridx < N_GROUPS
        gmax, gidx = first_max(gmask)
        g_gate = 1.0 / jnp.sum(jnp.where(gmask, jnp.exp(logits - gmax), 0.0), axis=0, keepdims=True)
        e_lo = N_GROUPS + EXPERTS_PER_GROUP * gidx
        emask = (ridx >= e_lo) & (ridx < e_lo + EXPERTS_PER_GROUP)
        v1, i1 = first_max(emask)
        v2, i2 = first_max(emask & (ridx != i1))
        tt = jnp.exp(v2 - v1)
        w1 = g_gate / (1.0 + tt)
        w2 = g_gate * tt / (1.0 + tt)

        sel1 = ridx == i1
        sel2 = ridx == i2
        onehot = jnp.where(sel1 | sel2, 1.0, 0.0)
        before = _dot(onehot.astype(BF16), utri) + carry[:, 0:1]
        rank1 = jnp.sum(jnp.where(sel1, before, 0.0), axis=0, keepdims=True)
        rank2 = jnp.sum(jnp.where(sel2, before, 0.0), axis=0, keepdims=True)
        carry = carry + jnp.sum(onehot, axis=1, keepdims=True)

        mt = jnp.zeros((SUBLANES, ts), F32)
        for c, val in enumerate((i1 - N_GROUPS, i2 - N_GROUPS, rank1, rank2, w1, w2)):
            mt = jnp.where(r8 == c, val, mt)
        metat_ref[:, rows] = mt
        meta_ref[rows, :] = jnp.concatenate([mt, jnp.zeros((LANES - SUBLANES, ts), F32)], axis=0).T

    carry_ref[...] = carry
    cnt_ref[...] = carry


def _proj_route(a, acol, b, bcol, resid, wa, wb, g, wt):
    t, d = resid.shape
    kw = wa.shape[0]
    tm = min(TM_ROUTE, t)
    return pl.pallas_call(
        _proj_route_kernel,
        grid=(t // tm,),
        in_specs=[pl.BlockSpec((tm, kw), lambda i: (i, acol)), pl.BlockSpec((tm, kw), lambda i: (i, bcol)),
                  pl.BlockSpec((tm, d), lambda i: (i, 0)),
                  _const_spec(wa.shape), _const_spec(wb.shape), _const_spec(g.shape), _const_spec(wt.shape)],
        out_specs=[pl.BlockSpec((tm, d), lambda i: (i, 0)),
                   pl.BlockSpec((tm * SUBLANES, LANES), lambda i: (i, 0)),
                   pl.BlockSpec((tm, LANES), lambda i: (i, 0)),
                   pl.BlockSpec((SUBLANES, tm), lambda i: (0, i)),
                   pl.BlockSpec((ROUTE_ROWS, LANES), lambda i: (0, 0))],
        out_shape=[jax.ShapeDtypeStruct((t, d), F32),
                   jax.ShapeDtypeStruct((t * SUBLANES, LANES), F32),
                   jax.ShapeDtypeStruct((t, LANES), F32),
                   jax.ShapeDtypeStruct((SUBLANES, t), F32),
                   jax.ShapeDtypeStruct((ROUTE_ROWS, LANES), F32)],
        scratch_shapes=[pltpu.VMEM((ROUTE_ROWS, LANES), F32)],
        compiler_params=_params(1),
        name="proj_route",
    )(a, b, resid, wa, wb, g, wt)


def _plan(meta_t, cnt, tb):
    t = meta_t.shape[1]
    counts = cnt[N_GROUPS:N_GROUPS + N_EXPERTS, 0].astype(jnp.int32)
    padded = (counts + tb - 1) // tb * tb
    pad_end = jnp.cumsum(padded)
    pad_start = pad_end - padded
    dest = []
    for k in range(TOP_K):
        expert = meta_t[k].astype(jnp.int32)
        start = jnp.zeros((t,), jnp.int32)
        for e in range(N_EXPERTS):
            start = jnp.where(expert == e, pad_start[e], start)
        dest.append(start + meta_t[TOP_K + k].astype(jnp.int32))
    dest = jnp.concatenate(dest)
    nblk = -(-(t * TOP_K + N_EXPERTS * (tb - 1)) // tb)
    blk_start = jnp.arange(nblk, dtype=jnp.int32) * tb
    blk_expert = jnp.minimum(jnp.sum((pad_end[None, :] <= blk_start[:, None]).astype(jnp.int32), axis=1),
                             N_EXPERTS - 1)
    nact = (pad_end[-1:] // tb).astype(jnp.int32)
    return dest, pad_start + counts, pad_end, blk_expert, nact, nblk


def _dispatch_kernel(dest_ref, plo_ref, phi_ref, nact_ref, hn_ref, xs_ref, zero_ref, sem, zsem):
    tm = hn_ref.shape[0] // SUBLANES
    t = dest_ref.shape[0] // TOP_K
    tb = zero_ref.shape[0] // SUBLANES
    nblk = xs_ref.shape[0] // zero_ref.shape[0]
    i = pl.program_id(0)

    def zero_copy(row, nrows):
        return pltpu.make_async_copy(
            zero_ref.at[pl.ds(0, nrows * SUBLANES), :],
            xs_ref.at[pl.ds(pl.multiple_of(row * SUBLANES, SUBLANES), nrows * SUBLANES), :], zsem)

    def zero_fill(start):
        def go(cp):
            cp.start() if start else cp.wait()

        def per_expert(e, c):
            off = plo_ref[e]
            n = phi_ref[e] - off
            bit = tb // 2
            while bit:
                pl.when((n & bit) != 0)(functools.partial(lambda o, b: go(zero_copy(o, b)), off, bit))
                off = off + (n & bit)
                bit //= 2
            return c

        def per_block(b, c):
            go(zero_copy(b * tb, tb))
            return c

        lax.fori_loop(0, N_EXPERTS, per_expert, 0)
        lax.fori_loop(nact_ref[0], nblk, per_block, 0)

    @pl.when(i == 0)
    def _():
        zero_ref[...] = jnp.zeros_like(zero_ref)
        zero_fill(True)

    def copy(r, k):
        d = dest_ref[k * t + i * tm + r]
        return pltpu.make_async_copy(
            hn_ref.at[pl.ds(pl.multiple_of(r * SUBLANES, SUBLANES), SUBLANES), :],
            xs_ref.at[pl.ds(pl.multiple_of(d * SUBLANES, SUBLANES), SUBLANES), :], sem)

    def start(r, c):
        for k in range(TOP_K):
            copy(r, k).start(priority=k)
        return c

    lax.fori_loop(0, tm, start, 0, unroll=DMA_UNROLL)
    for k in range(TOP_K):
        pltpu.make_async_copy(hn_ref, xs_ref.at[pl.ds(0, tm * SUBLANES), :], sem).wait()

    @pl.when(i == 0)
    def _():
        zero_fill(False)


def _dispatch(dest, pad_lo, pad_hi, nact, hn_slab, nblk, tb):
    t = hn_slab.shape[0] // SUBLANES
    tm = min(TM_COMB, t)
    return pl.pallas_call(
        _dispatch_kernel,
        grid_spec=pltpu.PrefetchScalarGridSpec(
            num_scalar_prefetch=4,
            grid=(t // tm,),
            in_specs=[pl.BlockSpec((tm * SUBLANES, LANES), lambda i, *_: (i, 0))],
            out_specs=pl.BlockSpec(memory_space=pl.ANY),
            scratch_shapes=[pltpu.VMEM((tb * SUBLANES, LANES), F32),
                            pltpu.SemaphoreType.DMA(()), pltpu.SemaphoreType.DMA(())]),
        out_shape=jax.ShapeDtypeStruct((nblk * tb * SUBLANES, LANES), F32),
        compiler_params=_params(1),
        name="dispatch",
    )(dest, pad_lo, pad_hi, nact, hn_slab)


def _expert_kernel(be_ref, nact_ref, xs_ref, wg_ref, wu_ref, wd_ref, ys_ref, xbuf, xsem, wg_s, wu_s, wd_s):
    i = pl.program_id(0)
    n = pl.num_programs(0)
    rows = ys_ref.shape[0]
    tb = rows // SUBLANES
    active = i < nact_ref[0]
    changed = (i == 0) | (be_ref[i] != be_ref[jnp.maximum(i - 1, 0)])
    slot = i % X_SLOTS
    ahead = X_SLOTS - 1

    def x_copy(step, sl):
        blk = jnp.minimum(step, nact_ref[0] - 1)
        return pltpu.make_async_copy(xs_ref.at[pl.ds(pl.multiple_of(blk * rows, rows), rows), :],
                                     xbuf.at[sl], xsem.at[sl])

    @pl.when(i == 0)
    def _():
        for a in range(ahead):
            x_copy(a, a).start()

    x_copy(i, slot).wait()
    x_copy(i + ahead, (i + ahead) % X_SLOTS).start()

    @pl.when(i == n - 1)
    def _():
        for a in range(1, X_SLOTS):
            x_copy(i + a, (i + a) % X_SLOTS).wait()

    @pl.when(active & changed)
    def _():
        wg_s[...] = wg_ref[...].astype(BF16)
        wu_s[...] = wu_ref[...].astype(BF16)
        wd_s[...] = wd_ref[...].astype(BF16)

    @pl.when(active)
    def _():
        x = _from_slab(xbuf.at[slot], tb).astype(BF16)
        gate = _dot(x, wg_s[...])
        up = _dot(x, wu_s[...])
        hid = (gate * jax.nn.sigmoid(gate) * up).astype(BF16)
        _to_slab(ys_ref, _dot(hid, wd_s[...]))

    @pl.when(jnp.logical_not(active))
    def _():
        ys_ref[...] = jnp.zeros_like(ys_ref)


def _experts(blk_expert, nact, xs, wg, wu, wd, layer, tb):
    nblk = blk_expert.shape[0]
    _, _, d, de = wg.shape

    def w_map(i, be, na):
        return (layer, be[jnp.minimum(i, na[0] - 1)], 0, 0)

    return pl.pallas_call(
        _expert_kernel,
        grid_spec=pltpu.PrefetchScalarGridSpec(
            num_scalar_prefetch=2,
            grid=(nblk,),
            in_specs=[pl.BlockSpec(memory_space=pl.ANY),
                      pl.BlockSpec((None, None, d, de), w_map), pl.BlockSpec((None, None, d, de), w_map),
                      pl.BlockSpec((None, None, de, d), w_map)],
            out_specs=pl.BlockSpec((tb * SUBLANES, LANES), lambda i, be, na: (i, 0)),
            scratch_shapes=[pltpu.VMEM((X_SLOTS, tb * SUBLANES, LANES), F32), pltpu.SemaphoreType.DMA((X_SLOTS,)),
                            pltpu.VMEM((d, de), BF16), pltpu.VMEM((d, de), BF16), pltpu.VMEM((de, d), BF16)]),
        out_shape=jax.ShapeDtypeStruct(xs.shape, F32),
        compiler_params=_params(1),
        name="experts",
    )(blk_expert, nact, xs, wg, wu, wd)


def _combine(dest_ref, ys_ref, meta_ref, buf_ref, sem_ref, tm):
    i = pl.program_id(0)
    n = pl.num_programs(0)
    slot = i % COMB_SLOTS
    ahead = COMB_SLOTS - 1
    t = dest_ref.shape[0] // TOP_K

    def copy(tile, sl, r, k):
        d = dest_ref[k * t + tile * tm + r]
        row = r * SUBLANES if isinstance(r, int) else pl.multiple_of(r * SUBLANES, SUBLANES)
        return pltpu.make_async_copy(
            ys_ref.at[pl.ds(pl.multiple_of(d * SUBLANES, SUBLANES), SUBLANES), :],
            buf_ref.at[sl, k, pl.ds(row, SUBLANES), :],
            sem_ref.at[sl])

    def start_tile(tile, sl):
        def body(r, c):
            for k in range(TOP_K):
                copy(tile, sl, r, k).start(priority=k)
            return c
        lax.fori_loop(0, tm, body, 0, unroll=DMA_UNROLL)

    def wait_tile(sl):
        for k in range(TOP_K):
            pltpu.make_async_copy(ys_ref.at[pl.ds(0, tm * SUBLANES), :], buf_ref.at[sl, k], sem_ref.at[sl]).wait()

    @pl.when(i == 0)
    def _():
        for a in range(ahead):
            start_tile(jnp.minimum(a, n - 1), a)

    wait_tile(slot)

    nxt = jnp.minimum(i + ahead, n - 1)
    nslot = (i + ahead) % COMB_SLOTS

    def issue(part, nparts):
        rows = tm // nparts
        for r in range(part * rows, (part + 1) * rows):
            for k in range(TOP_K):
                copy(nxt, nslot, r, k).start(priority=k)

    @pl.when(i == n - 1)
    def _():
        for a in range(1, ahead):
            wait_tile((i + a) % COMB_SLOTS)

    y = None
    for k in range(TOP_K):
        gate = meta_ref[:, 2 * TOP_K + k:2 * TOP_K + k + 1]
        term = _from_slab(buf_ref.at[slot, k], tm) * gate
        y = term if y is None else y + term
    return y, issue


def _combine_drain(buf_ref, sem_ref, ys_ref, tm):
    i = pl.program_id(0)

    @pl.when(i == pl.num_programs(0) - 1)
    def _():
        sl = (i + COMB_SLOTS - 1) % COMB_SLOTS
        for k in range(TOP_K):
            pltpu.make_async_copy(ys_ref.at[pl.ds(0, tm * SUBLANES), :], buf_ref.at[sl, k], sem_ref.at[sl]).wait()


def _combine_scratch(tm):
    return [pltpu.VMEM((COMB_SLOTS, TOP_K, tm * SUBLANES, LANES), F32), pltpu.SemaphoreType.DMA((COMB_SLOTS,))]


def _comb_proj1_kernel(dest_ref, h_ref, meta_ref, ys_ref, g_ref, w_ref, lbl_ref,
                       h2_ref, qs_ref, lf_ref, iv_ref, gs_ref, buf_ref, sem_ref, *, lb_rows):
    tm = h_ref.shape[0]
    y, issue = _combine(dest_ref, ys_ref, meta_ref, buf_ref, sem_ref, tm)
    h2 = h_ref[...] + y
    h2_ref[...] = h2
    hn = _rms(h2, g_ref[...]).astype(BF16)
    d = h2.shape[1]
    lbl = lbl_ref[...]
    ex = jnp.exp(lbl - jnp.max(lbl, axis=0, keepdims=True))
    lb = jnp.sum(ex[:lb_rows], axis=0, keepdims=True) / jnp.sum(ex, axis=0, keepdims=True)

    def col(c):
        return _dot(hn, w_ref[:, d * c:d * (c + 1)])

    q = col(0)
    issue(0, 4)
    fz = col(1)
    issue(1, 4)
    iv = col(2)
    issue(2, 4)
    gz = col(3)
    issue(3, 4)
    qs_ref[...] = (q * jax.nn.sigmoid(q)).astype(BF16)
    lf_ref[...] = jnp.log(lb + (1.0 - lb) * jax.nn.sigmoid(fz))
    iv_ref[...] = iv.astype(BF16)
    gs_ref[...] = (gz * jax.nn.sigmoid(gz)).astype(BF16)
    _combine_drain(buf_ref, sem_ref, ys_ref, tm)


def _comb_proj1(dest, h, meta, ys, g, w, lb_logits, lb_rows):
    t, d = h.shape
    tm = min(TM_COMB, t)
    row = lambda i, dd: (i, 0)
    const = lambda shape: pl.BlockSpec(shape, lambda i, dd: (0,) * len(shape))
    return pl.pallas_call(
        functools.partial(_comb_proj1_kernel, lb_rows=lb_rows),
        grid_spec=pltpu.PrefetchScalarGridSpec(
            num_scalar_prefetch=1,
            grid=(t // tm,),
            in_specs=[pl.BlockSpec((tm, d), row), pl.BlockSpec((tm, LANES), row),
                      pl.BlockSpec(memory_space=pl.ANY),
                      const(g.shape), const(w.shape), const(lb_logits.shape)],
            out_specs=[pl.BlockSpec((tm, d), row)] * 5,
            scratch_shapes=_combine_scratch(tm)),
        out_shape=[jax.ShapeDtypeStruct((t, d), F32), jax.ShapeDtypeStruct((t, d), BF16),
                   jax.ShapeDtypeStruct((t, d), F32), jax.ShapeDtypeStruct((t, d), BF16),
                   jax.ShapeDtypeStruct((t, d), BF16)],
        compiler_params=_params(1),
        name="comb_proj1",
    )(dest, h, meta, ys, g, w, lb_logits)


def _comb_final_kernel(dest_ref, h_ref, meta_ref, ys_ref, g_ref, o_ref, buf_ref, sem_ref):
    tm = h_ref.shape[0]
    y, issue = _combine(dest_ref, ys_ref, meta_ref, buf_ref, sem_ref, tm)
    issue(0, 1)
    o_ref[...] = _rms(h_ref[...] + y, g_ref[...])
    _combine_drain(buf_ref, sem_ref, ys_ref, tm)


def _comb_final(dest, h, meta, ys, g):
    t, d = h.shape
    tm = min(TM_COMB, t)
    row = lambda i, dd: (i, 0)
    return pl.pallas_call(
        _comb_final_kernel,
        grid_spec=pltpu.PrefetchScalarGridSpec(
            num_scalar_prefetch=1,
            grid=(t // tm,),
            in_specs=[pl.BlockSpec((tm, d), row), pl.BlockSpec((tm, LANES), row),
                      pl.BlockSpec(memory_space=pl.ANY),
                      pl.BlockSpec(g.shape, lambda i, dd: (0, 0))],
            out_specs=pl.BlockSpec((tm, d), row),
            scratch_shapes=_combine_scratch(tm)),
        out_shape=jax.ShapeDtypeStruct((t, d), F32),
        compiler_params=_params(1),
        name="comb_final",
    )(dest, h, meta, ys, g)


def _hgrn_kernel(qs_ref, lf_ref, iv_ref, gs_ref, ng_ref, o_ref, st_ref, d1_s, t16_s, *, dk):
    th = qs_ref.shape[0]
    nsub = CHUNK // SUB

    @pl.when(pl.program_id(1) == 0)
    def _():
        st_ref[...] = jnp.zeros_like(st_ref)

    rr = lax.broadcasted_iota(jnp.int32, (th, th), 0)
    cc = lax.broadcasted_iota(jnp.int32, (th, th), 1)
    same_sub = (rr // SUB) == (cc // SUB)
    same_chunk = (rr // CHUNK) == (cc // CHUNK)
    m_diag = same_sub & (cc <= rr)
    dsub = rr // SUB - cc // SUB
    m_off = [same_chunk & (dsub == dd) for dd in range(1, nsub)]

    hi, lo = _split_bf16(lf_ref[...])
    tri = jnp.where(m_diag, 1.0, 0.0).astype(BF16)
    d1 = _dot(tri, hi) + _dot(tri, lo)
    d1_s[...] = d1
    sub3 = (th // SUB, SUB, d1.shape[1])
    t16_s[...] = jnp.broadcast_to(d1.reshape(sub3)[:, SUB - 1:SUB, :], sub3).reshape(th, d1.shape[1])
    rrow = lax.broadcasted_iota(jnp.int32, (th, dk), 0)
    rsub = (rrow // SUB) % nsub
    rchunk = rrow // CHUNK
    nch = th // CHUNK

    def prepare(hd):
        ls = pl.ds(hd * dk, dk)
        d1 = d1_s[:, ls]
        t16 = t16_s[:, ls]
        q = qs_ref[:, ls].astype(F32)
        kt = 1.0 - jnp.exp(lf_ref[:, ls])
        suf = t16 - d1
        kx = (kt * jnp.exp(suf)).astype(BF16)
        qv = [q * jnp.exp(-suf)]
        acc = d1
        tail = suf
        for dd in range(1, nsub):
            qv.append(q * jnp.exp(acc))
            acc = acc + jnp.where(rsub >= dd, pltpu.roll(t16, SUB * dd, 0), 0.0)
            tail = tail + jnp.where(rsub < nsub - dd, pltpu.roll(t16, th - SUB * dd, 0), 0.0)
        b = acc
        qb = q * jnp.exp(b)
        kend = kt * jnp.exp(tail)
        zero = jnp.zeros_like(q)
        kend_x = jnp.concatenate([jnp.where(rchunk == c, kend, zero) for c in range(nch)], axis=1).astype(BF16)
        qb_x = jnp.concatenate([jnp.where(rchunk == c, qb, zero) for c in range(nch)], axis=1).astype(BF16)
        decs = [jnp.exp(b[CHUNK * (c + 1) - 1:CHUNK * (c + 1), :]) for c in range(nch)]
        return ls, jnp.concatenate(qv, axis=0).astype(BF16), kx, kend_x, qb_x, decs

    def group(gi, carry):
        heads = [gi * HGRN_GROUP + u for u in range(HGRN_GROUP)]
        prep = [prepare(hd) for hd in heads]
        a4s = [_dot_nt(p[1], p[2]) for p in prep]
        incs = [_dot_tn(iv_ref[:, p[0]], p[3]) for p in prep]
        o_intras = []
        for p, a4 in zip(prep, a4s):
            att = jnp.where(m_diag, a4[:th], 0.0)
            for dd in range(1, nsub):
                att = jnp.where(m_off[dd - 1], a4[dd * th:(dd + 1) * th], att)
            o_intras.append(_dot(att.astype(BF16), iv_ref[:, p[0]]))
        for hd, p, inc, o_intra in zip(heads, prep, incs, o_intras):
            ls, decs = p[0], p[5]
            st = st_ref[hd]
            starts = []
            for c in range(nch):
                starts.append(st)
                st = st * decs[c] + inc[:, dk * c:dk * (c + 1)]
            st_ref[hd] = st
            oh = o_intra + _dot_nt(p[4], jnp.concatenate(starts, axis=1).astype(BF16))
            on = oh * lax.rsqrt(jnp.mean(oh * oh, axis=-1, keepdims=True) + EPS)
            o_ref[:, ls] = (on * ng_ref[:, ls] * gs_ref[:, ls].astype(F32)).astype(BF16)
        return carry

    for gi in range(qs_ref.shape[1] // dk // HGRN_GROUP):
        group(gi, 0)


def _hgrn(qs, lf, iv, gs, ng, batch, dk):
    t, d = qs.shape
    s = t // batch
    th = min(TH, s)
    blk = pl.BlockSpec((None, th, d), lambda b, j: (b, j, 0))
    r3 = lambda a: a.reshape(batch, s, d)
    out = pl.pallas_call(
        functools.partial(_hgrn_kernel, dk=dk),
        grid=(batch, s // th),
        in_specs=[blk, blk, blk, blk, _const_spec(ng.shape)],
        out_specs=blk,
        out_shape=jax.ShapeDtypeStruct((batch, s, d), BF16),
        scratch_shapes=[pltpu.VMEM((d // dk, dk, dk), F32), pltpu.VMEM((th, d), F32), pltpu.VMEM((th, d), F32)],
        compiler_params=_params(2),
        name="hgrn2",
    )(r3(qs), r3(lf), r3(iv), r3(gs), ng)
    return out.reshape(t, d)


def _block_diag(w):
    n, c, dd = w.shape
    eye = jnp.eye(n, dtype=w.dtype)
    return (eye[:, None, :, None] * w[:, :, None, :]).reshape(n * c, n * dd)


def _router_weights(wg, we):
    d = wg.shape[0]
    w = jnp.concatenate([wg, we, jnp.zeros((d, LANES - wg.shape[1] - we.shape[1]), F32)], axis=1).T
    hi = w.astype(BF16)
    return jnp.concatenate([hi, (w - hi.astype(F32)).astype(BF16)], axis=0)


def _moe(layer, meta_t, cnt, hn_slab, moe_w_gate, moe_w_up, moe_w_down):
    dest, pad_lo, pad_hi, blk_expert, nact, nblk = _plan(meta_t, cnt, TB_EXPERT)
    xs = _dispatch(dest, pad_lo, pad_hi, nact, hn_slab, nblk, TB_EXPERT)
    ys = _experts(blk_expert, nact, xs, moe_w_gate, moe_w_up, moe_w_down, layer, TB_EXPERT)
    return dest, ys


def kernel(x, norm_mix_g, norm_ffn_g, norm_final_g, ab_w_in, ab_conv_w, ab_conv_b, rg_w_a, rg_b_a, rg_w_x, rg_b_x, rg_lambda, attn_rel_bias, ab_w_out, c_w_in, c_lb_logits, c_norm_g, c_w_out, moe_router_group, moe_router_expert, moe_w_gate, moe_w_up, moe_w_down):
    batch, seq, d = x.shape
    t = batch * seq
    xt = x.reshape(t, d)
    row = lambda v: v.reshape(1, -1)

    aw = ab_conv_w.shape[2]
    xg, qkv = _ln_proj0(xt, row(norm_mix_g[0]), ab_w_in[0].astype(BF16), 2 * aw)
    ya = _rglru(xg, ab_conv_w[0], row(ab_conv_b[0]),
                _block_diag(rg_w_a[0]).astype(BF16), _block_diag(rg_w_x[0]).astype(BF16),
                row(rg_b_a[0]), row(rg_b_x[0]), row(rg_lambda[0]), batch)
    tk = TQ + LEFT_CHUNKS * CHUNK
    dist = jnp.clip(tk - jnp.arange(TQ + tk), -REL_CLIP, REL_CLIP) + REL_CLIP
    bias = _attn_bias(attn_rel_bias[0][:, None, dist])
    yb = _attention(qkv, bias, batch)
    wo = ab_w_out[0].astype(BF16)
    wt = _router_weights(moe_router_group[0], moe_router_expert[0])
    h1, hn1, meta1, metat1, cnt1 = _proj_route(ya, 0, yb, 0, xt, wo[:aw], wo[aw:], row(norm_ffn_g[0]), wt)
    dest1, ys1 = _moe(0, metat1, cnt1, hn1, moe_w_gate, moe_w_up, moe_w_down)

    dk = c_norm_g.shape[1] // 8
    h2, qs, lf, iv, gs = _comb_proj1(dest1, h1, meta1, ys1, row(norm_mix_g[1]), c_w_in[0].astype(BF16),
                                     c_lb_logits, 1)
    om = _hgrn(qs, lf, iv, gs, row(c_norm_g[0]), batch, dk)
    wo = c_w_out[0].astype(BF16)
    half = wo.shape[0] // 2
    wt = _router_weights(moe_router_group[1], moe_router_expert[1])
    h3, hn3, meta3, metat3, cnt3 = _proj_route(om, 0, om, 1, h2, wo[:half], wo[half:], row(norm_ffn_g[1]), wt)
    dest3, ys3 = _moe(1, metat3, cnt3, hn3, moe_w_gate, moe_w_up, moe_w_down)

    out = _comb_final(dest3, h3, meta3, ys3, row(norm_final_g))
    return out.reshape(batch, seq, d)
```

```python
import functools

import jax
import jax.numpy as jnp
from jax import lax
from jax.experimental import pallas as pl
from jax.experimental.pallas import tpu as pltpu

F32 = jnp.float32
BF16 = jnp.bfloat16

EPS = 1e-6
LOG2E = 1.4426950408889634
RG_C = 8.0
CHUNK = 64
LEFT_CHUNKS = 8
REL_CLIP = 256
N_GROUPS = 4
EXPERTS_PER_GROUP = 8
N_EXPERTS = N_GROUPS * EXPERTS_PER_GROUP
TOP_K = 2
ROUTE_ROWS = 40

LANES = 128
SUBLANES = 8
SUB = 16

TS_LRU = 512
TQ = 256
ATTN_GROUP = 4
TM_ROUTE = 1024
TM_ROUTE_SUB = 256
TB_EXPERT = 512
X_SLOTS = 3
TM_COMB = 256
TH = 256
HGRN_GROUP = 4
DMA_UNROLL = 8
COMB_SLOTS = 3
VMEM_MB = 48


def _params(n_axes, vmem_mb=VMEM_MB):
    return pltpu.CompilerParams(dimension_semantics=("arbitrary",) * n_axes,
                                vmem_limit_bytes=vmem_mb * 1024 * 1024)


def _const_spec(shape):
    nd = len(shape)
    return pl.BlockSpec(shape, lambda *_: (0,) * nd)


def _rms(x, g):
    return x * lax.rsqrt(jnp.mean(x * x, axis=-1, keepdims=True) + EPS) * g


def _dot(a, b):
    return jnp.dot(a, b, preferred_element_type=F32)


def _dot_nt(a, b):
    return lax.dot_general(a, b, (((1,), (1,)), ((), ())), preferred_element_type=F32)


def _dot_tn(a, b):
    return lax.dot_general(a, b, (((0,), (0,)), ((), ())), preferred_element_type=F32)


def _split_bf16(x):
    hi = x.astype(BF16)
    lo = (x - hi.astype(F32)).astype(BF16)
    return hi, lo


def _to_slab(ref, val):
    m = val.shape[0]
    for j in range(val.shape[1] // LANES):
        ref[pl.ds(j, m, stride=SUBLANES), :] = val[:, LANES * j:LANES * (j + 1)]


def _from_slab(ref, m):
    return jnp.concatenate([ref[pl.ds(j, m, stride=SUBLANES), :] for j in range(SUBLANES)], axis=1)


def _proj0_lru_kernel(x_ref, g_ref, w_ref, cw_ref, cb_ref, wa_ref, wx_ref, ba_ref, bx_ref, lam_ref,
                      qkv_ref, ya_ref, xbuf, h_ref, a_s, u_s):
    ts = x_ref.shape[0]
    w = ya_ref.shape[1]
    bw = qkv_ref.shape[1] // 3

    @pl.when(pl.program_id(1) == 0)
    def _():
        xbuf[pl.ds(0, SUBLANES), :] = jnp.zeros((SUBLANES, w), F32)
        h_ref[...] = jnp.zeros_like(h_ref)

    hn = _rms(x_ref[...], g_ref[...]).astype(BF16)

    def qkv_cols(c, scale=None):
        z = _dot(hn, w_ref[:, 2 * w + c * bw:2 * w + (c + 1) * bw])
        qkv_ref[:, c * bw:(c + 1) * bw] = (z if scale is None else z * scale).astype(BF16)

    xg = _dot(hn, w_ref[:, :2 * w])

    xbuf[pl.ds(SUBLANES, ts), :] = xg[:, :w]
    nk = cw_ref.shape[0]
    y = cb_ref[...]
    for k in range(nk):
        y = y + cw_ref[nk - 1 - k:nk - k, :] * xbuf[pl.ds(SUBLANES - k, ts), :]
    xbuf[pl.ds(0, SUBLANES), :] = xbuf[pl.ds(ts, SUBLANES), :]

    yb = y.astype(BF16)
    rg = jax.nn.sigmoid(_dot(yb, wa_ref[...]) + ba_ref[...])
    ig = jax.nn.sigmoid(_dot(yb, wx_ref[...]) + bx_ref[...])
    qkv_cols(0, CHUNK ** -0.5 * LOG2E)
    lam = lam_ref[...]
    log_sig = jnp.minimum(lam, 0.0) - jnp.log1p(jnp.exp(-jnp.abs(lam)))
    log_a = RG_C * rg * log_sig
    a = jnp.exp(log_a)
    m = 1.0 - a * a
    u = jnp.where(m > 0.0, m * lax.rsqrt(m), 0.0) * (ig * y)
    qkv_cols(1)

    grp = (ts // SUBLANES, SUBLANES, w)
    a = a.reshape(grp)
    u = u.reshape(grp)
    rowm = lax.broadcasted_iota(jnp.int32, grp, 1)
    for s in (1, 2, 4):
        keep = rowm >= s
        a_sh = jnp.where(keep, pltpu.roll(a, s, 1), 1.0)
        u_sh = jnp.where(keep, pltpu.roll(u, s, 1), 0.0)
        u = a * u_sh + u
        a = a * a_sh
    a_s[...] = a.reshape(ts, w)
    u_s[...] = u.reshape(ts, w)
    qkv_cols(2)

    def group(gi, h):
        off = pl.multiple_of(gi * SUBLANES, SUBLANES)
        hg = a_s[pl.ds(off, SUBLANES), :] * h + u_s[pl.ds(off, SUBLANES), :]
        u_s[pl.ds(off, SUBLANES), :] = hg
        return jnp.broadcast_to(hg[SUBLANES - 1:SUBLANES, :], hg.shape)

    h_ref[...] = lax.fori_loop(0, ts // SUBLANES, group, h_ref[...])

    ga = xg[:, w:]
    gelu = 0.5 * ga * (1.0 + jnp.tanh(0.7978845608028654 * (ga + 0.044715 * (ga * ga * ga))))
    ya_ref[...] = (u_s[...] * gelu).astype(BF16)


def _proj0_lru(x, g, w_in, cw, cb, wa, wx, ba, bx, lam):
    batch, s, d = x.shape
    w = cw.shape[1]
    nqkv = w_in.shape[1] - 2 * w
    ts = min(TS_LRU, s)
    small = [g, w_in, cw, cb, wa, wx, ba, bx, lam]
    blk = lambda width: pl.BlockSpec((None, ts, width), lambda b, j: (b, j, 0))
    qkv, ya = pl.pallas_call(
        _proj0_lru_kernel,
        grid=(batch, s // ts),
        in_specs=[blk(d)] + [_const_spec(a.shape) for a in small],
        out_specs=[blk(nqkv), blk(w)],
        out_shape=[jax.ShapeDtypeStruct((batch, s, nqkv), BF16), jax.ShapeDtypeStruct((batch, s, w), BF16)],
        scratch_shapes=[pltpu.VMEM((ts + SUBLANES, w), F32), pltpu.VMEM((SUBLANES, w), F32),
                        pltpu.VMEM((ts, w), F32), pltpu.VMEM((ts, w), F32)],
        compiler_params=_params(2),
        name="proj0_lru",
    )(x, *small)
    return qkv.reshape(batch * s, nqkv), ya.reshape(batch * s, w)


def _attn_bias_kernel(base_ref, o_ref):
    tq, tk = o_ref.shape
    nkb = tk // tq
    row = base_ref[...]
    full = pltpu.roll(jnp.broadcast_to(row, (tq, tq + tk)), 0, 1, stride=1, stride_axis=0)
    qc = lax.broadcasted_iota(jnp.int32, (tq, tk), 0) // CHUNK
    col = lax.broadcasted_iota(jnp.int32, (tq, tk), 1)
    kc = col // CHUNK
    first_ok = (nkb - 1 - pl.program_id(0)) * tq
    valid = (kc >= qc) & (kc <= qc + LEFT_CHUNKS) & (col >= first_ok)
    o_ref[...] = jnp.where(valid, full[:, tq:] * LOG2E, -1e30)


def _attn_bias(base):
    h = base.shape[0]
    tk = TQ + LEFT_CHUNKS * CHUNK
    return pl.pallas_call(
        _attn_bias_kernel,
        grid=(tk // TQ, h),
        in_specs=[pl.BlockSpec((None, 1, TQ + tk), lambda v, i: (i, 0, 0))],
        out_specs=pl.BlockSpec((None, None, TQ, tk), lambda v, i: (v, i, 0, 0)),
        out_shape=jax.ShapeDtypeStruct((tk // TQ, h, TQ, tk), F32),
        compiler_params=_params(2),
        name="attn_bias",
    )(base)


def _attn_kernel(q_ref, k0_ref, k1_ref, k2_ref, v0_ref, v1_ref, v2_ref, bias_ref, o_ref):
    tq = q_ref.shape[0]
    half_w = LANES // 2
    lo = lax.broadcasted_iota(jnp.int32, (tq, LANES), 1) < half_w
    lo_k = lax.broadcasted_iota(jnp.int32, (3 * tq, LANES), 1) < half_w
    k_refs = (k0_ref, k1_ref, k2_ref)
    v_refs = (v0_ref, v1_ref, v2_ref)
    npair = q_ref.shape[1] // LANES
    for j0 in range(0, npair, ATTN_GROUP):
        pairs = range(j0, min(j0 + ATTN_GROUP, npair))
        qk, vh = {}, {}
        for j in pairs:
            sl = slice(LANES * j, LANES * (j + 1))
            q2 = q_ref[:, sl]
            kcat = jnp.concatenate([r[:, sl] for r in k_refs], axis=0)
            vcat = jnp.concatenate([r[:, sl] for r in v_refs], axis=0)
            one = jnp.ones_like(vcat)
            zero = jnp.zeros_like(q2)
            vh[j] = (jnp.where(lo_k, vcat, one), jnp.where(lo_k, one, vcat))
            qk[j] = [_dot_nt(jnp.where(lo if half == 0 else jnp.logical_not(lo), q2, zero), kcat)
                     for half in range(2)]
        ps = {}
        for j in pairs:
            ps[j] = []
            for half in range(2):
                s = qk[j][half] + bias_ref[2 * j + half]
                ps[j].append(jnp.exp2(s - jnp.max(s, axis=-1, keepdims=True)).astype(BF16))
        for j in pairs:
            pv = [_dot(ps[j][half], vh[j][half]) for half in range(2)]
            num = jnp.where(lo, pv[0], pv[1])
            den = pltpu.roll(jnp.where(lo, pv[1], pv[0]), half_w, 1)
            o_ref[:, LANES * j:LANES * (j + 1)] = (num / den).astype(BF16)


def _attention(qkv, bias, batch):
    t, w3 = qkv.shape
    w = w3 // 3
    s = t // batch
    qkv3 = qkv.reshape(batch, s, w3)
    nvar = bias.shape[0]

    def kv_spec(colblk, back):
        return pl.BlockSpec((None, TQ, w), lambda i, b: (b, jnp.maximum(i - back, 0), colblk))

    out = pl.pallas_call(
        _attn_kernel,
        grid=(s // TQ, batch),
        in_specs=[pl.BlockSpec((None, TQ, w), lambda i, b: (b, i, 0)),
                  kv_spec(1, 2), kv_spec(1, 1), kv_spec(1, 0),
                  kv_spec(2, 2), kv_spec(2, 1), kv_spec(2, 0),
                  pl.BlockSpec((None,) + bias.shape[1:], lambda i, b: (jnp.minimum(i, nvar - 1), 0, 0, 0))],
        out_specs=pl.BlockSpec((None, TQ, w), lambda i, b: (b, i, 0)),
        out_shape=jax.ShapeDtypeStruct((batch, s, w), BF16),
        compiler_params=_params(2),
        name="attention",
    )(qkv3, qkv3, qkv3, qkv3, qkv3, qkv3, qkv3, bias)
    return out.reshape(t, w)


def _proj_route_kernel(a_ref, b_ref, r_ref, wa_ref, wb_ref, g_ref, wt_ref,
                       h_ref, hn_ref, meta_ref, metat_ref, cnt_ref, carry_ref):
    tm = a_ref.shape[0]
    ts = min(TM_ROUTE_SUB, tm)

    @pl.when(pl.program_id(0) == 0)
    def _():
        carry_ref[...] = jnp.zeros_like(carry_ref)

    wt = wt_ref[...]
    ridx = lax.broadcasted_iota(jnp.int32, (ROUTE_ROWS, ts), 0).astype(F32)
    r8 = lax.broadcasted_iota(jnp.int32, (SUBLANES, ts), 0)
    rr = lax.broadcasted_iota(jnp.int32, (ts, ts), 0)
    cc = lax.broadcasted_iota(jnp.int32, (ts, ts), 1)
    utri = jnp.where(rr < cc, 1.0, 0.0).astype(BF16)
    neg = -jnp.inf
    far = float(LANES)
    carry = carry_ref[...]

    all_logits = []
    for sub in range(tm // ts):
        rows = pl.ds(sub * ts, ts)
        h = r_ref[rows, :] + (_dot(a_ref[rows, :], wa_ref[...]) + _dot(b_ref[rows, :], wb_ref[...]))
        h_ref[rows, :] = h
        hn = _rms(h, g_ref[...])
        _to_slab(hn_ref.at[pl.ds(sub * ts * SUBLANES, ts * SUBLANES), :], hn)

        hi, lo = _split_bf16(hn)
        p_hi = _dot_nt(wt, hi)
        p_lo = _dot_nt(wt[:LANES], lo)
        all_logits.append((p_hi[:ROUTE_ROWS] + p_hi[LANES:LANES + ROUTE_ROWS]) + p_lo[:ROUTE_ROWS])

    for sub, logits in enumerate(all_logits):
        rows = pl.ds(sub * ts, ts)

        def first_max(mask, logits=logits):
            v = jnp.max(jnp.where(mask, logits, neg), axis=0, keepdims=True)
            idx = jnp.min(jnp.where(mask & (logits == v), ridx, far), axis=0, keepdims=True)
            return v, idx

        gmask = ridx < N_GROUPS
        gmax, gidx = first_max(gmask)
        g_gate = 1.0 / jnp.sum(jnp.where(gmask, jnp.exp(logits - gmax), 0.0), axis=0, keepdims=True)
        e_lo = N_GROUPS + EXPERTS_PER_GROUP * gidx
        emask = (ridx >= e_lo) & (ridx < e_lo + EXPERTS_PER_GROUP)
        v1, i1 = first_max(emask)
        v2, i2 = first_max(emask & (ridx != i1))
        tt = jnp.exp(v2 - v1)
        w1 = g_gate / (1.0 + tt)
        w2 = g_gate * tt / (1.0 + tt)

        sel1 = ridx == i1
        sel2 = ridx == i2
        onehot = jnp.where(sel1 | sel2, 1.0, 0.0)
        before = _dot(onehot.astype(BF16), utri) + carry[:, 0:1]
        rank1 = jnp.sum(jnp.where(sel1, before, 0.0), axis=0, keepdims=True)
        rank2 = jnp.sum(jnp.where(sel2, before, 0.0), axis=0, keepdims=True)
        carry = carry + jnp.sum(onehot, axis=1, keepdims=True)

        mt = jnp.zeros((SUBLANES, ts), F32)
        for c, val in enumerate((i1 - N_GROUPS, i2 - N_GROUPS, rank1, rank2, w1, w2)):
            mt = jnp.where(r8 == c, val, mt)
        metat_ref[:, rows] = mt
        meta_ref[rows, :] = jnp.concatenate([mt, jnp.zeros((LANES - SUBLANES, ts), F32)], axis=0).T

    carry_ref[...] = carry
    cnt_ref[...] = carry


def _proj_route(a, acol, b, bcol, resid, wa, wb, g, wt):
    t, d = resid.shape
    kw = wa.shape[0]
    tm = min(TM_ROUTE, t)
    return pl.pallas_call(
        _proj_route_kernel,
        grid=(t // tm,),
        in_specs=[pl.BlockSpec((tm, kw), lambda i: (i, acol)), pl.BlockSpec((tm, kw), lambda i: (i, bcol)),
                  pl.BlockSpec((tm, d), lambda i: (i, 0)),
                  _const_spec(wa.shape), _const_spec(wb.shape), _const_spec(g.shape), _const_spec(wt.shape)],
        out_specs=[pl.BlockSpec((tm, d), lambda i: (i, 0)),
                   pl.BlockSpec((tm * SUBLANES, LANES), lambda i: (i, 0)),
                   pl.BlockSpec((tm, LANES), lambda i: (i, 0)),
                   pl.BlockSpec((SUBLANES, tm), lambda i: (0, i)),
                   pl.BlockSpec((ROUTE_ROWS, LANES), lambda i: (0, 0))],
        out_shape=[jax.ShapeDtypeStruct((t, d), F32),
                   jax.ShapeDtypeStruct((t * SUBLANES, LANES), F32),
                   jax.ShapeDtypeStruct((t, LANES), F32),
                   jax.ShapeDtypeStruct((SUBLANES, t), F32),
                   jax.ShapeDtypeStruct((ROUTE_ROWS, LANES), F32)],
        scratch_shapes=[pltpu.VMEM((ROUTE_ROWS, LANES), F32)],
        compiler_params=_params(1),
        name="proj_route",
    )(a, b, resid, wa, wb, g, wt)


def _plan(meta_t, cnt, tb):
    t = meta_t.shape[1]
    counts = cnt[N_GROUPS:N_GROUPS + N_EXPERTS, 0].astype(jnp.int32)
    padded = (counts + tb - 1) // tb * tb
    pad_end = jnp.cumsum(padded)
    pad_start = pad_end - padded
    dest = []
    for k in range(TOP_K):
        expert = meta_t[k].astype(jnp.int32)
        start = jnp.zeros((t,), jnp.int32)
        for e in range(N_EXPERTS):
            start = jnp.where(expert == e, pad_start[e], start)
        dest.append(start + meta_t[TOP_K + k].astype(jnp.int32))
    dest = jnp.concatenate(dest)
    nblk = -(-(t * TOP_K + N_EXPERTS * (tb - 1)) // tb)
    blk_start = jnp.arange(nblk, dtype=jnp.int32) * tb
    blk_expert = jnp.minimum(jnp.sum((pad_end[None, :] <= blk_start[:, None]).astype(jnp.int32), axis=1),
                             N_EXPERTS - 1)
    nact = (pad_end[-1:] // tb).astype(jnp.int32)
    return dest, pad_start + counts, pad_end, blk_expert, nact, nblk


def _dispatch_kernel(dest_ref, plo_ref, phi_ref, nact_ref, hn_ref, xs_ref, zero_ref, sem, zsem):
    tm = hn_ref.shape[0] // SUBLANES
    t = dest_ref.shape[0] // TOP_K
    tb = zero_ref.shape[0] // SUBLANES
    nblk = xs_ref.shape[0] // zero_ref.shape[0]
    i = pl.program_id(0)

    def zero_copy(row, nrows):
        return pltpu.make_async_copy(
            zero_ref.at[pl.ds(0, nrows * SUBLANES), :],
            xs_ref.at[pl.ds(pl.multiple_of(row * SUBLANES, SUBLANES), nrows * SUBLANES), :], zsem)

    def zero_fill(start):
        def go(cp):
            cp.start() if start else cp.wait()

        def per_expert(e, c):
            off = plo_ref[e]
            n = phi_ref[e] - off
            bit = tb // 2
            while bit:
                pl.when((n & bit) != 0)(functools.partial(lambda o, b: go(zero_copy(o, b)), off, bit))
                off = off + (n & bit)
                bit //= 2
            return c

        def per_block(b, c):
            go(zero_copy(b * tb, tb))
            return c

        lax.fori_loop(0, N_EXPERTS, per_expert, 0)
        lax.fori_loop(nact_ref[0], nblk, per_block, 0)

    @pl.when(i == 0)
    def _():
        zero_ref[...] = jnp.zeros_like(zero_ref)
        zero_fill(True)

    def copy(r, k):
        d = dest_ref[k * t + i * tm + r]
        return pltpu.make_async_copy(
            hn_ref.at[pl.ds(pl.multiple_of(r * SUBLANES, SUBLANES), SUBLANES), :],
            xs_ref.at[pl.ds(pl.multiple_of(d * SUBLANES, SUBLANES), SUBLANES), :], sem)

    def start(r, c):
        for k in range(TOP_K):
            copy(r, k).start(priority=k)
        return c

    lax.fori_loop(0, tm, start, 0, unroll=DMA_UNROLL)
    for k in range(TOP_K):
        pltpu.make_async_copy(hn_ref, xs_ref.at[pl.ds(0, tm * SUBLANES), :], sem).wait()

    @pl.when(i == 0)
    def _():
        zero_fill(False)


def _dispatch(dest, pad_lo, pad_hi, nact, hn_slab, nblk, tb):
    t = hn_slab.shape[0] // SUBLANES
    tm = min(TM_COMB, t)
    return pl.pallas_call(
        _dispatch_kernel,
        grid_spec=pltpu.PrefetchScalarGridSpec(
            num_scalar_prefetch=4,
            grid=(t // tm,),
            in_specs=[pl.BlockSpec((tm * SUBLANES, LANES), lambda i, *_: (i, 0))],
            out_specs=pl.BlockSpec(memory_space=pl.ANY),
            scratch_shapes=[pltpu.VMEM((tb * SUBLANES, LANES), F32),
                            pltpu.SemaphoreType.DMA(()), pltpu.SemaphoreType.DMA(())]),
        out_shape=jax.ShapeDtypeStruct((nblk * tb * SUBLANES, LANES), F32),
        compiler_params=_params(1),
        name="dispatch",
    )(dest, pad_lo, pad_hi, nact, hn_slab)


def _expert_kernel(be_ref, nact_ref, xs_ref, wg_ref, wu_ref, wd_ref, ys_ref, xbuf, xsem, wg_s, wu_s, wd_s):
    i = pl.program_id(0)
    n = pl.num_programs(0)
    rows = ys_ref.shape[0]
    tb = rows // SUBLANES
    active = i < nact_ref[0]
    changed = (i == 0) | (be_ref[i] != be_ref[jnp.maximum(i - 1, 0)])
    slot = i % X_SLOTS
    ahead = X_SLOTS - 1

    def x_copy(step, sl):
        blk = jnp.minimum(step, nact_ref[0] - 1)
        return pltpu.make_async_copy(xs_ref.at[pl.ds(pl.multiple_of(blk * rows, rows), rows), :],
                                     xbuf.at[sl], xsem.at[sl])

    @pl.when(i == 0)
    def _():
        for a in range(ahead):
            x_copy(a, a).start()

    x_copy(i, slot).wait()
    x_copy(i + ahead, (i + ahead) % X_SLOTS).start()

    @pl.when(i == n - 1)
    def _():
        for a in range(1, X_SLOTS):
            x_copy(i + a, (i + a) % X_SLOTS).wait()

    @pl.when(active & changed)
    def _():
        wg_s[...] = wg_ref[...].astype(BF16)
        wu_s[...] = wu_ref[...].astype(BF16)
        wd_s[...] = wd_ref[...].astype(BF16)

    @pl.when(active)
    def _():
        x = _from_slab(xbuf.at[slot], tb).astype(BF16)
        gate = _dot(x, wg_s[...])
        up = _dot(x, wu_s[...])
        hid = (gate * jax.nn.sigmoid(gate) * up).astype(BF16)
        _to_slab(ys_ref, _dot(hid, wd_s[...]))

    @pl.when(jnp.logical_not(active))
    def _():
        ys_ref[...] = jnp.zeros_like(ys_ref)


def _experts(blk_expert, nact, xs, wg, wu, wd, layer, tb):
    nblk = blk_expert.shape[0]
    _, _, d, de = wg.shape

    def w_map(i, be, na):
        return (layer, be[jnp.minimum(i, na[0] - 1)], 0, 0)

    return pl.pallas_call(
        _expert_kernel,
        grid_spec=pltpu.PrefetchScalarGridSpec(
            num_scalar_prefetch=2,
            grid=(nblk,),
            in_specs=[pl.BlockSpec(memory_space=pl.ANY),
                      pl.BlockSpec((None, None, d, de), w_map), pl.BlockSpec((None, None, d, de), w_map),
                      pl.BlockSpec((None, None, de, d), w_map)],
            out_specs=pl.BlockSpec((tb * SUBLANES, LANES), lambda i, be, na: (i, 0)),
            scratch_shapes=[pltpu.VMEM((X_SLOTS, tb * SUBLANES, LANES), F32), pltpu.SemaphoreType.DMA((X_SLOTS,)),
                            pltpu.VMEM((d, de), BF16), pltpu.VMEM((d, de), BF16), pltpu.VMEM((de, d), BF16)]),
        out_shape=jax.ShapeDtypeStruct(xs.shape, F32),
        compiler_params=_params(1),
        name="experts",
    )(blk_expert, nact, xs, wg, wu, wd)


def _combine(dest_ref, ys_ref, meta_ref, buf_ref, sem_ref, tm):
    i = pl.program_id(0)
    n = pl.num_programs(0)
    slot = i % COMB_SLOTS
    ahead = COMB_SLOTS - 1
    t = dest_ref.shape[0] // TOP_K

    def copy(tile, sl, r, k):
        d = dest_ref[k * t + tile * tm + r]
        row = r * SUBLANES if isinstance(r, int) else pl.multiple_of(r * SUBLANES, SUBLANES)
        return pltpu.make_async_copy(
            ys_ref.at[pl.ds(pl.multiple_of(d * SUBLANES, SUBLANES), SUBLANES), :],
            buf_ref.at[sl, k, pl.ds(row, SUBLANES), :],
            sem_ref.at[sl])

    def start_tile(tile, sl):
        def body(r, c):
            for k in range(TOP_K):
                copy(tile, sl, r, k).start(priority=k)
            return c
        lax.fori_loop(0, tm, body, 0, unroll=DMA_UNROLL)

    def wait_tile(sl):
        for k in range(TOP_K):
            pltpu.make_async_copy(ys_ref.at[pl.ds(0, tm * SUBLANES), :], buf_ref.at[sl, k], sem_ref.at[sl]).wait()

    @pl.when(i == 0)
    def _():
        for a in range(ahead):
            start_tile(jnp.minimum(a, n - 1), a)

    wait_tile(slot)

    nxt = jnp.minimum(i + ahead, n - 1)
    nslot = (i + ahead) % COMB_SLOTS

    def issue(part, nparts):
        rows = tm // nparts
        for r in range(part * rows, (part + 1) * rows):
            for k in range(TOP_K):
                copy(nxt, nslot, r, k).start(priority=k)

    @pl.when(i == n - 1)
    def _():
        for a in range(1, ahead):
            wait_tile((i + a) % COMB_SLOTS)

    y = None
    for k in range(TOP_K):
        gate = meta_ref[:, 2 * TOP_K + k:2 * TOP_K + k + 1]
        term = _from_slab(buf_ref.at[slot, k], tm) * gate
        y = term if y is None else y + term
    return y, issue


def _combine_drain(buf_ref, sem_ref, ys_ref, tm):
    i = pl.program_id(0)

    @pl.when(i == pl.num_programs(0) - 1)
    def _():
        sl = (i + COMB_SLOTS - 1) % COMB_SLOTS
        for k in range(TOP_K):
            pltpu.make_async_copy(ys_ref.at[pl.ds(0, tm * SUBLANES), :], buf_ref.at[sl, k], sem_ref.at[sl]).wait()


def _combine_scratch(tm):
    return [pltpu.VMEM((COMB_SLOTS, TOP_K, tm * SUBLANES, LANES), F32), pltpu.SemaphoreType.DMA((COMB_SLOTS,))]


def _comb_proj1_kernel(dest_ref, h_ref, meta_ref, ys_ref, g_ref, w_ref, lbl_ref,
                       h2_ref, qs_ref, lf_ref, iv_ref, gs_ref, buf_ref, sem_ref, *, lb_rows):
    tm = h_ref.shape[0]
    y, issue = _combine(dest_ref, ys_ref, meta_ref, buf_ref, sem_ref, tm)
    h2 = h_ref[...] + y
    h2_ref[...] = h2
    hn = _rms(h2, g_ref[...]).astype(BF16)
    d = h2.shape[1]
    lbl = lbl_ref[...]
    ex = jnp.exp(lbl - jnp.max(lbl, axis=0, keepdims=True))
    lb = jnp.sum(ex[:lb_rows], axis=0, keepdims=True) / jnp.sum(ex, axis=0, keepdims=True)

    def col(c):
        return _dot(hn, w_ref[:, d * c:d * (c + 1)])

    q = col(0)
    issue(0, 4)
    fz = col(1)
    issue(1, 4)
    iv = col(2)
    issue(2, 4)
    gz = col(3)
    issue(3, 4)
    qs_ref[...] = (q * jax.nn.sigmoid(q)).astype(BF16)
    lf_ref[...] = jnp.log(lb + (1.0 - lb) * jax.nn.sigmoid(fz))
    iv_ref[...] = iv.astype(BF16)
    gs_ref[...] = (gz * jax.nn.sigmoid(gz)).astype(BF16)
    _combine_drain(buf_ref, sem_ref, ys_ref, tm)


def _comb_proj1(dest, h, meta, ys, g, w, lb_logits, lb_rows):
    t, d = h.shape
    tm = min(TM_COMB, t)
    row = lambda i, dd: (i, 0)
    const = lambda shape: pl.BlockSpec(shape, lambda i, dd: (0,) * len(shape))
    return pl.pallas_call(
        functools.partial(_comb_proj1_kernel, lb_rows=lb_rows),
        grid_spec=pltpu.PrefetchScalarGridSpec(
            num_scalar_prefetch=1,
            grid=(t // tm,),
            in_specs=[pl.BlockSpec((tm, d), row), pl.BlockSpec((tm, LANES), row),
                      pl.BlockSpec(memory_space=pl.ANY),
                      const(g.shape), const(w.shape), const(lb_logits.shape)],
            out_specs=[pl.BlockSpec((tm, d), row)] * 5,
            scratch_shapes=_combine_scratch(tm)),
        out_shape=[jax.ShapeDtypeStruct((t, d), F32), jax.ShapeDtypeStruct((t, d), BF16),
                   jax.ShapeDtypeStruct((t, d), F32), jax.ShapeDtypeStruct((t, d), BF16),
                   jax.ShapeDtypeStruct((t, d), BF16)],
        compiler_params=_params(1),
        name="comb_proj1",
    )(dest, h, meta, ys, g, w, lb_logits)


def _comb_final_kernel(dest_ref, h_ref, meta_ref, ys_ref, g_ref, o_ref, buf_ref, sem_ref):
    tm = h_ref.shape[0]
    y, issue = _combine(dest_ref, ys_ref, meta_ref, buf_ref, sem_ref, tm)
    issue(0, 1)
    o_ref[...] = _rms(h_ref[...] + y, g_ref[...])
    _combine_drain(buf_ref, sem_ref, ys_ref, tm)


def _comb_final(dest, h, meta, ys, g):
    t, d = h.shape
    tm = min(TM_COMB, t)
    row = lambda i, dd: (i, 0)
    return pl.pallas_call(
        _comb_final_kernel,
        grid_spec=pltpu.PrefetchScalarGridSpec(
            num_scalar_prefetch=1,
            grid=(t // tm,),
            in_specs=[pl.BlockSpec((tm, d), row), pl.BlockSpec((tm, LANES), row),
                      pl.BlockSpec(memory_space=pl.ANY),
                      pl.BlockSpec(g.shape, lambda i, dd: (0, 0))],
            out_specs=pl.BlockSpec((tm, d), row),
            scratch_shapes=_combine_scratch(tm)),
        out_shape=jax.ShapeDtypeStruct((t, d), F32),
        compiler_params=_params(1),
        name="comb_final",
    )(dest, h, meta, ys, g)


def _hgrn_kernel(qs_ref, lf_ref, iv_ref, gs_ref, ng_ref, o_ref, st_ref, d1_s, t16_s, *, dk):
    th = qs_ref.shape[0]
    nsub = CHUNK // SUB

    @pl.when(pl.program_id(1) == 0)
    def _():
        st_ref[...] = jnp.zeros_like(st_ref)

    rr = lax.broadcasted_iota(jnp.int32, (th, th), 0)
    cc = lax.broadcasted_iota(jnp.int32, (th, th), 1)
    same_sub = (rr // SUB) == (cc // SUB)
    same_chunk = (rr // CHUNK) == (cc // CHUNK)
    m_diag = same_sub & (cc <= rr)
    dsub = rr // SUB - cc // SUB
    m_off = [same_chunk & (dsub == dd) for dd in range(1, nsub)]

    hi, lo = _split_bf16(lf_ref[...])
    tri = jnp.where(m_diag, 1.0, 0.0).astype(BF16)
    d1 = _dot(tri, hi) + _dot(tri, lo)
    d1_s[...] = d1
    sub3 = (th // SUB, SUB, d1.shape[1])
    t16_s[...] = jnp.broadcast_to(d1.reshape(sub3)[:, SUB - 1:SUB, :], sub3).reshape(th, d1.shape[1])
    rrow = lax.broadcasted_iota(jnp.int32, (th, dk), 0)
    rsub = (rrow // SUB) % nsub
    rchunk = rrow // CHUNK
    nch = th // CHUNK

    def prepare(hd):
        ls = pl.ds(hd * dk, dk)
        d1 = d1_s[:, ls]
        t16 = t16_s[:, ls]
        q = qs_ref[:, ls].astype(F32)
        kt = 1.0 - jnp.exp(lf_ref[:, ls])
        suf = t16 - d1
        kx = (kt * jnp.exp(suf)).astype(BF16)
        qv = [q * jnp.exp(-suf)]
        acc = d1
        tail = suf
        for dd in range(1, nsub):
            qv.append(q * jnp.exp(acc))
            acc = acc + jnp.where(rsub >= dd, pltpu.roll(t16, SUB * dd, 0), 0.0)
            tail = tail + jnp.where(rsub < nsub - dd, pltpu.roll(t16, th - SUB * dd, 0), 0.0)
        b = acc
        qb = q * jnp.exp(b)
        kend = kt * jnp.exp(tail)
        zero = jnp.zeros_like(q)
        kend_x = jnp.concatenate([jnp.where(rchunk == c, kend, zero) for c in range(nch)], axis=1).astype(BF16)
        qb_x = jnp.concatenate([jnp.where(rchunk == c, qb, zero) for c in range(nch)], axis=1).astype(BF16)
        decs = [jnp.exp(b[CHUNK * (c + 1) - 1:CHUNK * (c + 1), :]) for c in range(nch)]
        return ls, jnp.concatenate(qv, axis=0).astype(BF16), kx, kend_x, qb_x, decs

    def group(gi, carry):
        heads = [gi * HGRN_GROUP + u for u in range(HGRN_GROUP)]
        prep = [prepare(hd) for hd in heads]
        a4s = [_dot_nt(p[1], p[2]) for p in prep]
        incs = [_dot_tn(iv_ref[:, p[0]], p[3]) for p in prep]
        o_intras = []
        for p, a4 in zip(prep, a4s):
            att = jnp.where(m_diag, a4[:th], 0.0)
            for dd in range(1, nsub):
                att = jnp.where(m_off[dd - 1], a4[dd * th:(dd + 1) * th], att)
            o_intras.append(_dot(att.astype(BF16), iv_ref[:, p[0]]))
        for hd, p, inc, o_intra in zip(heads, prep, incs, o_intras):
            ls, decs = p[0], p[5]
            st = st_ref[hd]
            starts = []
            for c in range(nch):
                starts.append(st)
                st = st * decs[c] + inc[:, dk * c:dk * (c + 1)]
            st_ref[hd] = st
            oh = o_intra + _dot_nt(p[4], jnp.concatenate(starts, axis=1).astype(BF16))
            on = oh * lax.rsqrt(jnp.mean(oh * oh, axis=-1, keepdims=True) + EPS)
            o_ref[:, ls] = (on * ng_ref[:, ls] * gs_ref[:, ls].astype(F32)).astype(BF16)
        return carry

    for gi in range(qs_ref.shape[1] // dk // HGRN_GROUP):
        group(gi, 0)


def _hgrn(qs, lf, iv, gs, ng, batch, dk):
    t, d = qs.shape
    s = t // batch
    th = min(TH, s)
    blk = pl.BlockSpec((None, th, d), lambda b, j: (b, j, 0))
    r3 = lambda a: a.reshape(batch, s, d)
    out = pl.pallas_call(
        functools.partial(_hgrn_kernel, dk=dk),
        grid=(batch, s // th),
        in_specs=[blk, blk, blk, blk, _const_spec(ng.shape)],
        out_specs=blk,
        out_shape=jax.ShapeDtypeStruct((batch, s, d), BF16),
        scratch_shapes=[pltpu.VMEM((d // dk, dk, dk), F32), pltpu.VMEM((th, d), F32), pltpu.VMEM((th, d), F32)],
        compiler_params=_params(2),
        name="hgrn2",
    )(r3(qs), r3(lf), r3(iv), r3(gs), ng)
    return out.reshape(t, d)


def _block_diag(w):
    n, c, dd = w.shape
    eye = jnp.eye(n, dtype=w.dtype)
    return (eye[:, None, :, None] * w[:, :, None, :]).reshape(n * c, n * dd)


def _router_weights(wg, we):
    d = wg.shape[0]
    w = jnp.concatenate([wg, we, jnp.zeros((d, LANES - wg.shape[1] - we.shape[1]), F32)], axis=1).T
    hi = w.astype(BF16)
    return jnp.concatenate([hi, (w - hi.astype(F32)).astype(BF16)], axis=0)


def _moe(layer, meta_t, cnt, hn_slab, moe_w_gate, moe_w_up, moe_w_down):
    dest, pad_lo, pad_hi, blk_expert, nact, nblk = _plan(meta_t, cnt, TB_EXPERT)
    xs = _dispatch(dest, pad_lo, pad_hi, nact, hn_slab, nblk, TB_EXPERT)
    ys = _experts(blk_expert, nact, xs, moe_w_gate, moe_w_up, moe_w_down, layer, TB_EXPERT)
    return dest, ys


def kernel(x, norm_mix_g, norm_ffn_g, norm_final_g, ab_w_in, ab_conv_w, ab_conv_b, rg_w_a, rg_b_a, rg_w_x, rg_b_x, rg_lambda, attn_rel_bias, ab_w_out, c_w_in, c_lb_logits, c_norm_g, c_w_out, moe_router_group, moe_router_expert, moe_w_gate, moe_w_up, moe_w_down):
    batch, seq, d = x.shape
    t = batch * seq
    xt = x.reshape(t, d)
    row = lambda v: v.reshape(1, -1)

    aw = ab_conv_w.shape[2]
    qkv, ya = _proj0_lru(x, row(norm_mix_g[0]), ab_w_in[0].astype(BF16), ab_conv_w[0], row(ab_conv_b[0]),
                         _block_diag(rg_w_a[0]).astype(BF16), _block_diag(rg_w_x[0]).astype(BF16),
                         row(rg_b_a[0]), row(rg_b_x[0]), row(rg_lambda[0]))
    tk = TQ + LEFT_CHUNKS * CHUNK
    dist = jnp.clip(tk - jnp.arange(TQ + tk), -REL_CLIP, REL_CLIP) + REL_CLIP
    bias = _attn_bias(attn_rel_bias[0][:, None, dist])
    yb = _attention(qkv, bias, batch)
    wo = ab_w_out[0].astype(BF16)
    wt = _router_weights(moe_router_group[0], moe_router_expert[0])
    h1, hn1, meta1, metat1, cnt1 = _proj_route(ya, 0, yb, 0, xt, wo[:aw], wo[aw:], row(norm_ffn_g[0]), wt)
    dest1, ys1 = _moe(0, metat1, cnt1, hn1, moe_w_gate, moe_w_up, moe_w_down)

    dk = c_norm_g.shape[1] // 8
    h2, qs, lf, iv, gs = _comb_proj1(dest1, h1, meta1, ys1, row(norm_mix_g[1]), c_w_in[0].astype(BF16),
                                     c_lb_logits, 1)
    om = _hgrn(qs, lf, iv, gs, row(c_norm_g[0]), batch, dk)
    wo = c_w_out[0].astype(BF16)
    half = wo.shape[0] // 2
    wt = _router_weights(moe_router_group[1], moe_router_expert[1])
    h3, hn3, meta3, metat3, cnt3 = _proj_route(om, 0, om, 1, h2, wo[:half], wo[half:], row(norm_ffn_g[1]), wt)
    dest3, ys3 = _moe(1, metat3, cnt3, hn3, moe_w_gate, moe_w_up, moe_w_down)

    out = _comb_final(dest3, h3, meta3, ys3, row(norm_final_g))
    return out.reshape(batch, seq, d)
```

```python
import functools

import jax
import jax.numpy as jnp
from jax import lax
from jax.experimental import pallas as pl
from jax.experimental.pallas import tpu as pltpu

F32 = jnp.float32
BF16 = jnp.bfloat16

EPS = 1e-6
LOG2E = 1.4426950408889634
RG_C = 8.0
CHUNK = 64
LEFT_CHUNKS = 8
REL_CLIP = 256
N_GROUPS = 4
EXPERTS_PER_GROUP = 8
N_EXPERTS = N_GROUPS * EXPERTS_PER_GROUP
TOP_K = 2
ROUTE_ROWS = 40

LANES = 128
SUBLANES = 8
SUB = 16

TS_LRU = 512
TQ = 256
ATTN_GROUP = 4
TM_ROUTE = 1024
TM_ROUTE_SUB = 256
TB_EXPERT = 512
X_SLOTS = 3
TM_COMB = 256
TH = 256
HGRN_GROUP = 4
DMA_UNROLL = 8
COMB_SLOTS = 3
VMEM_MB = 48


def _params(n_axes, vmem_mb=VMEM_MB):
    return pltpu.CompilerParams(dimension_semantics=("arbitrary",) * n_axes,
                                vmem_limit_bytes=vmem_mb * 1024 * 1024)


def _const_spec(shape):
    nd = len(shape)
    return pl.BlockSpec(shape, lambda *_: (0,) * nd)


def _rms(x, g):
    return x * lax.rsqrt(jnp.mean(x * x, axis=-1, keepdims=True) + EPS) * g


def _dot(a, b):
    return jnp.dot(a, b, preferred_element_type=F32)


def _dot_nt(a, b):
    return lax.dot_general(a, b, (((1,), (1,)), ((), ())), preferred_element_type=F32)


def _dot_tn(a, b):
    return lax.dot_general(a, b, (((0,), (0,)), ((), ())), preferred_element_type=F32)


def _split_bf16(x):
    hi = x.astype(BF16)
    lo = (x - hi.astype(F32)).astype(BF16)
    return hi, lo


def _to_slab(ref, val):
    m = val.shape[0]
    for j in range(val.shape[1] // LANES):
        ref[pl.ds(j, m, stride=SUBLANES), :] = val[:, LANES * j:LANES * (j + 1)]


def _from_slab(ref, m):
    return jnp.concatenate([ref[pl.ds(j, m, stride=SUBLANES), :] for j in range(SUBLANES)], axis=1)


def _proj0_lru_kernel(x_ref, g_ref, w_ref, cw_ref, cb_ref, wa_ref, wx_ref, ba_ref, bx_ref, lam_ref,
                      qkv_ref, ya_ref, xbuf, h_ref, a_s, u_s):
    ts = x_ref.shape[0]
    w = ya_ref.shape[1]
    bw = qkv_ref.shape[1] // 3

    @pl.when(pl.program_id(1) == 0)
    def _():
        xbuf[pl.ds(0, SUBLANES), :] = jnp.zeros((SUBLANES, w), F32)
        h_ref[...] = jnp.zeros_like(h_ref)

    hn = _rms(x_ref[...], g_ref[...]).astype(BF16)

    def qkv_cols(c, scale=None):
        z = _dot(hn, w_ref[:, 2 * w + c * bw:2 * w + (c + 1) * bw])
        qkv_ref[:, c * bw:(c + 1) * bw] = (z if scale is None else z * scale).astype(BF16)

    xg = _dot(hn, w_ref[:, :2 * w])

    xbuf[pl.ds(SUBLANES, ts), :] = xg[:, :w]
    nk = cw_ref.shape[0]
    y = cb_ref[...]
    for k in range(nk):
        y = y + cw_ref[nk - 1 - k:nk - k, :] * xbuf[pl.ds(SUBLANES - k, ts), :]
    xbuf[pl.ds(0, SUBLANES), :] = xbuf[pl.ds(ts, SUBLANES), :]

    yb = y.astype(BF16)
    rg = jax.nn.sigmoid(_dot(yb, wa_ref[...]) + ba_ref[...])
    ig = jax.nn.sigmoid(_dot(yb, wx_ref[...]) + bx_ref[...])
    qkv_cols(0, CHUNK ** -0.5 * LOG2E)
    lam = lam_ref[...]
    log_sig = jnp.minimum(lam, 0.0) - jnp.log1p(jnp.exp(-jnp.abs(lam)))
    log_a = RG_C * rg * log_sig
    a = jnp.exp(log_a)
    m = 1.0 - a * a
    u = jnp.where(m > 0.0, m * lax.rsqrt(m), 0.0) * (ig * y)
    qkv_cols(1)

    grp = (ts // SUBLANES, SUBLANES, w)
    a = a.reshape(grp)
    u = u.reshape(grp)
    rowm = lax.broadcasted_iota(jnp.int32, grp, 1)
    for s in (1, 2, 4):
        keep = rowm >= s
        a_sh = jnp.where(keep, pltpu.roll(a, s, 1), 1.0)
        u_sh = jnp.where(keep, pltpu.roll(u, s, 1), 0.0)
        u = a * u_sh + u
        a = a * a_sh
    a_s[...] = a.reshape(ts, w)
    u_s[...] = u.reshape(ts, w)
    qkv_cols(2)

    def group(gi, h):
        off = pl.multiple_of(gi * SUBLANES, SUBLANES)
        hg = a_s[pl.ds(off, SUBLANES), :] * h + u_s[pl.ds(off, SUBLANES), :]
        u_s[pl.ds(off, SUBLANES), :] = hg
        return jnp.broadcast_to(hg[SUBLANES - 1:SUBLANES, :], hg.shape)

    h_ref[...] = lax.fori_loop(0, ts // SUBLANES, group, h_ref[...])

    ga = xg[:, w:]
    gelu = 0.5 * ga * (1.0 + jnp.tanh(0.7978845608028654 * (ga + 0.044715 * (ga * ga * ga))))
    ya_ref[...] = (u_s[...] * gelu).astype(BF16)


def _proj0_lru(x, g, w_in, cw, cb, wa, wx, ba, bx, lam):
    batch, s, d = x.shape
    w = cw.shape[1]
    nqkv = w_in.shape[1] - 2 * w
    ts = min(TS_LRU, s)
    small = [g, w_in, cw, cb, wa, wx, ba, bx, lam]
    blk = lambda width: pl.BlockSpec((None, ts, width), lambda b, j: (b, j, 0))
    qkv, ya = pl.pallas_call(
        _proj0_lru_kernel,
        grid=(batch, s // ts),
        in_specs=[blk(d)] + [_const_spec(a.shape) for a in small],
        out_specs=[blk(nqkv), blk(w)],
        out_shape=[jax.ShapeDtypeStruct((batch, s, nqkv), BF16), jax.ShapeDtypeStruct((batch, s, w), BF16)],
        scratch_shapes=[pltpu.VMEM((ts + SUBLANES, w), F32), pltpu.VMEM((SUBLANES, w), F32),
                        pltpu.VMEM((ts, w), F32), pltpu.VMEM((ts, w), F32)],
        compiler_params=_params(2),
        name="proj0_lru",
    )(x, *small)
    return qkv.reshape(batch * s, nqkv), ya.reshape(batch * s, w)


def _attn_bias_kernel(base_ref, o_ref):
    tq, tk = o_ref.shape
    nkb = tk // tq
    row = base_ref[...]
    full = pltpu.roll(jnp.broadcast_to(row, (tq, tq + tk)), 0, 1, stride=1, stride_axis=0)
    qc = lax.broadcasted_iota(jnp.int32, (tq, tk), 0) // CHUNK
    col = lax.broadcasted_iota(jnp.int32, (tq, tk), 1)
    kc = col // CHUNK
    first_ok = (nkb - 1 - pl.program_id(0)) * tq
    valid = (kc >= qc) & (kc <= qc + LEFT_CHUNKS) & (col >= first_ok)
    o_ref[...] = jnp.where(valid, full[:, tq:] * LOG2E, -1e30)


def _attn_bias(base):
    h = base.shape[0]
    tk = TQ + LEFT_CHUNKS * CHUNK
    return pl.pallas_call(
        _attn_bias_kernel,
        grid=(tk // TQ, h),
        in_specs=[pl.BlockSpec((None, 1, TQ + tk), lambda v, i: (i, 0, 0))],
        out_specs=pl.BlockSpec((None, None, TQ, tk), lambda v, i: (v, i, 0, 0)),
        out_shape=jax.ShapeDtypeStruct((tk // TQ, h, TQ, tk), F32),
        compiler_params=_params(2),
        name="attn_bias",
    )(base)


def _attn_kernel(q_ref, k0_ref, k1_ref, k2_ref, v0_ref, v1_ref, v2_ref, bias_ref, o_ref):
    tq = q_ref.shape[0]
    half_w = LANES // 2
    lo = lax.broadcasted_iota(jnp.int32, (tq, LANES), 1) < half_w
    lo_k = lax.broadcasted_iota(jnp.int32, (3 * tq, LANES), 1) < half_w
    k_refs = (k0_ref, k1_ref, k2_ref)
    v_refs = (v0_ref, v1_ref, v2_ref)
    npair = q_ref.shape[1] // LANES
    for j0 in range(0, npair, ATTN_GROUP):
        pairs = range(j0, min(j0 + ATTN_GROUP, npair))
        qk, vh = {}, {}
        for j in pairs:
            sl = slice(LANES * j, LANES * (j + 1))
            q2 = q_ref[:, sl]
            kcat = jnp.concatenate([r[:, sl] for r in k_refs], axis=0)
            vcat = jnp.concatenate([r[:, sl] for r in v_refs], axis=0)
            one = jnp.ones_like(vcat)
            zero = jnp.zeros_like(q2)
            vh[j] = (jnp.where(lo_k, vcat, one), jnp.where(lo_k, one, vcat))
            qk[j] = [_dot_nt(jnp.where(lo if half == 0 else jnp.logical_not(lo), q2, zero), kcat)
                     for half in range(2)]
        ps = {}
        for j in pairs:
            ps[j] = []
            for half in range(2):
                s = qk[j][half] + bias_ref[2 * j + half]
                ps[j].append(jnp.exp2(s - jnp.max(s, axis=-1, keepdims=True)).astype(BF16))
        for j in pairs:
            pv = [_dot(ps[j][half], vh[j][half]) for half in range(2)]
            num = jnp.where(lo, pv[0], pv[1])
            den = pltpu.roll(jnp.where(lo, pv[1], pv[0]), half_w, 1)
            o_ref[:, LANES * j:LANES * (j + 1)] = (num / den).astype(BF16)


def _attention(qkv, bias, batch):
    t, w3 = qkv.shape
    w = w3 // 3
    s = t // batch
    qkv3 = qkv.reshape(batch, s, w3)
    nvar = bias.shape[0]

    def kv_spec(colblk, back):
        return pl.BlockSpec((None, TQ, w), lambda i, b: (b, jnp.maximum(i - back, 0), colblk))

    out = pl.pallas_call(
        _attn_kernel,
        grid=(s // TQ, batch),
        in_specs=[pl.BlockSpec((None, TQ, w), lambda i, b: (b, i, 0)),
                  kv_spec(1, 2), kv_spec(1, 1), kv_spec(1, 0),
                  kv_spec(2, 2), kv_spec(2, 1), kv_spec(2, 0),
                  pl.BlockSpec((None,) + bias.shape[1:], lambda i, b: (jnp.minimum(i, nvar - 1), 0, 0, 0))],
        out_specs=pl.BlockSpec((None, TQ, w), lambda i, b: (b, i, 0)),
        out_shape=jax.ShapeDtypeStruct((batch, s, w), BF16),
        compiler_params=_params(2),
        name="attention",
    )(qkv3, qkv3, qkv3, qkv3, qkv3, qkv3, qkv3, bias)
    return out.reshape(t, w)


def _proj_route_kernel(a_ref, b_ref, r_ref, wa_ref, wb_ref, g_ref, wt_ref,
                       h_ref, hn_ref, meta_ref, metat_ref, cnt_ref, carry_ref):
    tm = a_ref.shape[0]
    ts = min(TM_ROUTE_SUB, tm)

    @pl.when(pl.program_id(0) == 0)
    def _():
        carry_ref[...] = jnp.zeros_like(carry_ref)

    wt = wt_ref[...]
    ridx = lax.broadcasted_iota(jnp.int32, (ROUTE_ROWS, ts), 0).astype(F32)
    r8 = lax.broadcasted_iota(jnp.int32, (SUBLANES, ts), 0)
    rr = lax.broadcasted_iota(jnp.int32, (ts, ts), 0)
    cc = lax.broadcasted_iota(jnp.int32, (ts, ts), 1)
    utri = jnp.where(rr < cc, 1.0, 0.0).astype(BF16)
    neg = -jnp.inf
    far = float(LANES)
    carry = carry_ref[...]

    all_logits = []
    for sub in range(tm // ts):
        rows = pl.ds(sub * ts, ts)
        h = r_ref[rows, :] + (_dot(a_ref[rows, :], wa_ref[...]) + _dot(b_ref[rows, :], wb_ref[...]))
        h_ref[rows, :] = h
        hn = _rms(h, g_ref[...])
        _to_slab(hn_ref.at[pl.ds(sub * ts * SUBLANES, ts * SUBLANES), :], hn)

        hi, lo = _split_bf16(hn)
        p_hi = _dot_nt(wt, hi)
        p_lo = _dot_nt(wt[:LANES], lo)
        all_logits.append((p_hi[:ROUTE_ROWS] + p_hi[LANES:LANES + ROUTE_ROWS]) + p_lo[:ROUTE_ROWS])

    for sub, logits in enumerate(all_logits):
        rows = pl.ds(sub * ts, ts)

        def first_max(mask, logits=logits):
            v = jnp.max(jnp.where(mask, logits, neg), axis=0, keepdims=True)
            idx = jnp.min(jnp.where(mask & (logits == v), ridx, far), axis=0, keepdims=True)
            return v, idx

        gmask = ridx < N_GROUPS
        gmax, gidx = first_max(gmask)
        g_gate = 1.0 / jnp.sum(jnp.where(gmask, jnp.exp(logits - gmax), 0.0), axis=0, keepdims=True)
        e_lo = N_GROUPS + EXPERTS_PER_GROUP * gidx
        emask = (ridx >= e_lo) & (ridx < e_lo + EXPERTS_PER_GROUP)
        v1, i1 = first_max(emask)
        v2, i2 = first_max(emask & (ridx != i1))
        tt = jnp.exp(v2 - v1)
        w1 = g_gate / (1.0 + tt)
        w2 = g_gate * tt / (1.0 + tt)

        sel1 = ridx == i1
        sel2 = ridx == i2
        onehot = jnp.where(sel1 | sel2, 1.0, 0.0)
        before = _dot(onehot.astype(BF16), utri) + carry[:, 0:1]
        rank1 = jnp.sum(jnp.where(sel1, before, 0.0), axis=0, keepdims=True)
        rank2 = jnp.sum(jnp.where(sel2, before, 0.0), axis=0, keepdims=True)
        carry = carry + jnp.sum(onehot, axis=1, keepdims=True)

        mt = jnp.zeros((SUBLANES, ts), F32)
        for c, val in enumerate((i1 - N_GROUPS, i2 - N_GROUPS, rank1, rank2, w1, w2)):
            mt = jnp.where(r8 == c, val, mt)
        metat_ref[:, rows] = mt
        meta_ref[rows, :] = jnp.concatenate([mt, jnp.zeros((LANES - SUBLANES, ts), F32)], axis=0).T

    carry_ref[...] = carry
    cnt_ref[...] = carry


def _proj_route(a, acol, b, bcol, resid, wa, wb, g, wt):
    t, d = resid.shape
    kw = wa.shape[0]
    tm = min(TM_ROUTE, t)
    return pl.pallas_call(
        _proj_route_kernel,
        grid=(t // tm,),
        in_specs=[pl.BlockSpec((tm, kw), lambda i: (i, acol)), pl.BlockSpec((tm, kw), lambda i: (i, bcol)),
                  pl.BlockSpec((tm, d), lambda i: (i, 0)),
                  _const_spec(wa.shape), _const_spec(wb.shape), _const_spec(g.shape), _const_spec(wt.shape)],
        out_specs=[pl.BlockSpec((tm, d), lambda i: (i, 0)),
                   pl.BlockSpec((tm * SUBLANES, LANES), lambda i: (i, 0)),
                   pl.BlockSpec((tm, LANES), lambda i: (i, 0)),
                   pl.BlockSpec((SUBLANES, tm), lambda i: (0, i)),
                   pl.BlockSpec((ROUTE_ROWS, LANES), lambda i: (0, 0))],
        out_shape=[jax.ShapeDtypeStruct((t, d), F32),
                   jax.ShapeDtypeStruct((t * SUBLANES, LANES), F32),
                   jax.ShapeDtypeStruct((t, LANES), F32),
                   jax.ShapeDtypeStruct((SUBLANES, t), F32),
                   jax.ShapeDtypeStruct((ROUTE_ROWS, LANES), F32)],
        scratch_shapes=[pltpu.VMEM((ROUTE_ROWS, LANES), F32)],
        compiler_params=_params(1),
        name="proj_route",
    )(a, b, resid, wa, wb, g, wt)


def _plan(meta_t, cnt, tb):
    t = meta_t.shape[1]
    counts = cnt[N_GROUPS:N_GROUPS + N_EXPERTS, 0].astype(jnp.int32)
    padded = (counts + tb - 1) // tb * tb
    pad_end = jnp.cumsum(padded)
    pad_start = pad_end - padded
    dest = []
    for k in range(TOP_K):
        expert = meta_t[k].astype(jnp.int32)
        start = jnp.zeros((t,), jnp.int32)
        for e in range(N_EXPERTS):
            start = jnp.where(expert == e, pad_start[e], start)
        dest.append(start + meta_t[TOP_K + k].astype(jnp.int32))
    dest = jnp.concatenate(dest)
    nblk = -(-(t * TOP_K + N_EXPERTS * (tb - 1)) // tb)
    blk_start = jnp.arange(nblk, dtype=jnp.int32) * tb
    blk_expert = jnp.minimum(jnp.sum((pad_end[None, :] <= blk_start[:, None]).astype(jnp.int32), axis=1),
                             N_EXPERTS - 1)
    nact = (pad_end[-1:] // tb).astype(jnp.int32)
    return dest, pad_start + counts, pad_end, blk_expert, nact, nblk


def _dispatch_kernel(dest_ref, plo_ref, phi_ref, nact_ref, hn_ref, xs_ref, zero_ref, sem, zsem):
    tm = hn_ref.shape[0] // SUBLANES
    t = dest_ref.shape[0] // TOP_K
    tb = zero_ref.shape[0] // SUBLANES
    nblk = xs_ref.shape[0] // zero_ref.shape[0]
    i = pl.program_id(0)

    def zero_copy(row, nrows):
        return pltpu.make_async_copy(
            zero_ref.at[pl.ds(0, nrows * SUBLANES), :],
            xs_ref.at[pl.ds(pl.multiple_of(row * SUBLANES, SUBLANES), nrows * SUBLANES), :], zsem)

    def zero_fill(start):
        def go(cp):
            cp.start() if start else cp.wait()

        def per_expert(e, c):
            off = plo_ref[e]
            n = phi_ref[e] - off
            bit = tb // 2
            while bit:
                pl.when((n & bit) != 0)(functools.partial(lambda o, b: go(zero_copy(o, b)), off, bit))
                off = off + (n & bit)
                bit //= 2
            return c

        def per_block(b, c):
            go(zero_copy(b * tb, tb))
            return c

        lax.fori_loop(0, N_EXPERTS, per_expert, 0)
        lax.fori_loop(nact_ref[0], nblk, per_block, 0)

    @pl.when(i == 0)
    def _():
        zero_ref[...] = jnp.zeros_like(zero_ref)
        zero_fill(True)

    def copy(r, k):
        d = dest_ref[k * t + i * tm + r]
        return pltpu.make_async_copy(
            hn_ref.at[pl.ds(pl.multiple_of(r * SUBLANES, SUBLANES), SUBLANES), :],
            xs_ref.at[pl.ds(pl.multiple_of(d * SUBLANES, SUBLANES), SUBLANES), :], sem)

    def start(r, c):
        for k in range(TOP_K):
            copy(r, k).start(priority=k)
        return c

    lax.fori_loop(0, tm, start, 0, unroll=DMA_UNROLL)
    for k in range(TOP_K):
        pltpu.make_async_copy(hn_ref, xs_ref.at[pl.ds(0, tm * SUBLANES), :], sem).wait()

    @pl.when(i == 0)
    def _():
        zero_fill(False)


def _dispatch(dest, pad_lo, pad_hi, nact, hn_slab, nblk, tb):
    t = hn_slab.shape[0] // SUBLANES
    tm = min(TM_COMB, t)
    return pl.pallas_call(
        _dispatch_kernel,
        grid_spec=pltpu.PrefetchScalarGridSpec(
            num_scalar_prefetch=4,
            grid=(t // tm,),
            in_specs=[pl.BlockSpec((tm * SUBLANES, LANES), lambda i, *_: (i, 0))],
            out_specs=pl.BlockSpec(memory_space=pl.ANY),
            scratch_shapes=[pltpu.VMEM((tb * SUBLANES, LANES), F32),
                            pltpu.SemaphoreType.DMA(()), pltpu.SemaphoreType.DMA(())]),
        out_shape=jax.ShapeDtypeStruct((nblk * tb * SUBLANES, LANES), F32),
        compiler_params=_params(1),
        name="dispatch",
    )(dest, pad_lo, pad_hi, nact, hn_slab)


def _expert_kernel(be_ref, nact_ref, xs_ref, wg_ref, wu_ref, wd_ref, ys_ref, xbuf, xsem, wg_s, wu_s, wd_s):
    i = pl.program_id(0)
    n = pl.num_programs(0)
    rows = ys_ref.shape[0]
    tb = rows // SUBLANES
    active = i < nact_ref[0]
    changed = (i == 0) | (be_ref[i] != be_ref[jnp.maximum(i - 1, 0)])
    slot = i % X_SLOTS
    ahead = X_SLOTS - 1

    def x_copy(step, sl):
        blk = jnp.minimum(step, nact_ref[0] - 1)
        return pltpu.make_async_copy(xs_ref.at[pl.ds(pl.multiple_of(blk * rows, rows), rows), :],
                                     xbuf.at[sl], xsem.at[sl])

    @pl.when(i == 0)
    def _():
        for a in range(ahead):
            x_copy(a, a).start()

    x_copy(i, slot).wait()
    x_copy(i + ahead, (i + ahead) % X_SLOTS).start()

    @pl.when(i == n - 1)
    def _():
        for a in range(1, X_SLOTS):
            x_copy(i + a, (i + a) % X_SLOTS).wait()

    @pl.when(active & changed)
    def _():
        wg_s[...] = wg_ref[...].astype(BF16)
        wu_s[...] = wu_ref[...].astype(BF16)
        wd_s[...] = wd_ref[...].astype(BF16)

    @pl.when(active)
    def _():
        x = _from_slab(xbuf.at[slot], tb).astype(BF16)
        gate = _dot(x, wg_s[...])
        up = _dot(x, wu_s[...])
        hid = (gate * jax.nn.sigmoid(gate) * up).astype(BF16)
        _to_slab(ys_ref, _dot(hid, wd_s[...]))

    @pl.when(jnp.logical_not(active))
    def _():
        ys_ref[...] = jnp.zeros_like(ys_ref)


def _experts(blk_expert, nact, xs, wg, wu, wd, layer, tb):
    nblk = blk_expert.shape[0]
    _, _, d, de = wg.shape

    def w_map(i, be, na):
        return (layer, be[jnp.minimum(i, na[0] - 1)], 0, 0)

    return pl.pallas_call(
        _expert_kernel,
        grid_spec=pltpu.PrefetchScalarGridSpec(
            num_scalar_prefetch=2,
            grid=(nblk,),
            in_specs=[pl.BlockSpec(memory_space=pl.ANY),
                      pl.BlockSpec((None, None, d, de), w_map), pl.BlockSpec((None, None, d, de), w_map),
                      pl.BlockSpec((None, None, de, d), w_map)],
            out_specs=pl.BlockSpec((tb * SUBLANES, LANES), lambda i, be, na: (i, 0)),
            scratch_shapes=[pltpu.VMEM((X_SLOTS, tb * SUBLANES, LANES), F32), pltpu.SemaphoreType.DMA((X_SLOTS,)),
                            pltpu.VMEM((d, de), BF16), pltpu.VMEM((d, de), BF16), pltpu.VMEM((de, d), BF16)]),
        out_shape=jax.ShapeDtypeStruct(xs.shape, F32),
        compiler_params=_params(1),
        name="experts",
    )(blk_expert, nact, xs, wg, wu, wd)


def _combine(dest_ref, ys_ref, meta_ref, buf_ref, sem_ref, tm):
    i = pl.program_id(0)
    n = pl.num_programs(0)
    slot = i % COMB_SLOTS
    ahead = COMB_SLOTS - 1
    t = dest_ref.shape[0] // TOP_K

    def copy(tile, sl, r, k):
        d = dest_ref[k * t + tile * tm + r]
        row = r * SUBLANES if isinstance(r, int) else pl.multiple_of(r * SUBLANES, SUBLANES)
        return pltpu.make_async_copy(
            ys_ref.at[pl.ds(pl.multiple_of(d * SUBLANES, SUBLANES), SUBLANES), :],
            buf_ref.at[sl, k, pl.ds(row, SUBLANES), :],
            sem_ref.at[sl])

    def start_tile(tile, sl):
        def body(r, c):
            for k in range(TOP_K):
                copy(tile, sl, r, k).start(priority=k)
            return c
        lax.fori_loop(0, tm, body, 0, unroll=DMA_UNROLL)

    def wait_tile(sl):
        for k in range(TOP_K):
            pltpu.make_async_copy(ys_ref.at[pl.ds(0, tm * SUBLANES), :], buf_ref.at[sl, k], sem_ref.at[sl]).wait()

    @pl.when(i == 0)
    def _():
        for a in range(ahead):
            start_tile(jnp.minimum(a, n - 1), a)

    wait_tile(slot)

    nxt = jnp.minimum(i + ahead, n - 1)
    nslot = (i + ahead) % COMB_SLOTS

    def issue(part, nparts):
        rows = tm // nparts
        for r in range(part * rows, (part + 1) * rows):
            for k in range(TOP_K):
                copy(nxt, nslot, r, k).start(priority=k)

    @pl.when(i == n - 1)
    def _():
        for a in range(1, ahead):
            wait_tile((i + a) % COMB_SLOTS)

    y = None
    for k in range(TOP_K):
        gate = meta_ref[:, 2 * TOP_K + k:2 * TOP_K + k + 1]
        term = _from_slab(buf_ref.at[slot, k], tm) * gate
        y = term if y is None else y + term
    return y, issue


def _combine_drain(buf_ref, sem_ref, ys_ref, tm):
    i = pl.program_id(0)

    @pl.when(i == pl.num_programs(0) - 1)
    def _():
        sl = (i + COMB_SLOTS - 1) % COMB_SLOTS
        for k in range(TOP_K):
            pltpu.make_async_copy(ys_ref.at[pl.ds(0, tm * SUBLANES), :], buf_ref.at[sl, k], sem_ref.at[sl]).wait()


def _combine_scratch(tm):
    return [pltpu.VMEM((COMB_SLOTS, TOP_K, tm * SUBLANES, LANES), F32), pltpu.SemaphoreType.DMA((COMB_SLOTS,))]


def _comb_proj1_kernel(dest_ref, h_ref, meta_ref, ys_ref, g_ref, w_ref, lbl_ref,
                       h2_ref, qs_ref, lf_ref, iv_ref, gs_ref, buf_ref, sem_ref, *, lb_rows):
    tm = h_ref.shape[0]
    y, issue = _combine(dest_ref, ys_ref, meta_ref, buf_ref, sem_ref, tm)
    h2 = h_ref[...] + y
    h2_ref[...] = h2
    hn = _rms(h2, g_ref[...]).astype(BF16)
    d = h2.shape[1]
    lbl = lbl_ref[...]
    ex = jnp.exp(lbl - jnp.max(lbl, axis=0, keepdims=True))
    lb = jnp.sum(ex[:lb_rows], axis=0, keepdims=True) / jnp.sum(ex, axis=0, keepdims=True)

    def col(c):
        return _dot(hn, w_ref[:, d * c:d * (c + 1)])

    q = col(0)
    issue(0, 4)
    fz = col(1)
    issue(1, 4)
    iv = col(2)
    issue(2, 4)
    gz = col(3)
    issue(3, 4)
    qs_ref[...] = (q * jax.nn.sigmoid(q)).astype(BF16)
    lf_ref[...] = jnp.log(lb + (1.0 - lb) * jax.nn.sigmoid(fz)) * LOG2E
    iv_ref[...] = iv.astype(BF16)
    gs_ref[...] = (gz * jax.nn.sigmoid(gz)).astype(BF16)
    _combine_drain(buf_ref, sem_ref, ys_ref, tm)


def _comb_proj1(dest, h, meta, ys, g, w, lb_logits, lb_rows):
    t, d = h.shape
    tm = min(TM_COMB, t)
    row = lambda i, dd: (i, 0)
    const = lambda shape: pl.BlockSpec(shape, lambda i, dd: (0,) * len(shape))
    return pl.pallas_call(
        functools.partial(_comb_proj1_kernel, lb_rows=lb_rows),
        grid_spec=pltpu.PrefetchScalarGridSpec(
            num_scalar_prefetch=1,
            grid=(t // tm,),
            in_specs=[pl.BlockSpec((tm, d), row), pl.BlockSpec((tm, LANES), row),
                      pl.BlockSpec(memory_space=pl.ANY),
                      const(g.shape), const(w.shape), const(lb_logits.shape)],
            out_specs=[pl.BlockSpec((tm, d), row)] * 5,
            scratch_shapes=_combine_scratch(tm)),
        out_shape=[jax.ShapeDtypeStruct((t, d), F32), jax.ShapeDtypeStruct((t, d), BF16),
                   jax.ShapeDtypeStruct((t, d), F32), jax.ShapeDtypeStruct((t, d), BF16),
                   jax.ShapeDtypeStruct((t, d), BF16)],
        compiler_params=_params(1),
        name="comb_proj1",
    )(dest, h, meta, ys, g, w, lb_logits)


def _comb_final_kernel(dest_ref, h_ref, meta_ref, ys_ref, g_ref, o_ref, buf_ref, sem_ref):
    tm = h_ref.shape[0]
    y, issue = _combine(dest_ref, ys_ref, meta_ref, buf_ref, sem_ref, tm)
    issue(0, 1)
    o_ref[...] = _rms(h_ref[...] + y, g_ref[...])
    _combine_drain(buf_ref, sem_ref, ys_ref, tm)


def _comb_final(dest, h, meta, ys, g):
    t, d = h.shape
    tm = min(TM_COMB, t)
    row = lambda i, dd: (i, 0)
    return pl.pallas_call(
        _comb_final_kernel,
        grid_spec=pltpu.PrefetchScalarGridSpec(
            num_scalar_prefetch=1,
            grid=(t // tm,),
            in_specs=[pl.BlockSpec((tm, d), row), pl.BlockSpec((tm, LANES), row),
                      pl.BlockSpec(memory_space=pl.ANY),
                      pl.BlockSpec(g.shape, lambda i, dd: (0, 0))],
            out_specs=pl.BlockSpec((tm, d), row),
            scratch_shapes=_combine_scratch(tm)),
        out_shape=jax.ShapeDtypeStruct((t, d), F32),
        compiler_params=_params(1),
        name="comb_final",
    )(dest, h, meta, ys, g)


def _hgrn_kernel(qs_ref, lf_ref, iv_ref, gs_ref, ng_ref, o_ref, st_ref, d1_s, t16_s, *, dk):
    th = qs_ref.shape[0]
    nsub = CHUNK // SUB

    @pl.when(pl.program_id(1) == 0)
    def _():
        st_ref[...] = jnp.zeros_like(st_ref)

    rr = lax.broadcasted_iota(jnp.int32, (th, th), 0)
    cc = lax.broadcasted_iota(jnp.int32, (th, th), 1)
    same_sub = (rr // SUB) == (cc // SUB)
    same_chunk = (rr // CHUNK) == (cc // CHUNK)
    m_diag = same_sub & (cc <= rr)
    dsub = rr // SUB - cc // SUB
    m_off = [same_chunk & (dsub == dd) for dd in range(1, nsub)]

    hi, lo = _split_bf16(lf_ref[...])
    tri = jnp.where(m_diag, 1.0, 0.0).astype(BF16)
    d1 = _dot(tri, hi) + _dot(tri, lo)
    d1_s[...] = d1
    sub3 = (th // SUB, SUB, d1.shape[1])
    t16_s[...] = jnp.broadcast_to(d1.reshape(sub3)[:, SUB - 1:SUB, :], sub3).reshape(th, d1.shape[1])
    rrow = lax.broadcasted_iota(jnp.int32, (th, dk), 0)
    rsub = (rrow // SUB) % nsub
    rchunk = rrow // CHUNK
    nch = th // CHUNK

    def back_rows(dd):
        return [slice(CHUNK * c + SUB * dd, CHUNK * (c + 1)) for c in range(nch)]

    def prepare(hd):
        ls = pl.ds(hd * dk, dk)
        d1 = d1_s[:, ls]
        t16 = t16_s[:, ls]
        q = qs_ref[:, ls].astype(F32)
        kt = 1.0 - jnp.exp2(lf_ref[:, ls])
        suf = t16 - d1
        kx = (kt * jnp.exp2(suf)).astype(BF16)
        qv = [q * jnp.exp2(-suf)]
        acc = d1
        tail = suf
        for dd in range(1, nsub):
            qv.append(jnp.concatenate([q[sl] * jnp.exp2(acc[sl]) for sl in back_rows(dd)], axis=0))
            acc = acc + jnp.where(rsub >= dd, pltpu.roll(t16, SUB * dd, 0), 0.0)
            tail = tail + jnp.where(rsub < nsub - dd, pltpu.roll(t16, th - SUB * dd, 0), 0.0)
        b = acc
        qb = q * jnp.exp2(b)
        kend = kt * jnp.exp2(tail)
        zero = jnp.zeros_like(q)
        kend_x = jnp.concatenate([jnp.where(rchunk == c, kend, zero) for c in range(nch)], axis=1).astype(BF16)
        qb_x = jnp.concatenate([jnp.where(rchunk == c, qb, zero) for c in range(nch)], axis=1).astype(BF16)
        decs = [jnp.exp2(b[CHUNK * (c + 1) - 1:CHUNK * (c + 1), :]) for c in range(nch)]
        return ls, jnp.concatenate(qv, axis=0).astype(BF16), kx, kend_x, qb_x, decs

    def group(gi, carry):
        heads = [gi * HGRN_GROUP + u for u in range(HGRN_GROUP)]
        prep = [prepare(hd) for hd in heads]
        a4s = [_dot_nt(p[1], p[2]) for p in prep]
        incs = [_dot_tn(iv_ref[:, p[0]], p[3]) for p in prep]
        o_intras = []
        for p, a4 in zip(prep, a4s):
            att = jnp.where(m_diag, a4[:th], 0.0)
            at = th
            for dd in range(1, nsub):
                pieces = []
                for sl in back_rows(dd):
                    n = sl.stop - sl.start
                    pieces += [jnp.zeros((SUB * dd, th), F32), a4[at:at + n]]
                    at += n
                att = jnp.where(m_off[dd - 1], jnp.concatenate(pieces, axis=0), att)
            o_intras.append(_dot(att.astype(BF16), iv_ref[:, p[0]]))
        for hd, p, inc, o_intra in zip(heads, prep, incs, o_intras):
            ls, decs = p[0], p[5]
            st = st_ref[hd]
            starts = []
            for c in range(nch):
                starts.append(st)
                st = st * decs[c] + inc[:, dk * c:dk * (c + 1)]
            st_ref[hd] = st
            oh = o_intra + _dot_nt(p[4], jnp.concatenate(starts, axis=1).astype(BF16))
            on = oh * lax.rsqrt(jnp.mean(oh * oh, axis=-1, keepdims=True) + EPS)
            o_ref[:, ls] = (on * ng_ref[:, ls] * gs_ref[:, ls].astype(F32)).astype(BF16)
        return carry

    for gi in range(qs_ref.shape[1] // dk // HGRN_GROUP):
        group(gi, 0)


def _hgrn(qs, lf, iv, gs, ng, batch, dk):
    t, d = qs.shape
    s = t // batch
    th = min(TH, s)
    blk = pl.BlockSpec((None, th, d), lambda b, j: (b, j, 0))
    r3 = lambda a: a.reshape(batch, s, d)
    out = pl.pallas_call(
        functools.partial(_hgrn_kernel, dk=dk),
        grid=(batch, s // th),
        in_specs=[blk, blk, blk, blk, _const_spec(ng.shape)],
        out_specs=blk,
        out_shape=jax.ShapeDtypeStruct((batch, s, d), BF16),
        scratch_shapes=[pltpu.VMEM((d // dk, dk, dk), F32), pltpu.VMEM((th, d), F32), pltpu.VMEM((th, d), F32)],
        compiler_params=_params(2),
        name="hgrn2",
    )(r3(qs), r3(lf), r3(iv), r3(gs), ng)
    return out.reshape(t, d)


def _block_diag(w):
    n, c, dd = w.shape
    eye = jnp.eye(n, dtype=w.dtype)
    return (eye[:, None, :, None] * w[:, :, None, :]).reshape(n * c, n * dd)


def _router_weights(wg, we):
    d = wg.shape[0]
    w = jnp.concatenate([wg, we, jnp.zeros((d, LANES - wg.shape[1] - we.shape[1]), F32)], axis=1).T
    hi = w.astype(BF16)
    return jnp.concatenate([hi, (w - hi.astype(F32)).astype(BF16)], axis=0)


def _moe(layer, meta_t, cnt, hn_slab, moe_w_gate, moe_w_up, moe_w_down):
    dest, pad_lo, pad_hi, blk_expert, nact, nblk = _plan(meta_t, cnt, TB_EXPERT)
    xs = _dispatch(dest, pad_lo, pad_hi, nact, hn_slab, nblk, TB_EXPERT)
    ys = _experts(blk_expert, nact, xs, moe_w_gate, moe_w_up, moe_w_down, layer, TB_EXPERT)
    return dest, ys


def kernel(x, norm_mix_g, norm_ffn_g, norm_final_g, ab_w_in, ab_conv_w, ab_conv_b, rg_w_a, rg_b_a, rg_w_x, rg_b_x, rg_lambda, attn_rel_bias, ab_w_out, c_w_in, c_lb_logits, c_norm_g, c_w_out, moe_router_group, moe_router_expert, moe_w_gate, moe_w_up, moe_w_down):
    batch, seq, d = x.shape
    t = batch * seq
    xt = x.reshape(t, d)
    row = lambda v: v.reshape(1, -1)

    aw = ab_conv_w.shape[2]
    qkv, ya = _proj0_lru(x, row(norm_mix_g[0]), ab_w_in[0].astype(BF16), ab_conv_w[0], row(ab_conv_b[0]),
                         _block_diag(rg_w_a[0]).astype(BF16), _block_diag(rg_w_x[0]).astype(BF16),
                         row(rg_b_a[0]), row(rg_b_x[0]), row(rg_lambda[0]))
    tk = TQ + LEFT_CHUNKS * CHUNK
    dist = jnp.clip(tk - jnp.arange(TQ + tk), -REL_CLIP, REL_CLIP) + REL_CLIP
    bias = _attn_bias(attn_rel_bias[0][:, None, dist])
    yb = _attention(qkv, bias, batch)
    wo = ab_w_out[0].astype(BF16)
    wt = _router_weights(moe_router_group[0], moe_router_expert[0])
    h1, hn1, meta1, metat1, cnt1 = _proj_route(ya, 0, yb, 0, xt, wo[:aw], wo[aw:], row(norm_ffn_g[0]), wt)
    dest1, ys1 = _moe(0, metat1, cnt1, hn1, moe_w_gate, moe_w_up, moe_w_down)

    dk = c_norm_g.shape[1] // 8
    h2, qs, lf, iv, gs = _comb_proj1(dest1, h1, meta1, ys1, row(norm_mix_g[1]), c_w_in[0].astype(BF16),
                                     c_lb_logits, 1)
    om = _hgrn(qs, lf, iv, gs, row(c_norm_g[0]), batch, dk)
    wo = c_w_out[0].astype(BF16)
    half = wo.shape[0] // 2
    wt = _router_weights(moe_router_group[1], moe_router_expert[1])
    h3, hn3, meta3, metat3, cnt3 = _proj_route(om, 0, om, 1, h2, wo[:half], wo[half:], row(norm_ffn_g[1]), wt)
    dest3, ys3 = _moe(1, metat3, cnt3, hn3, moe_w_gate, moe_w_up, moe_w_down)

    out = _comb_final(dest3, h3, meta3, ys3, row(norm_final_g))
    return out.reshape(batch, seq, d)
```

```python
import functools

import jax
import jax.numpy as jnp
from jax import lax
from jax.experimental import pallas as pl
from jax.experimental.pallas import tpu as pltpu

F32 = jnp.float32
BF16 = jnp.bfloat16

EPS = 1e-6
LOG2E = 1.4426950408889634
RG_C = 8.0
CHUNK = 64
LEFT_CHUNKS = 8
REL_CLIP = 256
N_GROUPS = 4
EXPERTS_PER_GROUP = 8
N_EXPERTS = N_GROUPS * EXPERTS_PER_GROUP
TOP_K = 2
ROUTE_ROWS = 40

LANES = 128
SUBLANES = 8
SUB = 16

TS_LRU = 512
TQ = 256
ATTN_GROUP = 4
TM_ROUTE = 1024
TM_ROUTE_SUB = 256
TB_EXPERT = 512
X_SLOTS = 3
TM_COMB = 256
TH = 256
HGRN_GROUP = 4
DMA_UNROLL = 8
COMB_SLOTS = 3
VMEM_MB = 48


def _params(n_axes, vmem_mb=VMEM_MB):
    return pltpu.CompilerParams(dimension_semantics=("arbitrary",) * n_axes,
                                vmem_limit_bytes=vmem_mb * 1024 * 1024)


def _const_spec(shape):
    nd = len(shape)
    return pl.BlockSpec(shape, lambda *_: (0,) * nd)


def _rms(x, g):
    return x * lax.rsqrt(jnp.mean(x * x, axis=-1, keepdims=True) + EPS) * g


def _dot(a, b):
    return jnp.dot(a, b, preferred_element_type=F32)


def _dot_nt(a, b):
    return lax.dot_general(a, b, (((1,), (1,)), ((), ())), preferred_element_type=F32)


def _dot_tn(a, b):
    return lax.dot_general(a, b, (((0,), (0,)), ((), ())), preferred_element_type=F32)


def _split_bf16(x):
    hi = x.astype(BF16)
    lo = (x - hi.astype(F32)).astype(BF16)
    return hi, lo


def _to_slab(ref, val):
    m = val.shape[0]
    for j in range(val.shape[1] // LANES):
        ref[pl.ds(j, m, stride=SUBLANES), :] = val[:, LANES * j:LANES * (j + 1)]


def _from_slab(ref, m):
    return jnp.concatenate([ref[pl.ds(j, m, stride=SUBLANES), :] for j in range(SUBLANES)], axis=1)


def _proj0_lru_kernel(x_ref, g_ref, w_ref, cw_ref, cb_ref, wa_ref, wx_ref, ba_ref, bx_ref, lam_ref,
                      qkv_ref, ya_ref, xbuf, h_ref, a_s, u_s):
    ts = x_ref.shape[0]
    w = ya_ref.shape[1]
    bw = qkv_ref.shape[1] // 3

    @pl.when(pl.program_id(1) == 0)
    def _():
        xbuf[pl.ds(0, SUBLANES), :] = jnp.zeros((SUBLANES, w), F32)
        h_ref[...] = jnp.zeros_like(h_ref)

    hn = _rms(x_ref[...], g_ref[...]).astype(BF16)

    def qkv_cols(c, scale=None):
        z = _dot(hn, w_ref[:, 2 * w + c * bw:2 * w + (c + 1) * bw])
        qkv_ref[:, c * bw:(c + 1) * bw] = (z if scale is None else z * scale).astype(BF16)

    xg = _dot(hn, w_ref[:, :2 * w])

    xbuf[pl.ds(SUBLANES, ts), :] = xg[:, :w]
    nk = cw_ref.shape[0]
    y = cb_ref[...]
    for k in range(nk):
        y = y + cw_ref[nk - 1 - k:nk - k, :] * xbuf[pl.ds(SUBLANES - k, ts), :]
    xbuf[pl.ds(0, SUBLANES), :] = xbuf[pl.ds(ts, SUBLANES), :]

    yb = y.astype(BF16)
    rg = jax.nn.sigmoid(_dot(yb, wa_ref[...]) + ba_ref[...])
    ig = jax.nn.sigmoid(_dot(yb, wx_ref[...]) + bx_ref[...])
    qkv_cols(0, CHUNK ** -0.5 * LOG2E)
    lam = lam_ref[...]
    log_sig = jnp.minimum(lam, 0.0) - jnp.log1p(jnp.exp(-jnp.abs(lam)))
    log_a = RG_C * rg * log_sig
    a = jnp.exp(log_a)
    m = 1.0 - a * a
    u = jnp.where(m > 0.0, m * lax.rsqrt(m), 0.0) * (ig * y)
    qkv_cols(1)

    grp = (ts // SUBLANES, SUBLANES, w)
    a = a.reshape(grp)
    u = u.reshape(grp)
    rowm = lax.broadcasted_iota(jnp.int32, grp, 1)
    for s in (1, 2, 4):
        keep = rowm >= s
        a_sh = jnp.where(keep, pltpu.roll(a, s, 1), 1.0)
        u_sh = jnp.where(keep, pltpu.roll(u, s, 1), 0.0)
        u = a * u_sh + u
        a = a * a_sh
    a_s[...] = a.reshape(ts, w)
    u_s[...] = u.reshape(ts, w)
    qkv_cols(2)

    def group(gi, h):
        off = pl.multiple_of(gi * SUBLANES, SUBLANES)
        hg = a_s[pl.ds(off, SUBLANES), :] * h + u_s[pl.ds(off, SUBLANES), :]
        u_s[pl.ds(off, SUBLANES), :] = hg
        return jnp.broadcast_to(hg[SUBLANES - 1:SUBLANES, :], hg.shape)

    h_ref[...] = lax.fori_loop(0, ts // SUBLANES, group, h_ref[...])

    ga = xg[:, w:]
    gelu = 0.5 * ga * (1.0 + jnp.tanh(0.7978845608028654 * (ga + 0.044715 * (ga * ga * ga))))
    ya_ref[...] = (u_s[...] * gelu).astype(BF16)


def _proj0_lru(x, g, w_in, cw, cb, wa, wx, ba, bx, lam):
    batch, s, d = x.shape
    w = cw.shape[1]
    nqkv = w_in.shape[1] - 2 * w
    ts = min(TS_LRU, s)
    small = [g, w_in, cw, cb, wa, wx, ba, bx, lam]
    blk = lambda width: pl.BlockSpec((None, ts, width), lambda b, j: (b, j, 0))
    qkv, ya = pl.pallas_call(
        _proj0_lru_kernel,
        grid=(batch, s // ts),
        in_specs=[blk(d)] + [_const_spec(a.shape) for a in small],
        out_specs=[blk(nqkv), blk(w)],
        out_shape=[jax.ShapeDtypeStruct((batch, s, nqkv), BF16), jax.ShapeDtypeStruct((batch, s, w), BF16)],
        scratch_shapes=[pltpu.VMEM((ts + SUBLANES, w), F32), pltpu.VMEM((SUBLANES, w), F32),
                        pltpu.VMEM((ts, w), F32), pltpu.VMEM((ts, w), F32)],
        compiler_params=_params(2),
        name="proj0_lru",
    )(x, *small)
    return qkv.reshape(batch * s, nqkv), ya.reshape(batch * s, w)


def _attn_bias_kernel(base_ref, o_ref):
    tq, tk = o_ref.shape
    nkb = tk // tq
    row = base_ref[...]
    full = pltpu.roll(jnp.broadcast_to(row, (tq, tq + tk)), 0, 1, stride=1, stride_axis=0)
    qc = lax.broadcasted_iota(jnp.int32, (tq, tk), 0) // CHUNK
    col = lax.broadcasted_iota(jnp.int32, (tq, tk), 1)
    kc = col // CHUNK
    first_ok = (nkb - 1 - pl.program_id(0)) * tq
    valid = (kc >= qc) & (kc <= qc + LEFT_CHUNKS) & (col >= first_ok)
    o_ref[...] = jnp.where(valid, full[:, tq:] * LOG2E, -1e30)


def _attn_bias(base):
    h = base.shape[0]
    tk = TQ + LEFT_CHUNKS * CHUNK
    return pl.pallas_call(
        _attn_bias_kernel,
        grid=(tk // TQ, h),
        in_specs=[pl.BlockSpec((None, 1, TQ + tk), lambda v, i: (i, 0, 0))],
        out_specs=pl.BlockSpec((None, None, TQ, tk), lambda v, i: (v, i, 0, 0)),
        out_shape=jax.ShapeDtypeStruct((tk // TQ, h, TQ, tk), F32),
        compiler_params=_params(2),
        name="attn_bias",
    )(base)


def _attn_kernel(q_ref, k0_ref, k1_ref, k2_ref, v0_ref, v1_ref, v2_ref, bias_ref, o_ref):
    tq = q_ref.shape[0]
    half_w = LANES // 2
    lo = lax.broadcasted_iota(jnp.int32, (tq, LANES), 1) < half_w
    lo_k = lax.broadcasted_iota(jnp.int32, (3 * tq, LANES), 1) < half_w
    k_refs = (k0_ref, k1_ref, k2_ref)
    v_refs = (v0_ref, v1_ref, v2_ref)
    npair = q_ref.shape[1] // LANES
    for j0 in range(0, npair, ATTN_GROUP):
        pairs = range(j0, min(j0 + ATTN_GROUP, npair))
        qk, vh = {}, {}
        for j in pairs:
            sl = slice(LANES * j, LANES * (j + 1))
            q2 = q_ref[:, sl]
            kcat = jnp.concatenate([r[:, sl] for r in k_refs], axis=0)
            vcat = jnp.concatenate([r[:, sl] for r in v_refs], axis=0)
            one = jnp.ones_like(vcat)
            zero = jnp.zeros_like(q2)
            vh[j] = (jnp.where(lo_k, vcat, one), jnp.where(lo_k, one, vcat))
            qk[j] = [_dot_nt(jnp.where(lo if half == 0 else jnp.logical_not(lo), q2, zero), kcat)
                     for half in range(2)]
        ps = {}
        for j in pairs:
            ps[j] = []
            for half in range(2):
                s = qk[j][half] + bias_ref[2 * j + half]
                ps[j].append(jnp.exp2(s - jnp.max(s, axis=-1, keepdims=True)).astype(BF16))
        for j in pairs:
            pv = [_dot(ps[j][half], vh[j][half]) for half in range(2)]
            num = jnp.where(lo, pv[0], pv[1])
            den = pltpu.roll(jnp.where(lo, pv[1], pv[0]), half_w, 1)
            o_ref[:, LANES * j:LANES * (j + 1)] = (num / den).astype(BF16)


def _attention(qkv, bias, batch):
    t, w3 = qkv.shape
    w = w3 // 3
    s = t // batch
    qkv3 = qkv.reshape(batch, s, w3)
    nvar = bias.shape[0]

    def kv_spec(colblk, back):
        return pl.BlockSpec((None, TQ, w), lambda i, b: (b, jnp.maximum(i - back, 0), colblk))

    out = pl.pallas_call(
        _attn_kernel,
        grid=(s // TQ, batch),
        in_specs=[pl.BlockSpec((None, TQ, w), lambda i, b: (b, i, 0)),
                  kv_spec(1, 2), kv_spec(1, 1), kv_spec(1, 0),
                  kv_spec(2, 2), kv_spec(2, 1), kv_spec(2, 0),
                  pl.BlockSpec((None,) + bias.shape[1:], lambda i, b: (jnp.minimum(i, nvar - 1), 0, 0, 0))],
        out_specs=pl.BlockSpec((None, TQ, w), lambda i, b: (b, i, 0)),
        out_shape=jax.ShapeDtypeStruct((batch, s, w), BF16),
        compiler_params=_params(2),
        name="attention",
    )(qkv3, qkv3, qkv3, qkv3, qkv3, qkv3, qkv3, bias)
    return out.reshape(t, w)


def _proj_route_kernel(a_ref, b_ref, r_ref, wa_ref, wb_ref, g_ref, wt_ref,
                       h_ref, hn_ref, meta_ref, metat_ref, cnt_ref, carry_ref):
    tm = a_ref.shape[0]
    ts = min(TM_ROUTE_SUB, tm)

    @pl.when(pl.program_id(0) == 0)
    def _():
        carry_ref[...] = jnp.zeros_like(carry_ref)

    wt = wt_ref[...]
    ridx = lax.broadcasted_iota(jnp.int32, (ROUTE_ROWS, ts), 0).astype(F32)
    r8 = lax.broadcasted_iota(jnp.int32, (SUBLANES, ts), 0)
    rr = lax.broadcasted_iota(jnp.int32, (ts, ts), 0)
    cc = lax.broadcasted_iota(jnp.int32, (ts, ts), 1)
    utri = jnp.where(rr < cc, 1.0, 0.0).astype(BF16)
    neg = -jnp.inf
    far = float(LANES)
    carry = carry_ref[...]

    all_logits = []
    for sub in range(tm // ts):
        rows = pl.ds(sub * ts, ts)
        h = r_ref[rows, :] + (_dot(a_ref[rows, :], wa_ref[...]) + _dot(b_ref[rows, :], wb_ref[...]))
        h_ref[rows, :] = h
        hn = _rms(h, g_ref[...])
        _to_slab(hn_ref.at[pl.ds(sub * ts * SUBLANES, ts * SUBLANES), :], hn)

        hi, lo = _split_bf16(hn)
        p_hi = _dot_nt(wt, hi)
        p_lo = _dot_nt(wt[:LANES], lo)
        all_logits.append((p_hi[:ROUTE_ROWS] + p_hi[LANES:LANES + ROUTE_ROWS]) + p_lo[:ROUTE_ROWS])

    for sub, logits in enumerate(all_logits):
        rows = pl.ds(sub * ts, ts)

        def first_max(mask, logits=logits):
            v = jnp.max(jnp.where(mask, logits, neg), axis=0, keepdims=True)
            idx = jnp.min(jnp.where(mask & (logits == v), ridx, far), axis=0, keepdims=True)
            return v, idx

        gmask = ridx < N_GROUPS
        gmax, gidx = first_max(gmask)
        g_gate = 1.0 / jnp.sum(jnp.where(gmask, jnp.exp(logits - gmax), 0.0), axis=0, keepdims=True)
        e_lo = N_GROUPS + EXPERTS_PER_GROUP * gidx
        emask = (ridx >= e_lo) & (ridx < e_lo + EXPERTS_PER_GROUP)
        v1, i1 = first_max(emask)
        v2, i2 = first_max(emask & (ridx != i1))
        tt = jnp.exp(v2 - v1)
        w1 = g_gate / (1.0 + tt)
        w2 = g_gate * tt / (1.0 + tt)

        sel1 = ridx == i1
        sel2 = ridx == i2
        onehot = jnp.where(sel1 | sel2, 1.0, 0.0)
        before = _dot(onehot.astype(BF16), utri) + carry[:, 0:1]
        rank1 = jnp.sum(jnp.where(sel1, before, 0.0), axis=0, keepdims=True)
        rank2 = jnp.sum(jnp.where(sel2, before, 0.0), axis=0, keepdims=True)
        carry = carry + jnp.sum(onehot, axis=1, keepdims=True)

        mt = jnp.zeros((SUBLANES, ts), F32)
        for c, val in enumerate((i1 - N_GROUPS, i2 - N_GROUPS, rank1, rank2, w1, w2)):
            mt = jnp.where(r8 == c, val, mt)
        metat_ref[:, rows] = mt
        meta_ref[rows, :] = jnp.concatenate([mt, jnp.zeros((LANES - SUBLANES, ts), F32)], axis=0).T

    carry_ref[...] = carry
    cnt_ref[...] = carry


def _proj_route(a, acol, b, bcol, resid, wa, wb, g, wt):
    t, d = resid.shape
    kw = wa.shape[0]
    tm = min(TM_ROUTE, t)
    return pl.pallas_call(
        _proj_route_kernel,
        grid=(t // tm,),
        in_specs=[pl.BlockSpec((tm, kw), lambda i: (i, acol)), pl.BlockSpec((tm, kw), lambda i: (i, bcol)),
                  pl.BlockSpec((tm, d), lambda i: (i, 0)),
                  _const_spec(wa.shape), _const_spec(wb.shape), _const_spec(g.shape), _const_spec(wt.shape)],
        out_specs=[pl.BlockSpec((tm, d), lambda i: (i, 0)),
                   pl.BlockSpec((tm * SUBLANES, LANES), lambda i: (i, 0)),
                   pl.BlockSpec((tm, LANES), lambda i: (i, 0)),
                   pl.BlockSpec((SUBLANES, tm), lambda i: (0, i)),
                   pl.BlockSpec((ROUTE_ROWS, LANES), lambda i: (0, 0))],
        out_shape=[jax.ShapeDtypeStruct((t, d), F32),
                   jax.ShapeDtypeStruct((t * SUBLANES, LANES), F32),
                   jax.ShapeDtypeStruct((t, LANES), F32),
                   jax.ShapeDtypeStruct((SUBLANES, t), F32),
                   jax.ShapeDtypeStruct((ROUTE_ROWS, LANES), F32)],
        scratch_shapes=[pltpu.VMEM((ROUTE_ROWS, LANES), F32)],
        compiler_params=_params(1),
        name="proj_route",
    )(a, b, resid, wa, wb, g, wt)


def _plan(meta_t, cnt, tb):
    t = meta_t.shape[1]
    counts = cnt[N_GROUPS:N_GROUPS + N_EXPERTS, 0].astype(jnp.int32)
    padded = (counts + tb - 1) // tb * tb
    pad_end = jnp.cumsum(padded)
    pad_start = pad_end - padded
    dest = []
    for k in range(TOP_K):
        expert = meta_t[k].astype(jnp.int32)
        start = jnp.zeros((t,), jnp.int32)
        for e in range(N_EXPERTS):
            start = jnp.where(expert == e, pad_start[e], start)
        dest.append(start + meta_t[TOP_K + k].astype(jnp.int32))
    dest = jnp.concatenate(dest)
    nblk = -(-(t * TOP_K + N_EXPERTS * (tb - 1)) // tb)
    blk_start = jnp.arange(nblk, dtype=jnp.int32) * tb
    blk_expert = jnp.minimum(jnp.sum((pad_end[None, :] <= blk_start[:, None]).astype(jnp.int32), axis=1),
                             N_EXPERTS - 1)
    used = padded > 0
    order = jnp.cumsum(used.astype(jnp.int32)) - 1
    nused = order[-1] + 1
    slots = jnp.minimum(jnp.arange(N_EXPERTS, dtype=jnp.int32), nused - 1)
    ids = jnp.arange(N_EXPERTS, dtype=jnp.int32)
    used_seq = jnp.sum(jnp.where(used[None, :] & (order[None, :] == slots[:, None]), ids[None, :], 0), axis=1)
    blk_order = order[blk_expert]
    nact = jnp.stack([pad_end[-1] // tb, nused]).astype(jnp.int32)
    return dest, pad_start + counts, pad_end, blk_order, used_seq, nact, nblk


def _dispatch_kernel(dest_ref, plo_ref, phi_ref, nact_ref, hn_ref, xs_ref, zero_ref, sem, zsem):
    tm = hn_ref.shape[0] // SUBLANES
    t = dest_ref.shape[0] // TOP_K
    tb = zero_ref.shape[0] // SUBLANES
    nblk = xs_ref.shape[0] // zero_ref.shape[0]
    i = pl.program_id(0)

    def zero_copy(row, nrows):
        return pltpu.make_async_copy(
            zero_ref.at[pl.ds(0, nrows * SUBLANES), :],
            xs_ref.at[pl.ds(pl.multiple_of(row * SUBLANES, SUBLANES), nrows * SUBLANES), :], zsem)

    def zero_fill(start):
        def go(cp):
            cp.start() if start else cp.wait()

        def per_expert(e, c):
            off = plo_ref[e]
            n = phi_ref[e] - off
            bit = tb // 2
            while bit:
                pl.when((n & bit) != 0)(functools.partial(lambda o, b: go(zero_copy(o, b)), off, bit))
                off = off + (n & bit)
                bit //= 2
            return c

        def per_block(b, c):
            go(zero_copy(b * tb, tb))
            return c

        lax.fori_loop(0, N_EXPERTS, per_expert, 0)
        lax.fori_loop(nact_ref[0], nblk, per_block, 0)

    @pl.when(i == 0)
    def _():
        zero_ref[...] = jnp.zeros_like(zero_ref)
        zero_fill(True)

    def copy(r, k):
        d = dest_ref[k * t + i * tm + r]
        return pltpu.make_async_copy(
            hn_ref.at[pl.ds(pl.multiple_of(r * SUBLANES, SUBLANES), SUBLANES), :],
            xs_ref.at[pl.ds(pl.multiple_of(d * SUBLANES, SUBLANES), SUBLANES), :], sem)

    def start(r, c):
        for k in range(TOP_K):
            copy(r, k).start(priority=k)
        return c

    lax.fori_loop(0, tm, start, 0, unroll=DMA_UNROLL)
    for k in range(TOP_K):
        pltpu.make_async_copy(hn_ref, xs_ref.at[pl.ds(0, tm * SUBLANES), :], sem).wait()

    @pl.when(i == 0)
    def _():
        zero_fill(False)


def _dispatch(dest, pad_lo, pad_hi, nact, hn_slab, nblk, tb):
    t = hn_slab.shape[0] // SUBLANES
    tm = min(TM_COMB, t)
    return pl.pallas_call(
        _dispatch_kernel,
        grid_spec=pltpu.PrefetchScalarGridSpec(
            num_scalar_prefetch=4,
            grid=(t // tm,),
            in_specs=[pl.BlockSpec((tm * SUBLANES, LANES), lambda i, *_: (i, 0))],
            out_specs=pl.BlockSpec(memory_space=pl.ANY),
            scratch_shapes=[pltpu.VMEM((tb * SUBLANES, LANES), F32),
                            pltpu.SemaphoreType.DMA(()), pltpu.SemaphoreType.DMA(())]),
        out_shape=jax.ShapeDtypeStruct((nblk * tb * SUBLANES, LANES), F32),
        compiler_params=_params(1),
        name="dispatch",
    )(dest, pad_lo, pad_hi, nact, hn_slab)


def _expert_kernel(ord_ref, useq_ref, nact_ref, xs_ref, wg_ref, wu_ref, wd_ref, ys_ref,
                   xbuf, xsem, wg_f, wu_f, wd_f, wsem, wg_s, wu_s, wd_s, *, layer):
    i = pl.program_id(0)
    n = pl.num_programs(0)
    rows = ys_ref.shape[0]
    tb = rows // SUBLANES
    active = i < nact_ref[0]
    order = ord_ref[i]
    changed = (i == 0) | (order != ord_ref[jnp.maximum(i - 1, 0)])
    slot = i % X_SLOTS
    ahead = X_SLOTS - 1

    def w_copies(j):
        e = useq_ref[jnp.minimum(j, useq_ref.shape[0] - 1)]
        ws = j % 2
        return [pltpu.make_async_copy(src.at[layer, e], dst.at[ws], wsem.at[ws])
                for src, dst in ((wg_ref, wg_f), (wu_ref, wu_f), (wd_ref, wd_f))]

    @pl.when(i == 0)
    def _():
        for cp in w_copies(0):
            cp.start()

    def x_copy(step, sl):
        blk = jnp.minimum(step, nact_ref[0] - 1)
        return pltpu.make_async_copy(xs_ref.at[pl.ds(pl.multiple_of(blk * rows, rows), rows), :],
                                     xbuf.at[sl], xsem.at[sl])

    @pl.when(i == 0)
    def _():
        for a in range(ahead):
            x_copy(a, a).start()

    x_copy(i, slot).wait()
    x_copy(i + ahead, (i + ahead) % X_SLOTS).start()

    @pl.when(i == n - 1)
    def _():
        for a in range(1, X_SLOTS):
            x_copy(i + a, (i + a) % X_SLOTS).wait()

    @pl.when(active & changed)
    def _():
        for cp in w_copies(order):
            cp.wait()
        for cp in w_copies(order + 1):
            cp.start()
        ws = order % 2
        wg_s[...] = wg_f[ws].astype(BF16)
        wu_s[...] = wu_f[ws].astype(BF16)
        wd_s[...] = wd_f[ws].astype(BF16)

    @pl.when(i == n - 1)
    def _():
        for cp in w_copies(nact_ref[1]):
            cp.wait()

    @pl.when(active)
    def _():
        x = _from_slab(xbuf.at[slot], tb).astype(BF16)
        gate = _dot(x, wg_s[...])
        up = _dot(x, wu_s[...])
        hid = (gate * jax.nn.sigmoid(gate) * up).astype(BF16)
        _to_slab(ys_ref, _dot(hid, wd_s[...]))

    @pl.when(jnp.logical_not(active))
    def _():
        ys_ref[...] = jnp.zeros_like(ys_ref)


def _experts(blk_order, used_seq, nact, xs, wg, wu, wd, layer, tb):
    nblk = blk_order.shape[0]
    _, _, d, de = wg.shape
    anyspec = pl.BlockSpec(memory_space=pl.ANY)
    return pl.pallas_call(
        functools.partial(_expert_kernel, layer=layer),
        grid_spec=pltpu.PrefetchScalarGridSpec(
            num_scalar_prefetch=3,
            grid=(nblk,),
            in_specs=[anyspec, anyspec, anyspec, anyspec],
            out_specs=pl.BlockSpec((tb * SUBLANES, LANES), lambda i, *_: (i, 0)),
            scratch_shapes=[pltpu.VMEM((X_SLOTS, tb * SUBLANES, LANES), F32), pltpu.SemaphoreType.DMA((X_SLOTS,)),
                            pltpu.VMEM((2, d, de), F32), pltpu.VMEM((2, d, de), F32), pltpu.VMEM((2, de, d), F32),
                            pltpu.SemaphoreType.DMA((2,)),
                            pltpu.VMEM((d, de), BF16), pltpu.VMEM((d, de), BF16), pltpu.VMEM((de, d), BF16)]),
        out_shape=jax.ShapeDtypeStruct(xs.shape, F32),
        compiler_params=_params(1),
        name="experts",
    )(blk_order, used_seq, nact, xs, wg, wu, wd)


def _combine(dest_ref, ys_ref, meta_ref, buf_ref, sem_ref, tm):
    i = pl.program_id(0)
    n = pl.num_programs(0)
    slot = i % COMB_SLOTS
    ahead = COMB_SLOTS - 1
    t = dest_ref.shape[0] // TOP_K

    def copy(tile, sl, r, k):
        d = dest_ref[k * t + tile * tm + r]
        row = r * SUBLANES if isinstance(r, int) else pl.multiple_of(r * SUBLANES, SUBLANES)
        return pltpu.make_async_copy(
            ys_ref.at[pl.ds(pl.multiple_of(d * SUBLANES, SUBLANES), SUBLANES), :],
            buf_ref.at[sl, k, pl.ds(row, SUBLANES), :],
            sem_ref.at[sl])

    def start_tile(tile, sl):
        def body(r, c):
            for k in range(TOP_K):
                copy(tile, sl, r, k).start(priority=k)
            return c
        lax.fori_loop(0, tm, body, 0, unroll=DMA_UNROLL)

    def wait_tile(sl):
        for k in range(TOP_K):
            pltpu.make_async_copy(ys_ref.at[pl.ds(0, tm * SUBLANES), :], buf_ref.at[sl, k], sem_ref.at[sl]).wait()

    @pl.when(i == 0)
    def _():
        for a in range(ahead):
            start_tile(jnp.minimum(a, n - 1), a)

    wait_tile(slot)

    nxt = jnp.minimum(i + ahead, n - 1)
    nslot = (i + ahead) % COMB_SLOTS

    def issue(part, nparts):
        rows = tm // nparts
        for r in range(part * rows, (part + 1) * rows):
            for k in range(TOP_K):
                copy(nxt, nslot, r, k).start(priority=k)

    @pl.when(i == n - 1)
    def _():
        for a in range(1, ahead):
            wait_tile((i + a) % COMB_SLOTS)

    y = None
    for k in range(TOP_K):
        gate = meta_ref[:, 2 * TOP_K + k:2 * TOP_K + k + 1]
        term = _from_slab(buf_ref.at[slot, k], tm) * gate
        y = term if y is None else y + term
    return y, issue


def _combine_drain(buf_ref, sem_ref, ys_ref, tm):
    i = pl.program_id(0)

    @pl.when(i == pl.num_programs(0) - 1)
    def _():
        sl = (i + COMB_SLOTS - 1) % COMB_SLOTS
        for k in range(TOP_K):
            pltpu.make_async_copy(ys_ref.at[pl.ds(0, tm * SUBLANES), :], buf_ref.at[sl, k], sem_ref.at[sl]).wait()


def _combine_scratch(tm):
    return [pltpu.VMEM((COMB_SLOTS, TOP_K, tm * SUBLANES, LANES), F32), pltpu.SemaphoreType.DMA((COMB_SLOTS,))]


def _comb_proj1_kernel(dest_ref, h_ref, meta_ref, ys_ref, g_ref, w_ref, lbl_ref,
                       h2_ref, qs_ref, lf_ref, iv_ref, gs_ref, buf_ref, sem_ref, *, lb_rows):
    tm = h_ref.shape[0]
    y, issue = _combine(dest_ref, ys_ref, meta_ref, buf_ref, sem_ref, tm)
    h2 = h_ref[...] + y
    h2_ref[...] = h2
    hn = _rms(h2, g_ref[...]).astype(BF16)
    d = h2.shape[1]
    lbl = lbl_ref[...]
    ex = jnp.exp(lbl - jnp.max(lbl, axis=0, keepdims=True))
    lb = jnp.sum(ex[:lb_rows], axis=0, keepdims=True) / jnp.sum(ex, axis=0, keepdims=True)

    def col(c):
        return _dot(hn, w_ref[:, d * c:d * (c + 1)])

    q = col(0)
    issue(0, 4)
    fz = col(1)
    issue(1, 4)
    iv = col(2)
    issue(2, 4)
    gz = col(3)
    issue(3, 4)
    qs_ref[...] = (q * jax.nn.sigmoid(q)).astype(BF16)
    lf_ref[...] = jnp.log(lb + (1.0 - lb) * jax.nn.sigmoid(fz)) * LOG2E
    iv_ref[...] = iv.astype(BF16)
    gs_ref[...] = (gz * jax.nn.sigmoid(gz)).astype(BF16)
    _combine_drain(buf_ref, sem_ref, ys_ref, tm)


def _comb_proj1(dest, h, meta, ys, g, w, lb_logits, lb_rows):
    t, d = h.shape
    tm = min(TM_COMB, t)
    row = lambda i, dd: (i, 0)
    const = lambda shape: pl.BlockSpec(shape, lambda i, dd: (0,) * len(shape))
    return pl.pallas_call(
        functools.partial(_comb_proj1_kernel, lb_rows=lb_rows),
        grid_spec=pltpu.PrefetchScalarGridSpec(
            num_scalar_prefetch=1,
            grid=(t // tm,),
            in_specs=[pl.BlockSpec((tm, d), row), pl.BlockSpec((tm, LANES), row),
                      pl.BlockSpec(memory_space=pl.ANY),
                      const(g.shape), const(w.shape), const(lb_logits.shape)],
            out_specs=[pl.BlockSpec((tm, d), row)] * 5,
            scratch_shapes=_combine_scratch(tm)),
        out_shape=[jax.ShapeDtypeStruct((t, d), F32), jax.ShapeDtypeStruct((t, d), BF16),
                   jax.ShapeDtypeStruct((t, d), F32), jax.ShapeDtypeStruct((t, d), BF16),
                   jax.ShapeDtypeStruct((t, d), BF16)],
        compiler_params=_params(1),
        name="comb_proj1",
    )(dest, h, meta, ys, g, w, lb_logits)


def _comb_final_kernel(dest_ref, h_ref, meta_ref, ys_ref, g_ref, o_ref, buf_ref, sem_ref):
    tm = h_ref.shape[0]
    y, issue = _combine(dest_ref, ys_ref, meta_ref, buf_ref, sem_ref, tm)
    issue(0, 1)
    o_ref[...] = _rms(h_ref[...] + y, g_ref[...])
    _combine_drain(buf_ref, sem_ref, ys_ref, tm)


def _comb_final(dest, h, meta, ys, g):
    t, d = h.shape
    tm = min(TM_COMB, t)
    row = lambda i, dd: (i, 0)
    return pl.pallas_call(
        _comb_final_kernel,
        grid_spec=pltpu.PrefetchScalarGridSpec(
            num_scalar_prefetch=1,
            grid=(t // tm,),
            in_specs=[pl.BlockSpec((tm, d), row), pl.BlockSpec((tm, LANES), row),
                      pl.BlockSpec(memory_space=pl.ANY),
                      pl.BlockSpec(g.shape, lambda i, dd: (0, 0))],
            out_specs=pl.BlockSpec((tm, d), row),
            scratch_shapes=_combine_scratch(tm)),
        out_shape=jax.ShapeDtypeStruct((t, d), F32),
        compiler_params=_params(1),
        name="comb_final",
    )(dest, h, meta, ys, g)


def _hgrn_kernel(qs_ref, lf_ref, iv_ref, gs_ref, ng_ref, o_ref, st_ref, d1_s, t16_s, *, dk):
    th = qs_ref.shape[0]
    nsub = CHUNK // SUB

    @pl.when(pl.program_id(1) == 0)
    def _():
        st_ref[...] = jnp.zeros_like(st_ref)

    rr = lax.broadcasted_iota(jnp.int32, (th, th), 0)
    cc = lax.broadcasted_iota(jnp.int32, (th, th), 1)
    same_sub = (rr // SUB) == (cc // SUB)
    same_chunk = (rr // CHUNK) == (cc // CHUNK)
    m_diag = same_sub & (cc <= rr)
    dsub = rr // SUB - cc // SUB
    m_off = [same_chunk & (dsub == dd) for dd in range(1, nsub)]

    hi, lo = _split_bf16(lf_ref[...])
    tri = jnp.where(m_diag, 1.0, 0.0).astype(BF16)
    d1 = _dot(tri, hi) + _dot(tri, lo)
    d1_s[...] = d1
    sub3 = (th // SUB, SUB, d1.shape[1])
    t16_s[...] = jnp.broadcast_to(d1.reshape(sub3)[:, SUB - 1:SUB, :], sub3).reshape(th, d1.shape[1])
    rrow = lax.broadcasted_iota(jnp.int32, (th, dk), 0)
    rsub = (rrow // SUB) % nsub
    rchunk = rrow // CHUNK
    nch = th // CHUNK

    def back_rows(dd):
        return [slice(CHUNK * c + SUB * dd, CHUNK * (c + 1)) for c in range(nch)]

    def prepare(hd):
        ls = pl.ds(hd * dk, dk)
        d1 = d1_s[:, ls]
        t16 = t16_s[:, ls]
        q = qs_ref[:, ls].astype(F32)
        kt = 1.0 - jnp.exp2(lf_ref[:, ls])
        suf = t16 - d1
        kx = (kt * jnp.exp2(suf)).astype(BF16)
        qv = [q * jnp.exp2(-suf)]
        acc = d1
        tail = suf
        for dd in range(1, nsub):
            qv.append(jnp.concatenate([q[sl] * jnp.exp2(acc[sl]) for sl in back_rows(dd)], axis=0))
            acc = acc + jnp.where(rsub >= dd, pltpu.roll(t16, SUB * dd, 0), 0.0)
            tail = tail + jnp.where(rsub < nsub - dd, pltpu.roll(t16, th - SUB * dd, 0), 0.0)
        b = acc
        qb = q * jnp.exp2(b)
        kend = kt * jnp.exp2(tail)
        zero = jnp.zeros_like(q)
        kend_x = jnp.concatenate([jnp.where(rchunk == c, kend, zero) for c in range(nch)], axis=1).astype(BF16)
        qb_x = jnp.concatenate([jnp.where(rchunk == c, qb, zero) for c in range(nch)], axis=1).astype(BF16)
        decs = [jnp.exp2(b[CHUNK * (c + 1) - 1:CHUNK * (c + 1), :]) for c in range(nch)]
        return ls, jnp.concatenate(qv, axis=0).astype(BF16), kx, kend_x, qb_x, decs

    def group(gi, carry):
        heads = [gi * HGRN_GROUP + u for u in range(HGRN_GROUP)]
        prep = [prepare(hd) for hd in heads]
        a4s = [_dot_nt(p[1], p[2]) for p in prep]
        incs = [_dot_tn(iv_ref[:, p[0]], p[3]) for p in prep]
        o_intras = []
        for p, a4 in zip(prep, a4s):
            att = jnp.where(m_diag, a4[:th], 0.0)
            at = th
            for dd in range(1, nsub):
                pieces = []
                for sl in back_rows(dd):
                    n = sl.stop - sl.start
                    pieces += [jnp.zeros((SUB * dd, th), F32), a4[at:at + n]]
                    at += n
                att = jnp.where(m_off[dd - 1], jnp.concatenate(pieces, axis=0), att)
            o_intras.append(_dot(att.astype(BF16), iv_ref[:, p[0]]))
        for hd, p, inc, o_intra in zip(heads, prep, incs, o_intras):
            ls, decs = p[0], p[5]
            st = st_ref[hd]
            starts = []
            for c in range(nch):
                starts.append(st)
                st = st * decs[c] + inc[:, dk * c:dk * (c + 1)]
            st_ref[hd] = st
            oh = o_intra + _dot_nt(p[4], jnp.concatenate(starts, axis=1).astype(BF16))
            on = oh * lax.rsqrt(jnp.mean(oh * oh, axis=-1, keepdims=True) + EPS)
            o_ref[:, ls] = (on * ng_ref[:, ls] * gs_ref[:, ls].astype(F32)).astype(BF16)
        return carry

    for gi in range(qs_ref.shape[1] // dk // HGRN_GROUP):
        group(gi, 0)


def _hgrn(qs, lf, iv, gs, ng, batch, dk):
    t, d = qs.shape
    s = t // batch
    th = min(TH, s)
    blk = pl.BlockSpec((None, th, d), lambda b, j: (b, j, 0))
    r3 = lambda a: a.reshape(batch, s, d)
    out = pl.pallas_call(
        functools.partial(_hgrn_kernel, dk=dk),
        grid=(batch, s // th),
        in_specs=[blk, blk, blk, blk, _const_spec(ng.shape)],
        out_specs=blk,
        out_shape=jax.ShapeDtypeStruct((batch, s, d), BF16),
        scratch_shapes=[pltpu.VMEM((d // dk, dk, dk), F32), pltpu.VMEM((th, d), F32), pltpu.VMEM((th, d), F32)],
        compiler_params=_params(2),
        name="hgrn2",
    )(r3(qs), r3(lf), r3(iv), r3(gs), ng)
    return out.reshape(t, d)


def _block_diag(w):
    n, c, dd = w.shape
    eye = jnp.eye(n, dtype=w.dtype)
    return (eye[:, None, :, None] * w[:, :, None, :]).reshape(n * c, n * dd)


def _router_weights(wg, we):
    d = wg.shape[0]
    w = jnp.concatenate([wg, we, jnp.zeros((d, LANES - wg.shape[1] - we.shape[1]), F32)], axis=1).T
    hi = w.astype(BF16)
    return jnp.concatenate([hi, (w - hi.astype(F32)).astype(BF16)], axis=0)


def _moe(layer, meta_t, cnt, hn_slab, moe_w_gate, moe_w_up, moe_w_down):
    dest, pad_lo, pad_hi, blk_order, used_seq, nact, nblk = _plan(meta_t, cnt, TB_EXPERT)
    xs = _dispatch(dest, pad_lo, pad_hi, nact, hn_slab, nblk, TB_EXPERT)
    ys = _experts(blk_order, used_seq, nact, xs, moe_w_gate, moe_w_up, moe_w_down, layer, TB_EXPERT)
    return dest, ys


def kernel(x, norm_mix_g, norm_ffn_g, norm_final_g, ab_w_in, ab_conv_w, ab_conv_b, rg_w_a, rg_b_a, rg_w_x, rg_b_x, rg_lambda, attn_rel_bias, ab_w_out, c_w_in, c_lb_logits, c_norm_g, c_w_out, moe_router_group, moe_router_expert, moe_w_gate, moe_w_up, moe_w_down):
    batch, seq, d = x.shape
    t = batch * seq
    xt = x.reshape(t, d)
    row = lambda v: v.reshape(1, -1)

    aw = ab_conv_w.shape[2]
    qkv, ya = _proj0_lru(x, row(norm_mix_g[0]), ab_w_in[0].astype(BF16), ab_conv_w[0], row(ab_conv_b[0]),
                         _block_diag(rg_w_a[0]).astype(BF16), _block_diag(rg_w_x[0]).astype(BF16),
                         row(rg_b_a[0]), row(rg_b_x[0]), row(rg_lambda[0]))
    tk = TQ + LEFT_CHUNKS * CHUNK
    dist = jnp.clip(tk - jnp.arange(TQ + tk), -REL_CLIP, REL_CLIP) + REL_CLIP
    bias = _attn_bias(attn_rel_bias[0][:, None, dist])
    yb = _attention(qkv, bias, batch)
    wo = ab_w_out[0].astype(BF16)
    wt = _router_weights(moe_router_group[0], moe_router_expert[0])
    h1, hn1, meta1, metat1, cnt1 = _proj_route(ya, 0, yb, 0, xt, wo[:aw], wo[aw:], row(norm_ffn_g[0]), wt)
    dest1, ys1 = _moe(0, metat1, cnt1, hn1, moe_w_gate, moe_w_up, moe_w_down)

    dk = c_norm_g.shape[1] // 8
    h2, qs, lf, iv, gs = _comb_proj1(dest1, h1, meta1, ys1, row(norm_mix_g[1]), c_w_in[0].astype(BF16),
                                     c_lb_logits, 1)
    om = _hgrn(qs, lf, iv, gs, row(c_norm_g[0]), batch, dk)
    wo = c_w_out[0].astype(BF16)
    half = wo.shape[0] // 2
    wt = _router_weights(moe_router_group[1], moe_router_expert[1])
    h3, hn3, meta3, metat3, cnt3 = _proj_route(om, 0, om, 1, h2, wo[:half], wo[half:], row(norm_ffn_g[1]), wt)
    dest3, ys3 = _moe(1, metat3, cnt3, hn3, moe_w_gate, moe_w_up, moe_w_down)

    out = _comb_final(dest3, h3, meta3, ys3, row(norm_final_g))
    return out.reshape(batch, seq, d)
```

```python
import functools

import jax
import jax.numpy as jnp
from jax import lax
from jax.experimental import pallas as pl
from jax.experimental.pallas import tpu as pltpu

F32 = jnp.float32
BF16 = jnp.bfloat16

EPS = 1e-6
LOG2E = 1.4426950408889634
RG_C = 8.0
CHUNK = 64
LEFT_CHUNKS = 8
REL_CLIP = 256
N_GROUPS = 4
EXPERTS_PER_GROUP = 8
N_EXPERTS = N_GROUPS * EXPERTS_PER_GROUP
TOP_K = 2
ROUTE_ROWS = 40

LANES = 128
SUBLANES = 8
SUB = 16

TS_LRU = 512
TQ = 256
ATTN_GROUP = 4
TM_ROUTE = 1024
TM_ROUTE_SUB = 256
TB_EXPERT = 512
X_SLOTS = 3
TM_COMB = 256
TH = 256
HGRN_GROUP = 4
DMA_UNROLL = 8
COMB_SLOTS = 3
VMEM_MB = 48


def _params(n_axes, vmem_mb=VMEM_MB):
    return pltpu.CompilerParams(dimension_semantics=("arbitrary",) * n_axes,
                                vmem_limit_bytes=vmem_mb * 1024 * 1024)


def _const_spec(shape):
    nd = len(shape)
    return pl.BlockSpec(shape, lambda *_: (0,) * nd)


def _rms(x, g):
    return x * lax.rsqrt(jnp.mean(x * x, axis=-1, keepdims=True) + EPS) * g


def _dot(a, b):
    return jnp.dot(a, b, preferred_element_type=F32)


def _dot_nt(a, b):
    return lax.dot_general(a, b, (((1,), (1,)), ((), ())), preferred_element_type=F32)


def _dot_tn(a, b):
    return lax.dot_general(a, b, (((0,), (0,)), ((), ())), preferred_element_type=F32)


def _split_bf16(x):
    hi = x.astype(BF16)
    lo = (x - hi.astype(F32)).astype(BF16)
    return hi, lo


def _to_slab(ref, val):
    m = val.shape[0]
    for j in range(val.shape[1] // LANES):
        ref[pl.ds(j, m, stride=SUBLANES), :] = val[:, LANES * j:LANES * (j + 1)]


def _from_slab(ref, m):
    return jnp.concatenate([ref[pl.ds(j, m, stride=SUBLANES), :] for j in range(SUBLANES)], axis=1)


def _proj0_lru_kernel(x_ref, g_ref, w_ref, cw_ref, cb_ref, wa_ref, wx_ref, ba_ref, bx_ref, lam_ref,
                      qkv_ref, ya_ref, xbuf, h_ref, a_s, u_s):
    ts = x_ref.shape[0]
    w = ya_ref.shape[1]
    bw = qkv_ref.shape[1] // 3

    @pl.when(pl.program_id(1) == 0)
    def _():
        xbuf[pl.ds(0, SUBLANES), :] = jnp.zeros((SUBLANES, w), F32)
        h_ref[...] = jnp.zeros_like(h_ref)

    hn = _rms(x_ref[...], g_ref[...]).astype(BF16)

    def qkv_cols(c, scale=None):
        z = _dot(hn, w_ref[:, 2 * w + c * bw:2 * w + (c + 1) * bw])
        qkv_ref[:, c * bw:(c + 1) * bw] = (z if scale is None else z * scale).astype(BF16)

    xg = _dot(hn, w_ref[:, :2 * w])

    xbuf[pl.ds(SUBLANES, ts), :] = xg[:, :w]
    nk = cw_ref.shape[0]
    y = cb_ref[...]
    for k in range(nk):
        y = y + cw_ref[nk - 1 - k:nk - k, :] * xbuf[pl.ds(SUBLANES - k, ts), :]
    xbuf[pl.ds(0, SUBLANES), :] = xbuf[pl.ds(ts, SUBLANES), :]

    yb = y.astype(BF16)
    rg = jax.nn.sigmoid(_dot(yb, wa_ref[...]) + ba_ref[...])
    ig = jax.nn.sigmoid(_dot(yb, wx_ref[...]) + bx_ref[...])
    qkv_cols(0, CHUNK ** -0.5 * LOG2E)
    lam = lam_ref[...]
    log_sig = jnp.minimum(lam, 0.0) - jnp.log1p(jnp.exp(-jnp.abs(lam)))
    log_a = RG_C * rg * log_sig
    a = jnp.exp(log_a)
    m = 1.0 - a * a
    u = jnp.where(m > 0.0, m * lax.rsqrt(m), 0.0) * (ig * y)
    qkv_cols(1)

    grp = (ts // SUBLANES, SUBLANES, w)
    a = a.reshape(grp)
    u = u.reshape(grp)
    rowm = lax.broadcasted_iota(jnp.int32, grp, 1)
    for s in (1, 2, 4):
        keep = rowm >= s
        a_sh = jnp.where(keep, pltpu.roll(a, s, 1), 1.0)
        u_sh = jnp.where(keep, pltpu.roll(u, s, 1), 0.0)
        u = a * u_sh + u
        a = a * a_sh
    a_s[...] = a.reshape(ts, w)
    u_s[...] = u.reshape(ts, w)
    qkv_cols(2)

    def group(gi, h):
        off = pl.multiple_of(gi * SUBLANES, SUBLANES)
        hg = a_s[pl.ds(off, SUBLANES), :] * h + u_s[pl.ds(off, SUBLANES), :]
        u_s[pl.ds(off, SUBLANES), :] = hg
        return jnp.broadcast_to(hg[SUBLANES - 1:SUBLANES, :], hg.shape)

    h_ref[...] = lax.fori_loop(0, ts // SUBLANES, group, h_ref[...])

    ga = xg[:, w:]
    gelu = 0.5 * ga * (1.0 + jnp.tanh(0.7978845608028654 * (ga + 0.044715 * (ga * ga * ga))))
    ya_ref[...] = (u_s[...] * gelu).astype(BF16)


def _proj0_lru(x, g, w_in, cw, cb, wa, wx, ba, bx, lam):
    batch, s, d = x.shape
    w = cw.shape[1]
    nqkv = w_in.shape[1] - 2 * w
    ts = min(TS_LRU, s)
    small = [g, w_in, cw, cb, wa, wx, ba, bx, lam]
    blk = lambda width: pl.BlockSpec((None, ts, width), lambda b, j: (b, j, 0))
    qkv, ya = pl.pallas_call(
        _proj0_lru_kernel,
        grid=(batch, s // ts),
        in_specs=[blk(d)] + [_const_spec(a.shape) for a in small],
        out_specs=[blk(nqkv), blk(w)],
        out_shape=[jax.ShapeDtypeStruct((batch, s, nqkv), BF16), jax.ShapeDtypeStruct((batch, s, w), BF16)],
        scratch_shapes=[pltpu.VMEM((ts + SUBLANES, w), F32), pltpu.VMEM((SUBLANES, w), F32),
                        pltpu.VMEM((ts, w), F32), pltpu.VMEM((ts, w), F32)],
        compiler_params=_params(2),
        name="proj0_lru",
    )(x, *small)
    return qkv.reshape(batch * s, nqkv), ya.reshape(batch * s, w)


def _attn_bias_kernel(base_ref, o_ref):
    tq, tk = o_ref.shape
    nkb = tk // tq
    row = base_ref[...]
    full = pltpu.roll(jnp.broadcast_to(row, (tq, tq + tk)), 0, 1, stride=1, stride_axis=0)
    qc = lax.broadcasted_iota(jnp.int32, (tq, tk), 0) // CHUNK
    col = lax.broadcasted_iota(jnp.int32, (tq, tk), 1)
    kc = col // CHUNK
    first_ok = (nkb - 1 - pl.program_id(0)) * tq
    valid = (kc >= qc) & (kc <= qc + LEFT_CHUNKS) & (col >= first_ok)
    o_ref[...] = jnp.where(valid, full[:, tq:] * LOG2E, -1e30)


def _attn_bias(base):
    h = base.shape[0]
    tk = TQ + LEFT_CHUNKS * CHUNK
    return pl.pallas_call(
        _attn_bias_kernel,
        grid=(tk // TQ, h),
        in_specs=[pl.BlockSpec((None, 1, TQ + tk), lambda v, i: (i, 0, 0))],
        out_specs=pl.BlockSpec((None, None, TQ, tk), lambda v, i: (v, i, 0, 0)),
        out_shape=jax.ShapeDtypeStruct((tk // TQ, h, TQ, tk), F32),
        compiler_params=_params(2),
        name="attn_bias",
    )(base)


def _attn_kernel(q_ref, k0_ref, k1_ref, k2_ref, v0_ref, v1_ref, v2_ref, bias_ref, o_ref):
    tq = q_ref.shape[0]
    half_w = LANES // 2
    lo = lax.broadcasted_iota(jnp.int32, (tq, LANES), 1) < half_w
    lo_k = lax.broadcasted_iota(jnp.int32, (3 * tq, LANES), 1) < half_w
    k_refs = (k0_ref, k1_ref, k2_ref)
    v_refs = (v0_ref, v1_ref, v2_ref)
    npair = q_ref.shape[1] // LANES
    for j0 in range(0, npair, ATTN_GROUP):
        pairs = range(j0, min(j0 + ATTN_GROUP, npair))
        qk, vh = {}, {}
        for j in pairs:
            sl = slice(LANES * j, LANES * (j + 1))
            q2 = q_ref[:, sl]
            kcat = jnp.concatenate([r[:, sl] for r in k_refs], axis=0)
            vcat = jnp.concatenate([r[:, sl] for r in v_refs], axis=0)
            one = jnp.ones_like(vcat)
            zero = jnp.zeros_like(q2)
            vh[j] = (jnp.where(lo_k, vcat, one), jnp.where(lo_k, one, vcat))
            qk[j] = [_dot_nt(jnp.where(lo if half == 0 else jnp.logical_not(lo), q2, zero), kcat)
                     for half in range(2)]
        ps = {}
        for j in pairs:
            ps[j] = []
            for half in range(2):
                s = qk[j][half] + bias_ref[2 * j + half]
                ps[j].append(jnp.exp2(s - jnp.max(s, axis=-1, keepdims=True)).astype(BF16))
        for j in pairs:
            pv = [_dot(ps[j][half], vh[j][half]) for half in range(2)]
            num = jnp.where(lo, pv[0], pv[1])
            den = pltpu.roll(jnp.where(lo, pv[1], pv[0]), half_w, 1)
            o_ref[:, LANES * j:LANES * (j + 1)] = (num / den).astype(BF16)


def _attention(qkv, bias, batch):
    t, w3 = qkv.shape
    w = w3 // 3
    s = t // batch
    qkv3 = qkv.reshape(batch, s, w3)
    nvar = bias.shape[0]

    def kv_spec(colblk, back):
        return pl.BlockSpec((None, TQ, w), lambda i, b: (b, jnp.maximum(i - back, 0), colblk))

    out = pl.pallas_call(
        _attn_kernel,
        grid=(s // TQ, batch),
        in_specs=[pl.BlockSpec((None, TQ, w), lambda i, b: (b, i, 0)),
                  kv_spec(1, 2), kv_spec(1, 1), kv_spec(1, 0),
                  kv_spec(2, 2), kv_spec(2, 1), kv_spec(2, 0),
                  pl.BlockSpec((None,) + bias.shape[1:], lambda i, b: (jnp.minimum(i, nvar - 1), 0, 0, 0))],
        out_specs=pl.BlockSpec((None, TQ, w), lambda i, b: (b, i, 0)),
        out_shape=jax.ShapeDtypeStruct((batch, s, w), BF16),
        compiler_params=_params(2),
        name="attention",
    )(qkv3, qkv3, qkv3, qkv3, qkv3, qkv3, qkv3, bias)
    return out.reshape(t, w)


def _proj_route_kernel(a_ref, b_ref, r_ref, wa_ref, wb_ref, g_ref, wt_ref,
                       h_ref, hn_ref, meta_ref, metat_ref, cnt_ref, carry_ref):
    tm = a_ref.shape[0]
    ts = min(TM_ROUTE_SUB, tm)

    @pl.when(pl.program_id(0) == 0)
    def _():
        carry_ref[...] = jnp.zeros_like(carry_ref)

    wt = wt_ref[...]
    ridx = lax.broadcasted_iota(jnp.int32, (ROUTE_ROWS, ts), 0).astype(F32)
    r8 = lax.broadcasted_iota(jnp.int32, (SUBLANES, ts), 0)
    rr = lax.broadcasted_iota(jnp.int32, (ts, ts), 0)
    cc = lax.broadcasted_iota(jnp.int32, (ts, ts), 1)
    utri = jnp.where(rr < cc, 1.0, 0.0).astype(BF16)
    neg = -jnp.inf
    far = float(LANES)
    carry = carry_ref[...]

    all_logits = []
    for sub in range(tm // ts):
        rows = pl.ds(sub * ts, ts)
        h = r_ref[rows, :] + (_dot(a_ref[rows, :], wa_ref[...]) + _dot(b_ref[rows, :], wb_ref[...]))
        h_ref[rows, :] = h
        hn = _rms(h, g_ref[...])
        _to_slab(hn_ref.at[pl.ds(sub * ts * SUBLANES, ts * SUBLANES), :], hn)

        hi, lo = _split_bf16(hn)
        p_hi = _dot_nt(wt, hi)
        p_lo = _dot_nt(wt[:LANES], lo)
        all_logits.append((p_hi[:ROUTE_ROWS] + p_hi[LANES:LANES + ROUTE_ROWS]) + p_lo[:ROUTE_ROWS])

    for sub, logits in enumerate(all_logits):
        rows = pl.ds(sub * ts, ts)

        def first_max(mask, logits=logits):
            v = jnp.max(jnp.where(mask, logits, neg), axis=0, keepdims=True)
            idx = jnp.min(jnp.where(mask & (logits == v), ridx, far), axis=0, keepdims=True)
            return v, idx

        gmask = ridx < N_GROUPS
        gmax, gidx = first_max(gmask)
        g_gate = 1.0 / jnp.sum(jnp.where(gmask, jnp.exp(logits - gmax), 0.0), axis=0, keepdims=True)
        e_lo = N_GROUPS + EXPERTS_PER_GROUP * gidx
        emask = (ridx >= e_lo) & (ridx < e_lo + EXPERTS_PER_GROUP)
        v1, i1 = first_max(emask)
        v2, i2 = first_max(emask & (ridx != i1))
        tt = jnp.exp(v2 - v1)
        w1 = g_gate / (1.0 + tt)
        w2 = g_gate * tt / (1.0 + tt)

        sel1 = ridx == i1
        sel2 = ridx == i2
        onehot = jnp.where(sel1 | sel2, 1.0, 0.0)
        before = _dot(onehot.astype(BF16), utri) + carry[:, 0:1]
        rank1 = jnp.sum(jnp.where(sel1, before, 0.0), axis=0, keepdims=True)
        rank2 = jnp.sum(jnp.where(sel2, before, 0.0), axis=0, keepdims=True)
        carry = carry + jnp.sum(onehot, axis=1, keepdims=True)

        mt = jnp.zeros((SUBLANES, ts), F32)
        for c, val in enumerate((i1 - N_GROUPS, i2 - N_GROUPS, rank1, rank2, w1, w2)):
            mt = jnp.where(r8 == c, val, mt)
        metat_ref[:, rows] = mt
        meta_ref[rows, :] = jnp.concatenate([mt, jnp.zeros((LANES - SUBLANES, ts), F32)], axis=0).T

    carry_ref[...] = carry
    cnt_ref[...] = carry


def _proj_route(a, acol, b, bcol, resid, wa, wb, g, wt):
    t, d = resid.shape
    kw = wa.shape[0]
    tm = min(TM_ROUTE, t)
    return pl.pallas_call(
        _proj_route_kernel,
        grid=(t // tm,),
        in_specs=[pl.BlockSpec((tm, kw), lambda i: (i, acol)), pl.BlockSpec((tm, kw), lambda i: (i, bcol)),
                  pl.BlockSpec((tm, d), lambda i: (i, 0)),
                  _const_spec(wa.shape), _const_spec(wb.shape), _const_spec(g.shape), _const_spec(wt.shape)],
        out_specs=[pl.BlockSpec((tm, d), lambda i: (i, 0)),
                   pl.BlockSpec((tm * SUBLANES, LANES), lambda i: (i, 0)),
                   pl.BlockSpec((tm, LANES), lambda i: (i, 0)),
                   pl.BlockSpec((SUBLANES, tm), lambda i: (0, i)),
                   pl.BlockSpec((ROUTE_ROWS, LANES), lambda i: (0, 0))],
        out_shape=[jax.ShapeDtypeStruct((t, d), F32),
                   jax.ShapeDtypeStruct((t * SUBLANES, LANES), F32),
                   jax.ShapeDtypeStruct((t, LANES), F32),
                   jax.ShapeDtypeStruct((SUBLANES, t), F32),
                   jax.ShapeDtypeStruct((ROUTE_ROWS, LANES), F32)],
        scratch_shapes=[pltpu.VMEM((ROUTE_ROWS, LANES), F32)],
        compiler_params=_params(1),
        name="proj_route",
    )(a, b, resid, wa, wb, g, wt)


def _plan(meta_t, cnt, tb):
    t = meta_t.shape[1]
    counts = cnt[N_GROUPS:N_GROUPS + N_EXPERTS, 0].astype(jnp.int32)
    padded = (counts + tb - 1) // tb * tb
    pad_end = jnp.cumsum(padded)
    pad_start = pad_end - padded
    dest = []
    for k in range(TOP_K):
        expert = meta_t[k].astype(jnp.int32)
        start = jnp.zeros((t,), jnp.int32)
        for e in range(N_EXPERTS):
            start = jnp.where(expert == e, pad_start[e], start)
        dest.append(start + meta_t[TOP_K + k].astype(jnp.int32))
    dest = jnp.concatenate(dest)
    nblk = -(-(t * TOP_K + N_EXPERTS * (tb - 1)) // tb)
    blk_start = jnp.arange(nblk, dtype=jnp.int32) * tb
    blk_expert = jnp.minimum(jnp.sum((pad_end[None, :] <= blk_start[:, None]).astype(jnp.int32), axis=1),
                             N_EXPERTS - 1)
    used = padded > 0
    order = jnp.cumsum(used.astype(jnp.int32)) - 1
    nused = order[-1] + 1
    slots = jnp.minimum(jnp.arange(N_EXPERTS, dtype=jnp.int32), nused - 1)
    ids = jnp.arange(N_EXPERTS, dtype=jnp.int32)
    used_seq = jnp.sum(jnp.where(used[None, :] & (order[None, :] == slots[:, None]), ids[None, :], 0), axis=1)
    blk_order = order[blk_expert]
    blk_fill = jnp.clip((pad_start + counts)[blk_expert] - blk_start, 0, tb)
    nact = jnp.stack([pad_end[-1] // tb, nused]).astype(jnp.int32)
    return dest, pad_start + counts, pad_end, blk_order, blk_fill, used_seq, nact, nblk


def _dispatch_kernel(dest_ref, plo_ref, phi_ref, nact_ref, hn_ref, xs_ref, zero_ref, sem, zsem):
    tm = hn_ref.shape[0] // SUBLANES
    t = dest_ref.shape[0] // TOP_K
    tb = zero_ref.shape[0] // SUBLANES
    nblk = xs_ref.shape[0] // zero_ref.shape[0]
    i = pl.program_id(0)

    def zero_copy(row, nrows):
        return pltpu.make_async_copy(
            zero_ref.at[pl.ds(0, nrows * SUBLANES), :],
            xs_ref.at[pl.ds(pl.multiple_of(row * SUBLANES, SUBLANES), nrows * SUBLANES), :], zsem)

    def zero_fill(start):
        def go(cp):
            cp.start() if start else cp.wait()

        def per_expert(e, c):
            off = plo_ref[e]
            n = phi_ref[e] - off
            bit = tb // 2
            while bit:
                pl.when((n & bit) != 0)(functools.partial(lambda o, b: go(zero_copy(o, b)), off, bit))
                off = off + (n & bit)
                bit //= 2
            return c

        def per_block(b, c):
            go(zero_copy(b * tb, tb))
            return c

        lax.fori_loop(0, N_EXPERTS, per_expert, 0)
        lax.fori_loop(nact_ref[0], nblk, per_block, 0)

    @pl.when(i == 0)
    def _():
        zero_ref[...] = jnp.zeros_like(zero_ref)
        zero_fill(True)

    def copy(r, k):
        d = dest_ref[k * t + i * tm + r]
        return pltpu.make_async_copy(
            hn_ref.at[pl.ds(pl.multiple_of(r * SUBLANES, SUBLANES), SUBLANES), :],
            xs_ref.at[pl.ds(pl.multiple_of(d * SUBLANES, SUBLANES), SUBLANES), :], sem)

    def start(r, c):
        for k in range(TOP_K):
            copy(r, k).start(priority=k)
        return c

    lax.fori_loop(0, tm, start, 0, unroll=DMA_UNROLL)
    for k in range(TOP_K):
        pltpu.make_async_copy(hn_ref, xs_ref.at[pl.ds(0, tm * SUBLANES), :], sem).wait()

    @pl.when(i == 0)
    def _():
        zero_fill(False)


def _dispatch(dest, pad_lo, pad_hi, nact, hn_slab, nblk, tb):
    t = hn_slab.shape[0] // SUBLANES
    tm = min(TM_COMB, t)
    return pl.pallas_call(
        _dispatch_kernel,
        grid_spec=pltpu.PrefetchScalarGridSpec(
            num_scalar_prefetch=4,
            grid=(t // tm,),
            in_specs=[pl.BlockSpec((tm * SUBLANES, LANES), lambda i, *_: (i, 0))],
            out_specs=pl.BlockSpec(memory_space=pl.ANY),
            scratch_shapes=[pltpu.VMEM((tb * SUBLANES, LANES), F32),
                            pltpu.SemaphoreType.DMA(()), pltpu.SemaphoreType.DMA(())]),
        out_shape=jax.ShapeDtypeStruct((nblk * tb * SUBLANES, LANES), F32),
        compiler_params=_params(1),
        name="dispatch",
    )(dest, pad_lo, pad_hi, nact, hn_slab)


def _expert_kernel(ord_ref, fill_ref, useq_ref, nact_ref, xs_ref, wg_ref, wu_ref, wd_ref, ys_ref,
                   xbuf, xsem, wg_f, wu_f, wd_f, wsem, wg_s, wu_s, wd_s, *, layer):
    i = pl.program_id(0)
    n = pl.num_programs(0)
    rows = ys_ref.shape[0]
    tb = rows // SUBLANES
    active = i < nact_ref[0]
    order = ord_ref[i]
    changed = (i == 0) | (order != ord_ref[jnp.maximum(i - 1, 0)])
    slot = i % X_SLOTS
    ahead = X_SLOTS - 1

    def w_copies(j):
        e = useq_ref[jnp.minimum(j, useq_ref.shape[0] - 1)]
        ws = j % 2
        return [pltpu.make_async_copy(src.at[layer, e], dst.at[ws], wsem.at[ws])
                for src, dst in ((wg_ref, wg_f), (wu_ref, wu_f), (wd_ref, wd_f))]

    @pl.when(i == 0)
    def _():
        for cp in w_copies(0):
            cp.start()

    def x_copy(step, sl):
        blk = jnp.minimum(step, nact_ref[0] - 1)
        return pltpu.make_async_copy(xs_ref.at[pl.ds(pl.multiple_of(blk * rows, rows), rows), :],
                                     xbuf.at[sl], xsem.at[sl])

    @pl.when(i == 0)
    def _():
        for a in range(ahead):
            x_copy(a, a).start()

    x_copy(i, slot).wait()
    x_copy(i + ahead, (i + ahead) % X_SLOTS).start()

    @pl.when(i == n - 1)
    def _():
        for a in range(1, X_SLOTS):
            x_copy(i + a, (i + a) % X_SLOTS).wait()

    @pl.when(active & changed)
    def _():
        for cp in w_copies(order):
            cp.wait()
        for cp in w_copies(order + 1):
            cp.start()
        ws = order % 2
        wg_s[...] = wg_f[ws].astype(BF16)
        wu_s[...] = wu_f[ws].astype(BF16)
        wd_s[...] = wd_f[ws].astype(BF16)

    @pl.when(i == n - 1)
    def _():
        for cp in w_copies(nact_ref[1]):
            cp.wait()

    def mlp(nrows):
        part = pl.ds(0, nrows * SUBLANES)
        x = _from_slab(xbuf.at[slot, part, :], nrows).astype(BF16)
        gate = _dot(x, wg_s[...])
        up = _dot(x, wu_s[...])
        hid = (gate * jax.nn.sigmoid(gate) * up).astype(BF16)
        _to_slab(ys_ref.at[part, :], _dot(hid, wd_s[...]))

    half = tb // 2
    upper_empty = fill_ref[i] <= half

    @pl.when(active & jnp.logical_not(upper_empty))
    def _():
        mlp(tb)

    @pl.when(active & upper_empty)
    def _():
        mlp(half)
        ys_ref[pl.ds(half * SUBLANES, half * SUBLANES), :] = jnp.zeros((half * SUBLANES, LANES), F32)

    @pl.when(jnp.logical_not(active))
    def _():
        ys_ref[...] = jnp.zeros_like(ys_ref)


def _experts(blk_order, blk_fill, used_seq, nact, xs, wg, wu, wd, layer, tb):
    nblk = blk_order.shape[0]
    _, _, d, de = wg.shape
    anyspec = pl.BlockSpec(memory_space=pl.ANY)
    return pl.pallas_call(
        functools.partial(_expert_kernel, layer=layer),
        grid_spec=pltpu.PrefetchScalarGridSpec(
            num_scalar_prefetch=4,
            grid=(nblk,),
            in_specs=[anyspec, anyspec, anyspec, anyspec],
            out_specs=pl.BlockSpec((tb * SUBLANES, LANES), lambda i, *_: (i, 0)),
            scratch_shapes=[pltpu.VMEM((X_SLOTS, tb * SUBLANES, LANES), F32), pltpu.SemaphoreType.DMA((X_SLOTS,)),
                            pltpu.VMEM((2, d, de), F32), pltpu.VMEM((2, d, de), F32), pltpu.VMEM((2, de, d), F32),
                            pltpu.SemaphoreType.DMA((2,)),
                            pltpu.VMEM((d, de), BF16), pltpu.VMEM((d, de), BF16), pltpu.VMEM((de, d), BF16)]),
        out_shape=jax.ShapeDtypeStruct(xs.shape, F32),
        compiler_params=_params(1),
        name="experts",
    )(blk_order, blk_fill, used_seq, nact, xs, wg, wu, wd)


def _combine(dest_ref, ys_ref, meta_ref, buf_ref, sem_ref, tm):
    i = pl.program_id(0)
    n = pl.num_programs(0)
    slot = i % COMB_SLOTS
    ahead = COMB_SLOTS - 1
    t = dest_ref.shape[0] // TOP_K

    def copy(tile, sl, r, k):
        d = dest_ref[k * t + tile * tm + r]
        row = r * SUBLANES if isinstance(r, int) else pl.multiple_of(r * SUBLANES, SUBLANES)
        return pltpu.make_async_copy(
            ys_ref.at[pl.ds(pl.multiple_of(d * SUBLANES, SUBLANES), SUBLANES), :],
            buf_ref.at[sl, k, pl.ds(row, SUBLANES), :],
            sem_ref.at[sl])

    def start_tile(tile, sl):
        def body(r, c):
            for k in range(TOP_K):
                copy(tile, sl, r, k).start(priority=k)
            return c
        lax.fori_loop(0, tm, body, 0, unroll=DMA_UNROLL)

    def wait_tile(sl):
        for k in range(TOP_K):
            pltpu.make_async_copy(ys_ref.at[pl.ds(0, tm * SUBLANES), :], buf_ref.at[sl, k], sem_ref.at[sl]).wait()

    @pl.when(i == 0)
    def _():
        for a in range(ahead):
            start_tile(jnp.minimum(a, n - 1), a)

    wait_tile(slot)

    nxt = jnp.minimum(i + ahead, n - 1)
    nslot = (i + ahead) % COMB_SLOTS

    def issue(part, nparts):
        rows = tm // nparts
        for r in range(part * rows, (part + 1) * rows):
            for k in range(TOP_K):
                copy(nxt, nslot, r, k).start(priority=k)

    @pl.when(i == n - 1)
    def _():
        for a in range(1, ahead):
            wait_tile((i + a) % COMB_SLOTS)

    y = None
    for k in range(TOP_K):
        gate = meta_ref[:, 2 * TOP_K + k:2 * TOP_K + k + 1]
        term = _from_slab(buf_ref.at[slot, k], tm) * gate
        y = term if y is None else y + term
    return y, issue


def _combine_drain(buf_ref, sem_ref, ys_ref, tm):
    i = pl.program_id(0)

    @pl.when(i == pl.num_programs(0) - 1)
    def _():
        sl = (i + COMB_SLOTS - 1) % COMB_SLOTS
        for k in range(TOP_K):
            pltpu.make_async_copy(ys_ref.at[pl.ds(0, tm * SUBLANES), :], buf_ref.at[sl, k], sem_ref.at[sl]).wait()


def _combine_scratch(tm):
    return [pltpu.VMEM((COMB_SLOTS, TOP_K, tm * SUBLANES, LANES), F32), pltpu.SemaphoreType.DMA((COMB_SLOTS,))]


def _comb_proj1_kernel(dest_ref, h_ref, meta_ref, ys_ref, g_ref, w_ref, lbl_ref,
                       h2_ref, qs_ref, lf_ref, iv_ref, gs_ref, buf_ref, sem_ref, *, lb_rows):
    tm = h_ref.shape[0]
    y, issue = _combine(dest_ref, ys_ref, meta_ref, buf_ref, sem_ref, tm)
    h2 = h_ref[...] + y
    h2_ref[...] = h2
    hn = _rms(h2, g_ref[...]).astype(BF16)
    d = h2.shape[1]
    lbl = lbl_ref[...]
    ex = jnp.exp(lbl - jnp.max(lbl, axis=0, keepdims=True))
    lb = jnp.sum(ex[:lb_rows], axis=0, keepdims=True) / jnp.sum(ex, axis=0, keepdims=True)

    def col(c):
        return _dot(hn, w_ref[:, d * c:d * (c + 1)])

    q = col(0)
    issue(0, 4)
    fz = col(1)
    issue(1, 4)
    iv = col(2)
    issue(2, 4)
    gz = col(3)
    issue(3, 4)
    qs_ref[...] = (q * jax.nn.sigmoid(q)).astype(BF16)
    lf_ref[...] = jnp.log(lb + (1.0 - lb) * jax.nn.sigmoid(fz)) * LOG2E
    iv_ref[...] = iv.astype(BF16)
    gs_ref[...] = (gz * jax.nn.sigmoid(gz)).astype(BF16)
    _combine_drain(buf_ref, sem_ref, ys_ref, tm)


def _comb_proj1(dest, h, meta, ys, g, w, lb_logits, lb_rows):
    t, d = h.shape
    tm = min(TM_COMB, t)
    row = lambda i, dd: (i, 0)
    const = lambda shape: pl.BlockSpec(shape, lambda i, dd: (0,) * len(shape))
    return pl.pallas_call(
        functools.partial(_comb_proj1_kernel, lb_rows=lb_rows),
        grid_spec=pltpu.PrefetchScalarGridSpec(
            num_scalar_prefetch=1,
            grid=(t // tm,),
            in_specs=[pl.BlockSpec((tm, d), row), pl.BlockSpec((tm, LANES), row),
                      pl.BlockSpec(memory_space=pl.ANY),
                      const(g.shape), const(w.shape), const(lb_logits.shape)],
            out_specs=[pl.BlockSpec((tm, d), row)] * 5,
            scratch_shapes=_combine_scratch(tm)),
        out_shape=[jax.ShapeDtypeStruct((t, d), F32), jax.ShapeDtypeStruct((t, d), BF16),
                   jax.ShapeDtypeStruct((t, d), F32), jax.ShapeDtypeStruct((t, d), BF16),
                   jax.ShapeDtypeStruct((t, d), BF16)],
        compiler_params=_params(1),
        name="comb_proj1",
    )(dest, h, meta, ys, g, w, lb_logits)


def _comb_final_kernel(dest_ref, h_ref, meta_ref, ys_ref, g_ref, o_ref, buf_ref, sem_ref):
    tm = h_ref.shape[0]
    y, issue = _combine(dest_ref, ys_ref, meta_ref, buf_ref, sem_ref, tm)
    issue(0, 1)
    o_ref[...] = _rms(h_ref[...] + y, g_ref[...])
    _combine_drain(buf_ref, sem_ref, ys_ref, tm)


def _comb_final(dest, h, meta, ys, g):
    t, d = h.shape
    tm = min(TM_COMB, t)
    row = lambda i, dd: (i, 0)
    return pl.pallas_call(
        _comb_final_kernel,
        grid_spec=pltpu.PrefetchScalarGridSpec(
            num_scalar_prefetch=1,
            grid=(t // tm,),
            in_specs=[pl.BlockSpec((tm, d), row), pl.BlockSpec((tm, LANES), row),
                      pl.BlockSpec(memory_space=pl.ANY),
                      pl.BlockSpec(g.shape, lambda i, dd: (0, 0))],
            out_specs=pl.BlockSpec((tm, d), row),
            scratch_shapes=_combine_scratch(tm)),
        out_shape=jax.ShapeDtypeStruct((t, d), F32),
        compiler_params=_params(1),
        name="comb_final",
    )(dest, h, meta, ys, g)


def _hgrn_kernel(qs_ref, lf_ref, iv_ref, gs_ref, ng_ref, o_ref, st_ref, d1_s, t16_s, *, dk):
    th = qs_ref.shape[0]
    nsub = CHUNK // SUB

    @pl.when(pl.program_id(1) == 0)
    def _():
        st_ref[...] = jnp.zeros_like(st_ref)

    rr = lax.broadcasted_iota(jnp.int32, (th, th), 0)
    cc = lax.broadcasted_iota(jnp.int32, (th, th), 1)
    same_sub = (rr // SUB) == (cc // SUB)
    same_chunk = (rr // CHUNK) == (cc // CHUNK)
    m_diag = same_sub & (cc <= rr)
    dsub = rr // SUB - cc // SUB
    m_off = [same_chunk & (dsub == dd) for dd in range(1, nsub)]

    hi, lo = _split_bf16(lf_ref[...])
    tri = jnp.where(m_diag, 1.0, 0.0).astype(BF16)
    d1 = _dot(tri, hi) + _dot(tri, lo)
    d1_s[...] = d1
    sub3 = (th // SUB, SUB, d1.shape[1])
    t16_s[...] = jnp.broadcast_to(d1.reshape(sub3)[:, SUB - 1:SUB, :], sub3).reshape(th, d1.shape[1])
    rrow = lax.broadcasted_iota(jnp.int32, (th, dk), 0)
    rsub = (rrow // SUB) % nsub
    rchunk = rrow // CHUNK
    nch = th // CHUNK

    def back_rows(dd):
        return [slice(CHUNK * c + SUB * dd, CHUNK * (c + 1)) for c in range(nch)]

    def prepare(hd):
        ls = pl.ds(hd * dk, dk)
        d1 = d1_s[:, ls]
        t16 = t16_s[:, ls]
        q = qs_ref[:, ls].astype(F32)
        kt = 1.0 - jnp.exp2(lf_ref[:, ls])
        suf = t16 - d1
        kx = (kt * jnp.exp2(suf)).astype(BF16)
        qv = [q * jnp.exp2(-suf)]
        acc = d1
        tail = suf
        for dd in range(1, nsub):
            qv.append(jnp.concatenate([q[sl] * jnp.exp2(acc[sl]) for sl in back_rows(dd)], axis=0))
            acc = acc + jnp.where(rsub >= dd, pltpu.roll(t16, SUB * dd, 0), 0.0)
            tail = tail + jnp.where(rsub < nsub - dd, pltpu.roll(t16, th - SUB * dd, 0), 0.0)
        b = acc
        qb = q * jnp.exp2(b)
        kend = kt * jnp.exp2(tail)
        zero = jnp.zeros_like(q)
        kend_x = jnp.concatenate([jnp.where(rchunk == c, kend, zero) for c in range(nch)], axis=1).astype(BF16)
        qb_x = jnp.concatenate([jnp.where(rchunk == c, qb, zero) for c in range(nch)], axis=1).astype(BF16)
        decs = [jnp.exp2(b[CHUNK * (c + 1) - 1:CHUNK * (c + 1), :]) for c in range(nch)]
        return ls, jnp.concatenate(qv, axis=0).astype(BF16), kx, kend_x, qb_x, decs

    def group(gi, carry):
        heads = [gi * HGRN_GROUP + u for u in range(HGRN_GROUP)]
        prep = [prepare(hd) for hd in heads]
        a4s = [_dot_nt(p[1], p[2]) for p in prep]
        incs = [_dot_tn(iv_ref[:, p[0]], p[3]) for p in prep]
        o_intras = []
        for p, a4 in zip(prep, a4s):
            att = jnp.where(m_diag, a4[:th], 0.0)
            at = th
            for dd in range(1, nsub):
                pieces = []
                for sl in back_rows(dd):
                    n = sl.stop - sl.start
                    pieces += [jnp.zeros((SUB * dd, th), F32), a4[at:at + n]]
                    at += n
                att = jnp.where(m_off[dd - 1], jnp.concatenate(pieces, axis=0), att)
            o_intras.append(_dot(att.astype(BF16), iv_ref[:, p[0]]))
        for hd, p, inc, o_intra in zip(heads, prep, incs, o_intras):
            ls, decs = p[0], p[5]
            st = st_ref[hd]
            starts = []
            for c in range(nch):
                starts.append(st)
                st = st * decs[c] + inc[:, dk * c:dk * (c + 1)]
            st_ref[hd] = st
            oh = o_intra + _dot_nt(p[4], jnp.concatenate(starts, axis=1).astype(BF16))
            on = oh * lax.rsqrt(jnp.mean(oh * oh, axis=-1, keepdims=True) + EPS)
            o_ref[:, ls] = (on * ng_ref[:, ls] * gs_ref[:, ls].astype(F32)).astype(BF16)
        return carry

    for gi in range(qs_ref.shape[1] // dk // HGRN_GROUP):
        group(gi, 0)


def _hgrn(qs, lf, iv, gs, ng, batch, dk):
    t, d = qs.shape
    s = t // batch
    th = min(TH, s)
    blk = pl.BlockSpec((None, th, d), lambda b, j: (b, j, 0))
    r3 = lambda a: a.reshape(batch, s, d)
    out = pl.pallas_call(
        functools.partial(_hgrn_kernel, dk=dk),
        grid=(batch, s // th),
        in_specs=[blk, blk, blk, blk, _const_spec(ng.shape)],
        out_specs=blk,
        out_shape=jax.ShapeDtypeStruct((batch, s, d), BF16),
        scratch_shapes=[pltpu.VMEM((d // dk, dk, dk), F32), pltpu.VMEM((th, d), F32), pltpu.VMEM((th, d), F32)],
        compiler_params=_params(2),
        name="hgrn2",
    )(r3(qs), r3(lf), r3(iv), r3(gs), ng)
    return out.reshape(t, d)


def _block_diag(w):
    n, c, dd = w.shape
    eye = jnp.eye(n, dtype=w.dtype)
    return (eye[:, None, :, None] * w[:, :, None, :]).reshape(n * c, n * dd)


def _router_weights(wg, we):
    d = wg.shape[0]
    w = jnp.concatenate([wg, we, jnp.zeros((d, LANES - wg.shape[1] - we.shape[1]), F32)], axis=1).T
    hi = w.astype(BF16)
    return jnp.concatenate([hi, (w - hi.astype(F32)).astype(BF16)], axis=0)


def _moe(layer, meta_t, cnt, hn_slab, moe_w_gate, moe_w_up, moe_w_down):
    dest, pad_lo, pad_hi, blk_order, blk_fill, used_seq, nact, nblk = _plan(meta_t, cnt, TB_EXPERT)
    xs = _dispatch(dest, pad_lo, pad_hi, nact, hn_slab, nblk, TB_EXPERT)
    ys = _experts(blk_order, blk_fill, used_seq, nact, xs, moe_w_gate, moe_w_up, moe_w_down, layer, TB_EXPERT)
    return dest, ys


def kernel(x, norm_mix_g, norm_ffn_g, norm_final_g, ab_w_in, ab_conv_w, ab_conv_b, rg_w_a, rg_b_a, rg_w_x, rg_b_x, rg_lambda, attn_rel_bias, ab_w_out, c_w_in, c_lb_logits, c_norm_g, c_w_out, moe_router_group, moe_router_expert, moe_w_gate, moe_w_up, moe_w_down):
    batch, seq, d = x.shape
    t = batch * seq
    xt = x.reshape(t, d)
    row = lambda v: v.reshape(1, -1)

    aw = ab_conv_w.shape[2]
    qkv, ya = _proj0_lru(x, row(norm_mix_g[0]), ab_w_in[0].astype(BF16), ab_conv_w[0], row(ab_conv_b[0]),
                         _block_diag(rg_w_a[0]).astype(BF16), _block_diag(rg_w_x[0]).astype(BF16),
                         row(rg_b_a[0]), row(rg_b_x[0]), row(rg_lambda[0]))
    tk = TQ + LEFT_CHUNKS * CHUNK
    dist = jnp.clip(tk - jnp.arange(TQ + tk), -REL_CLIP, REL_CLIP) + REL_CLIP
    bias = _attn_bias(attn_rel_bias[0][:, None, dist])
    yb = _attention(qkv, bias, batch)
    wo = ab_w_out[0].astype(BF16)
    wt = _router_weights(moe_router_group[0], moe_router_expert[0])
    h1, hn1, meta1, metat1, cnt1 = _proj_route(ya, 0, yb, 0, xt, wo[:aw], wo[aw:], row(norm_ffn_g[0]), wt)
    dest1, ys1 = _moe(0, metat1, cnt1, hn1, moe_w_gate, moe_w_up, moe_w_down)

    dk = c_norm_g.shape[1] // 8
    h2, qs, lf, iv, gs = _comb_proj1(dest1, h1, meta1, ys1, row(norm_mix_g[1]), c_w_in[0].astype(BF16),
                                     c_lb_logits, 1)
    om = _hgrn(qs, lf, iv, gs, row(c_norm_g[0]), batch, dk)
    wo = c_w_out[0].astype(BF16)
    half = wo.shape[0] // 2
    wt = _router_weights(moe_router_group[1], moe_router_expert[1])
    h3, hn3, meta3, metat3, cnt3 = _proj_route(om, 0, om, 1, h2, wo[:half], wo[half:], row(norm_ffn_g[1]), wt)
    dest3, ys3 = _moe(1, metat3, cnt3, hn3, moe_w_gate, moe_w_up, moe_w_down)

    out = _comb_final(dest3, h3, meta3, ys3, row(norm_final_g))
    return out.reshape(batch, seq, d)
```

```python
import functools

import jax
import jax.numpy as jnp
from jax import lax
from jax.experimental import pallas as pl
from jax.experimental.pallas import tpu as pltpu

F32 = jnp.float32
BF16 = jnp.bfloat16

EPS = 1e-6
LOG2E = 1.4426950408889634
RG_C = 8.0
CHUNK = 64
LEFT_CHUNKS = 8
REL_CLIP = 256
N_GROUPS = 4
EXPERTS_PER_GROUP = 8
N_EXPERTS = N_GROUPS * EXPERTS_PER_GROUP
TOP_K = 2
ROUTE_ROWS = 40

LANES = 128
SUBLANES = 8
SUB = 16

TS_LRU = 512
TQ = 256
ATTN_GROUP = 4
TM_ROUTE = 1024
TM_ROUTE_SUB = 256
TB_EXPERT = 512
X_SLOTS = 3
TM_COMB = 256
PLAN_TILE = 2048
TH = 256
HGRN_GROUP = 4
DMA_UNROLL = 8
COMB_SLOTS = 3
VMEM_MB = 48


def _params(n_axes, vmem_mb=VMEM_MB):
    return pltpu.CompilerParams(dimension_semantics=("arbitrary",) * n_axes,
                                vmem_limit_bytes=vmem_mb * 1024 * 1024)


def _const_spec(shape):
    nd = len(shape)
    return pl.BlockSpec(shape, lambda *_: (0,) * nd)


def _rms(x, g):
    return x * lax.rsqrt(jnp.mean(x * x, axis=-1, keepdims=True) + EPS) * g


def _dot(a, b):
    return jnp.dot(a, b, preferred_element_type=F32)


def _dot_nt(a, b):
    return lax.dot_general(a, b, (((1,), (1,)), ((), ())), preferred_element_type=F32)


def _dot_tn(a, b):
    return lax.dot_general(a, b, (((0,), (0,)), ((), ())), preferred_element_type=F32)


def _split_bf16(x):
    hi = x.astype(BF16)
    lo = (x - hi.astype(F32)).astype(BF16)
    return hi, lo


def _to_slab(ref, val):
    m = val.shape[0]
    for j in range(val.shape[1] // LANES):
        ref[pl.ds(j, m, stride=SUBLANES), :] = val[:, LANES * j:LANES * (j + 1)]


def _from_slab(ref, m):
    return jnp.concatenate([ref[pl.ds(j, m, stride=SUBLANES), :] for j in range(SUBLANES)], axis=1)


def _proj0_lru_kernel(x_ref, g_ref, w_ref, cw_ref, cb_ref, wa_ref, wx_ref, ba_ref, bx_ref, lam_ref,
                      qkv_ref, ya_ref, xbuf, h_ref, a_s, u_s):
    ts = x_ref.shape[0]
    w = ya_ref.shape[1]
    bw = qkv_ref.shape[1] // 3

    @pl.when(pl.program_id(1) == 0)
    def _():
        xbuf[pl.ds(0, SUBLANES), :] = jnp.zeros((SUBLANES, w), F32)
        h_ref[...] = jnp.zeros_like(h_ref)

    hn = _rms(x_ref[...], g_ref[...]).astype(BF16)

    def qkv_cols(c, scale=None):
        z = _dot(hn, w_ref[:, 2 * w + c * bw:2 * w + (c + 1) * bw])
        qkv_ref[:, c * bw:(c + 1) * bw] = (z if scale is None else z * scale).astype(BF16)

    xg = _dot(hn, w_ref[:, :2 * w])

    xbuf[pl.ds(SUBLANES, ts), :] = xg[:, :w]
    nk = cw_ref.shape[0]
    y = cb_ref[...]
    for k in range(nk):
        y = y + cw_ref[nk - 1 - k:nk - k, :] * xbuf[pl.ds(SUBLANES - k, ts), :]
    xbuf[pl.ds(0, SUBLANES), :] = xbuf[pl.ds(ts, SUBLANES), :]

    yb = y.astype(BF16)
    rg = jax.nn.sigmoid(_dot(yb, wa_ref[...]) + ba_ref[...])
    ig = jax.nn.sigmoid(_dot(yb, wx_ref[...]) + bx_ref[...])
    qkv_cols(0, CHUNK ** -0.5 * LOG2E)
    lam = lam_ref[...]
    log_sig = jnp.minimum(lam, 0.0) - jnp.log1p(jnp.exp(-jnp.abs(lam)))
    log_a = RG_C * rg * log_sig
    a = jnp.exp(log_a)
    m = 1.0 - a * a
    u = jnp.where(m > 0.0, m * lax.rsqrt(m), 0.0) * (ig * y)
    qkv_cols(1)

    grp = (ts // SUBLANES, SUBLANES, w)
    a = a.reshape(grp)
    u = u.reshape(grp)
    rowm = lax.broadcasted_iota(jnp.int32, grp, 1)
    for s in (1, 2, 4):
        keep = rowm >= s
        a_sh = jnp.where(keep, pltpu.roll(a, s, 1), 1.0)
        u_sh = jnp.where(keep, pltpu.roll(u, s, 1), 0.0)
        u = a * u_sh + u
        a = a * a_sh
    a_s[...] = a.reshape(ts, w)
    u_s[...] = u.reshape(ts, w)
    qkv_cols(2)

    def group(gi, h):
        off = pl.multiple_of(gi * SUBLANES, SUBLANES)
        hg = a_s[pl.ds(off, SUBLANES), :] * h + u_s[pl.ds(off, SUBLANES), :]
        u_s[pl.ds(off, SUBLANES), :] = hg
        return jnp.broadcast_to(hg[SUBLANES - 1:SUBLANES, :], hg.shape)

    h_ref[...] = lax.fori_loop(0, ts // SUBLANES, group, h_ref[...])

    ga = xg[:, w:]
    gelu = 0.5 * ga * (1.0 + jnp.tanh(0.7978845608028654 * (ga + 0.044715 * (ga * ga * ga))))
    ya_ref[...] = (u_s[...] * gelu).astype(BF16)


def _proj0_lru(x, g, w_in, cw, cb, wa, wx, ba, bx, lam):
    batch, s, d = x.shape
    w = cw.shape[1]
    nqkv = w_in.shape[1] - 2 * w
    ts = min(TS_LRU, s)
    small = [g, w_in, cw, cb, wa, wx, ba, bx, lam]
    blk = lambda width: pl.BlockSpec((None, ts, width), lambda b, j: (b, j, 0))
    qkv, ya = pl.pallas_call(
        _proj0_lru_kernel,
        grid=(batch, s // ts),
        in_specs=[blk(d)] + [_const_spec(a.shape) for a in small],
        out_specs=[blk(nqkv), blk(w)],
        out_shape=[jax.ShapeDtypeStruct((batch, s, nqkv), BF16), jax.ShapeDtypeStruct((batch, s, w), BF16)],
        scratch_shapes=[pltpu.VMEM((ts + SUBLANES, w), F32), pltpu.VMEM((SUBLANES, w), F32),
                        pltpu.VMEM((ts, w), F32), pltpu.VMEM((ts, w), F32)],
        compiler_params=_params(2),
        name="proj0_lru",
    )(x, *small)
    return qkv.reshape(batch * s, nqkv), ya.reshape(batch * s, w)


def _attn_bias_kernel(base_ref, o_ref):
    tq, tk = o_ref.shape
    nkb = tk // tq
    row = base_ref[...]
    full = pltpu.roll(jnp.broadcast_to(row, (tq, tq + tk)), 0, 1, stride=1, stride_axis=0)
    qc = lax.broadcasted_iota(jnp.int32, (tq, tk), 0) // CHUNK
    col = lax.broadcasted_iota(jnp.int32, (tq, tk), 1)
    kc = col // CHUNK
    first_ok = (nkb - 1 - pl.program_id(0)) * tq
    valid = (kc >= qc) & (kc <= qc + LEFT_CHUNKS) & (col >= first_ok)
    o_ref[...] = jnp.where(valid, full[:, tq:] * LOG2E, -1e30)


def _attn_bias(base):
    h = base.shape[0]
    tk = TQ + LEFT_CHUNKS * CHUNK
    return pl.pallas_call(
        _attn_bias_kernel,
        grid=(tk // TQ, h),
        in_specs=[pl.BlockSpec((None, 1, TQ + tk), lambda v, i: (i, 0, 0))],
        out_specs=pl.BlockSpec((None, None, TQ, tk), lambda v, i: (v, i, 0, 0)),
        out_shape=jax.ShapeDtypeStruct((tk // TQ, h, TQ, tk), F32),
        compiler_params=_params(2),
        name="attn_bias",
    )(base)


def _attn_kernel(q_ref, k0_ref, k1_ref, k2_ref, v0_ref, v1_ref, v2_ref, bias_ref, o_ref):
    tq = q_ref.shape[0]
    half_w = LANES // 2
    lo = lax.broadcasted_iota(jnp.int32, (tq, LANES), 1) < half_w
    lo_k = lax.broadcasted_iota(jnp.int32, (3 * tq, LANES), 1) < half_w
    k_refs = (k0_ref, k1_ref, k2_ref)
    v_refs = (v0_ref, v1_ref, v2_ref)
    npair = q_ref.shape[1] // LANES
    for j0 in range(0, npair, ATTN_GROUP):
        pairs = range(j0, min(j0 + ATTN_GROUP, npair))
        qk, vh = {}, {}
        for j in pairs:
            sl = slice(LANES * j, LANES * (j + 1))
            q2 = q_ref[:, sl]
            kcat = jnp.concatenate([r[:, sl] for r in k_refs], axis=0)
            vcat = jnp.concatenate([r[:, sl] for r in v_refs], axis=0)
            one = jnp.ones_like(vcat)
            zero = jnp.zeros_like(q2)
            vh[j] = (jnp.where(lo_k, vcat, one), jnp.where(lo_k, one, vcat))
            qk[j] = [_dot_nt(jnp.where(lo if half == 0 else jnp.logical_not(lo), q2, zero), kcat)
                     for half in range(2)]
        ps = {}
        for j in pairs:
            ps[j] = []
            for half in range(2):
                s = qk[j][half] + bias_ref[2 * j + half]
                ps[j].append(jnp.exp2(s - jnp.max(s, axis=-1, keepdims=True)).astype(BF16))
        for j in pairs:
            pv = [_dot(ps[j][half], vh[j][half]) for half in range(2)]
            num = jnp.where(lo, pv[0], pv[1])
            den = pltpu.roll(jnp.where(lo, pv[1], pv[0]), half_w, 1)
            o_ref[:, LANES * j:LANES * (j + 1)] = (num / den).astype(BF16)


def _attention(qkv, bias, batch):
    t, w3 = qkv.shape
    w = w3 // 3
    s = t // batch
    qkv3 = qkv.reshape(batch, s, w3)
    nvar = bias.shape[0]

    def kv_spec(colblk, back):
        return pl.BlockSpec((None, TQ, w), lambda i, b: (b, jnp.maximum(i - back, 0), colblk))

    out = pl.pallas_call(
        _attn_kernel,
        grid=(s // TQ, batch),
        in_specs=[pl.BlockSpec((None, TQ, w), lambda i, b: (b, i, 0)),
                  kv_spec(1, 2), kv_spec(1, 1), kv_spec(1, 0),
                  kv_spec(2, 2), kv_spec(2, 1), kv_spec(2, 0),
                  pl.BlockSpec((None,) + bias.shape[1:], lambda i, b: (jnp.minimum(i, nvar - 1), 0, 0, 0))],
        out_specs=pl.BlockSpec((None, TQ, w), lambda i, b: (b, i, 0)),
        out_shape=jax.ShapeDtypeStruct((batch, s, w), BF16),
        compiler_params=_params(2),
        name="attention",
    )(qkv3, qkv3, qkv3, qkv3, qkv3, qkv3, qkv3, bias)
    return out.reshape(t, w)


def _proj_route_kernel(a_ref, b_ref, r_ref, wa_ref, wb_ref, g_ref, wt_ref,
                       h_ref, hn_ref, meta_ref, metat_ref, cnt_ref, carry_ref):
    tm = a_ref.shape[0]
    ts = min(TM_ROUTE_SUB, tm)

    @pl.when(pl.program_id(0) == 0)
    def _():
        carry_ref[...] = jnp.zeros_like(carry_ref)

    wt = wt_ref[...]
    ridx = lax.broadcasted_iota(jnp.int32, (ROUTE_ROWS, ts), 0).astype(F32)
    r8 = lax.broadcasted_iota(jnp.int32, (SUBLANES, ts), 0)
    rr = lax.broadcasted_iota(jnp.int32, (ts, ts), 0)
    cc = lax.broadcasted_iota(jnp.int32, (ts, ts), 1)
    utri = jnp.where(rr < cc, 1.0, 0.0).astype(BF16)
    neg = -jnp.inf
    far = float(LANES)
    carry = carry_ref[...]

    all_logits = []
    for sub in range(tm // ts):
        rows = pl.ds(sub * ts, ts)
        h = r_ref[rows, :] + (_dot(a_ref[rows, :], wa_ref[...]) + _dot(b_ref[rows, :], wb_ref[...]))
        h_ref[rows, :] = h
        hn = _rms(h, g_ref[...])
        _to_slab(hn_ref.at[pl.ds(sub * ts * SUBLANES, ts * SUBLANES), :], hn)

        hi, lo = _split_bf16(hn)
        p_hi = _dot_nt(wt, hi)
        p_lo = _dot_nt(wt[:LANES], lo)
        all_logits.append((p_hi[:ROUTE_ROWS] + p_hi[LANES:LANES + ROUTE_ROWS]) + p_lo[:ROUTE_ROWS])

    for sub, logits in enumerate(all_logits):
        rows = pl.ds(sub * ts, ts)

        def first_max(mask, logits=logits):
            v = jnp.max(jnp.where(mask, logits, neg), axis=0, keepdims=True)
            idx = jnp.min(jnp.where(mask & (logits == v), ridx, far), axis=0, keepdims=True)
            return v, idx

        gmask = ridx < N_GROUPS
        gmax, gidx = first_max(gmask)
        g_gate = 1.0 / jnp.sum(jnp.where(gmask, jnp.exp(logits - gmax), 0.0), axis=0, keepdims=True)
        e_lo = N_GROUPS + EXPERTS_PER_GROUP * gidx
        emask = (ridx >= e_lo) & (ridx < e_lo + EXPERTS_PER_GROUP)
        v1, i1 = first_max(emask)
        v2, i2 = first_max(emask & (ridx != i1))
        tt = jnp.exp(v2 - v1)
        w1 = g_gate / (1.0 + tt)
        w2 = g_gate * tt / (1.0 + tt)

        sel1 = ridx == i1
        sel2 = ridx == i2
        onehot = jnp.where(sel1 | sel2, 1.0, 0.0)
        before = _dot(onehot.astype(BF16), utri) + carry[:, 0:1]
        rank1 = jnp.sum(jnp.where(sel1, before, 0.0), axis=0, keepdims=True)
        rank2 = jnp.sum(jnp.where(sel2, before, 0.0), axis=0, keepdims=True)
        carry = carry + jnp.sum(onehot, axis=1, keepdims=True)

        mt = jnp.zeros((SUBLANES, ts), F32)
        for c, val in enumerate((i1 - N_GROUPS, i2 - N_GROUPS, rank1, rank2, w1, w2)):
            mt = jnp.where(r8 == c, val, mt)
        metat_ref[:, rows] = mt
        meta_ref[rows, :] = jnp.concatenate([mt, jnp.zeros((LANES - SUBLANES, ts), F32)], axis=0).T

    carry_ref[...] = carry
    cnt_ref[...] = carry


def _proj_route(a, acol, b, bcol, resid, wa, wb, g, wt):
    t, d = resid.shape
    kw = wa.shape[0]
    tm = min(TM_ROUTE, t)
    return pl.pallas_call(
        _proj_route_kernel,
        grid=(t // tm,),
        in_specs=[pl.BlockSpec((tm, kw), lambda i: (i, acol)), pl.BlockSpec((tm, kw), lambda i: (i, bcol)),
                  pl.BlockSpec((tm, d), lambda i: (i, 0)),
                  _const_spec(wa.shape), _const_spec(wb.shape), _const_spec(g.shape), _const_spec(wt.shape)],
        out_specs=[pl.BlockSpec((tm, d), lambda i: (i, 0)),
                   pl.BlockSpec((tm * SUBLANES, LANES), lambda i: (i, 0)),
                   pl.BlockSpec((tm, LANES), lambda i: (i, 0)),
                   pl.BlockSpec((SUBLANES, tm), lambda i: (0, i)),
                   pl.BlockSpec((ROUTE_ROWS, LANES), lambda i: (0, 0))],
        out_shape=[jax.ShapeDtypeStruct((t, d), F32),
                   jax.ShapeDtypeStruct((t * SUBLANES, LANES), F32),
                   jax.ShapeDtypeStruct((t, LANES), F32),
                   jax.ShapeDtypeStruct((SUBLANES, t), F32),
                   jax.ShapeDtypeStruct((ROUTE_ROWS, LANES), F32)],
        scratch_shapes=[pltpu.VMEM((ROUTE_ROWS, LANES), F32)],
        compiler_params=_params(1),
        name="proj_route",
    )(a, b, resid, wa, wb, g, wt)


def _plan_kernel(cnt_ref, meta_ref, dest_ref, tab_ref, start_s, *, tb):
    r_n = ROUTE_ROWS
    tab_w = tab_ref.shape[1]

    @pl.when(pl.program_id(0) == 0)
    def _():
        row = lax.broadcasted_iota(jnp.int32, (r_n, LANES), 0)
        lane = lax.broadcasted_iota(jnp.int32, (r_n, LANES), 1)

        def prefix(v):
            s = 1
            while s < r_n:
                v = v + jnp.where(row >= s, pltpu.roll(v, s, 0), 0.0)
                s *= 2
            return v

        counts = cnt_ref[...]
        padded = jnp.floor((counts + (tb - 1)) * (1.0 / tb)) * tb
        pad_end = prefix(padded)
        pad_start = pad_end - padded
        pad_lo = pad_start + counts
        start_s[...] = pad_start
        used = padded > 0.0
        order = prefix(jnp.where(used, 1.0, 0.0)) - 1.0
        nused = order[r_n - 1:r_n, :] + 1.0
        nact = pad_end[r_n - 1:r_n, :] * (1.0 / tb)
        expert = (row - N_GROUPS).astype(F32)
        lane_f = lane.astype(F32)
        used_seq = jnp.sum(jnp.where(used & (order == jnp.minimum(lane_f, nused - 1.0)), expert, 0.0),
                           axis=0, keepdims=True)

        erow = lax.broadcasted_iota(jnp.int32, (r_n, tab_w), 0)
        blk_start = lax.broadcasted_iota(jnp.int32, (r_n, tab_w), 1).astype(F32) * tb
        is_expert = (erow >= N_GROUPS) & (erow < N_GROUPS + N_EXPERTS)
        ended = jnp.where(is_expert & (pad_end[:, 0:1] <= blk_start), 1.0, 0.0)
        blk_expert = jnp.minimum(jnp.sum(ended, axis=0, keepdims=True), N_EXPERTS - 1.0)
        mine = (erow - N_GROUPS).astype(F32) == blk_expert
        blk_order = jnp.sum(jnp.where(mine, order[:, 0:1], 0.0), axis=0, keepdims=True)
        blk_lo = jnp.sum(jnp.where(mine, pad_lo[:, 0:1], 0.0), axis=0, keepdims=True)
        blk_fill = jnp.clip(blk_lo - blk_start[0:1, :], 0.0, float(tb))

        cols = jnp.where(lane == 0, pad_lo, jnp.where(lane == 1, pad_end, 0.0))
        rows_t = jnp.concatenate([cols, jnp.zeros((LANES - r_n, LANES), F32)], axis=0).T
        misc = jnp.where(lane[0:1, :] == 0, nact, jnp.where(lane[0:1, :] == 1, nused, 0.0))
        wide = lambda v: v if tab_w == LANES else jnp.concatenate(
            [v, jnp.zeros((v.shape[0], tab_w - LANES), F32)], axis=1)
        tab = jnp.concatenate([wide(rows_t[0:2, :]), wide(used_seq), wide(misc), blk_order, blk_fill,
                               jnp.zeros((2, tab_w), F32)], axis=0)
        tab_ref[...] = tab.astype(jnp.int32)

    for c in range(dest_ref.shape[1] // LANES):
        sl = slice(LANES * c, LANES * (c + 1))
        for k in range(TOP_K):
            expert = meta_ref[k:k + 1, sl]
            start = jnp.zeros((1, LANES), F32)
            for e in range(N_EXPERTS):
                start = jnp.where(expert == float(e), start_s[N_GROUPS + e:N_GROUPS + e + 1, :], start)
            dest_ref[k:k + 1, sl] = (start + meta_ref[TOP_K + k:TOP_K + k + 1, sl]).astype(jnp.int32)


def _plan(meta_t, cnt, tb):
    t = meta_t.shape[1]
    nblk = -(-(t * TOP_K + N_EXPERTS * (tb - 1)) // tb)
    tile = min(PLAN_TILE, t)
    tab_w = -(-nblk // LANES) * LANES
    dest, tab = pl.pallas_call(
        functools.partial(_plan_kernel, tb=tb),
        grid=(t // tile,),
        in_specs=[_const_spec(cnt.shape), pl.BlockSpec((SUBLANES, tile), lambda i: (0, i))],
        out_specs=[pl.BlockSpec((TOP_K, tile), lambda i: (0, i)), pl.BlockSpec((SUBLANES, tab_w), lambda i: (0, 0))],
        out_shape=[jax.ShapeDtypeStruct((TOP_K, t), jnp.int32), jax.ShapeDtypeStruct((SUBLANES, tab_w), jnp.int32)],
        scratch_shapes=[pltpu.VMEM((ROUTE_ROWS, LANES), F32)],
        compiler_params=_params(1),
        name="plan",
    )(cnt, meta_t)
    experts = slice(N_GROUPS, N_GROUPS + N_EXPERTS)
    return (dest.reshape(-1), tab[0, experts], tab[1, experts], tab[4, :nblk], tab[5, :nblk],
            tab[2, :N_EXPERTS], tab[3, :2], nblk)


def _dispatch_kernel(dest_ref, plo_ref, phi_ref, nact_ref, hn_ref, xs_ref, zero_ref, sem, zsem):
    tm = hn_ref.shape[0] // SUBLANES
    t = dest_ref.shape[0] // TOP_K
    tb = zero_ref.shape[0] // SUBLANES
    nblk = xs_ref.shape[0] // zero_ref.shape[0]
    i = pl.program_id(0)

    def zero_copy(row, nrows):
        return pltpu.make_async_copy(
            zero_ref.at[pl.ds(0, nrows * SUBLANES), :],
            xs_ref.at[pl.ds(pl.multiple_of(row * SUBLANES, SUBLANES), nrows * SUBLANES), :], zsem)

    def zero_fill(start):
        def go(cp):
            cp.start() if start else cp.wait()

        def per_expert(e, c):
            off = plo_ref[e]
            n = phi_ref[e] - off
            bit = tb // 2
            while bit:
                pl.when((n & bit) != 0)(functools.partial(lambda o, b: go(zero_copy(o, b)), off, bit))
                off = off + (n & bit)
                bit //= 2
            return c

        def per_block(b, c):
            go(zero_copy(b * tb, tb))
            return c

        lax.fori_loop(0, N_EXPERTS, per_expert, 0)
        lax.fori_loop(nact_ref[0], nblk, per_block, 0)

    @pl.when(i == 0)
    def _():
        zero_ref[...] = jnp.zeros_like(zero_ref)
        zero_fill(True)

    def copy(r, k):
        d = dest_ref[k * t + i * tm + r]
        return pltpu.make_async_copy(
            hn_ref.at[pl.ds(pl.multiple_of(r * SUBLANES, SUBLANES), SUBLANES), :],
            xs_ref.at[pl.ds(pl.multiple_of(d * SUBLANES, SUBLANES), SUBLANES), :], sem)

    def start(r, c):
        for k in range(TOP_K):
            copy(r, k).start(priority=k)
        return c

    lax.fori_loop(0, tm, start, 0, unroll=DMA_UNROLL)
    for k in range(TOP_K):
        pltpu.make_async_copy(hn_ref, xs_ref.at[pl.ds(0, tm * SUBLANES), :], sem).wait()

    @pl.when(i == 0)
    def _():
        zero_fill(False)


def _dispatch(dest, pad_lo, pad_hi, nact, hn_slab, nblk, tb):
    t = hn_slab.shape[0] // SUBLANES
    tm = min(TM_COMB, t)
    return pl.pallas_call(
        _dispatch_kernel,
        grid_spec=pltpu.PrefetchScalarGridSpec(
            num_scalar_prefetch=4,
            grid=(t // tm,),
            in_specs=[pl.BlockSpec((tm * SUBLANES, LANES), lambda i, *_: (i, 0))],
            out_specs=pl.BlockSpec(memory_space=pl.ANY),
            scratch_shapes=[pltpu.VMEM((tb * SUBLANES, LANES), F32),
                            pltpu.SemaphoreType.DMA(()), pltpu.SemaphoreType.DMA(())]),
        out_shape=jax.ShapeDtypeStruct((nblk * tb * SUBLANES, LANES), F32),
        compiler_params=_params(1),
        name="dispatch",
    )(dest, pad_lo, pad_hi, nact, hn_slab)


def _expert_kernel(ord_ref, fill_ref, useq_ref, nact_ref, xs_ref, wg_ref, wu_ref, wd_ref, ys_ref,
                   xbuf, xsem, wg_f, wu_f, wd_f, wsem, wg_s, wu_s, wd_s, *, layer):
    i = pl.program_id(0)
    n = pl.num_programs(0)
    rows = ys_ref.shape[0]
    tb = rows // SUBLANES
    active = i < nact_ref[0]
    order = ord_ref[i]
    changed = (i == 0) | (order != ord_ref[jnp.maximum(i - 1, 0)])
    slot = i % X_SLOTS
    ahead = X_SLOTS - 1

    def w_copies(j):
        e = useq_ref[jnp.minimum(j, useq_ref.shape[0] - 1)]
        ws = j % 2
        return [pltpu.make_async_copy(src.at[layer, e], dst.at[ws], wsem.at[ws])
                for src, dst in ((wg_ref, wg_f), (wu_ref, wu_f), (wd_ref, wd_f))]

    @pl.when(i == 0)
    def _():
        for cp in w_copies(0):
            cp.start()

    def x_copy(step, sl):
        blk = jnp.minimum(step, nact_ref[0] - 1)
        return pltpu.make_async_copy(xs_ref.at[pl.ds(pl.multiple_of(blk * rows, rows), rows), :],
                                     xbuf.at[sl], xsem.at[sl])

    @pl.when(i == 0)
    def _():
        for a in range(ahead):
            x_copy(a, a).start()

    x_copy(i, slot).wait()
    x_copy(i + ahead, (i + ahead) % X_SLOTS).start()

    @pl.when(i == n - 1)
    def _():
        for a in range(1, X_SLOTS):
            x_copy(i + a, (i + a) % X_SLOTS).wait()

    @pl.when(active & changed)
    def _():
        for cp in w_copies(order):
            cp.wait()
        for cp in w_copies(order + 1):
            cp.start()
        ws = order % 2
        wg_s[...] = wg_f[ws].astype(BF16)
        wu_s[...] = wu_f[ws].astype(BF16)
        wd_s[...] = wd_f[ws].astype(BF16)

    @pl.when(i == n - 1)
    def _():
        for cp in w_copies(nact_ref[1]):
            cp.wait()

    def mlp(nrows):
        part = pl.ds(0, nrows * SUBLANES)
        x = _from_slab(xbuf.at[slot, part, :], nrows).astype(BF16)
        gate = _dot(x, wg_s[...])
        up = _dot(x, wu_s[...])
        hid = (gate * jax.nn.sigmoid(gate) * up).astype(BF16)
        _to_slab(ys_ref.at[part, :], _dot(hid, wd_s[...]))

    half = tb // 2
    upper_empty = fill_ref[i] <= half

    @pl.when(active & jnp.logical_not(upper_empty))
    def _():
        mlp(tb)

    @pl.when(active & upper_empty)
    def _():
        mlp(half)
        ys_ref[pl.ds(half * SUBLANES, half * SUBLANES), :] = jnp.zeros((half * SUBLANES, LANES), F32)

    @pl.when(jnp.logical_not(active))
    def _():
        ys_ref[...] = jnp.zeros_like(ys_ref)


def _experts(blk_order, blk_fill, used_seq, nact, xs, wg, wu, wd, layer, tb):
    nblk = blk_order.shape[0]
    _, _, d, de = wg.shape
    anyspec = pl.BlockSpec(memory_space=pl.ANY)
    return pl.pallas_call(
        functools.partial(_expert_kernel, layer=layer),
        grid_spec=pltpu.PrefetchScalarGridSpec(
            num_scalar_prefetch=4,
            grid=(nblk,),
            in_specs=[anyspec, anyspec, anyspec, anyspec],
            out_specs=pl.BlockSpec((tb * SUBLANES, LANES), lambda i, *_: (i, 0)),
            scratch_shapes=[pltpu.VMEM((X_SLOTS, tb * SUBLANES, LANES), F32), pltpu.SemaphoreType.DMA((X_SLOTS,)),
                            pltpu.VMEM((2, d, de), F32), pltpu.VMEM((2, d, de), F32), pltpu.VMEM((2, de, d), F32),
                            pltpu.SemaphoreType.DMA((2,)),
                            pltpu.VMEM((d, de), BF16), pltpu.VMEM((d, de), BF16), pltpu.VMEM((de, d), BF16)]),
        out_shape=jax.ShapeDtypeStruct(xs.shape, F32),
        compiler_params=_params(1),
        name="experts",
    )(blk_order, blk_fill, used_seq, nact, xs, wg, wu, wd)


def _combine(dest_ref, ys_ref, meta_ref, buf_ref, sem_ref, tm):
    i = pl.program_id(0)
    n = pl.num_programs(0)
    slot = i % COMB_SLOTS
    ahead = COMB_SLOTS - 1
    t = dest_ref.shape[0] // TOP_K

    def copy(tile, sl, r, k):
        d = dest_ref[k * t + tile * tm + r]
        row = r * SUBLANES if isinstance(r, int) else pl.multiple_of(r * SUBLANES, SUBLANES)
        return pltpu.make_async_copy(
            ys_ref.at[pl.ds(pl.multiple_of(d * SUBLANES, SUBLANES), SUBLANES), :],
            buf_ref.at[sl, k, pl.ds(row, SUBLANES), :],
            sem_ref.at[sl])

    def start_tile(tile, sl):
        def body(r, c):
            for k in range(TOP_K):
                copy(tile, sl, r, k).start(priority=k)
            return c
        lax.fori_loop(0, tm, body, 0, unroll=DMA_UNROLL)

    def wait_tile(sl):
        for k in range(TOP_K):
            pltpu.make_async_copy(ys_ref.at[pl.ds(0, tm * SUBLANES), :], buf_ref.at[sl, k], sem_ref.at[sl]).wait()

    @pl.when(i == 0)
    def _():
        for a in range(ahead):
            start_tile(jnp.minimum(a, n - 1), a)

    wait_tile(slot)

    nxt = jnp.minimum(i + ahead, n - 1)
    nslot = (i + ahead) % COMB_SLOTS

    def issue(part, nparts):
        rows = tm // nparts
        for r in range(part * rows, (part + 1) * rows):
            for k in range(TOP_K):
                copy(nxt, nslot, r, k).start(priority=k)

    @pl.when(i == n - 1)
    def _():
        for a in range(1, ahead):
            wait_tile((i + a) % COMB_SLOTS)

    y = None
    for k in range(TOP_K):
        gate = meta_ref[:, 2 * TOP_K + k:2 * TOP_K + k + 1]
        term = _from_slab(buf_ref.at[slot, k], tm) * gate
        y = term if y is None else y + term
    return y, issue


def _combine_drain(buf_ref, sem_ref, ys_ref, tm):
    i = pl.program_id(0)

    @pl.when(i == pl.num_programs(0) - 1)
    def _():
        sl = (i + COMB_SLOTS - 1) % COMB_SLOTS
        for k in range(TOP_K):
            pltpu.make_async_copy(ys_ref.at[pl.ds(0, tm * SUBLANES), :], buf_ref.at[sl, k], sem_ref.at[sl]).wait()


def _combine_scratch(tm):
    return [pltpu.VMEM((COMB_SLOTS, TOP_K, tm * SUBLANES, LANES), F32), pltpu.SemaphoreType.DMA((COMB_SLOTS,))]


def _comb_proj1_kernel(dest_ref, h_ref, meta_ref, ys_ref, g_ref, w_ref, lbl_ref,
                       h2_ref, qs_ref, lf_ref, iv_ref, gs_ref, buf_ref, sem_ref, *, lb_rows):
    tm = h_ref.shape[0]
    y, issue = _combine(dest_ref, ys_ref, meta_ref, buf_ref, sem_ref, tm)
    h2 = h_ref[...] + y
    h2_ref[...] = h2
    hn = _rms(h2, g_ref[...]).astype(BF16)
    d = h2.shape[1]
    lbl = lbl_ref[...]
    ex = jnp.exp(lbl - jnp.max(lbl, axis=0, keepdims=True))
    lb = jnp.sum(ex[:lb_rows], axis=0, keepdims=True) / jnp.sum(ex, axis=0, keepdims=True)

    def col(c):
        return _dot(hn, w_ref[:, d * c:d * (c + 1)])

    q = col(0)
    issue(0, 4)
    fz = col(1)
    issue(1, 4)
    iv = col(2)
    issue(2, 4)
    gz = col(3)
    issue(3, 4)
    qs_ref[...] = (q * jax.nn.sigmoid(q)).astype(BF16)
    lf_ref[...] = jnp.log(lb + (1.0 - lb) * jax.nn.sigmoid(fz)) * LOG2E
    iv_ref[...] = iv.astype(BF16)
    gs_ref[...] = (gz * jax.nn.sigmoid(gz)).astype(BF16)
    _combine_drain(buf_ref, sem_ref, ys_ref, tm)


def _comb_proj1(dest, h, meta, ys, g, w, lb_logits, lb_rows):
    t, d = h.shape
    tm = min(TM_COMB, t)
    row = lambda i, dd: (i, 0)
    const = lambda shape: pl.BlockSpec(shape, lambda i, dd: (0,) * len(shape))
    return pl.pallas_call(
        functools.partial(_comb_proj1_kernel, lb_rows=lb_rows),
        grid_spec=pltpu.PrefetchScalarGridSpec(
            num_scalar_prefetch=1,
            grid=(t // tm,),
            in_specs=[pl.BlockSpec((tm, d), row), pl.BlockSpec((tm, LANES), row),
                      pl.BlockSpec(memory_space=pl.ANY),
                      const(g.shape), const(w.shape), const(lb_logits.shape)],
            out_specs=[pl.BlockSpec((tm, d), row)] * 5,
            scratch_shapes=_combine_scratch(tm)),
        out_shape=[jax.ShapeDtypeStruct((t, d), F32), jax.ShapeDtypeStruct((t, d), BF16),
                   jax.ShapeDtypeStruct((t, d), F32), jax.ShapeDtypeStruct((t, d), BF16),
                   jax.ShapeDtypeStruct((t, d), BF16)],
        compiler_params=_params(1),
        name="comb_proj1",
    )(dest, h, meta, ys, g, w, lb_logits)


def _comb_final_kernel(dest_ref, h_ref, meta_ref, ys_ref, g_ref, o_ref, buf_ref, sem_ref):
    tm = h_ref.shape[0]
    y, issue = _combine(dest_ref, ys_ref, meta_ref, buf_ref, sem_ref, tm)
    issue(0, 1)
    o_ref[...] = _rms(h_ref[...] + y, g_ref[...])
    _combine_drain(buf_ref, sem_ref, ys_ref, tm)


def _comb_final(dest, h, meta, ys, g):
    t, d = h.shape
    tm = min(TM_COMB, t)
    row = lambda i, dd: (i, 0)
    return pl.pallas_call(
        _comb_final_kernel,
        grid_spec=pltpu.PrefetchScalarGridSpec(
            num_scalar_prefetch=1,
            grid=(t // tm,),
            in_specs=[pl.BlockSpec((tm, d), row), pl.BlockSpec((tm, LANES), row),
                      pl.BlockSpec(memory_space=pl.ANY),
                      pl.BlockSpec(g.shape, lambda i, dd: (0, 0))],
            out_specs=pl.BlockSpec((tm, d), row),
            scratch_shapes=_combine_scratch(tm)),
        out_shape=jax.ShapeDtypeStruct((t, d), F32),
        compiler_params=_params(1),
        name="comb_final",
    )(dest, h, meta, ys, g)


def _hgrn_kernel(qs_ref, lf_ref, iv_ref, gs_ref, ng_ref, o_ref, st_ref, d1_s, t16_s, *, dk):
    th = qs_ref.shape[0]
    nsub = CHUNK // SUB

    @pl.when(pl.program_id(1) == 0)
    def _():
        st_ref[...] = jnp.zeros_like(st_ref)

    rr = lax.broadcasted_iota(jnp.int32, (th, th), 0)
    cc = lax.broadcasted_iota(jnp.int32, (th, th), 1)
    same_sub = (rr // SUB) == (cc // SUB)
    same_chunk = (rr // CHUNK) == (cc // CHUNK)
    m_diag = same_sub & (cc <= rr)
    dsub = rr // SUB - cc // SUB
    m_off = [same_chunk & (dsub == dd) for dd in range(1, nsub)]

    hi, lo = _split_bf16(lf_ref[...])
    tri = jnp.where(m_diag, 1.0, 0.0).astype(BF16)
    d1 = _dot(tri, hi) + _dot(tri, lo)
    d1_s[...] = d1
    sub3 = (th // SUB, SUB, d1.shape[1])
    t16_s[...] = jnp.broadcast_to(d1.reshape(sub3)[:, SUB - 1:SUB, :], sub3).reshape(th, d1.shape[1])
    rrow = lax.broadcasted_iota(jnp.int32, (th, dk), 0)
    rsub = (rrow // SUB) % nsub
    rchunk = rrow // CHUNK
    nch = th // CHUNK

    def back_rows(dd):
        return [slice(CHUNK * c + SUB * dd, CHUNK * (c + 1)) for c in range(nch)]

    def prepare(hd):
        ls = pl.ds(hd * dk, dk)
        d1 = d1_s[:, ls]
        t16 = t16_s[:, ls]
        q = qs_ref[:, ls].astype(F32)
        kt = 1.0 - jnp.exp2(lf_ref[:, ls])
        suf = t16 - d1
        kx = (kt * jnp.exp2(suf)).astype(BF16)
        qv = [q * jnp.exp2(-suf)]
        acc = d1
        tail = suf
        for dd in range(1, nsub):
            qv.append(jnp.concatenate([q[sl] * jnp.exp2(acc[sl]) for sl in back_rows(dd)], axis=0))
            acc = acc + jnp.where(rsub >= dd, pltpu.roll(t16, SUB * dd, 0), 0.0)
            tail = tail + jnp.where(rsub < nsub - dd, pltpu.roll(t16, th - SUB * dd, 0), 0.0)
        b = acc
        qb = q * jnp.exp2(b)
        kend = kt * jnp.exp2(tail)
        zero = jnp.zeros_like(q)
        kend_x = jnp.concatenate([jnp.where(rchunk == c, kend, zero) for c in range(nch)], axis=1).astype(BF16)
        qb_x = jnp.concatenate([jnp.where(rchunk == c, qb, zero) for c in range(nch)], axis=1).astype(BF16)
        decs = [jnp.exp2(b[CHUNK * (c + 1) - 1:CHUNK * (c + 1), :]) for c in range(nch)]
        return ls, jnp.concatenate(qv, axis=0).astype(BF16), kx, kend_x, qb_x, decs

    def group(gi, carry):
        heads = [gi * HGRN_GROUP + u for u in range(HGRN_GROUP)]
        prep = [prepare(hd) for hd in heads]
        a4s = [_dot_nt(p[1], p[2]) for p in prep]
        incs = [_dot_tn(iv_ref[:, p[0]], p[3]) for p in prep]
        o_intras = []
        for p, a4 in zip(prep, a4s):
            att = jnp.where(m_diag, a4[:th], 0.0)
            at = th
            for dd in range(1, nsub):
                pieces = []
                for sl in back_rows(dd):
                    n = sl.stop - sl.start
                    pieces += [jnp.zeros((SUB * dd, th), F32), a4[at:at + n]]
                    at += n
                att = jnp.where(m_off[dd - 1], jnp.concatenate(pieces, axis=0), att)
            o_intras.append(_dot(att.astype(BF16), iv_ref[:, p[0]]))
        for hd, p, inc, o_intra in zip(heads, prep, incs, o_intras):
            ls, decs = p[0], p[5]
            st = st_ref[hd]
            starts = []
            for c in range(nch):
                starts.append(st)
                st = st * decs[c] + inc[:, dk * c:dk * (c + 1)]
            st_ref[hd] = st
            oh = o_intra + _dot_nt(p[4], jnp.concatenate(starts, axis=1).astype(BF16))
            on = oh * lax.rsqrt(jnp.mean(oh * oh, axis=-1, keepdims=True) + EPS)
            o_ref[:, ls] = (on * ng_ref[:, ls] * gs_ref[:, ls].astype(F32)).astype(BF16)
        return carry

    for gi in range(qs_ref.shape[1] // dk // HGRN_GROUP):
        group(gi, 0)


def _hgrn(qs, lf, iv, gs, ng, batch, dk):
    t, d = qs.shape
    s = t // batch
    th = min(TH, s)
    blk = pl.BlockSpec((None, th, d), lambda b, j: (b, j, 0))
    r3 = lambda a: a.reshape(batch, s, d)
    out = pl.pallas_call(
        functools.partial(_hgrn_kernel, dk=dk),
        grid=(batch, s // th),
        in_specs=[blk, blk, blk, blk, _const_spec(ng.shape)],
        out_specs=blk,
        out_shape=jax.ShapeDtypeStruct((batch, s, d), BF16),
        scratch_shapes=[pltpu.VMEM((d // dk, dk, dk), F32), pltpu.VMEM((th, d), F32), pltpu.VMEM((th, d), F32)],
        compiler_params=_params(2),
        name="hgrn2",
    )(r3(qs), r3(lf), r3(iv), r3(gs), ng)
    return out.reshape(t, d)


def _block_diag(w):
    n, c, dd = w.shape
    eye = jnp.eye(n, dtype=w.dtype)
    return (eye[:, None, :, None] * w[:, :, None, :]).reshape(n * c, n * dd)


def _router_weights(wg, we):
    d = wg.shape[0]
    w = jnp.concatenate([wg, we, jnp.zeros((d, LANES - wg.shape[1] - we.shape[1]), F32)], axis=1).T
    hi = w.astype(BF16)
    return jnp.concatenate([hi, (w - hi.astype(F32)).astype(BF16)], axis=0)


def _moe(layer, meta_t, cnt, hn_slab, moe_w_gate, moe_w_up, moe_w_down):
    dest, pad_lo, pad_hi, blk_order, blk_fill, used_seq, nact, nblk = _plan(meta_t, cnt, TB_EXPERT)
    xs = _dispatch(dest, pad_lo, pad_hi, nact, hn_slab, nblk, TB_EXPERT)
    ys = _experts(blk_order, blk_fill, used_seq, nact, xs, moe_w_gate, moe_w_up, moe_w_down, layer, TB_EXPERT)
    return dest, ys


def kernel(x, norm_mix_g, norm_ffn_g, norm_final_g, ab_w_in, ab_conv_w, ab_conv_b, rg_w_a, rg_b_a, rg_w_x, rg_b_x, rg_lambda, attn_rel_bias, ab_w_out, c_w_in, c_lb_logits, c_norm_g, c_w_out, moe_router_group, moe_router_expert, moe_w_gate, moe_w_up, moe_w_down):
    batch, seq, d = x.shape
    t = batch * seq
    xt = x.reshape(t, d)
    row = lambda v: v.reshape(1, -1)

    aw = ab_conv_w.shape[2]
    qkv, ya = _proj0_lru(x, row(norm_mix_g[0]), ab_w_in[0].astype(BF16), ab_conv_w[0], row(ab_conv_b[0]),
                         _block_diag(rg_w_a[0]).astype(BF16), _block_diag(rg_w_x[0]).astype(BF16),
                         row(rg_b_a[0]), row(rg_b_x[0]), row(rg_lambda[0]))
    tk = TQ + LEFT_CHUNKS * CHUNK
    dist = jnp.clip(tk - jnp.arange(TQ + tk), -REL_CLIP, REL_CLIP) + REL_CLIP
    bias = _attn_bias(attn_rel_bias[0][:, None, dist])
    yb = _attention(qkv, bias, batch)
    wo = ab_w_out[0].astype(BF16)
    wt = _router_weights(moe_router_group[0], moe_router_expert[0])
    h1, hn1, meta1, metat1, cnt1 = _proj_route(ya, 0, yb, 0, xt, wo[:aw], wo[aw:], row(norm_ffn_g[0]), wt)
    dest1, ys1 = _moe(0, metat1, cnt1, hn1, moe_w_gate, moe_w_up, moe_w_down)

    dk = c_norm_g.shape[1] // 8
    h2, qs, lf, iv, gs = _comb_proj1(dest1, h1, meta1, ys1, row(norm_mix_g[1]), c_w_in[0].astype(BF16),
                                     c_lb_logits, 1)
    om = _hgrn(qs, lf, iv, gs, row(c_norm_g[0]), batch, dk)
    wo = c_w_out[0].astype(BF16)
    half = wo.shape[0] // 2
    wt = _router_weights(moe_router_group[1], moe_router_expert[1])
    h3, hn3, meta3, metat3, cnt3 = _proj_route(om, 0, om, 1, h2, wo[:half], wo[half:], row(norm_ffn_g[1]), wt)
    dest3, ys3 = _moe(1, metat3, cnt3, hn3, moe_w_gate, moe_w_up, moe_w_down)

    out = _comb_final(dest3, h3, meta3, ys3, row(norm_final_g))
    return out.reshape(batch, seq, d)
```

```python
import functools

import jax
import jax.numpy as jnp
from jax import lax
from jax.experimental import pallas as pl
from jax.experimental.pallas import tpu as pltpu

F32 = jnp.float32
BF16 = jnp.bfloat16

EPS = 1e-6
LOG2E = 1.4426950408889634
RG_C = 8.0
CHUNK = 64
LEFT_CHUNKS = 8
REL_CLIP = 256
N_GROUPS = 4
EXPERTS_PER_GROUP = 8
N_EXPERTS = N_GROUPS * EXPERTS_PER_GROUP
TOP_K = 2
ROUTE_ROWS = 40

LANES = 128
SUBLANES = 8
XROWS = 4
SUB = 16

TS_LRU = 512
TQ = 256
ATTN_GROUP = 4
TM_ROUTE = 1024
TM_ROUTE_SUB = 256
TB_EXPERT = 512
X_SLOTS = 3
TM_COMB = 256
PLAN_TILE = 2048
TH = 256
HGRN_GROUP = 4
DMA_UNROLL = 8
COMB_SLOTS = 3
VMEM_MB = 48


def _params(n_axes, vmem_mb=VMEM_MB):
    return pltpu.CompilerParams(dimension_semantics=("arbitrary",) * n_axes,
                                vmem_limit_bytes=vmem_mb * 1024 * 1024)


def _const_spec(shape):
    nd = len(shape)
    return pl.BlockSpec(shape, lambda *_: (0,) * nd)


def _rms(x, g):
    return x * lax.rsqrt(jnp.mean(x * x, axis=-1, keepdims=True) + EPS) * g


def _dot(a, b):
    return jnp.dot(a, b, preferred_element_type=F32)


def _dot_nt(a, b):
    return lax.dot_general(a, b, (((1,), (1,)), ((), ())), preferred_element_type=F32)


def _dot_tn(a, b):
    return lax.dot_general(a, b, (((0,), (0,)), ((), ())), preferred_element_type=F32)


def _split_bf16(x):
    hi = x.astype(BF16)
    lo = (x - hi.astype(F32)).astype(BF16)
    return hi, lo


def _to_slab(ref, val):
    m = val.shape[0]
    for j in range(val.shape[1] // LANES):
        ref[pl.ds(j, m, stride=SUBLANES), :] = val[:, LANES * j:LANES * (j + 1)]


def _from_slab(ref, m):
    return jnp.concatenate([ref[pl.ds(j, m, stride=SUBLANES), :] for j in range(SUBLANES)], axis=1)


def _pack_rows(ref, val, pair_s):
    m = val.shape[0]
    for j in range(XROWS):
        pair_s[pl.ds(0, m, stride=2), :] = val[:, LANES * j:LANES * (j + 1)]
        pair_s[pl.ds(1, m, stride=2), :] = val[:, LANES * (j + XROWS):LANES * (j + XROWS + 1)]
        ref[pl.ds(j, m, stride=XROWS), :] = pltpu.bitcast(pair_s[pl.ds(0, 2 * m), :].astype(BF16), jnp.uint32)


def _unpack_rows(ref, m, pair_s):
    lo, hi = [], []
    for j in range(XROWS):
        pair_s[pl.ds(0, 2 * m), :] = pltpu.bitcast(ref[pl.ds(j, m, stride=XROWS), :], BF16).astype(F32)
        lo.append(pair_s[pl.ds(0, m, stride=2), :])
        hi.append(pair_s[pl.ds(1, m, stride=2), :])
    return jnp.concatenate(lo + hi, axis=1)


def _proj0_lru_kernel(x_ref, g_ref, w_ref, cw_ref, cb_ref, wa_ref, wx_ref, ba_ref, bx_ref, lam_ref,
                      qkv_ref, ya_ref, xbuf, h_ref, a_s, u_s):
    ts = x_ref.shape[0]
    w = ya_ref.shape[1]
    bw = qkv_ref.shape[1] // 3

    @pl.when(pl.program_id(1) == 0)
    def _():
        xbuf[pl.ds(0, SUBLANES), :] = jnp.zeros((SUBLANES, w), F32)
        h_ref[...] = jnp.zeros_like(h_ref)

    hn = _rms(x_ref[...], g_ref[...]).astype(BF16)

    def qkv_cols(c, scale=None):
        z = _dot(hn, w_ref[:, 2 * w + c * bw:2 * w + (c + 1) * bw])
        qkv_ref[:, c * bw:(c + 1) * bw] = (z if scale is None else z * scale).astype(BF16)

    xg = _dot(hn, w_ref[:, :2 * w])

    xbuf[pl.ds(SUBLANES, ts), :] = xg[:, :w]
    nk = cw_ref.shape[0]
    y = cb_ref[...]
    for k in range(nk):
        y = y + cw_ref[nk - 1 - k:nk - k, :] * xbuf[pl.ds(SUBLANES - k, ts), :]
    xbuf[pl.ds(0, SUBLANES), :] = xbuf[pl.ds(ts, SUBLANES), :]

    yb = y.astype(BF16)
    rg = jax.nn.sigmoid(_dot(yb, wa_ref[...]) + ba_ref[...])
    ig = jax.nn.sigmoid(_dot(yb, wx_ref[...]) + bx_ref[...])
    qkv_cols(0, CHUNK ** -0.5 * LOG2E)
    lam = lam_ref[...]
    log_sig = jnp.minimum(lam, 0.0) - jnp.log1p(jnp.exp(-jnp.abs(lam)))
    log_a = RG_C * rg * log_sig
    a = jnp.exp(log_a)
    m = 1.0 - a * a
    u = jnp.where(m > 0.0, m * lax.rsqrt(m), 0.0) * (ig * y)
    qkv_cols(1)

    grp = (ts // SUBLANES, SUBLANES, w)
    a = a.reshape(grp)
    u = u.reshape(grp)
    rowm = lax.broadcasted_iota(jnp.int32, grp, 1)
    for s in (1, 2, 4):
        keep = rowm >= s
        a_sh = jnp.where(keep, pltpu.roll(a, s, 1), 1.0)
        u_sh = jnp.where(keep, pltpu.roll(u, s, 1), 0.0)
        u = a * u_sh + u
        a = a * a_sh
    a_s[...] = a.reshape(ts, w)
    u_s[...] = u.reshape(ts, w)
    qkv_cols(2)

    def group(gi, h):
        off = pl.multiple_of(gi * SUBLANES, SUBLANES)
        hg = a_s[pl.ds(off, SUBLANES), :] * h + u_s[pl.ds(off, SUBLANES), :]
        u_s[pl.ds(off, SUBLANES), :] = hg
        return jnp.broadcast_to(hg[SUBLANES - 1:SUBLANES, :], hg.shape)

    h_ref[...] = lax.fori_loop(0, ts // SUBLANES, group, h_ref[...])

    ga = xg[:, w:]
    gelu = 0.5 * ga * (1.0 + jnp.tanh(0.7978845608028654 * (ga + 0.044715 * (ga * ga * ga))))
    ya_ref[...] = (u_s[...] * gelu).astype(BF16)


def _proj0_lru(x, g, w_in, cw, cb, wa, wx, ba, bx, lam):
    batch, s, d = x.shape
    w = cw.shape[1]
    nqkv = w_in.shape[1] - 2 * w
    ts = min(TS_LRU, s)
    small = [g, w_in, cw, cb, wa, wx, ba, bx, lam]
    blk = lambda width: pl.BlockSpec((None, ts, width), lambda b, j: (b, j, 0))
    qkv, ya = pl.pallas_call(
        _proj0_lru_kernel,
        grid=(batch, s // ts),
        in_specs=[blk(d)] + [_const_spec(a.shape) for a in small],
        out_specs=[blk(nqkv), blk(w)],
        out_shape=[jax.ShapeDtypeStruct((batch, s, nqkv), BF16), jax.ShapeDtypeStruct((batch, s, w), BF16)],
        scratch_shapes=[pltpu.VMEM((ts + SUBLANES, w), F32), pltpu.VMEM((SUBLANES, w), F32),
                        pltpu.VMEM((ts, w), F32), pltpu.VMEM((ts, w), F32)],
        compiler_params=_params(2),
        name="proj0_lru",
    )(x, *small)
    return qkv.reshape(batch * s, nqkv), ya.reshape(batch * s, w)


def _attn_bias_kernel(base_ref, o_ref):
    tq, tk = o_ref.shape
    nkb = tk // tq
    row = base_ref[...]
    full = pltpu.roll(jnp.broadcast_to(row, (tq, tq + tk)), 0, 1, stride=1, stride_axis=0)
    qc = lax.broadcasted_iota(jnp.int32, (tq, tk), 0) // CHUNK
    col = lax.broadcasted_iota(jnp.int32, (tq, tk), 1)
    kc = col // CHUNK
    first_ok = (nkb - 1 - pl.program_id(0)) * tq
    valid = (kc >= qc) & (kc <= qc + LEFT_CHUNKS) & (col >= first_ok)
    o_ref[...] = jnp.where(valid, full[:, tq:] * LOG2E, -1e30)


def _attn_bias(base):
    h = base.shape[0]
    tk = TQ + LEFT_CHUNKS * CHUNK
    return pl.pallas_call(
        _attn_bias_kernel,
        grid=(tk // TQ, h),
        in_specs=[pl.BlockSpec((None, 1, TQ + tk), lambda v, i: (i, 0, 0))],
        out_specs=pl.BlockSpec((None, None, TQ, tk), lambda v, i: (v, i, 0, 0)),
        out_shape=jax.ShapeDtypeStruct((tk // TQ, h, TQ, tk), F32),
        compiler_params=_params(2),
        name="attn_bias",
    )(base)


def _attn_kernel(q_ref, k0_ref, k1_ref, k2_ref, v0_ref, v1_ref, v2_ref, bias_ref, o_ref):
    tq = q_ref.shape[0]
    half_w = LANES // 2
    lo = lax.broadcasted_iota(jnp.int32, (tq, LANES), 1) < half_w
    lo_k = lax.broadcasted_iota(jnp.int32, (3 * tq, LANES), 1) < half_w
    k_refs = (k0_ref, k1_ref, k2_ref)
    v_refs = (v0_ref, v1_ref, v2_ref)
    npair = q_ref.shape[1] // LANES
    for j0 in range(0, npair, ATTN_GROUP):
        pairs = range(j0, min(j0 + ATTN_GROUP, npair))
        qk, vh = {}, {}
        for j in pairs:
            sl = slice(LANES * j, LANES * (j + 1))
            q2 = q_ref[:, sl]
            kcat = jnp.concatenate([r[:, sl] for r in k_refs], axis=0)
            vcat = jnp.concatenate([r[:, sl] for r in v_refs], axis=0)
            one = jnp.ones_like(vcat)
            zero = jnp.zeros_like(q2)
            vh[j] = (jnp.where(lo_k, vcat, one), jnp.where(lo_k, one, vcat))
            qk[j] = [_dot_nt(jnp.where(lo if half == 0 else jnp.logical_not(lo), q2, zero), kcat)
                     for half in range(2)]
        ps = {}
        for j in pairs:
            ps[j] = []
            for half in range(2):
                s = qk[j][half] + bias_ref[2 * j + half]
                ps[j].append(jnp.exp2(s - jnp.max(s, axis=-1, keepdims=True)).astype(BF16))
        for j in pairs:
            pv = [_dot(ps[j][half], vh[j][half]) for half in range(2)]
            num = jnp.where(lo, pv[0], pv[1])
            den = pltpu.roll(jnp.where(lo, pv[1], pv[0]), half_w, 1)
            o_ref[:, LANES * j:LANES * (j + 1)] = (num / den).astype(BF16)


def _attention(qkv, bias, batch):
    t, w3 = qkv.shape
    w = w3 // 3
    s = t // batch
    qkv3 = qkv.reshape(batch, s, w3)
    nvar = bias.shape[0]

    def kv_spec(colblk, back):
        return pl.BlockSpec((None, TQ, w), lambda i, b: (b, jnp.maximum(i - back, 0), colblk))

    out = pl.pallas_call(
        _attn_kernel,
        grid=(s // TQ, batch),
        in_specs=[pl.BlockSpec((None, TQ, w), lambda i, b: (b, i, 0)),
                  kv_spec(1, 2), kv_spec(1, 1), kv_spec(1, 0),
                  kv_spec(2, 2), kv_spec(2, 1), kv_spec(2, 0),
                  pl.BlockSpec((None,) + bias.shape[1:], lambda i, b: (jnp.minimum(i, nvar - 1), 0, 0, 0))],
        out_specs=pl.BlockSpec((None, TQ, w), lambda i, b: (b, i, 0)),
        out_shape=jax.ShapeDtypeStruct((batch, s, w), BF16),
        compiler_params=_params(2),
        name="attention",
    )(qkv3, qkv3, qkv3, qkv3, qkv3, qkv3, qkv3, bias)
    return out.reshape(t, w)


def _proj_route_kernel(a_ref, b_ref, r_ref, wa_ref, wb_ref, g_ref, wt_ref,
                       h_ref, hn_ref, meta_ref, metat_ref, cnt_ref, carry_ref, pair_s):
    tm = a_ref.shape[0]
    ts = min(TM_ROUTE_SUB, tm)

    @pl.when(pl.program_id(0) == 0)
    def _():
        carry_ref[...] = jnp.zeros_like(carry_ref)

    wt = wt_ref[...]
    ridx = lax.broadcasted_iota(jnp.int32, (ROUTE_ROWS, ts), 0).astype(F32)
    r8 = lax.broadcasted_iota(jnp.int32, (SUBLANES, ts), 0)
    rr = lax.broadcasted_iota(jnp.int32, (ts, ts), 0)
    cc = lax.broadcasted_iota(jnp.int32, (ts, ts), 1)
    utri = jnp.where(rr < cc, 1.0, 0.0).astype(BF16)
    neg = -jnp.inf
    far = float(LANES)
    carry = carry_ref[...]

    all_logits = []
    for sub in range(tm // ts):
        rows = pl.ds(sub * ts, ts)
        h = r_ref[rows, :] + (_dot(a_ref[rows, :], wa_ref[...]) + _dot(b_ref[rows, :], wb_ref[...]))
        h_ref[rows, :] = h
        hn = _rms(h, g_ref[...])
        _pack_rows(hn_ref.at[pl.ds(sub * ts * XROWS, ts * XROWS), :], hn, pair_s)

        hi, lo = _split_bf16(hn)
        p_hi = _dot_nt(wt, hi)
        p_lo = _dot_nt(wt[:LANES], lo)
        all_logits.append((p_hi[:ROUTE_ROWS] + p_hi[LANES:LANES + ROUTE_ROWS]) + p_lo[:ROUTE_ROWS])

    for sub, logits in enumerate(all_logits):
        rows = pl.ds(sub * ts, ts)

        def first_max(mask, logits=logits):
            v = jnp.max(jnp.where(mask, logits, neg), axis=0, keepdims=True)
            idx = jnp.min(jnp.where(mask & (logits == v), ridx, far), axis=0, keepdims=True)
            return v, idx

        gmask = ridx < N_GROUPS
        gmax, gidx = first_max(gmask)
        g_gate = 1.0 / jnp.sum(jnp.where(gmask, jnp.exp(logits - gmax), 0.0), axis=0, keepdims=True)
        e_lo = N_GROUPS + EXPERTS_PER_GROUP * gidx
        emask = (ridx >= e_lo) & (ridx < e_lo + EXPERTS_PER_GROUP)
        v1, i1 = first_max(emask)
        v2, i2 = first_max(emask & (ridx != i1))
        tt = jnp.exp(v2 - v1)
        w1 = g_gate / (1.0 + tt)
        w2 = g_gate * tt / (1.0 + tt)

        sel1 = ridx == i1
        sel2 = ridx == i2
        onehot = jnp.where(sel1 | sel2, 1.0, 0.0)
        before = _dot(onehot.astype(BF16), utri) + carry[:, 0:1]
        rank1 = jnp.sum(jnp.where(sel1, before, 0.0), axis=0, keepdims=True)
        rank2 = jnp.sum(jnp.where(sel2, before, 0.0), axis=0, keepdims=True)
        carry = carry + jnp.sum(onehot, axis=1, keepdims=True)

        mt = jnp.zeros((SUBLANES, ts), F32)
        for c, val in enumerate((i1 - N_GROUPS, i2 - N_GROUPS, rank1, rank2, w1, w2)):
            mt = jnp.where(r8 == c, val, mt)
        metat_ref[:, rows] = mt
        meta_ref[rows, :] = jnp.concatenate([mt, jnp.zeros((LANES - SUBLANES, ts), F32)], axis=0).T

    carry_ref[...] = carry
    cnt_ref[...] = carry


def _proj_route(a, acol, b, bcol, resid, wa, wb, g, wt):
    t, d = resid.shape
    kw = wa.shape[0]
    tm = min(TM_ROUTE, t)
    return pl.pallas_call(
        _proj_route_kernel,
        grid=(t // tm,),
        in_specs=[pl.BlockSpec((tm, kw), lambda i: (i, acol)), pl.BlockSpec((tm, kw), lambda i: (i, bcol)),
                  pl.BlockSpec((tm, d), lambda i: (i, 0)),
                  _const_spec(wa.shape), _const_spec(wb.shape), _const_spec(g.shape), _const_spec(wt.shape)],
        out_specs=[pl.BlockSpec((tm, d), lambda i: (i, 0)),
                   pl.BlockSpec((tm * XROWS, LANES), lambda i: (i, 0)),
                   pl.BlockSpec((tm, LANES), lambda i: (i, 0)),
                   pl.BlockSpec((SUBLANES, tm), lambda i: (0, i)),
                   pl.BlockSpec((ROUTE_ROWS, LANES), lambda i: (0, 0))],
        out_shape=[jax.ShapeDtypeStruct((t, d), F32),
                   jax.ShapeDtypeStruct((t * XROWS, LANES), jnp.uint32),
                   jax.ShapeDtypeStruct((t, LANES), F32),
                   jax.ShapeDtypeStruct((SUBLANES, t), F32),
                   jax.ShapeDtypeStruct((ROUTE_ROWS, LANES), F32)],
        scratch_shapes=[pltpu.VMEM((ROUTE_ROWS, LANES), F32),
                        pltpu.VMEM((2 * min(TM_ROUTE_SUB, tm), LANES), F32)],
        compiler_params=_params(1),
        name="proj_route",
    )(a, b, resid, wa, wb, g, wt)


def _plan_kernel(cnt_ref, meta_ref, dest_ref, tab_ref, start_s, *, tb):
    r_n = ROUTE_ROWS
    tab_w = tab_ref.shape[1]

    @pl.when(pl.program_id(0) == 0)
    def _():
        row = lax.broadcasted_iota(jnp.int32, (r_n, LANES), 0)
        lane = lax.broadcasted_iota(jnp.int32, (r_n, LANES), 1)

        def prefix(v):
            s = 1
            while s < r_n:
                v = v + jnp.where(row >= s, pltpu.roll(v, s, 0), 0.0)
                s *= 2
            return v

        counts = cnt_ref[...]
        padded = jnp.floor((counts + (tb - 1)) * (1.0 / tb)) * tb
        pad_end = prefix(padded)
        pad_start = pad_end - padded
        pad_lo = pad_start + counts
        start_s[...] = pad_start
        used = padded > 0.0
        order = prefix(jnp.where(used, 1.0, 0.0)) - 1.0
        nused = order[r_n - 1:r_n, :] + 1.0
        nact = pad_end[r_n - 1:r_n, :] * (1.0 / tb)
        expert = (row - N_GROUPS).astype(F32)
        lane_f = lane.astype(F32)
        used_seq = jnp.sum(jnp.where(used & (order == jnp.minimum(lane_f, nused - 1.0)), expert, 0.0),
                           axis=0, keepdims=True)

        erow = lax.broadcasted_iota(jnp.int32, (r_n, tab_w), 0)
        blk_start = lax.broadcasted_iota(jnp.int32, (r_n, tab_w), 1).astype(F32) * tb
        is_expert = (erow >= N_GROUPS) & (erow < N_GROUPS + N_EXPERTS)
        ended = jnp.where(is_expert & (pad_end[:, 0:1] <= blk_start), 1.0, 0.0)
        blk_expert = jnp.minimum(jnp.sum(ended, axis=0, keepdims=True), N_EXPERTS - 1.0)
        mine = (erow - N_GROUPS).astype(F32) == blk_expert
        blk_order = jnp.sum(jnp.where(mine, order[:, 0:1], 0.0), axis=0, keepdims=True)
        blk_lo = jnp.sum(jnp.where(mine, pad_lo[:, 0:1], 0.0), axis=0, keepdims=True)
        blk_fill = jnp.clip(blk_lo - blk_start[0:1, :], 0.0, float(tb))

        cols = jnp.where(lane == 0, pad_lo, jnp.where(lane == 1, pad_end, 0.0))
        rows_t = jnp.concatenate([cols, jnp.zeros((LANES - r_n, LANES), F32)], axis=0).T
        misc = jnp.where(lane[0:1, :] == 0, nact, jnp.where(lane[0:1, :] == 1, nused, 0.0))
        wide = lambda v: v if tab_w == LANES else jnp.concatenate(
            [v, jnp.zeros((v.shape[0], tab_w - LANES), F32)], axis=1)
        tab = jnp.concatenate([wide(rows_t[0:2, :]), wide(used_seq), wide(misc), blk_order, blk_fill,
                               jnp.zeros((2, tab_w), F32)], axis=0)
        tab_ref[...] = tab.astype(jnp.int32)

    for c in range(dest_ref.shape[1] // LANES):
        sl = slice(LANES * c, LANES * (c + 1))
        for k in range(TOP_K):
            expert = meta_ref[k:k + 1, sl]
            start = jnp.zeros((1, LANES), F32)
            for e in range(N_EXPERTS):
                start = jnp.where(expert == float(e), start_s[N_GROUPS + e:N_GROUPS + e + 1, :], start)
            dest_ref[k:k + 1, sl] = (start + meta_ref[TOP_K + k:TOP_K + k + 1, sl]).astype(jnp.int32)


def _plan(meta_t, cnt, tb):
    t = meta_t.shape[1]
    nblk = -(-(t * TOP_K + N_EXPERTS * (tb - 1)) // tb)
    tile = min(PLAN_TILE, t)
    tab_w = -(-nblk // LANES) * LANES
    dest, tab = pl.pallas_call(
        functools.partial(_plan_kernel, tb=tb),
        grid=(t // tile,),
        in_specs=[_const_spec(cnt.shape), pl.BlockSpec((SUBLANES, tile), lambda i: (0, i))],
        out_specs=[pl.BlockSpec((TOP_K, tile), lambda i: (0, i)), pl.BlockSpec((SUBLANES, tab_w), lambda i: (0, 0))],
        out_shape=[jax.ShapeDtypeStruct((TOP_K, t), jnp.int32), jax.ShapeDtypeStruct((SUBLANES, tab_w), jnp.int32)],
        scratch_shapes=[pltpu.VMEM((ROUTE_ROWS, LANES), F32)],
        compiler_params=_params(1),
        name="plan",
    )(cnt, meta_t)
    experts = slice(N_GROUPS, N_GROUPS + N_EXPERTS)
    return (dest.reshape(-1), tab[0, experts], tab[1, experts], tab[4, :nblk], tab[5, :nblk],
            tab[2, :N_EXPERTS], tab[3, :2], nblk)


def _dispatch_kernel(dest_ref, plo_ref, phi_ref, nact_ref, hn_ref, xs_ref, zero_ref, sem, zsem):
    tm = hn_ref.shape[0] // XROWS
    t = dest_ref.shape[0] // TOP_K
    tb = zero_ref.shape[0] // XROWS
    nblk = xs_ref.shape[0] // zero_ref.shape[0]
    i = pl.program_id(0)

    def zero_copy(row, nrows):
        return pltpu.make_async_copy(
            zero_ref.at[pl.ds(0, nrows * XROWS), :],
            xs_ref.at[pl.ds(pl.multiple_of(row * XROWS, XROWS), nrows * XROWS), :], zsem)

    def zero_fill(start):
        def go(cp):
            cp.start() if start else cp.wait()

        def per_expert(e, c):
            off = plo_ref[e]
            n = phi_ref[e] - off
            bit = tb // 2
            while bit:
                pl.when((n & bit) != 0)(functools.partial(lambda o, b: go(zero_copy(o, b)), off, bit))
                off = off + (n & bit)
                bit //= 2
            return c

        def per_block(b, c):
            go(zero_copy(b * tb, tb))
            return c

        lax.fori_loop(0, N_EXPERTS, per_expert, 0)
        lax.fori_loop(nact_ref[0], nblk, per_block, 0)

    @pl.when(i == 0)
    def _():
        zero_ref[...] = jnp.zeros_like(zero_ref)
        zero_fill(True)

    def copy(r, k):
        d = dest_ref[k * t + i * tm + r]
        return pltpu.make_async_copy(
            hn_ref.at[pl.ds(pl.multiple_of(r * XROWS, XROWS), XROWS), :],
            xs_ref.at[pl.ds(pl.multiple_of(d * XROWS, XROWS), XROWS), :], sem)

    def start(r, c):
        for k in range(TOP_K):
            copy(r, k).start(priority=k)
        return c

    lax.fori_loop(0, tm, start, 0, unroll=DMA_UNROLL)
    for k in range(TOP_K):
        pltpu.make_async_copy(hn_ref, xs_ref.at[pl.ds(0, tm * XROWS), :], sem).wait()

    @pl.when(i == 0)
    def _():
        zero_fill(False)


def _dispatch(dest, pad_lo, pad_hi, nact, hn_slab, nblk, tb):
    t = hn_slab.shape[0] // XROWS
    tm = min(TM_COMB, t)
    return pl.pallas_call(
        _dispatch_kernel,
        grid_spec=pltpu.PrefetchScalarGridSpec(
            num_scalar_prefetch=4,
            grid=(t // tm,),
            in_specs=[pl.BlockSpec((tm * XROWS, LANES), lambda i, *_: (i, 0))],
            out_specs=pl.BlockSpec(memory_space=pl.ANY),
            scratch_shapes=[pltpu.VMEM((tb * XROWS, LANES), jnp.uint32),
                            pltpu.SemaphoreType.DMA(()), pltpu.SemaphoreType.DMA(())]),
        out_shape=jax.ShapeDtypeStruct((nblk * tb * XROWS, LANES), jnp.uint32),
        compiler_params=_params(1),
        name="dispatch",
    )(dest, pad_lo, pad_hi, nact, hn_slab)


def _expert_kernel(ord_ref, fill_ref, useq_ref, nact_ref, xs_ref, wg_ref, wu_ref, wd_ref, ys_ref,
                   xbuf, xsem, wg_f, wu_f, wd_f, wsem, wg_s, wu_s, wd_s, pair_s, *, layer):
    i = pl.program_id(0)
    n = pl.num_programs(0)
    tb = ys_ref.shape[0] // SUBLANES
    rows = tb * XROWS
    active = i < nact_ref[0]
    order = ord_ref[i]
    changed = (i == 0) | (order != ord_ref[jnp.maximum(i - 1, 0)])
    slot = i % X_SLOTS
    ahead = X_SLOTS - 1

    def w_copies(j):
        e = useq_ref[jnp.minimum(j, useq_ref.shape[0] - 1)]
        ws = j % 2
        return [pltpu.make_async_copy(src.at[layer, e], dst.at[ws], wsem.at[ws])
                for src, dst in ((wg_ref, wg_f), (wu_ref, wu_f), (wd_ref, wd_f))]

    @pl.when(i == 0)
    def _():
        for cp in w_copies(0):
            cp.start()

    def x_copy(step, sl):
        blk = jnp.minimum(step, nact_ref[0] - 1)
        return pltpu.make_async_copy(xs_ref.at[pl.ds(pl.multiple_of(blk * rows, rows), rows), :],
                                     xbuf.at[sl], xsem.at[sl])

    @pl.when(i == 0)
    def _():
        for a in range(ahead):
            x_copy(a, a).start()

    x_copy(i, slot).wait()
    x_copy(i + ahead, (i + ahead) % X_SLOTS).start()

    @pl.when(i == n - 1)
    def _():
        for a in range(1, X_SLOTS):
            x_copy(i + a, (i + a) % X_SLOTS).wait()

    @pl.when(active & changed)
    def _():
        for cp in w_copies(order):
            cp.wait()
        for cp in w_copies(order + 1):
            cp.start()
        ws = order % 2
        wg_s[...] = wg_f[ws].astype(BF16)
        wu_s[...] = wu_f[ws].astype(BF16)
        wd_s[...] = wd_f[ws].astype(BF16)

    @pl.when(i == n - 1)
    def _():
        for cp in w_copies(nact_ref[1]):
            cp.wait()

    def mlp(nrows):
        x = _unpack_rows(xbuf.at[slot, pl.ds(0, nrows * XROWS), :], nrows, pair_s).astype(BF16)
        gate = _dot(x, wg_s[...])
        up = _dot(x, wu_s[...])
        hid = (gate * jax.nn.sigmoid(gate) * up).astype(BF16)
        _to_slab(ys_ref.at[pl.ds(0, nrows * SUBLANES), :], _dot(hid, wd_s[...]))

    half = tb // 2
    upper_empty = fill_ref[i] <= half

    @pl.when(active & jnp.logical_not(upper_empty))
    def _():
        mlp(tb)

    @pl.when(active & upper_empty)
    def _():
        mlp(half)
        ys_ref[pl.ds(half * SUBLANES, half * SUBLANES), :] = jnp.zeros((half * SUBLANES, LANES), F32)

    @pl.when(jnp.logical_not(active))
    def _():
        ys_ref[...] = jnp.zeros_like(ys_ref)


def _experts(blk_order, blk_fill, used_seq, nact, xs, wg, wu, wd, layer, tb):
    nblk = blk_order.shape[0]
    _, _, d, de = wg.shape
    anyspec = pl.BlockSpec(memory_space=pl.ANY)
    return pl.pallas_call(
        functools.partial(_expert_kernel, layer=layer),
        grid_spec=pltpu.PrefetchScalarGridSpec(
            num_scalar_prefetch=4,
            grid=(nblk,),
            in_specs=[anyspec, anyspec, anyspec, anyspec],
            out_specs=pl.BlockSpec((tb * SUBLANES, LANES), lambda i, *_: (i, 0)),
            scratch_shapes=[pltpu.VMEM((X_SLOTS, tb * XROWS, LANES), jnp.uint32),
                            pltpu.SemaphoreType.DMA((X_SLOTS,)),
                            pltpu.VMEM((2, d, de), F32), pltpu.VMEM((2, d, de), F32), pltpu.VMEM((2, de, d), F32),
                            pltpu.SemaphoreType.DMA((2,)),
                            pltpu.VMEM((d, de), BF16), pltpu.VMEM((d, de), BF16), pltpu.VMEM((de, d), BF16),
                            pltpu.VMEM((2 * tb, LANES), F32)]),
        out_shape=jax.ShapeDtypeStruct((nblk * tb * SUBLANES, LANES), F32),
        compiler_params=_params(1),
        name="experts",
    )(blk_order, blk_fill, used_seq, nact, xs, wg, wu, wd)


def _combine(dest_ref, ys_ref, meta_ref, buf_ref, sem_ref, tm):
    i = pl.program_id(0)
    n = pl.num_programs(0)
    slot = i % COMB_SLOTS
    ahead = COMB_SLOTS - 1
    t = dest_ref.shape[0] // TOP_K

    def copy(tile, sl, r, k):
        d = dest_ref[k * t + tile * tm + r]
        row = r * SUBLANES if isinstance(r, int) else pl.multiple_of(r * SUBLANES, SUBLANES)
        return pltpu.make_async_copy(
            ys_ref.at[pl.ds(pl.multiple_of(d * SUBLANES, SUBLANES), SUBLANES), :],
            buf_ref.at[sl, k, pl.ds(row, SUBLANES), :],
            sem_ref.at[sl])

    def start_tile(tile, sl):
        def body(r, c):
            for k in range(TOP_K):
                copy(tile, sl, r, k).start(priority=k)
            return c
        lax.fori_loop(0, tm, body, 0, unroll=DMA_UNROLL)

    def wait_tile(sl):
        for k in range(TOP_K):
            pltpu.make_async_copy(ys_ref.at[pl.ds(0, tm * SUBLANES), :], buf_ref.at[sl, k], sem_ref.at[sl]).wait()

    @pl.when(i == 0)
    def _():
        for a in range(ahead):
            start_tile(jnp.minimum(a, n - 1), a)

    wait_tile(slot)

    nxt = jnp.minimum(i + ahead, n - 1)
    nslot = (i + ahead) % COMB_SLOTS

    def issue(part, nparts):
        rows = tm // nparts
        for r in range(part * rows, (part + 1) * rows):
            for k in range(TOP_K):
                copy(nxt, nslot, r, k).start(priority=k)

    @pl.when(i == n - 1)
    def _():
        for a in range(1, ahead):
            wait_tile((i + a) % COMB_SLOTS)

    y = None
    for k in range(TOP_K):
        gate = meta_ref[:, 2 * TOP_K + k:2 * TOP_K + k + 1]
        term = _from_slab(buf_ref.at[slot, k], tm) * gate
        y = term if y is None else y + term
    return y, issue


def _combine_drain(buf_ref, sem_ref, ys_ref, tm):
    i = pl.program_id(0)

    @pl.when(i == pl.num_programs(0) - 1)
    def _():
        sl = (i + COMB_SLOTS - 1) % COMB_SLOTS
        for k in range(TOP_K):
            pltpu.make_async_copy(ys_ref.at[pl.ds(0, tm * SUBLANES), :], buf_ref.at[sl, k], sem_ref.at[sl]).wait()


def _combine_scratch(tm):
    return [pltpu.VMEM((COMB_SLOTS, TOP_K, tm * SUBLANES, LANES), F32), pltpu.SemaphoreType.DMA((COMB_SLOTS,))]


def _comb_proj1_kernel(dest_ref, h_ref, meta_ref, ys_ref, g_ref, w_ref, lbl_ref,
                       h2_ref, qs_ref, lf_ref, iv_ref, gs_ref, buf_ref, sem_ref, *, lb_rows):
    tm = h_ref.shape[0]
    y, issue = _combine(dest_ref, ys_ref, meta_ref, buf_ref, sem_ref, tm)
    h2 = h_ref[...] + y
    h2_ref[...] = h2
    hn = _rms(h2, g_ref[...]).astype(BF16)
    d = h2.shape[1]
    lbl = lbl_ref[...]
    ex = jnp.exp(lbl - jnp.max(lbl, axis=0, keepdims=True))
    lb = jnp.sum(ex[:lb_rows], axis=0, keepdims=True) / jnp.sum(ex, axis=0, keepdims=True)

    def col(c):
        return _dot(hn, w_ref[:, d * c:d * (c + 1)])

    q = col(0)
    issue(0, 4)
    fz = col(1)
    issue(1, 4)
    iv = col(2)
    issue(2, 4)
    gz = col(3)
    issue(3, 4)
    qs_ref[...] = (q * jax.nn.sigmoid(q)).astype(BF16)
    lf_ref[...] = jnp.log(lb + (1.0 - lb) * jax.nn.sigmoid(fz)) * LOG2E
    iv_ref[...] = iv.astype(BF16)
    gs_ref[...] = (gz * jax.nn.sigmoid(gz)).astype(BF16)
    _combine_drain(buf_ref, sem_ref, ys_ref, tm)


def _comb_proj1(dest, h, meta, ys, g, w, lb_logits, lb_rows):
    t, d = h.shape
    tm = min(TM_COMB, t)
    row = lambda i, dd: (i, 0)
    const = lambda shape: pl.BlockSpec(shape, lambda i, dd: (0,) * len(shape))
    return pl.pallas_call(
        functools.partial(_comb_proj1_kernel, lb_rows=lb_rows),
        grid_spec=pltpu.PrefetchScalarGridSpec(
            num_scalar_prefetch=1,
            grid=(t // tm,),
            in_specs=[pl.BlockSpec((tm, d), row), pl.BlockSpec((tm, LANES), row),
                      pl.BlockSpec(memory_space=pl.ANY),
                      const(g.shape), const(w.shape), const(lb_logits.shape)],
            out_specs=[pl.BlockSpec((tm, d), row)] * 5,
            scratch_shapes=_combine_scratch(tm)),
        out_shape=[jax.ShapeDtypeStruct((t, d), F32), jax.ShapeDtypeStruct((t, d), BF16),
                   jax.ShapeDtypeStruct((t, d), F32), jax.ShapeDtypeStruct((t, d), BF16),
                   jax.ShapeDtypeStruct((t, d), BF16)],
        compiler_params=_params(1),
        name="comb_proj1",
    )(dest, h, meta, ys, g, w, lb_logits)


def _comb_final_kernel(dest_ref, h_ref, meta_ref, ys_ref, g_ref, o_ref, buf_ref, sem_ref):
    tm = h_ref.shape[0]
    y, issue = _combine(dest_ref, ys_ref, meta_ref, buf_ref, sem_ref, tm)
    issue(0, 1)
    o_ref[...] = _rms(h_ref[...] + y, g_ref[...])
    _combine_drain(buf_ref, sem_ref, ys_ref, tm)


def _comb_final(dest, h, meta, ys, g):
    t, d = h.shape
    tm = min(TM_COMB, t)
    row = lambda i, dd: (i, 0)
    return pl.pallas_call(
        _comb_final_kernel,
        grid_spec=pltpu.PrefetchScalarGridSpec(
            num_scalar_prefetch=1,
            grid=(t // tm,),
            in_specs=[pl.BlockSpec((tm, d), row), pl.BlockSpec((tm, LANES), row),
                      pl.BlockSpec(memory_space=pl.ANY),
                      pl.BlockSpec(g.shape, lambda i, dd: (0, 0))],
            out_specs=pl.BlockSpec((tm, d), row),
            scratch_shapes=_combine_scratch(tm)),
        out_shape=jax.ShapeDtypeStruct((t, d), F32),
        compiler_params=_params(1),
        name="comb_final",
    )(dest, h, meta, ys, g)


def _hgrn_kernel(qs_ref, lf_ref, iv_ref, gs_ref, ng_ref, o_ref, st_ref, d1_s, t16_s, *, dk):
    th = qs_ref.shape[0]
    nsub = CHUNK // SUB

    @pl.when(pl.program_id(1) == 0)
    def _():
        st_ref[...] = jnp.zeros_like(st_ref)

    rr = lax.broadcasted_iota(jnp.int32, (th, th), 0)
    cc = lax.broadcasted_iota(jnp.int32, (th, th), 1)
    same_sub = (rr // SUB) == (cc // SUB)
    same_chunk = (rr // CHUNK) == (cc // CHUNK)
    m_diag = same_sub & (cc <= rr)
    dsub = rr // SUB - cc // SUB
    m_off = [same_chunk & (dsub == dd) for dd in range(1, nsub)]

    hi, lo = _split_bf16(lf_ref[...])
    tri = jnp.where(m_diag, 1.0, 0.0).astype(BF16)
    d1 = _dot(tri, hi) + _dot(tri, lo)
    d1_s[...] = d1
    sub3 = (th // SUB, SUB, d1.shape[1])
    t16_s[...] = jnp.broadcast_to(d1.reshape(sub3)[:, SUB - 1:SUB, :], sub3).reshape(th, d1.shape[1])
    rrow = lax.broadcasted_iota(jnp.int32, (th, dk), 0)
    rsub = (rrow // SUB) % nsub
    rchunk = rrow // CHUNK
    nch = th // CHUNK

    def back_rows(dd):
        return [slice(CHUNK * c + SUB * dd, CHUNK * (c + 1)) for c in range(nch)]

    def prepare(hd):
        ls = pl.ds(hd * dk, dk)
        d1 = d1_s[:, ls]
        t16 = t16_s[:, ls]
        q = qs_ref[:, ls].astype(F32)
        kt = 1.0 - jnp.exp2(lf_ref[:, ls])
        suf = t16 - d1
        kx = (kt * jnp.exp2(suf)).astype(BF16)
        qv = [q * jnp.exp2(-suf)]
        acc = d1
        tail = suf
        for dd in range(1, nsub):
            qv.append(jnp.concatenate([q[sl] * jnp.exp2(acc[sl]) for sl in back_rows(dd)], axis=0))
            acc = acc + jnp.where(rsub >= dd, pltpu.roll(t16, SUB * dd, 0), 0.0)
            tail = tail + jnp.where(rsub < nsub - dd, pltpu.roll(t16, th - SUB * dd, 0), 0.0)
        b = acc
        qb = q * jnp.exp2(b)
        kend = kt * jnp.exp2(tail)
        zero = jnp.zeros_like(q)
        kend_x = jnp.concatenate([jnp.where(rchunk == c, kend, zero) for c in range(nch)], axis=1).astype(BF16)
        qb_x = jnp.concatenate([jnp.where(rchunk == c, qb, zero) for c in range(nch)], axis=1).astype(BF16)
        decs = [jnp.exp2(b[CHUNK * (c + 1) - 1:CHUNK * (c + 1), :]) for c in range(nch)]
        return ls, jnp.concatenate(qv, axis=0).astype(BF16), kx, kend_x, qb_x, decs

    def group(gi, carry):
        heads = [gi * HGRN_GROUP + u for u in range(HGRN_GROUP)]
        prep = [prepare(hd) for hd in heads]
        a4s = [_dot_nt(p[1], p[2]) for p in prep]
        incs = [_dot_tn(iv_ref[:, p[0]], p[3]) for p in prep]
        o_intras = []
        for p, a4 in zip(prep, a4s):
            att = jnp.where(m_diag, a4[:th], 0.0)
            at = th
            for dd in range(1, nsub):
                pieces = []
                for sl in back_rows(dd):
                    n = sl.stop - sl.start
                    pieces += [jnp.zeros((SUB * dd, th), F32), a4[at:at + n]]
                    at += n
                att = jnp.where(m_off[dd - 1], jnp.concatenate(pieces, axis=0), att)
            o_intras.append(_dot(att.astype(BF16), iv_ref[:, p[0]]))
        for hd, p, inc, o_intra in zip(heads, prep, incs, o_intras):
            ls, decs = p[0], p[5]
            st = st_ref[hd]
            starts = []
            for c in range(nch):
                starts.append(st)
                st = st * decs[c] + inc[:, dk * c:dk * (c + 1)]
            st_ref[hd] = st
            oh = o_intra + _dot_nt(p[4], jnp.concatenate(starts, axis=1).astype(BF16))
            on = oh * lax.rsqrt(jnp.mean(oh * oh, axis=-1, keepdims=True) + EPS)
            o_ref[:, ls] = (on * ng_ref[:, ls] * gs_ref[:, ls].astype(F32)).astype(BF16)
        return carry

    for gi in range(qs_ref.shape[1] // dk // HGRN_GROUP):
        group(gi, 0)


def _hgrn(qs, lf, iv, gs, ng, batch, dk):
    t, d = qs.shape
    s = t // batch
    th = min(TH, s)
    blk = pl.BlockSpec((None, th, d), lambda b, j: (b, j, 0))
    r3 = lambda a: a.reshape(batch, s, d)
    out = pl.pallas_call(
        functools.partial(_hgrn_kernel, dk=dk),
        grid=(batch, s // th),
        in_specs=[blk, blk, blk, blk, _const_spec(ng.shape)],
        out_specs=blk,
        out_shape=jax.ShapeDtypeStruct((batch, s, d), BF16),
        scratch_shapes=[pltpu.VMEM((d // dk, dk, dk), F32), pltpu.VMEM((th, d), F32), pltpu.VMEM((th, d), F32)],
        compiler_params=_params(2),
        name="hgrn2",
    )(r3(qs), r3(lf), r3(iv), r3(gs), ng)
    return out.reshape(t, d)


def _block_diag(w):
    n, c, dd = w.shape
    eye = jnp.eye(n, dtype=w.dtype)
    return (eye[:, None, :, None] * w[:, :, None, :]).reshape(n * c, n * dd)


def _router_weights(wg, we):
    d = wg.shape[0]
    w = jnp.concatenate([wg, we, jnp.zeros((d, LANES - wg.shape[1] - we.shape[1]), F32)], axis=1).T
    hi = w.astype(BF16)
    return jnp.concatenate([hi, (w - hi.astype(F32)).astype(BF16)], axis=0)


def _moe(layer, meta_t, cnt, hn_slab, moe_w_gate, moe_w_up, moe_w_down):
    dest, pad_lo, pad_hi, blk_order, blk_fill, used_seq, nact, nblk = _plan(meta_t, cnt, TB_EXPERT)
    xs = _dispatch(dest, pad_lo, pad_hi, nact, hn_slab, nblk, TB_EXPERT)
    ys = _experts(blk_order, blk_fill, used_seq, nact, xs, moe_w_gate, moe_w_up, moe_w_down, layer, TB_EXPERT)
    return dest, ys


def kernel(x, norm_mix_g, norm_ffn_g, norm_final_g, ab_w_in, ab_conv_w, ab_conv_b, rg_w_a, rg_b_a, rg_w_x, rg_b_x, rg_lambda, attn_rel_bias, ab_w_out, c_w_in, c_lb_logits, c_norm_g, c_w_out, moe_router_group, moe_router_expert, moe_w_gate, moe_w_up, moe_w_down):
    batch, seq, d = x.shape
    t = batch * seq
    xt = x.reshape(t, d)
    row = lambda v: v.reshape(1, -1)

    aw = ab_conv_w.shape[2]
    qkv, ya = _proj0_lru(x, row(norm_mix_g[0]), ab_w_in[0].astype(BF16), ab_conv_w[0], row(ab_conv_b[0]),
                         _block_diag(rg_w_a[0]).astype(BF16), _block_diag(rg_w_x[0]).astype(BF16),
                         row(rg_b_a[0]), row(rg_b_x[0]), row(rg_lambda[0]))
    tk = TQ + LEFT_CHUNKS * CHUNK
    dist = jnp.clip(tk - jnp.arange(TQ + tk), -REL_CLIP, REL_CLIP) + REL_CLIP
    bias = _attn_bias(attn_rel_bias[0][:, None, dist])
    yb = _attention(qkv, bias, batch)
    wo = ab_w_out[0].astype(BF16)
    wt = _router_weights(moe_router_group[0], moe_router_expert[0])
    h1, hn1, meta1, metat1, cnt1 = _proj_route(ya, 0, yb, 0, xt, wo[:aw], wo[aw:], row(norm_ffn_g[0]), wt)
    dest1, ys1 = _moe(0, metat1, cnt1, hn1, moe_w_gate, moe_w_up, moe_w_down)

    dk = c_norm_g.shape[1] // 8
    h2, qs, lf, iv, gs = _comb_proj1(dest1, h1, meta1, ys1, row(norm_mix_g[1]), c_w_in[0].astype(BF16),
                                     c_lb_logits, 1)
    om = _hgrn(qs, lf, iv, gs, row(c_norm_g[0]), batch, dk)
    wo = c_w_out[0].astype(BF16)
    half = wo.shape[0] // 2
    wt = _router_weights(moe_router_group[1], moe_router_expert[1])
    h3, hn3, meta3, metat3, cnt3 = _proj_route(om, 0, om, 1, h2, wo[:half], wo[half:], row(norm_ffn_g[1]), wt)
    dest3, ys3 = _moe(1, metat3, cnt3, hn3, moe_w_gate, moe_w_up, moe_w_down)

    out = _comb_final(dest3, h3, meta3, ys3, row(norm_final_g))
    return out.reshape(batch, seq, d)
```

```python
import functools

import jax
import jax.numpy as jnp
from jax import lax
from jax.experimental import pallas as pl
from jax.experimental.pallas import tpu as pltpu

F32 = jnp.float32
BF16 = jnp.bfloat16

EPS = 1e-6
LOG2E = 1.4426950408889634
RG_C = 8.0
CHUNK = 64
LEFT_CHUNKS = 8
REL_CLIP = 256
N_GROUPS = 4
EXPERTS_PER_GROUP = 8
N_EXPERTS = N_GROUPS * EXPERTS_PER_GROUP
TOP_K = 2
ROUTE_ROWS = 40

LANES = 128
SUBLANES = 8
XROWS = 4
SUB = 16

TS_LRU = 512
TQ = 256
ATTN_GROUP = 4
TM_ROUTE = 1024
TM_ROUTE_SUB = 256
TB_EXPERT = 512
X_SLOTS = 3
TM_COMB = 256
PLAN_TILE = 2048
TH = 256
HGRN_GROUP = 4
DMA_UNROLL = 8
COMB_SLOTS = 3
VMEM_MB = 48


def _params(n_axes, vmem_mb=VMEM_MB):
    return pltpu.CompilerParams(dimension_semantics=("arbitrary",) * n_axes,
                                vmem_limit_bytes=vmem_mb * 1024 * 1024)


def _const_spec(shape):
    nd = len(shape)
    return pl.BlockSpec(shape, lambda *_: (0,) * nd)


def _rms(x, g):
    return x * lax.rsqrt(jnp.mean(x * x, axis=-1, keepdims=True) + EPS) * g


def _dot(a, b):
    return jnp.dot(a, b, preferred_element_type=F32)


def _dot_nt(a, b):
    return lax.dot_general(a, b, (((1,), (1,)), ((), ())), preferred_element_type=F32)


def _dot_tn(a, b):
    return lax.dot_general(a, b, (((0,), (0,)), ((), ())), preferred_element_type=F32)


def _split_bf16(x):
    hi = x.astype(BF16)
    lo = (x - hi.astype(F32)).astype(BF16)
    return hi, lo


def _to_slab(ref, val):
    m = val.shape[0]
    for j in range(val.shape[1] // LANES):
        ref[pl.ds(j, m, stride=SUBLANES), :] = val[:, LANES * j:LANES * (j + 1)]


def _from_slab(ref, m):
    return jnp.concatenate([ref[pl.ds(j, m, stride=SUBLANES), :] for j in range(SUBLANES)], axis=1)


def _pack_rows(ref, val, pair_s):
    m = val.shape[0]
    for j in range(XROWS):
        pair_s[j, pl.ds(0, m, stride=2), :] = val[:, LANES * j:LANES * (j + 1)]
        pair_s[j, pl.ds(1, m, stride=2), :] = val[:, LANES * (j + XROWS):LANES * (j + XROWS + 1)]
        ref[pl.ds(j, m, stride=XROWS), :] = pltpu.bitcast(pair_s[j, pl.ds(0, 2 * m), :].astype(BF16), jnp.uint32)


def _unpack_rows(ref, m, pair_s):
    lo, hi = [], []
    for j in range(XROWS):
        pair_s[j, pl.ds(0, 2 * m), :] = pltpu.bitcast(ref[pl.ds(j, m, stride=XROWS), :], BF16).astype(F32)
        lo.append(pair_s[j, pl.ds(0, m, stride=2), :])
        hi.append(pair_s[j, pl.ds(1, m, stride=2), :])
    return jnp.concatenate(lo + hi, axis=1)


def _proj0_lru_kernel(x_ref, g_ref, w_ref, cw_ref, cb_ref, wa_ref, wx_ref, ba_ref, bx_ref, lam_ref,
                      qkv_ref, ya_ref, xbuf, h_ref, a_s, u_s):
    ts = x_ref.shape[0]
    w = ya_ref.shape[1]
    bw = qkv_ref.shape[1] // 3

    @pl.when(pl.program_id(1) == 0)
    def _():
        xbuf[pl.ds(0, SUBLANES), :] = jnp.zeros((SUBLANES, w), F32)
        h_ref[...] = jnp.zeros_like(h_ref)

    hn = _rms(x_ref[...], g_ref[...]).astype(BF16)

    def qkv_cols(c, scale=None):
        z = _dot(hn, w_ref[:, 2 * w + c * bw:2 * w + (c + 1) * bw])
        qkv_ref[:, c * bw:(c + 1) * bw] = (z if scale is None else z * scale).astype(BF16)

    xg = _dot(hn, w_ref[:, :2 * w])

    xbuf[pl.ds(SUBLANES, ts), :] = xg[:, :w]
    nk = cw_ref.shape[0]
    y = cb_ref[...]
    for k in range(nk):
        y = y + cw_ref[nk - 1 - k:nk - k, :] * xbuf[pl.ds(SUBLANES - k, ts), :]
    xbuf[pl.ds(0, SUBLANES), :] = xbuf[pl.ds(ts, SUBLANES), :]

    yb = y.astype(BF16)
    rg = jax.nn.sigmoid(_dot(yb, wa_ref[...]) + ba_ref[...])
    ig = jax.nn.sigmoid(_dot(yb, wx_ref[...]) + bx_ref[...])
    qkv_cols(0, CHUNK ** -0.5 * LOG2E)
    lam = lam_ref[...]
    log_sig = jnp.minimum(lam, 0.0) - jnp.log1p(jnp.exp(-jnp.abs(lam)))
    log_a = RG_C * rg * log_sig
    a = jnp.exp(log_a)
    m = 1.0 - a * a
    u = jnp.where(m > 0.0, m * lax.rsqrt(m), 0.0) * (ig * y)
    qkv_cols(1)

    grp = (ts // SUBLANES, SUBLANES, w)
    a = a.reshape(grp)
    u = u.reshape(grp)
    rowm = lax.broadcasted_iota(jnp.int32, grp, 1)
    for s in (1, 2, 4):
        keep = rowm >= s
        a_sh = jnp.where(keep, pltpu.roll(a, s, 1), 1.0)
        u_sh = jnp.where(keep, pltpu.roll(u, s, 1), 0.0)
        u = a * u_sh + u
        a = a * a_sh
    a_s[...] = a.reshape(ts, w)
    u_s[...] = u.reshape(ts, w)
    qkv_cols(2)

    def group(gi, h):
        off = pl.multiple_of(gi * SUBLANES, SUBLANES)
        hg = a_s[pl.ds(off, SUBLANES), :] * h + u_s[pl.ds(off, SUBLANES), :]
        u_s[pl.ds(off, SUBLANES), :] = hg
        return jnp.broadcast_to(hg[SUBLANES - 1:SUBLANES, :], hg.shape)

    h_ref[...] = lax.fori_loop(0, ts // SUBLANES, group, h_ref[...])

    ga = xg[:, w:]
    gelu = 0.5 * ga * (1.0 + jnp.tanh(0.7978845608028654 * (ga + 0.044715 * (ga * ga * ga))))
    ya_ref[...] = (u_s[...] * gelu).astype(BF16)


def _proj0_lru(x, g, w_in, cw, cb, wa, wx, ba, bx, lam):
    batch, s, d = x.shape
    w = cw.shape[1]
    nqkv = w_in.shape[1] - 2 * w
    ts = min(TS_LRU, s)
    small = [g, w_in, cw, cb, wa, wx, ba, bx, lam]
    blk = lambda width: pl.BlockSpec((None, ts, width), lambda b, j: (b, j, 0))
    qkv, ya = pl.pallas_call(
        _proj0_lru_kernel,
        grid=(batch, s // ts),
        in_specs=[blk(d)] + [_const_spec(a.shape) for a in small],
        out_specs=[blk(nqkv), blk(w)],
        out_shape=[jax.ShapeDtypeStruct((batch, s, nqkv), BF16), jax.ShapeDtypeStruct((batch, s, w), BF16)],
        scratch_shapes=[pltpu.VMEM((ts + SUBLANES, w), F32), pltpu.VMEM((SUBLANES, w), F32),
                        pltpu.VMEM((ts, w), F32), pltpu.VMEM((ts, w), F32)],
        compiler_params=_params(2),
        name="proj0_lru",
    )(x, *small)
    return qkv.reshape(batch * s, nqkv), ya.reshape(batch * s, w)


def _attn_bias_kernel(base_ref, o_ref):
    tq, tk = o_ref.shape
    nkb = tk // tq
    row = base_ref[...]
    full = pltpu.roll(jnp.broadcast_to(row, (tq, tq + tk)), 0, 1, stride=1, stride_axis=0)
    qc = lax.broadcasted_iota(jnp.int32, (tq, tk), 0) // CHUNK
    col = lax.broadcasted_iota(jnp.int32, (tq, tk), 1)
    kc = col // CHUNK
    first_ok = (nkb - 1 - pl.program_id(0)) * tq
    valid = (kc >= qc) & (kc <= qc + LEFT_CHUNKS) & (col >= first_ok)
    o_ref[...] = jnp.where(valid, full[:, tq:] * LOG2E, -1e30)


def _attn_bias(base):
    h = base.shape[0]
    tk = TQ + LEFT_CHUNKS * CHUNK
    return pl.pallas_call(
        _attn_bias_kernel,
        grid=(tk // TQ, h),
        in_specs=[pl.BlockSpec((None, 1, TQ + tk), lambda v, i: (i, 0, 0))],
        out_specs=pl.BlockSpec((None, None, TQ, tk), lambda v, i: (v, i, 0, 0)),
        out_shape=jax.ShapeDtypeStruct((tk // TQ, h, TQ, tk), F32),
        compiler_params=_params(2),
        name="attn_bias",
    )(base)


def _attn_kernel(q_ref, k0_ref, k1_ref, k2_ref, v0_ref, v1_ref, v2_ref, bias_ref, o_ref):
    tq = q_ref.shape[0]
    half_w = LANES // 2
    lo = lax.broadcasted_iota(jnp.int32, (tq, LANES), 1) < half_w
    lo_k = lax.broadcasted_iota(jnp.int32, (3 * tq, LANES), 1) < half_w
    k_refs = (k0_ref, k1_ref, k2_ref)
    v_refs = (v0_ref, v1_ref, v2_ref)
    npair = q_ref.shape[1] // LANES
    for j0 in range(0, npair, ATTN_GROUP):
        pairs = range(j0, min(j0 + ATTN_GROUP, npair))
        qk, vh = {}, {}
        for j in pairs:
            sl = slice(LANES * j, LANES * (j + 1))
            q2 = q_ref[:, sl]
            kcat = jnp.concatenate([r[:, sl] for r in k_refs], axis=0)
            vcat = jnp.concatenate([r[:, sl] for r in v_refs], axis=0)
            one = jnp.ones_like(vcat)
            zero = jnp.zeros_like(q2)
            vh[j] = (jnp.where(lo_k, vcat, one), jnp.where(lo_k, one, vcat))
            qk[j] = [_dot_nt(jnp.where(lo if half == 0 else jnp.logical_not(lo), q2, zero), kcat)
                     for half in range(2)]
        ps = {}
        for j in pairs:
            ps[j] = []
            for half in range(2):
                s = qk[j][half] + bias_ref[2 * j + half]
                ps[j].append(jnp.exp2(s - jnp.max(s, axis=-1, keepdims=True)).astype(BF16))
        for j in pairs:
            pv = [_dot(ps[j][half], vh[j][half]) for half in range(2)]
            num = jnp.where(lo, pv[0], pv[1])
            den = pltpu.roll(jnp.where(lo, pv[1], pv[0]), half_w, 1)
            o_ref[:, LANES * j:LANES * (j + 1)] = (num / den).astype(BF16)


def _attention(qkv, bias, batch):
    t, w3 = qkv.shape
    w = w3 // 3
    s = t // batch
    qkv3 = qkv.reshape(batch, s, w3)
    nvar = bias.shape[0]

    def kv_spec(colblk, back):
        return pl.BlockSpec((None, TQ, w), lambda i, b: (b, jnp.maximum(i - back, 0), colblk))

    out = pl.pallas_call(
        _attn_kernel,
        grid=(s // TQ, batch),
        in_specs=[pl.BlockSpec((None, TQ, w), lambda i, b: (b, i, 0)),
                  kv_spec(1, 2), kv_spec(1, 1), kv_spec(1, 0),
                  kv_spec(2, 2), kv_spec(2, 1), kv_spec(2, 0),
                  pl.BlockSpec((None,) + bias.shape[1:], lambda i, b: (jnp.minimum(i, nvar - 1), 0, 0, 0))],
        out_specs=pl.BlockSpec((None, TQ, w), lambda i, b: (b, i, 0)),
        out_shape=jax.ShapeDtypeStruct((batch, s, w), BF16),
        compiler_params=_params(2),
        name="attention",
    )(qkv3, qkv3, qkv3, qkv3, qkv3, qkv3, qkv3, bias)
    return out.reshape(t, w)


def _proj_route_kernel(a_ref, b_ref, r_ref, wa_ref, wb_ref, g_ref, wt_ref,
                       h_ref, hn_ref, meta_ref, metat_ref, cnt_ref, carry_ref, pair_s):
    tm = a_ref.shape[0]
    ts = min(TM_ROUTE_SUB, tm)

    @pl.when(pl.program_id(0) == 0)
    def _():
        carry_ref[...] = jnp.zeros_like(carry_ref)

    wt = wt_ref[...]
    ridx = lax.broadcasted_iota(jnp.int32, (ROUTE_ROWS, ts), 0).astype(F32)
    r8 = lax.broadcasted_iota(jnp.int32, (SUBLANES, ts), 0)
    rr = lax.broadcasted_iota(jnp.int32, (ts, ts), 0)
    cc = lax.broadcasted_iota(jnp.int32, (ts, ts), 1)
    utri = jnp.where(rr < cc, 1.0, 0.0).astype(BF16)
    neg = -jnp.inf
    far = float(LANES)
    carry = carry_ref[...]

    all_logits = []
    for sub in range(tm // ts):
        rows = pl.ds(sub * ts, ts)
        h = r_ref[rows, :] + (_dot(a_ref[rows, :], wa_ref[...]) + _dot(b_ref[rows, :], wb_ref[...]))
        h_ref[rows, :] = h
        hn = _rms(h, g_ref[...])
        _pack_rows(hn_ref.at[pl.ds(sub * ts * XROWS, ts * XROWS), :], hn, pair_s)

        hi, lo = _split_bf16(hn)
        p_hi = _dot_nt(wt, hi)
        p_lo = _dot_nt(wt[:LANES], lo)
        all_logits.append((p_hi[:ROUTE_ROWS] + p_hi[LANES:LANES + ROUTE_ROWS]) + p_lo[:ROUTE_ROWS])

    for sub, logits in enumerate(all_logits):
        rows = pl.ds(sub * ts, ts)

        def first_max(mask, logits=logits):
            v = jnp.max(jnp.where(mask, logits, neg), axis=0, keepdims=True)
            idx = jnp.min(jnp.where(mask & (logits == v), ridx, far), axis=0, keepdims=True)
            return v, idx

        gmask = ridx < N_GROUPS
        gmax, gidx = first_max(gmask)
        g_gate = 1.0 / jnp.sum(jnp.where(gmask, jnp.exp(logits - gmax), 0.0), axis=0, keepdims=True)
        e_lo = N_GROUPS + EXPERTS_PER_GROUP * gidx
        emask = (ridx >= e_lo) & (ridx < e_lo + EXPERTS_PER_GROUP)
        v1, i1 = first_max(emask)
        v2, i2 = first_max(emask & (ridx != i1))
        tt = jnp.exp(v2 - v1)
        w1 = g_gate / (1.0 + tt)
        w2 = g_gate * tt / (1.0 + tt)

        sel1 = ridx == i1
        sel2 = ridx == i2
        onehot = jnp.where(sel1 | sel2, 1.0, 0.0)
        before = _dot(onehot.astype(BF16), utri) + carry[:, 0:1]
        rank1 = jnp.sum(jnp.where(sel1, before, 0.0), axis=0, keepdims=True)
        rank2 = jnp.sum(jnp.where(sel2, before, 0.0), axis=0, keepdims=True)
        carry = carry + jnp.sum(onehot, axis=1, keepdims=True)

        mt = jnp.zeros((SUBLANES, ts), F32)
        for c, val in enumerate((i1 - N_GROUPS, i2 - N_GROUPS, rank1, rank2, w1, w2)):
            mt = jnp.where(r8 == c, val, mt)
        metat_ref[:, rows] = mt
        meta_ref[rows, :] = jnp.concatenate([mt, jnp.zeros((LANES - SUBLANES, ts), F32)], axis=0).T

    carry_ref[...] = carry
    cnt_ref[...] = carry


def _proj_route(a, acol, b, bcol, resid, wa, wb, g, wt):
    t, d = resid.shape
    kw = wa.shape[0]
    tm = min(TM_ROUTE, t)
    return pl.pallas_call(
        _proj_route_kernel,
        grid=(t // tm,),
        in_specs=[pl.BlockSpec((tm, kw), lambda i: (i, acol)), pl.BlockSpec((tm, kw), lambda i: (i, bcol)),
                  pl.BlockSpec((tm, d), lambda i: (i, 0)),
                  _const_spec(wa.shape), _const_spec(wb.shape), _const_spec(g.shape), _const_spec(wt.shape)],
        out_specs=[pl.BlockSpec((tm, d), lambda i: (i, 0)),
                   pl.BlockSpec((tm * XROWS, LANES), lambda i: (i, 0)),
                   pl.BlockSpec((tm, LANES), lambda i: (i, 0)),
                   pl.BlockSpec((SUBLANES, tm), lambda i: (0, i)),
                   pl.BlockSpec((ROUTE_ROWS, LANES), lambda i: (0, 0))],
        out_shape=[jax.ShapeDtypeStruct((t, d), F32),
                   jax.ShapeDtypeStruct((t * XROWS, LANES), jnp.uint32),
                   jax.ShapeDtypeStruct((t, LANES), F32),
                   jax.ShapeDtypeStruct((SUBLANES, t), F32),
                   jax.ShapeDtypeStruct((ROUTE_ROWS, LANES), F32)],
        scratch_shapes=[pltpu.VMEM((ROUTE_ROWS, LANES), F32),
                        pltpu.VMEM((XROWS, 2 * min(TM_ROUTE_SUB, tm), LANES), F32)],
        compiler_params=_params(1),
        name="proj_route",
    )(a, b, resid, wa, wb, g, wt)


def _plan_kernel(cnt_ref, meta_ref, dest_ref, tab_ref, start_s, *, tb):
    r_n = ROUTE_ROWS
    tab_w = tab_ref.shape[1]

    @pl.when(pl.program_id(0) == 0)
    def _():
        row = lax.broadcasted_iota(jnp.int32, (r_n, LANES), 0)
        lane = lax.broadcasted_iota(jnp.int32, (r_n, LANES), 1)

        def prefix(v):
            s = 1
            while s < r_n:
                v = v + jnp.where(row >= s, pltpu.roll(v, s, 0), 0.0)
                s *= 2
            return v

        counts = cnt_ref[...]
        padded = jnp.floor((counts + (tb - 1)) * (1.0 / tb)) * tb
        pad_end = prefix(padded)
        pad_start = pad_end - padded
        pad_lo = pad_start + counts
        start_s[...] = pad_start
        used = padded > 0.0
        order = prefix(jnp.where(used, 1.0, 0.0)) - 1.0
        nused = order[r_n - 1:r_n, :] + 1.0
        nact = pad_end[r_n - 1:r_n, :] * (1.0 / tb)
        expert = (row - N_GROUPS).astype(F32)
        lane_f = lane.astype(F32)
        used_seq = jnp.sum(jnp.where(used & (order == jnp.minimum(lane_f, nused - 1.0)), expert, 0.0),
                           axis=0, keepdims=True)

        erow = lax.broadcasted_iota(jnp.int32, (r_n, tab_w), 0)
        blk_start = lax.broadcasted_iota(jnp.int32, (r_n, tab_w), 1).astype(F32) * tb
        is_expert = (erow >= N_GROUPS) & (erow < N_GROUPS + N_EXPERTS)
        ended = jnp.where(is_expert & (pad_end[:, 0:1] <= blk_start), 1.0, 0.0)
        blk_expert = jnp.minimum(jnp.sum(ended, axis=0, keepdims=True), N_EXPERTS - 1.0)
        mine = (erow - N_GROUPS).astype(F32) == blk_expert
        blk_order = jnp.sum(jnp.where(mine, order[:, 0:1], 0.0), axis=0, keepdims=True)
        blk_lo = jnp.sum(jnp.where(mine, pad_lo[:, 0:1], 0.0), axis=0, keepdims=True)
        blk_fill = jnp.clip(blk_lo - blk_start[0:1, :], 0.0, float(tb))

        cols = jnp.where(lane == 0, pad_lo, jnp.where(lane == 1, pad_end, 0.0))
        rows_t = jnp.concatenate([cols, jnp.zeros((LANES - r_n, LANES), F32)], axis=0).T
        misc = jnp.where(lane[0:1, :] == 0, nact, jnp.where(lane[0:1, :] == 1, nused, 0.0))
        wide = lambda v: v if tab_w == LANES else jnp.concatenate(
            [v, jnp.zeros((v.shape[0], tab_w - LANES), F32)], axis=1)
        tab = jnp.concatenate([wide(rows_t[0:2, :]), wide(used_seq), wide(misc), blk_order, blk_fill,
                               jnp.zeros((2, tab_w), F32)], axis=0)
        tab_ref[...] = tab.astype(jnp.int32)

    for c in range(dest_ref.shape[1] // LANES):
        sl = slice(LANES * c, LANES * (c + 1))
        for k in range(TOP_K):
            expert = meta_ref[k:k + 1, sl]
            start = jnp.zeros((1, LANES), F32)
            for e in range(N_EXPERTS):
                start = jnp.where(expert == float(e), start_s[N_GROUPS + e:N_GROUPS + e + 1, :], start)
            dest_ref[k:k + 1, sl] = (start + meta_ref[TOP_K + k:TOP_K + k + 1, sl]).astype(jnp.int32)


def _plan(meta_t, cnt, tb):
    t = meta_t.shape[1]
    nblk = -(-(t * TOP_K + N_EXPERTS * (tb - 1)) // tb)
    tile = min(PLAN_TILE, t)
    tab_w = -(-nblk // LANES) * LANES
    dest, tab = pl.pallas_call(
        functools.partial(_plan_kernel, tb=tb),
        grid=(t // tile,),
        in_specs=[_const_spec(cnt.shape), pl.BlockSpec((SUBLANES, tile), lambda i: (0, i))],
        out_specs=[pl.BlockSpec((TOP_K, tile), lambda i: (0, i)), pl.BlockSpec((SUBLANES, tab_w), lambda i: (0, 0))],
        out_shape=[jax.ShapeDtypeStruct((TOP_K, t), jnp.int32), jax.ShapeDtypeStruct((SUBLANES, tab_w), jnp.int32)],
        scratch_shapes=[pltpu.VMEM((ROUTE_ROWS, LANES), F32)],
        compiler_params=_params(1),
        name="plan",
    )(cnt, meta_t)
    experts = slice(N_GROUPS, N_GROUPS + N_EXPERTS)
    return (dest.reshape(-1), tab[0, experts], tab[1, experts], tab[4, :nblk], tab[5, :nblk],
            tab[2, :N_EXPERTS], tab[3, :2], nblk)


def _dispatch_kernel(dest_ref, plo_ref, phi_ref, nact_ref, hn_ref, xs_ref, zero_ref, sem, zsem):
    tm = hn_ref.shape[0] // XROWS
    t = dest_ref.shape[0] // TOP_K
    tb = zero_ref.shape[0] // XROWS
    nblk = xs_ref.shape[0] // zero_ref.shape[0]
    i = pl.program_id(0)

    def zero_copy(row, nrows):
        return pltpu.make_async_copy(
            zero_ref.at[pl.ds(0, nrows * XROWS), :],
            xs_ref.at[pl.ds(pl.multiple_of(row * XROWS, XROWS), nrows * XROWS), :], zsem)

    def zero_fill(start):
        def go(cp):
            cp.start() if start else cp.wait()

        def per_expert(e, c):
            off = plo_ref[e]
            n = phi_ref[e] - off
            bit = tb // 2
            while bit:
                pl.when((n & bit) != 0)(functools.partial(lambda o, b: go(zero_copy(o, b)), off, bit))
                off = off + (n & bit)
                bit //= 2
            return c

        def per_block(b, c):
            go(zero_copy(b * tb, tb))
            return c

        lax.fori_loop(0, N_EXPERTS, per_expert, 0)
        lax.fori_loop(nact_ref[0], nblk, per_block, 0)

    @pl.when(i == 0)
    def _():
        zero_ref[...] = jnp.zeros_like(zero_ref)
        zero_fill(True)

    def copy(r, k):
        d = dest_ref[k * t + i * tm + r]
        return pltpu.make_async_copy(
            hn_ref.at[pl.ds(pl.multiple_of(r * XROWS, XROWS), XROWS), :],
            xs_ref.at[pl.ds(pl.multiple_of(d * XROWS, XROWS), XROWS), :], sem)

    def start(r, c):
        for k in range(TOP_K):
            copy(r, k).start(priority=k)
        return c

    lax.fori_loop(0, tm, start, 0, unroll=DMA_UNROLL)
    for k in range(TOP_K):
        pltpu.make_async_copy(hn_ref, xs_ref.at[pl.ds(0, tm * XROWS), :], sem).wait()

    @pl.when(i == 0)
    def _():
        zero_fill(False)


def _dispatch(dest, pad_lo, pad_hi, nact, hn_slab, nblk, tb):
    t = hn_slab.shape[0] // XROWS
    tm = min(TM_COMB, t)
    return pl.pallas_call(
        _dispatch_kernel,
        grid_spec=pltpu.PrefetchScalarGridSpec(
            num_scalar_prefetch=4,
            grid=(t // tm,),
            in_specs=[pl.BlockSpec((tm * XROWS, LANES), lambda i, *_: (i, 0))],
            out_specs=pl.BlockSpec(memory_space=pl.ANY),
            scratch_shapes=[pltpu.VMEM((tb * XROWS, LANES), jnp.uint32),
                            pltpu.SemaphoreType.DMA(()), pltpu.SemaphoreType.DMA(())]),
        out_shape=jax.ShapeDtypeStruct((nblk * tb * XROWS, LANES), jnp.uint32),
        compiler_params=_params(1),
        name="dispatch",
    )(dest, pad_lo, pad_hi, nact, hn_slab)


def _expert_kernel(ord_ref, fill_ref, useq_ref, nact_ref, xs_ref, wg_ref, wu_ref, wd_ref, ys_ref,
                   xbuf, xsem, wg_f, wu_f, wd_f, wsem, wg_s, wu_s, wd_s, pair_s, *, layer):
    i = pl.program_id(0)
    n = pl.num_programs(0)
    tb = ys_ref.shape[0] // SUBLANES
    rows = tb * XROWS
    active = i < nact_ref[0]
    order = ord_ref[i]
    changed = (i == 0) | (order != ord_ref[jnp.maximum(i - 1, 0)])
    slot = i % X_SLOTS
    ahead = X_SLOTS - 1

    def w_copies(j):
        e = useq_ref[jnp.minimum(j, useq_ref.shape[0] - 1)]
        ws = j % 2
        return [pltpu.make_async_copy(src.at[layer, e], dst.at[ws], wsem.at[ws])
                for src, dst in ((wg_ref, wg_f), (wu_ref, wu_f), (wd_ref, wd_f))]

    @pl.when(i == 0)
    def _():
        for cp in w_copies(0):
            cp.start()

    def x_copy(step, sl):
        blk = jnp.minimum(step, nact_ref[0] - 1)
        return pltpu.make_async_copy(xs_ref.at[pl.ds(pl.multiple_of(blk * rows, rows), rows), :],
                                     xbuf.at[sl], xsem.at[sl])

    @pl.when(i == 0)
    def _():
        for a in range(ahead):
            x_copy(a, a).start()

    x_copy(i, slot).wait()
    x_copy(i + ahead, (i + ahead) % X_SLOTS).start()

    @pl.when(i == n - 1)
    def _():
        for a in range(1, X_SLOTS):
            x_copy(i + a, (i + a) % X_SLOTS).wait()

    @pl.when(active & changed)
    def _():
        for cp in w_copies(order):
            cp.wait()
        for cp in w_copies(order + 1):
            cp.start()
        ws = order % 2
        wg_s[...] = wg_f[ws].astype(BF16)
        wu_s[...] = wu_f[ws].astype(BF16)
        wd_s[...] = wd_f[ws].astype(BF16)

    @pl.when(i == n - 1)
    def _():
        for cp in w_copies(nact_ref[1]):
            cp.wait()

    def mlp(nrows):
        x = _unpack_rows(xbuf.at[slot, pl.ds(0, nrows * XROWS), :], nrows, pair_s).astype(BF16)
        gate = _dot(x, wg_s[...])
        up = _dot(x, wu_s[...])
        hid = (gate * jax.nn.sigmoid(gate) * up).astype(BF16)
        _to_slab(ys_ref.at[pl.ds(0, nrows * SUBLANES), :], _dot(hid, wd_s[...]))

    half = tb // 2
    upper_empty = fill_ref[i] <= half

    @pl.when(active & jnp.logical_not(upper_empty))
    def _():
        mlp(tb)

    @pl.when(active & upper_empty)
    def _():
        mlp(half)
        ys_ref[pl.ds(half * SUBLANES, half * SUBLANES), :] = jnp.zeros((half * SUBLANES, LANES), F32)

    @pl.when(jnp.logical_not(active))
    def _():
        ys_ref[...] = jnp.zeros_like(ys_ref)


def _experts(blk_order, blk_fill, used_seq, nact, xs, wg, wu, wd, layer, tb):
    nblk = blk_order.shape[0]
    _, _, d, de = wg.shape
    anyspec = pl.BlockSpec(memory_space=pl.ANY)
    return pl.pallas_call(
        functools.partial(_expert_kernel, layer=layer),
        grid_spec=pltpu.PrefetchScalarGridSpec(
            num_scalar_prefetch=4,
            grid=(nblk,),
            in_specs=[anyspec, anyspec, anyspec, anyspec],
            out_specs=pl.BlockSpec((tb * SUBLANES, LANES), lambda i, *_: (i, 0)),
            scratch_shapes=[pltpu.VMEM((X_SLOTS, tb * XROWS, LANES), jnp.uint32),
                            pltpu.SemaphoreType.DMA((X_SLOTS,)),
                            pltpu.VMEM((2, d, de), F32), pltpu.VMEM((2, d, de), F32), pltpu.VMEM((2, de, d), F32),
                            pltpu.SemaphoreType.DMA((2,)),
                            pltpu.VMEM((d, de), BF16), pltpu.VMEM((d, de), BF16), pltpu.VMEM((de, d), BF16),
                            pltpu.VMEM((XROWS, 2 * tb, LANES), F32)]),
        out_shape=jax.ShapeDtypeStruct((nblk * tb * SUBLANES, LANES), F32),
        compiler_params=_params(1),
        name="experts",
    )(blk_order, blk_fill, used_seq, nact, xs, wg, wu, wd)


def _combine(dest_ref, ys_ref, meta_ref, buf_ref, sem_ref, tm):
    i = pl.program_id(0)
    n = pl.num_programs(0)
    slot = i % COMB_SLOTS
    ahead = COMB_SLOTS - 1
    t = dest_ref.shape[0] // TOP_K

    def copy(tile, sl, r, k):
        d = dest_ref[k * t + tile * tm + r]
        row = r * SUBLANES if isinstance(r, int) else pl.multiple_of(r * SUBLANES, SUBLANES)
        return pltpu.make_async_copy(
            ys_ref.at[pl.ds(pl.multiple_of(d * SUBLANES, SUBLANES), SUBLANES), :],
            buf_ref.at[sl, k, pl.ds(row, SUBLANES), :],
            sem_ref.at[sl])

    def start_tile(tile, sl):
        def body(r, c):
            for k in range(TOP_K):
                copy(tile, sl, r, k).start(priority=k)
            return c
        lax.fori_loop(0, tm, body, 0, unroll=DMA_UNROLL)

    def wait_tile(sl):
        for k in range(TOP_K):
            pltpu.make_async_copy(ys_ref.at[pl.ds(0, tm * SUBLANES), :], buf_ref.at[sl, k], sem_ref.at[sl]).wait()

    @pl.when(i == 0)
    def _():
        for a in range(ahead):
            start_tile(jnp.minimum(a, n - 1), a)

    wait_tile(slot)

    nxt = jnp.minimum(i + ahead, n - 1)
    nslot = (i + ahead) % COMB_SLOTS

    def issue(part, nparts):
        rows = tm // nparts
        for r in range(part * rows, (part + 1) * rows):
            for k in range(TOP_K):
                copy(nxt, nslot, r, k).start(priority=k)

    @pl.when(i == n - 1)
    def _():
        for a in range(1, ahead):
            wait_tile((i + a) % COMB_SLOTS)

    y = None
    for k in range(TOP_K):
        gate = meta_ref[:, 2 * TOP_K + k:2 * TOP_K + k + 1]
        term = _from_slab(buf_ref.at[slot, k], tm) * gate
        y = term if y is None else y + term
    return y, issue


def _combine_drain(buf_ref, sem_ref, ys_ref, tm):
    i = pl.program_id(0)

    @pl.when(i == pl.num_programs(0) - 1)
    def _():
        sl = (i + COMB_SLOTS - 1) % COMB_SLOTS
        for k in range(TOP_K):
            pltpu.make_async_copy(ys_ref.at[pl.ds(0, tm * SUBLANES), :], buf_ref.at[sl, k], sem_ref.at[sl]).wait()


def _combine_scratch(tm):
    return [pltpu.VMEM((COMB_SLOTS, TOP_K, tm * SUBLANES, LANES), F32), pltpu.SemaphoreType.DMA((COMB_SLOTS,))]


def _comb_proj1_kernel(dest_ref, h_ref, meta_ref, ys_ref, g_ref, w_ref, lbl_ref,
                       h2_ref, qs_ref, lf_ref, iv_ref, gs_ref, buf_ref, sem_ref, *, lb_rows):
    tm = h_ref.shape[0]
    y, issue = _combine(dest_ref, ys_ref, meta_ref, buf_ref, sem_ref, tm)
    h2 = h_ref[...] + y
    h2_ref[...] = h2
    hn = _rms(h2, g_ref[...]).astype(BF16)
    d = h2.shape[1]
    lbl = lbl_ref[...]
    ex = jnp.exp(lbl - jnp.max(lbl, axis=0, keepdims=True))
    lb = jnp.sum(ex[:lb_rows], axis=0, keepdims=True) / jnp.sum(ex, axis=0, keepdims=True)

    def col(c):
        return _dot(hn, w_ref[:, d * c:d * (c + 1)])

    q = col(0)
    issue(0, 4)
    fz = col(1)
    issue(1, 4)
    iv = col(2)
    issue(2, 4)
    gz = col(3)
    issue(3, 4)
    qs_ref[...] = (q * jax.nn.sigmoid(q)).astype(BF16)
    lf_ref[...] = jnp.log(lb + (1.0 - lb) * jax.nn.sigmoid(fz)) * LOG2E
    iv_ref[...] = iv.astype(BF16)
    gs_ref[...] = (gz * jax.nn.sigmoid(gz)).astype(BF16)
    _combine_drain(buf_ref, sem_ref, ys_ref, tm)


def _comb_proj1(dest, h, meta, ys, g, w, lb_logits, lb_rows):
    t, d = h.shape
    tm = min(TM_COMB, t)
    row = lambda i, dd: (i, 0)
    const = lambda shape: pl.BlockSpec(shape, lambda i, dd: (0,) * len(shape))
    return pl.pallas_call(
        functools.partial(_comb_proj1_kernel, lb_rows=lb_rows),
        grid_spec=pltpu.PrefetchScalarGridSpec(
            num_scalar_prefetch=1,
            grid=(t // tm,),
            in_specs=[pl.BlockSpec((tm, d), row), pl.BlockSpec((tm, LANES), row),
                      pl.BlockSpec(memory_space=pl.ANY),
                      const(g.shape), const(w.shape), const(lb_logits.shape)],
            out_specs=[pl.BlockSpec((tm, d), row)] * 5,
            scratch_shapes=_combine_scratch(tm)),
        out_shape=[jax.ShapeDtypeStruct((t, d), F32), jax.ShapeDtypeStruct((t, d), BF16),
                   jax.ShapeDtypeStruct((t, d), F32), jax.ShapeDtypeStruct((t, d), BF16),
                   jax.ShapeDtypeStruct((t, d), BF16)],
        compiler_params=_params(1),
        name="comb_proj1",
    )(dest, h, meta, ys, g, w, lb_logits)


def _comb_final_kernel(dest_ref, h_ref, meta_ref, ys_ref, g_ref, o_ref, buf_ref, sem_ref):
    tm = h_ref.shape[0]
    y, issue = _combine(dest_ref, ys_ref, meta_ref, buf_ref, sem_ref, tm)
    issue(0, 1)
    o_ref[...] = _rms(h_ref[...] + y, g_ref[...])
    _combine_drain(buf_ref, sem_ref, ys_ref, tm)


def _comb_final(dest, h, meta, ys, g):
    t, d = h.shape
    tm = min(TM_COMB, t)
    row = lambda i, dd: (i, 0)
    return pl.pallas_call(
        _comb_final_kernel,
        grid_spec=pltpu.PrefetchScalarGridSpec(
            num_scalar_prefetch=1,
            grid=(t // tm,),
            in_specs=[pl.BlockSpec((tm, d), row), pl.BlockSpec((tm, LANES), row),
                      pl.BlockSpec(memory_space=pl.ANY),
                      pl.BlockSpec(g.shape, lambda i, dd: (0, 0))],
            out_specs=pl.BlockSpec((tm, d), row),
            scratch_shapes=_combine_scratch(tm)),
        out_shape=jax.ShapeDtypeStruct((t, d), F32),
        compiler_params=_params(1),
        name="comb_final",
    )(dest, h, meta, ys, g)


def _hgrn_kernel(qs_ref, lf_ref, iv_ref, gs_ref, ng_ref, o_ref, st_ref, d1_s, t16_s, *, dk):
    th = qs_ref.shape[0]
    nsub = CHUNK // SUB

    @pl.when(pl.program_id(1) == 0)
    def _():
        st_ref[...] = jnp.zeros_like(st_ref)

    rr = lax.broadcasted_iota(jnp.int32, (th, th), 0)
    cc = lax.broadcasted_iota(jnp.int32, (th, th), 1)
    same_sub = (rr // SUB) == (cc // SUB)
    same_chunk = (rr // CHUNK) == (cc // CHUNK)
    m_diag = same_sub & (cc <= rr)
    dsub = rr // SUB - cc // SUB
    m_off = [same_chunk & (dsub == dd) for dd in range(1, nsub)]

    hi, lo = _split_bf16(lf_ref[...])
    tri = jnp.where(m_diag, 1.0, 0.0).astype(BF16)
    d1 = _dot(tri, hi) + _dot(tri, lo)
    d1_s[...] = d1
    sub3 = (th // SUB, SUB, d1.shape[1])
    t16_s[...] = jnp.broadcast_to(d1.reshape(sub3)[:, SUB - 1:SUB, :], sub3).reshape(th, d1.shape[1])
    rrow = lax.broadcasted_iota(jnp.int32, (th, dk), 0)
    rsub = (rrow // SUB) % nsub
    rchunk = rrow // CHUNK
    nch = th // CHUNK

    def back_rows(dd):
        return [slice(CHUNK * c + SUB * dd, CHUNK * (c + 1)) for c in range(nch)]

    def prepare(hd):
        ls = pl.ds(hd * dk, dk)
        d1 = d1_s[:, ls]
        t16 = t16_s[:, ls]
        q = qs_ref[:, ls].astype(F32)
        kt = 1.0 - jnp.exp2(lf_ref[:, ls])
        suf = t16 - d1
        kx = (kt * jnp.exp2(suf)).astype(BF16)
        qv = [q * jnp.exp2(-suf)]
        acc = d1
        tail = suf
        for dd in range(1, nsub):
            qv.append(jnp.concatenate([q[sl] * jnp.exp2(acc[sl]) for sl in back_rows(dd)], axis=0))
            acc = acc + jnp.where(rsub >= dd, pltpu.roll(t16, SUB * dd, 0), 0.0)
            tail = tail + jnp.where(rsub < nsub - dd, pltpu.roll(t16, th - SUB * dd, 0), 0.0)
        b = acc
        qb = q * jnp.exp2(b)
        kend = kt * jnp.exp2(tail)
        zero = jnp.zeros_like(q)
        kend_x = jnp.concatenate([jnp.where(rchunk == c, kend, zero) for c in range(nch)], axis=1).astype(BF16)
        qb_x = jnp.concatenate([jnp.where(rchunk == c, qb, zero) for c in range(nch)], axis=1).astype(BF16)
        decs = [jnp.exp2(b[CHUNK * (c + 1) - 1:CHUNK * (c + 1), :]) for c in range(nch)]
        return ls, jnp.concatenate(qv, axis=0).astype(BF16), kx, kend_x, qb_x, decs

    def group(gi, carry):
        heads = [gi * HGRN_GROUP + u for u in range(HGRN_GROUP)]
        prep = [prepare(hd) for hd in heads]
        a4s = [_dot_nt(p[1], p[2]) for p in prep]
        incs = [_dot_tn(iv_ref[:, p[0]], p[3]) for p in prep]
        o_intras = []
        for p, a4 in zip(prep, a4s):
            att = jnp.where(m_diag, a4[:th], 0.0)
            at = th
            for dd in range(1, nsub):
                pieces = []
                for sl in back_rows(dd):
                    n = sl.stop - sl.start
                    pieces += [jnp.zeros((SUB * dd, th), F32), a4[at:at + n]]
                    at += n
                att = jnp.where(m_off[dd - 1], jnp.concatenate(pieces, axis=0), att)
            o_intras.append(_dot(att.astype(BF16), iv_ref[:, p[0]]))
        for hd, p, inc, o_intra in zip(heads, prep, incs, o_intras):
            ls, decs = p[0], p[5]
            st = st_ref[hd]
            starts = []
            for c in range(nch):
                starts.append(st)
                st = st * decs[c] + inc[:, dk * c:dk * (c + 1)]
            st_ref[hd] = st
            oh = o_intra + _dot_nt(p[4], jnp.concatenate(starts, axis=1).astype(BF16))
            on = oh * lax.rsqrt(jnp.mean(oh * oh, axis=-1, keepdims=True) + EPS)
            o_ref[:, ls] = (on * ng_ref[:, ls] * gs_ref[:, ls].astype(F32)).astype(BF16)
        return carry

    for gi in range(qs_ref.shape[1] // dk // HGRN_GROUP):
        group(gi, 0)


def _hgrn(qs, lf, iv, gs, ng, batch, dk):
    t, d = qs.shape
    s = t // batch
    th = min(TH, s)
    blk = pl.BlockSpec((None, th, d), lambda b, j: (b, j, 0))
    r3 = lambda a: a.reshape(batch, s, d)
    out = pl.pallas_call(
        functools.partial(_hgrn_kernel, dk=dk),
        grid=(batch, s // th),
        in_specs=[blk, blk, blk, blk, _const_spec(ng.shape)],
        out_specs=blk,
        out_shape=jax.ShapeDtypeStruct((batch, s, d), BF16),
        scratch_shapes=[pltpu.VMEM((d // dk, dk, dk), F32), pltpu.VMEM((th, d), F32), pltpu.VMEM((th, d), F32)],
        compiler_params=_params(2),
        name="hgrn2",
    )(r3(qs), r3(lf), r3(iv), r3(gs), ng)
    return out.reshape(t, d)


def _block_diag(w):
    n, c, dd = w.shape
    eye = jnp.eye(n, dtype=w.dtype)
    return (eye[:, None, :, None] * w[:, :, None, :]).reshape(n * c, n * dd)


def _router_weights(wg, we):
    d = wg.shape[0]
    w = jnp.concatenate([wg, we, jnp.zeros((d, LANES - wg.shape[1] - we.shape[1]), F32)], axis=1).T
    hi = w.astype(BF16)
    return jnp.concatenate([hi, (w - hi.astype(F32)).astype(BF16)], axis=0)


def _moe(layer, meta_t, cnt, hn_slab, moe_w_gate, moe_w_up, moe_w_down):
    dest, pad_lo, pad_hi, blk_order, blk_fill, used_seq, nact, nblk = _plan(meta_t, cnt, TB_EXPERT)
    xs = _dispatch(dest, pad_lo, pad_hi, nact, hn_slab, nblk, TB_EXPERT)
    ys = _experts(blk_order, blk_fill, used_seq, nact, xs, moe_w_gate, moe_w_up, moe_w_down, layer, TB_EXPERT)
    return dest, ys


def kernel(x, norm_mix_g, norm_ffn_g, norm_final_g, ab_w_in, ab_conv_w, ab_conv_b, rg_w_a, rg_b_a, rg_w_x, rg_b_x, rg_lambda, attn_rel_bias, ab_w_out, c_w_in, c_lb_logits, c_norm_g, c_w_out, moe_router_group, moe_router_expert, moe_w_gate, moe_w_up, moe_w_down):
    batch, seq, d = x.shape
    t = batch * seq
    xt = x.reshape(t, d)
    row = lambda v: v.reshape(1, -1)

    aw = ab_conv_w.shape[2]
    qkv, ya = _proj0_lru(x, row(norm_mix_g[0]), ab_w_in[0].astype(BF16), ab_conv_w[0], row(ab_conv_b[0]),
                         _block_diag(rg_w_a[0]).astype(BF16), _block_diag(rg_w_x[0]).astype(BF16),
                         row(rg_b_a[0]), row(rg_b_x[0]), row(rg_lambda[0]))
    tk = TQ + LEFT_CHUNKS * CHUNK
    dist = jnp.clip(tk - jnp.arange(TQ + tk), -REL_CLIP, REL_CLIP) + REL_CLIP
    bias = _attn_bias(attn_rel_bias[0][:, None, dist])
    yb = _attention(qkv, bias, batch)
    wo = ab_w_out[0].astype(BF16)
    wt = _router_weights(moe_router_group[0], moe_router_expert[0])
    h1, hn1, meta1, metat1, cnt1 = _proj_route(ya, 0, yb, 0, xt, wo[:aw], wo[aw:], row(norm_ffn_g[0]), wt)
    dest1, ys1 = _moe(0, metat1, cnt1, hn1, moe_w_gate, moe_w_up, moe_w_down)

    dk = c_norm_g.shape[1] // 8
    h2, qs, lf, iv, gs = _comb_proj1(dest1, h1, meta1, ys1, row(norm_mix_g[1]), c_w_in[0].astype(BF16),
                                     c_lb_logits, 1)
    om = _hgrn(qs, lf, iv, gs, row(c_norm_g[0]), batch, dk)
    wo = c_w_out[0].astype(BF16)
    half = wo.shape[0] // 2
    wt = _router_weights(moe_router_group[1], moe_router_expert[1])
    h3, hn3, meta3, metat3, cnt3 = _proj_route(om, 0, om, 1, h2, wo[:half], wo[half:], row(norm_ffn_g[1]), wt)
    dest3, ys3 = _moe(1, metat3, cnt3, hn3, moe_w_gate, moe_w_up, moe_w_down)

    out = _comb_final(dest3, h3, meta3, ys3, row(norm_final_g))
    return out.reshape(batch, seq, d)
```

```python
import functools

import jax
import jax.numpy as jnp
from jax import lax
from jax.experimental import pallas as pl
from jax.experimental.pallas import tpu as pltpu

F32 = jnp.float32
BF16 = jnp.bfloat16

EPS = 1e-6
LOG2E = 1.4426950408889634
RG_C = 8.0
CHUNK = 64
LEFT_CHUNKS = 8
REL_CLIP = 256
N_GROUPS = 4
EXPERTS_PER_GROUP = 8
N_EXPERTS = N_GROUPS * EXPERTS_PER_GROUP
TOP_K = 2
ROUTE_ROWS = 40

LANES = 128
SUBLANES = 8
XROWS = 4
SUB = 16

TS_LRU = 512
TQ = 256
ATTN_GROUP = 4
TM_ROUTE = 1024
TM_ROUTE_SUB = 256
TB_EXPERT = 512
X_SLOTS = 3
TM_COMB = 256
PLAN_TILE = 2048
TH = 256
HGRN_GROUP = 4
DMA_UNROLL = 8
COMB_SLOTS = 3
VMEM_MB = 48


def _params(n_axes, vmem_mb=VMEM_MB):
    return pltpu.CompilerParams(dimension_semantics=("arbitrary",) * n_axes,
                                vmem_limit_bytes=vmem_mb * 1024 * 1024)


def _const_spec(shape):
    nd = len(shape)
    return pl.BlockSpec(shape, lambda *_: (0,) * nd)


def _rms(x, g):
    return x * lax.rsqrt(jnp.mean(x * x, axis=-1, keepdims=True) + EPS) * g


def _dot(a, b):
    return jnp.dot(a, b, preferred_element_type=F32)


def _dot_nt(a, b):
    return lax.dot_general(a, b, (((1,), (1,)), ((), ())), preferred_element_type=F32)


def _dot_tn(a, b):
    return lax.dot_general(a, b, (((0,), (0,)), ((), ())), preferred_element_type=F32)


def _split_bf16(x):
    hi = x.astype(BF16)
    lo = (x - hi.astype(F32)).astype(BF16)
    return hi, lo


def _pack_rows(ref, val, pair_s):
    m = val.shape[0]
    for j in range(XROWS):
        pair_s[j, pl.ds(0, m, stride=2), :] = val[:, LANES * j:LANES * (j + 1)]
        pair_s[j, pl.ds(1, m, stride=2), :] = val[:, LANES * (j + XROWS):LANES * (j + XROWS + 1)]
        ref[pl.ds(j, m, stride=XROWS), :] = pltpu.bitcast(pair_s[j, pl.ds(0, 2 * m), :].astype(BF16), jnp.uint32)


def _packed_zeros(m):
    return pltpu.bitcast(jnp.zeros((2 * m * XROWS, LANES), BF16), jnp.uint32)


def _unpack_rows(ref, m, pair_s):
    lo, hi = [], []
    for j in range(XROWS):
        pair_s[j, pl.ds(0, 2 * m), :] = pltpu.bitcast(ref[pl.ds(j, m, stride=XROWS), :], BF16).astype(F32)
        lo.append(pair_s[j, pl.ds(0, m, stride=2), :])
        hi.append(pair_s[j, pl.ds(1, m, stride=2), :])
    return jnp.concatenate(lo + hi, axis=1)


def _proj0_lru_kernel(x_ref, g_ref, w_ref, cw_ref, cb_ref, wa_ref, wx_ref, ba_ref, bx_ref, lam_ref,
                      qkv_ref, ya_ref, xbuf, h_ref, a_s, u_s):
    ts = x_ref.shape[0]
    w = ya_ref.shape[1]
    bw = qkv_ref.shape[1] // 3

    @pl.when(pl.program_id(1) == 0)
    def _():
        xbuf[pl.ds(0, SUBLANES), :] = jnp.zeros((SUBLANES, w), F32)
        h_ref[...] = jnp.zeros_like(h_ref)

    hn = _rms(x_ref[...], g_ref[...]).astype(BF16)

    def qkv_cols(c, scale=None):
        z = _dot(hn, w_ref[:, 2 * w + c * bw:2 * w + (c + 1) * bw])
        qkv_ref[:, c * bw:(c + 1) * bw] = (z if scale is None else z * scale).astype(BF16)

    xg = _dot(hn, w_ref[:, :2 * w])

    xbuf[pl.ds(SUBLANES, ts), :] = xg[:, :w]
    nk = cw_ref.shape[0]
    y = cb_ref[...]
    for k in range(nk):
        y = y + cw_ref[nk - 1 - k:nk - k, :] * xbuf[pl.ds(SUBLANES - k, ts), :]
    xbuf[pl.ds(0, SUBLANES), :] = xbuf[pl.ds(ts, SUBLANES), :]

    yb = y.astype(BF16)
    rg = jax.nn.sigmoid(_dot(yb, wa_ref[...]) + ba_ref[...])
    ig = jax.nn.sigmoid(_dot(yb, wx_ref[...]) + bx_ref[...])
    qkv_cols(0, CHUNK ** -0.5 * LOG2E)
    lam = lam_ref[...]
    log_sig = jnp.minimum(lam, 0.0) - jnp.log1p(jnp.exp(-jnp.abs(lam)))
    log_a = RG_C * rg * log_sig
    a = jnp.exp(log_a)
    m = 1.0 - a * a
    u = jnp.where(m > 0.0, m * lax.rsqrt(m), 0.0) * (ig * y)
    qkv_cols(1)

    grp = (ts // SUBLANES, SUBLANES, w)
    a = a.reshape(grp)
    u = u.reshape(grp)
    rowm = lax.broadcasted_iota(jnp.int32, grp, 1)
    for s in (1, 2, 4):
        keep = rowm >= s
        a_sh = jnp.where(keep, pltpu.roll(a, s, 1), 1.0)
        u_sh = jnp.where(keep, pltpu.roll(u, s, 1), 0.0)
        u = a * u_sh + u
        a = a * a_sh
    a_s[...] = a.reshape(ts, w)
    u_s[...] = u.reshape(ts, w)
    qkv_cols(2)

    def group(gi, h):
        off = pl.multiple_of(gi * SUBLANES, SUBLANES)
        hg = a_s[pl.ds(off, SUBLANES), :] * h + u_s[pl.ds(off, SUBLANES), :]
        u_s[pl.ds(off, SUBLANES), :] = hg
        return jnp.broadcast_to(hg[SUBLANES - 1:SUBLANES, :], hg.shape)

    h_ref[...] = lax.fori_loop(0, ts // SUBLANES, group, h_ref[...])

    ga = xg[:, w:]
    gelu = 0.5 * ga * (1.0 + jnp.tanh(0.7978845608028654 * (ga + 0.044715 * (ga * ga * ga))))
    ya_ref[...] = (u_s[...] * gelu).astype(BF16)


def _proj0_lru(x, g, w_in, cw, cb, wa, wx, ba, bx, lam):
    batch, s, d = x.shape
    w = cw.shape[1]
    nqkv = w_in.shape[1] - 2 * w
    ts = min(TS_LRU, s)
    small = [g, w_in, cw, cb, wa, wx, ba, bx, lam]
    blk = lambda width: pl.BlockSpec((None, ts, width), lambda b, j: (b, j, 0))
    qkv, ya = pl.pallas_call(
        _proj0_lru_kernel,
        grid=(batch, s // ts),
        in_specs=[blk(d)] + [_const_spec(a.shape) for a in small],
        out_specs=[blk(nqkv), blk(w)],
        out_shape=[jax.ShapeDtypeStruct((batch, s, nqkv), BF16), jax.ShapeDtypeStruct((batch, s, w), BF16)],
        scratch_shapes=[pltpu.VMEM((ts + SUBLANES, w), F32), pltpu.VMEM((SUBLANES, w), F32),
                        pltpu.VMEM((ts, w), F32), pltpu.VMEM((ts, w), F32)],
        compiler_params=_params(2),
        name="proj0_lru",
    )(x, *small)
    return qkv.reshape(batch * s, nqkv), ya.reshape(batch * s, w)


def _attn_bias_kernel(base_ref, o_ref):
    tq, tk = o_ref.shape
    nkb = tk // tq
    row = base_ref[...]
    full = pltpu.roll(jnp.broadcast_to(row, (tq, tq + tk)), 0, 1, stride=1, stride_axis=0)
    qc = lax.broadcasted_iota(jnp.int32, (tq, tk), 0) // CHUNK
    col = lax.broadcasted_iota(jnp.int32, (tq, tk), 1)
    kc = col // CHUNK
    first_ok = (nkb - 1 - pl.program_id(0)) * tq
    valid = (kc >= qc) & (kc <= qc + LEFT_CHUNKS) & (col >= first_ok)
    o_ref[...] = jnp.where(valid, full[:, tq:] * LOG2E, -1e30)


def _attn_bias(base):
    h = base.shape[0]
    tk = TQ + LEFT_CHUNKS * CHUNK
    return pl.pallas_call(
        _attn_bias_kernel,
        grid=(tk // TQ, h),
        in_specs=[pl.BlockSpec((None, 1, TQ + tk), lambda v, i: (i, 0, 0))],
        out_specs=pl.BlockSpec((None, None, TQ, tk), lambda v, i: (v, i, 0, 0)),
        out_shape=jax.ShapeDtypeStruct((tk // TQ, h, TQ, tk), F32),
        compiler_params=_params(2),
        name="attn_bias",
    )(base)


def _attn_kernel(q_ref, k0_ref, k1_ref, k2_ref, v0_ref, v1_ref, v2_ref, bias_ref, o_ref):
    tq = q_ref.shape[0]
    half_w = LANES // 2
    lo = lax.broadcasted_iota(jnp.int32, (tq, LANES), 1) < half_w
    lo_k = lax.broadcasted_iota(jnp.int32, (3 * tq, LANES), 1) < half_w
    k_refs = (k0_ref, k1_ref, k2_ref)
    v_refs = (v0_ref, v1_ref, v2_ref)
    npair = q_ref.shape[1] // LANES
    for j0 in range(0, npair, ATTN_GROUP):
        pairs = range(j0, min(j0 + ATTN_GROUP, npair))
        qk, vh = {}, {}
        for j in pairs:
            sl = slice(LANES * j, LANES * (j + 1))
            q2 = q_ref[:, sl]
            kcat = jnp.concatenate([r[:, sl] for r in k_refs], axis=0)
            vcat = jnp.concatenate([r[:, sl] for r in v_refs], axis=0)
            one = jnp.ones_like(vcat)
            zero = jnp.zeros_like(q2)
            vh[j] = (jnp.where(lo_k, vcat, one), jnp.where(lo_k, one, vcat))
            qk[j] = [_dot_nt(jnp.where(lo if half == 0 else jnp.logical_not(lo), q2, zero), kcat)
                     for half in range(2)]
        ps = {}
        for j in pairs:
            ps[j] = []
            for half in range(2):
                s = qk[j][half] + bias_ref[2 * j + half]
                ps[j].append(jnp.exp2(s - jnp.max(s, axis=-1, keepdims=True)).astype(BF16))
        for j in pairs:
            pv = [_dot(ps[j][half], vh[j][half]) for half in range(2)]
            num = jnp.where(lo, pv[0], pv[1])
            den = pltpu.roll(jnp.where(lo, pv[1], pv[0]), half_w, 1)
            o_ref[:, LANES * j:LANES * (j + 1)] = (num / den).astype(BF16)


def _attention(qkv, bias, batch):
    t, w3 = qkv.shape
    w = w3 // 3
    s = t // batch
    qkv3 = qkv.reshape(batch, s, w3)
    nvar = bias.shape[0]

    def kv_spec(colblk, back):
        return pl.BlockSpec((None, TQ, w), lambda i, b: (b, jnp.maximum(i - back, 0), colblk))

    out = pl.pallas_call(
        _attn_kernel,
        grid=(s // TQ, batch),
        in_specs=[pl.BlockSpec((None, TQ, w), lambda i, b: (b, i, 0)),
                  kv_spec(1, 2), kv_spec(1, 1), kv_spec(1, 0),
                  kv_spec(2, 2), kv_spec(2, 1), kv_spec(2, 0),
                  pl.BlockSpec((None,) + bias.shape[1:], lambda i, b: (jnp.minimum(i, nvar - 1), 0, 0, 0))],
        out_specs=pl.BlockSpec((None, TQ, w), lambda i, b: (b, i, 0)),
        out_shape=jax.ShapeDtypeStruct((batch, s, w), BF16),
        compiler_params=_params(2),
        name="attention",
    )(qkv3, qkv3, qkv3, qkv3, qkv3, qkv3, qkv3, bias)
    return out.reshape(t, w)


def _proj_route_kernel(a_ref, b_ref, r_ref, wa_ref, wb_ref, g_ref, wt_ref,
                       h_ref, hn_ref, meta_ref, metat_ref, cnt_ref, carry_ref, pair_s):
    tm = a_ref.shape[0]
    ts = min(TM_ROUTE_SUB, tm)

    @pl.when(pl.program_id(0) == 0)
    def _():
        carry_ref[...] = jnp.zeros_like(carry_ref)

    wt = wt_ref[...]
    ridx = lax.broadcasted_iota(jnp.int32, (ROUTE_ROWS, ts), 0).astype(F32)
    r8 = lax.broadcasted_iota(jnp.int32, (SUBLANES, ts), 0)
    rr = lax.broadcasted_iota(jnp.int32, (ts, ts), 0)
    cc = lax.broadcasted_iota(jnp.int32, (ts, ts), 1)
    utri = jnp.where(rr < cc, 1.0, 0.0).astype(BF16)
    neg = -jnp.inf
    far = float(LANES)
    carry = carry_ref[...]

    all_logits = []
    for sub in range(tm // ts):
        rows = pl.ds(sub * ts, ts)
        h = r_ref[rows, :] + (_dot(a_ref[rows, :], wa_ref[...]) + _dot(b_ref[rows, :], wb_ref[...]))
        h_ref[rows, :] = h
        hn = _rms(h, g_ref[...])
        _pack_rows(hn_ref.at[pl.ds(sub * ts * XROWS, ts * XROWS), :], hn, pair_s)

        hi, lo = _split_bf16(hn)
        p_hi = _dot_nt(wt, hi)
        p_lo = _dot_nt(wt[:LANES], lo)
        all_logits.append((p_hi[:ROUTE_ROWS] + p_hi[LANES:LANES + ROUTE_ROWS]) + p_lo[:ROUTE_ROWS])

    for sub, logits in enumerate(all_logits):
        rows = pl.ds(sub * ts, ts)

        def first_max(mask, logits=logits):
            v = jnp.max(jnp.where(mask, logits, neg), axis=0, keepdims=True)
            idx = jnp.min(jnp.where(mask & (logits == v), ridx, far), axis=0, keepdims=True)
            return v, idx

        gmask = ridx < N_GROUPS
        gmax, gidx = first_max(gmask)
        g_gate = 1.0 / jnp.sum(jnp.where(gmask, jnp.exp(logits - gmax), 0.0), axis=0, keepdims=True)
        e_lo = N_GROUPS + EXPERTS_PER_GROUP * gidx
        emask = (ridx >= e_lo) & (ridx < e_lo + EXPERTS_PER_GROUP)
        v1, i1 = first_max(emask)
        v2, i2 = first_max(emask & (ridx != i1))
        tt = jnp.exp(v2 - v1)
        w1 = g_gate / (1.0 + tt)
        w2 = g_gate * tt / (1.0 + tt)

        sel1 = ridx == i1
        sel2 = ridx == i2
        onehot = jnp.where(sel1 | sel2, 1.0, 0.0)
        before = _dot(onehot.astype(BF16), utri) + carry[:, 0:1]
        rank1 = jnp.sum(jnp.where(sel1, before, 0.0), axis=0, keepdims=True)
        rank2 = jnp.sum(jnp.where(sel2, before, 0.0), axis=0, keepdims=True)
        carry = carry + jnp.sum(onehot, axis=1, keepdims=True)

        mt = jnp.zeros((SUBLANES, ts), F32)
        for c, val in enumerate((i1 - N_GROUPS, i2 - N_GROUPS, rank1, rank2, w1, w2)):
            mt = jnp.where(r8 == c, val, mt)
        metat_ref[:, rows] = mt
        meta_ref[rows, :] = jnp.concatenate([mt, jnp.zeros((LANES - SUBLANES, ts), F32)], axis=0).T

    carry_ref[...] = carry
    cnt_ref[...] = carry


def _proj_route(a, acol, b, bcol, resid, wa, wb, g, wt):
    t, d = resid.shape
    kw = wa.shape[0]
    tm = min(TM_ROUTE, t)
    return pl.pallas_call(
        _proj_route_kernel,
        grid=(t // tm,),
        in_specs=[pl.BlockSpec((tm, kw), lambda i: (i, acol)), pl.BlockSpec((tm, kw), lambda i: (i, bcol)),
                  pl.BlockSpec((tm, d), lambda i: (i, 0)),
                  _const_spec(wa.shape), _const_spec(wb.shape), _const_spec(g.shape), _const_spec(wt.shape)],
        out_specs=[pl.BlockSpec((tm, d), lambda i: (i, 0)),
                   pl.BlockSpec((tm * XROWS, LANES), lambda i: (i, 0)),
                   pl.BlockSpec((tm, LANES), lambda i: (i, 0)),
                   pl.BlockSpec((SUBLANES, tm), lambda i: (0, i)),
                   pl.BlockSpec((ROUTE_ROWS, LANES), lambda i: (0, 0))],
        out_shape=[jax.ShapeDtypeStruct((t, d), F32),
                   jax.ShapeDtypeStruct((t * XROWS, LANES), jnp.uint32),
                   jax.ShapeDtypeStruct((t, LANES), F32),
                   jax.ShapeDtypeStruct((SUBLANES, t), F32),
                   jax.ShapeDtypeStruct((ROUTE_ROWS, LANES), F32)],
        scratch_shapes=[pltpu.VMEM((ROUTE_ROWS, LANES), F32),
                        pltpu.VMEM((XROWS, 2 * min(TM_ROUTE_SUB, tm), LANES), F32)],
        compiler_params=_params(1),
        name="proj_route",
    )(a, b, resid, wa, wb, g, wt)


def _plan_kernel(cnt_ref, meta_ref, dest_ref, tab_ref, start_s, *, tb):
    r_n = ROUTE_ROWS
    tab_w = tab_ref.shape[1]

    @pl.when(pl.program_id(0) == 0)
    def _():
        row = lax.broadcasted_iota(jnp.int32, (r_n, LANES), 0)
        lane = lax.broadcasted_iota(jnp.int32, (r_n, LANES), 1)

        def prefix(v):
            s = 1
            while s < r_n:
                v = v + jnp.where(row >= s, pltpu.roll(v, s, 0), 0.0)
                s *= 2
            return v

        counts = cnt_ref[...]
        padded = jnp.floor((counts + (tb - 1)) * (1.0 / tb)) * tb
        pad_end = prefix(padded)
        pad_start = pad_end - padded
        pad_lo = pad_start + counts
        start_s[...] = pad_start
        used = padded > 0.0
        order = prefix(jnp.where(used, 1.0, 0.0)) - 1.0
        nused = order[r_n - 1:r_n, :] + 1.0
        nact = pad_end[r_n - 1:r_n, :] * (1.0 / tb)
        expert = (row - N_GROUPS).astype(F32)
        lane_f = lane.astype(F32)
        used_seq = jnp.sum(jnp.where(used & (order == jnp.minimum(lane_f, nused - 1.0)), expert, 0.0),
                           axis=0, keepdims=True)

        erow = lax.broadcasted_iota(jnp.int32, (r_n, tab_w), 0)
        blk_start = lax.broadcasted_iota(jnp.int32, (r_n, tab_w), 1).astype(F32) * tb
        is_expert = (erow >= N_GROUPS) & (erow < N_GROUPS + N_EXPERTS)
        ended = jnp.where(is_expert & (pad_end[:, 0:1] <= blk_start), 1.0, 0.0)
        blk_expert = jnp.minimum(jnp.sum(ended, axis=0, keepdims=True), N_EXPERTS - 1.0)
        mine = (erow - N_GROUPS).astype(F32) == blk_expert
        blk_order = jnp.sum(jnp.where(mine, order[:, 0:1], 0.0), axis=0, keepdims=True)
        blk_lo = jnp.sum(jnp.where(mine, pad_lo[:, 0:1], 0.0), axis=0, keepdims=True)
        blk_fill = jnp.clip(blk_lo - blk_start[0:1, :], 0.0, float(tb))

        cols = jnp.where(lane == 0, pad_lo, jnp.where(lane == 1, pad_end, 0.0))
        rows_t = jnp.concatenate([cols, jnp.zeros((LANES - r_n, LANES), F32)], axis=0).T
        misc = jnp.where(lane[0:1, :] == 0, nact, jnp.where(lane[0:1, :] == 1, nused, 0.0))
        wide = lambda v: v if tab_w == LANES else jnp.concatenate(
            [v, jnp.zeros((v.shape[0], tab_w - LANES), F32)], axis=1)
        tab = jnp.concatenate([wide(rows_t[0:2, :]), wide(used_seq), wide(misc), blk_order, blk_fill,
                               jnp.zeros((2, tab_w), F32)], axis=0)
        tab_ref[...] = tab.astype(jnp.int32)

    for c in range(dest_ref.shape[1] // LANES):
        sl = slice(LANES * c, LANES * (c + 1))
        for k in range(TOP_K):
            expert = meta_ref[k:k + 1, sl]
            start = jnp.zeros((1, LANES), F32)
            for e in range(N_EXPERTS):
                start = jnp.where(expert == float(e), start_s[N_GROUPS + e:N_GROUPS + e + 1, :], start)
            dest_ref[k:k + 1, sl] = (start + meta_ref[TOP_K + k:TOP_K + k + 1, sl]).astype(jnp.int32)


def _plan(meta_t, cnt, tb):
    t = meta_t.shape[1]
    nblk = -(-(t * TOP_K + N_EXPERTS * (tb - 1)) // tb)
    tile = min(PLAN_TILE, t)
    tab_w = -(-nblk // LANES) * LANES
    dest, tab = pl.pallas_call(
        functools.partial(_plan_kernel, tb=tb),
        grid=(t // tile,),
        in_specs=[_const_spec(cnt.shape), pl.BlockSpec((SUBLANES, tile), lambda i: (0, i))],
        out_specs=[pl.BlockSpec((TOP_K, tile), lambda i: (0, i)), pl.BlockSpec((SUBLANES, tab_w), lambda i: (0, 0))],
        out_shape=[jax.ShapeDtypeStruct((TOP_K, t), jnp.int32), jax.ShapeDtypeStruct((SUBLANES, tab_w), jnp.int32)],
        scratch_shapes=[pltpu.VMEM((ROUTE_ROWS, LANES), F32)],
        compiler_params=_params(1),
        name="plan",
    )(cnt, meta_t)
    experts = slice(N_GROUPS, N_GROUPS + N_EXPERTS)
    return (dest.reshape(-1), tab[0, experts], tab[1, experts], tab[4, :nblk], tab[5, :nblk],
            tab[2, :N_EXPERTS], tab[3, :2], nblk)


def _dispatch_kernel(dest_ref, plo_ref, phi_ref, nact_ref, hn_ref, xs_ref, zero_ref, sem, zsem):
    tm = hn_ref.shape[0] // XROWS
    t = dest_ref.shape[0] // TOP_K
    tb = zero_ref.shape[0] // XROWS
    nblk = xs_ref.shape[0] // zero_ref.shape[0]
    i = pl.program_id(0)

    def zero_copy(row, nrows):
        return pltpu.make_async_copy(
            zero_ref.at[pl.ds(0, nrows * XROWS), :],
            xs_ref.at[pl.ds(pl.multiple_of(row * XROWS, XROWS), nrows * XROWS), :], zsem)

    def zero_fill(start):
        def go(cp):
            cp.start() if start else cp.wait()

        def per_expert(e, c):
            off = plo_ref[e]
            n = phi_ref[e] - off
            bit = tb // 2
            while bit:
                pl.when((n & bit) != 0)(functools.partial(lambda o, b: go(zero_copy(o, b)), off, bit))
                off = off + (n & bit)
                bit //= 2
            return c

        def per_block(b, c):
            go(zero_copy(b * tb, tb))
            return c

        lax.fori_loop(0, N_EXPERTS, per_expert, 0)
        lax.fori_loop(nact_ref[0], nblk, per_block, 0)

    @pl.when(i == 0)
    def _():
        zero_ref[...] = jnp.zeros_like(zero_ref)
        zero_fill(True)

    def copy(r, k):
        d = dest_ref[k * t + i * tm + r]
        return pltpu.make_async_copy(
            hn_ref.at[pl.ds(pl.multiple_of(r * XROWS, XROWS), XROWS), :],
            xs_ref.at[pl.ds(pl.multiple_of(d * XROWS, XROWS), XROWS), :], sem)

    def start(r, c):
        for k in range(TOP_K):
            copy(r, k).start(priority=k)
        return c

    lax.fori_loop(0, tm, start, 0, unroll=DMA_UNROLL)
    for k in range(TOP_K):
        pltpu.make_async_copy(hn_ref, xs_ref.at[pl.ds(0, tm * XROWS), :], sem).wait()

    @pl.when(i == 0)
    def _():
        zero_fill(False)


def _dispatch(dest, pad_lo, pad_hi, nact, hn_slab, nblk, tb):
    t = hn_slab.shape[0] // XROWS
    tm = min(TM_COMB, t)
    return pl.pallas_call(
        _dispatch_kernel,
        grid_spec=pltpu.PrefetchScalarGridSpec(
            num_scalar_prefetch=4,
            grid=(t // tm,),
            in_specs=[pl.BlockSpec((tm * XROWS, LANES), lambda i, *_: (i, 0))],
            out_specs=pl.BlockSpec(memory_space=pl.ANY),
            scratch_shapes=[pltpu.VMEM((tb * XROWS, LANES), jnp.uint32),
                            pltpu.SemaphoreType.DMA(()), pltpu.SemaphoreType.DMA(())]),
        out_shape=jax.ShapeDtypeStruct((nblk * tb * XROWS, LANES), jnp.uint32),
        compiler_params=_params(1),
        name="dispatch",
    )(dest, pad_lo, pad_hi, nact, hn_slab)


def _expert_kernel(ord_ref, fill_ref, useq_ref, nact_ref, xs_ref, wg_ref, wu_ref, wd_ref, ys_ref,
                   xbuf, xsem, wg_f, wu_f, wd_f, wsem, wg_s, wu_s, wd_s, pair_s, *, layer):
    i = pl.program_id(0)
    n = pl.num_programs(0)
    tb = ys_ref.shape[0] // XROWS
    rows = tb * XROWS
    active = i < nact_ref[0]
    order = ord_ref[i]
    changed = (i == 0) | (order != ord_ref[jnp.maximum(i - 1, 0)])
    slot = i % X_SLOTS
    ahead = X_SLOTS - 1

    def w_copies(j):
        e = useq_ref[jnp.minimum(j, useq_ref.shape[0] - 1)]
        ws = j % 2
        return [pltpu.make_async_copy(src.at[layer, e], dst.at[ws], wsem.at[ws])
                for src, dst in ((wg_ref, wg_f), (wu_ref, wu_f), (wd_ref, wd_f))]

    @pl.when(i == 0)
    def _():
        for cp in w_copies(0):
            cp.start()

    def x_copy(step, sl):
        blk = jnp.minimum(step, nact_ref[0] - 1)
        return pltpu.make_async_copy(xs_ref.at[pl.ds(pl.multiple_of(blk * rows, rows), rows), :],
                                     xbuf.at[sl], xsem.at[sl])

    @pl.when(i == 0)
    def _():
        for a in range(ahead):
            x_copy(a, a).start()

    x_copy(i, slot).wait()
    x_copy(i + ahead, (i + ahead) % X_SLOTS).start()

    @pl.when(i == n - 1)
    def _():
        for a in range(1, X_SLOTS):
            x_copy(i + a, (i + a) % X_SLOTS).wait()

    @pl.when(active & changed)
    def _():
        for cp in w_copies(order):
            cp.wait()
        for cp in w_copies(order + 1):
            cp.start()
        ws = order % 2
        wg_s[...] = wg_f[ws].astype(BF16)
        wu_s[...] = wu_f[ws].astype(BF16)
        wd_s[...] = wd_f[ws].astype(BF16)

    @pl.when(i == n - 1)
    def _():
        for cp in w_copies(nact_ref[1]):
            cp.wait()

    def mlp(nrows):
        x = _unpack_rows(xbuf.at[slot, pl.ds(0, nrows * XROWS), :], nrows, pair_s).astype(BF16)
        gate = _dot(x, wg_s[...])
        up = _dot(x, wu_s[...])
        hid = (gate * jax.nn.sigmoid(gate) * up).astype(BF16)
        _pack_rows(ys_ref.at[pl.ds(0, nrows * XROWS), :], _dot(hid, wd_s[...]), pair_s)

    half = tb // 2
    upper_empty = fill_ref[i] <= half

    @pl.when(active & jnp.logical_not(upper_empty))
    def _():
        mlp(tb)

    @pl.when(active & upper_empty)
    def _():
        mlp(half)
        ys_ref[pl.ds(half * XROWS, half * XROWS), :] = _packed_zeros(half)

    @pl.when(jnp.logical_not(active))
    def _():
        ys_ref[...] = _packed_zeros(tb)


def _experts(blk_order, blk_fill, used_seq, nact, xs, wg, wu, wd, layer, tb):
    nblk = blk_order.shape[0]
    _, _, d, de = wg.shape
    anyspec = pl.BlockSpec(memory_space=pl.ANY)
    return pl.pallas_call(
        functools.partial(_expert_kernel, layer=layer),
        grid_spec=pltpu.PrefetchScalarGridSpec(
            num_scalar_prefetch=4,
            grid=(nblk,),
            in_specs=[anyspec, anyspec, anyspec, anyspec],
            out_specs=pl.BlockSpec((tb * XROWS, LANES), lambda i, *_: (i, 0)),
            scratch_shapes=[pltpu.VMEM((X_SLOTS, tb * XROWS, LANES), jnp.uint32),
                            pltpu.SemaphoreType.DMA((X_SLOTS,)),
                            pltpu.VMEM((2, d, de), F32), pltpu.VMEM((2, d, de), F32), pltpu.VMEM((2, de, d), F32),
                            pltpu.SemaphoreType.DMA((2,)),
                            pltpu.VMEM((d, de), BF16), pltpu.VMEM((d, de), BF16), pltpu.VMEM((de, d), BF16),
                            pltpu.VMEM((XROWS, 2 * tb, LANES), F32)]),
        out_shape=jax.ShapeDtypeStruct((nblk * tb * XROWS, LANES), jnp.uint32),
        compiler_params=_params(1),
        name="experts",
    )(blk_order, blk_fill, used_seq, nact, xs, wg, wu, wd)


def _combine(dest_ref, ys_ref, meta_ref, buf_ref, sem_ref, pair_s, tm):
    i = pl.program_id(0)
    n = pl.num_programs(0)
    slot = i % COMB_SLOTS
    ahead = COMB_SLOTS - 1
    t = dest_ref.shape[0] // TOP_K

    def copy(tile, sl, r, k):
        d = dest_ref[k * t + tile * tm + r]
        row = r * XROWS if isinstance(r, int) else pl.multiple_of(r * XROWS, XROWS)
        return pltpu.make_async_copy(
            ys_ref.at[pl.ds(pl.multiple_of(d * XROWS, XROWS), XROWS), :],
            buf_ref.at[sl, k, pl.ds(row, XROWS), :],
            sem_ref.at[sl])

    def start_tile(tile, sl):
        def body(r, c):
            for k in range(TOP_K):
                copy(tile, sl, r, k).start(priority=k)
            return c
        lax.fori_loop(0, tm, body, 0, unroll=DMA_UNROLL)

    def wait_tile(sl):
        for k in range(TOP_K):
            pltpu.make_async_copy(ys_ref.at[pl.ds(0, tm * XROWS), :], buf_ref.at[sl, k], sem_ref.at[sl]).wait()

    @pl.when(i == 0)
    def _():
        for a in range(ahead):
            start_tile(jnp.minimum(a, n - 1), a)

    wait_tile(slot)

    nxt = jnp.minimum(i + ahead, n - 1)
    nslot = (i + ahead) % COMB_SLOTS

    def issue(part, nparts):
        rows = tm // nparts
        for r in range(part * rows, (part + 1) * rows):
            for k in range(TOP_K):
                copy(nxt, nslot, r, k).start(priority=k)

    @pl.when(i == n - 1)
    def _():
        for a in range(1, ahead):
            wait_tile((i + a) % COMB_SLOTS)

    y = None
    for k in range(TOP_K):
        gate = meta_ref[:, 2 * TOP_K + k:2 * TOP_K + k + 1]
        term = _unpack_rows(buf_ref.at[slot, k], tm, pair_s.at[k]) * gate
        y = term if y is None else y + term
    return y, issue


def _combine_drain(buf_ref, sem_ref, ys_ref, tm):
    i = pl.program_id(0)

    @pl.when(i == pl.num_programs(0) - 1)
    def _():
        sl = (i + COMB_SLOTS - 1) % COMB_SLOTS
        for k in range(TOP_K):
            pltpu.make_async_copy(ys_ref.at[pl.ds(0, tm * XROWS), :], buf_ref.at[sl, k], sem_ref.at[sl]).wait()


def _combine_scratch(tm):
    return [pltpu.VMEM((COMB_SLOTS, TOP_K, tm * XROWS, LANES), jnp.uint32), pltpu.SemaphoreType.DMA((COMB_SLOTS,)),
            pltpu.VMEM((TOP_K, XROWS, 2 * tm, LANES), F32)]


def _comb_proj1_kernel(dest_ref, h_ref, meta_ref, ys_ref, g_ref, w_ref, lbl_ref,
                       h2_ref, qs_ref, lf_ref, iv_ref, gs_ref, buf_ref, sem_ref, pair_s, *, lb_rows):
    tm = h_ref.shape[0]
    y, issue = _combine(dest_ref, ys_ref, meta_ref, buf_ref, sem_ref, pair_s, tm)
    h2 = h_ref[...] + y
    h2_ref[...] = h2
    hn = _rms(h2, g_ref[...]).astype(BF16)
    d = h2.shape[1]
    lbl = lbl_ref[...]
    ex = jnp.exp(lbl - jnp.max(lbl, axis=0, keepdims=True))
    lb = jnp.sum(ex[:lb_rows], axis=0, keepdims=True) / jnp.sum(ex, axis=0, keepdims=True)

    def col(c):
        return _dot(hn, w_ref[:, d * c:d * (c + 1)])

    q = col(0)
    issue(0, 4)
    fz = col(1)
    issue(1, 4)
    iv = col(2)
    issue(2, 4)
    gz = col(3)
    issue(3, 4)
    qs_ref[...] = (q * jax.nn.sigmoid(q)).astype(BF16)
    lf_ref[...] = jnp.log(lb + (1.0 - lb) * jax.nn.sigmoid(fz)) * LOG2E
    iv_ref[...] = iv.astype(BF16)
    gs_ref[...] = (gz * jax.nn.sigmoid(gz)).astype(BF16)
    _combine_drain(buf_ref, sem_ref, ys_ref, tm)


def _comb_proj1(dest, h, meta, ys, g, w, lb_logits, lb_rows):
    t, d = h.shape
    tm = min(TM_COMB, t)
    row = lambda i, dd: (i, 0)
    const = lambda shape: pl.BlockSpec(shape, lambda i, dd: (0,) * len(shape))
    return pl.pallas_call(
        functools.partial(_comb_proj1_kernel, lb_rows=lb_rows),
        grid_spec=pltpu.PrefetchScalarGridSpec(
            num_scalar_prefetch=1,
            grid=(t // tm,),
            in_specs=[pl.BlockSpec((tm, d), row), pl.BlockSpec((tm, LANES), row),
                      pl.BlockSpec(memory_space=pl.ANY),
                      const(g.shape), const(w.shape), const(lb_logits.shape)],
            out_specs=[pl.BlockSpec((tm, d), row)] * 5,
            scratch_shapes=_combine_scratch(tm)),
        out_shape=[jax.ShapeDtypeStruct((t, d), F32), jax.ShapeDtypeStruct((t, d), BF16),
                   jax.ShapeDtypeStruct((t, d), F32), jax.ShapeDtypeStruct((t, d), BF16),
                   jax.ShapeDtypeStruct((t, d), BF16)],
        compiler_params=_params(1),
        name="comb_proj1",
    )(dest, h, meta, ys, g, w, lb_logits)


def _comb_final_kernel(dest_ref, h_ref, meta_ref, ys_ref, g_ref, o_ref, buf_ref, sem_ref, pair_s):
    tm = h_ref.shape[0]
    y, issue = _combine(dest_ref, ys_ref, meta_ref, buf_ref, sem_ref, pair_s, tm)
    issue(0, 1)
    o_ref[...] = _rms(h_ref[...] + y, g_ref[...])
    _combine_drain(buf_ref, sem_ref, ys_ref, tm)


def _comb_final(dest, h, meta, ys, g):
    t, d = h.shape
    tm = min(TM_COMB, t)
    row = lambda i, dd: (i, 0)
    return pl.pallas_call(
        _comb_final_kernel,
        grid_spec=pltpu.PrefetchScalarGridSpec(
            num_scalar_prefetch=1,
            grid=(t // tm,),
            in_specs=[pl.BlockSpec((tm, d), row), pl.BlockSpec((tm, LANES), row),
                      pl.BlockSpec(memory_space=pl.ANY),
                      pl.BlockSpec(g.shape, lambda i, dd: (0, 0))],
            out_specs=pl.BlockSpec((tm, d), row),
            scratch_shapes=_combine_scratch(tm)),
        out_shape=jax.ShapeDtypeStruct((t, d), F32),
        compiler_params=_params(1),
        name="comb_final",
    )(dest, h, meta, ys, g)


def _hgrn_kernel(qs_ref, lf_ref, iv_ref, gs_ref, ng_ref, o_ref, st_ref, d1_s, t16_s, *, dk):
    th = qs_ref.shape[0]
    nsub = CHUNK // SUB

    @pl.when(pl.program_id(1) == 0)
    def _():
        st_ref[...] = jnp.zeros_like(st_ref)

    rr = lax.broadcasted_iota(jnp.int32, (th, th), 0)
    cc = lax.broadcasted_iota(jnp.int32, (th, th), 1)
    same_sub = (rr // SUB) == (cc // SUB)
    same_chunk = (rr // CHUNK) == (cc // CHUNK)
    m_diag = same_sub & (cc <= rr)
    dsub = rr // SUB - cc // SUB
    m_off = [same_chunk & (dsub == dd) for dd in range(1, nsub)]

    hi, lo = _split_bf16(lf_ref[...])
    tri = jnp.where(m_diag, 1.0, 0.0).astype(BF16)
    d1 = _dot(tri, hi) + _dot(tri, lo)
    d1_s[...] = d1
    sub3 = (th // SUB, SUB, d1.shape[1])
    t16_s[...] = jnp.broadcast_to(d1.reshape(sub3)[:, SUB - 1:SUB, :], sub3).reshape(th, d1.shape[1])
    rrow = lax.broadcasted_iota(jnp.int32, (th, dk), 0)
    rsub = (rrow // SUB) % nsub
    rchunk = rrow // CHUNK
    nch = th // CHUNK

    def back_rows(dd):
        return [slice(CHUNK * c + SUB * dd, CHUNK * (c + 1)) for c in range(nch)]

    def prepare(hd):
        ls = pl.ds(hd * dk, dk)
        d1 = d1_s[:, ls]
        t16 = t16_s[:, ls]
        q = qs_ref[:, ls].astype(F32)
        kt = 1.0 - jnp.exp2(lf_ref[:, ls])
        suf = t16 - d1
        kx = (kt * jnp.exp2(suf)).astype(BF16)
        qv = [q * jnp.exp2(-suf)]
        acc = d1
        tail = suf
        for dd in range(1, nsub):
            qv.append(jnp.concatenate([q[sl] * jnp.exp2(acc[sl]) for sl in back_rows(dd)], axis=0))
            acc = acc + jnp.where(rsub >= dd, pltpu.roll(t16, SUB * dd, 0), 0.0)
            tail = tail + jnp.where(rsub < nsub - dd, pltpu.roll(t16, th - SUB * dd, 0), 0.0)
        b = acc
        qb = q * jnp.exp2(b)
        kend = kt * jnp.exp2(tail)
        zero = jnp.zeros_like(q)
        kend_x = jnp.concatenate([jnp.where(rchunk == c, kend, zero) for c in range(nch)], axis=1).astype(BF16)
        qb_x = jnp.concatenate([jnp.where(rchunk == c, qb, zero) for c in range(nch)], axis=1).astype(BF16)
        decs = [jnp.exp2(b[CHUNK * (c + 1) - 1:CHUNK * (c + 1), :]) for c in range(nch)]
        return ls, jnp.concatenate(qv, axis=0).astype(BF16), kx, kend_x, qb_x, decs

    def group(gi, carry):
        heads = [gi * HGRN_GROUP + u for u in range(HGRN_GROUP)]
        prep = [prepare(hd) for hd in heads]
        a4s = [_dot_nt(p[1], p[2]) for p in prep]
        incs = [_dot_tn(iv_ref[:, p[0]], p[3]) for p in prep]
        o_intras = []
        for p, a4 in zip(prep, a4s):
            att = jnp.where(m_diag, a4[:th], 0.0)
            at = th
            for dd in range(1, nsub):
                pieces = []
                for sl in back_rows(dd):
                    n = sl.stop - sl.start
                    pieces += [jnp.zeros((SUB * dd, th), F32), a4[at:at + n]]
                    at += n
                att = jnp.where(m_off[dd - 1], jnp.concatenate(pieces, axis=0), att)
            o_intras.append(_dot(att.astype(BF16), iv_ref[:, p[0]]))
        for hd, p, inc, o_intra in zip(heads, prep, incs, o_intras):
            ls, decs = p[0], p[5]
            st = st_ref[hd]
            starts = []
            for c in range(nch):
                starts.append(st)
                st = st * decs[c] + inc[:, dk * c:dk * (c + 1)]
            st_ref[hd] = st
            oh = o_intra + _dot_nt(p[4], jnp.concatenate(starts, axis=1).astype(BF16))
            on = oh * lax.rsqrt(jnp.mean(oh * oh, axis=-1, keepdims=True) + EPS)
            o_ref[:, ls] = (on * ng_ref[:, ls] * gs_ref[:, ls].astype(F32)).astype(BF16)
        return carry

    for gi in range(qs_ref.shape[1] // dk // HGRN_GROUP):
        group(gi, 0)


def _hgrn(qs, lf, iv, gs, ng, batch, dk):
    t, d = qs.shape
    s = t // batch
    th = min(TH, s)
    blk = pl.BlockSpec((None, th, d), lambda b, j: (b, j, 0))
    r3 = lambda a: a.reshape(batch, s, d)
    out = pl.pallas_call(
        functools.partial(_hgrn_kernel, dk=dk),
        grid=(batch, s // th),
        in_specs=[blk, blk, blk, blk, _const_spec(ng.shape)],
        out_specs=blk,
        out_shape=jax.ShapeDtypeStruct((batch, s, d), BF16),
        scratch_shapes=[pltpu.VMEM((d // dk, dk, dk), F32), pltpu.VMEM((th, d), F32), pltpu.VMEM((th, d), F32)],
        compiler_params=_params(2),
        name="hgrn2",
    )(r3(qs), r3(lf), r3(iv), r3(gs), ng)
    return out.reshape(t, d)


def _block_diag(w):
    n, c, dd = w.shape
    eye = jnp.eye(n, dtype=w.dtype)
    return (eye[:, None, :, None] * w[:, :, None, :]).reshape(n * c, n * dd)


def _router_weights(wg, we):
    d = wg.shape[0]
    w = jnp.concatenate([wg, we, jnp.zeros((d, LANES - wg.shape[1] - we.shape[1]), F32)], axis=1).T
    hi = w.astype(BF16)
    return jnp.concatenate([hi, (w - hi.astype(F32)).astype(BF16)], axis=0)


def _moe(layer, meta_t, cnt, hn_slab, moe_w_gate, moe_w_up, moe_w_down):
    dest, pad_lo, pad_hi, blk_order, blk_fill, used_seq, nact, nblk = _plan(meta_t, cnt, TB_EXPERT)
    xs = _dispatch(dest, pad_lo, pad_hi, nact, hn_slab, nblk, TB_EXPERT)
    ys = _experts(blk_order, blk_fill, used_seq, nact, xs, moe_w_gate, moe_w_up, moe_w_down, layer, TB_EXPERT)
    return dest, ys


def kernel(x, norm_mix_g, norm_ffn_g, norm_final_g, ab_w_in, ab_conv_w, ab_conv_b, rg_w_a, rg_b_a, rg_w_x, rg_b_x, rg_lambda, attn_rel_bias, ab_w_out, c_w_in, c_lb_logits, c_norm_g, c_w_out, moe_router_group, moe_router_expert, moe_w_gate, moe_w_up, moe_w_down):
    batch, seq, d = x.shape
    t = batch * seq
    xt = x.reshape(t, d)
    row = lambda v: v.reshape(1, -1)

    aw = ab_conv_w.shape[2]
    qkv, ya = _proj0_lru(x, row(norm_mix_g[0]), ab_w_in[0].astype(BF16), ab_conv_w[0], row(ab_conv_b[0]),
                         _block_diag(rg_w_a[0]).astype(BF16), _block_diag(rg_w_x[0]).astype(BF16),
                         row(rg_b_a[0]), row(rg_b_x[0]), row(rg_lambda[0]))
    tk = TQ + LEFT_CHUNKS * CHUNK
    dist = jnp.clip(tk - jnp.arange(TQ + tk), -REL_CLIP, REL_CLIP) + REL_CLIP
    bias = _attn_bias(attn_rel_bias[0][:, None, dist])
    yb = _attention(qkv, bias, batch)
    wo = ab_w_out[0].astype(BF16)
    wt = _router_weights(moe_router_group[0], moe_router_expert[0])
    h1, hn1, meta1, metat1, cnt1 = _proj_route(ya, 0, yb, 0, xt, wo[:aw], wo[aw:], row(norm_ffn_g[0]), wt)
    dest1, ys1 = _moe(0, metat1, cnt1, hn1, moe_w_gate, moe_w_up, moe_w_down)

    dk = c_norm_g.shape[1] // 8
    h2, qs, lf, iv, gs = _comb_proj1(dest1, h1, meta1, ys1, row(norm_mix_g[1]), c_w_in[0].astype(BF16),
                                     c_lb_logits, 1)
    om = _hgrn(qs, lf, iv, gs, row(c_norm_g[0]), batch, dk)
    wo = c_w_out[0].astype(BF16)
    half = wo.shape[0] // 2
    wt = _router_weights(moe_router_group[1], moe_router_expert[1])
    h3, hn3, meta3, metat3, cnt3 = _proj_route(om, 0, om, 1, h2, wo[:half], wo[half:], row(norm_ffn_g[1]), wt)
    dest3, ys3 = _moe(1, metat3, cnt3, hn3, moe_w_gate, moe_w_up, moe_w_down)

    out = _comb_final(dest3, h3, meta3, ys3, row(norm_final_g))
    return out.reshape(batch, seq, d)
```

```python
import functools

import jax
import jax.numpy as jnp
from jax import lax
from jax.experimental import pallas as pl
from jax.experimental.pallas import tpu as pltpu

F32 = jnp.float32
BF16 = jnp.bfloat16

EPS = 1e-6
LOG2E = 1.4426950408889634
RG_C = 8.0
CHUNK = 64
LEFT_CHUNKS = 8
REL_CLIP = 256
N_GROUPS = 4
EXPERTS_PER_GROUP = 8
N_EXPERTS = N_GROUPS * EXPERTS_PER_GROUP
TOP_K = 2
ROUTE_ROWS = 40

LANES = 128
SUBLANES = 8
XROWS = 4
SUB = 16

TS_LRU = 512
TQ = 256
ATTN_GROUP = 4
TM_ROUTE = 1024
TM_ROUTE_SUB = 256
TB_EXPERT = 512
X_SLOTS = 3
TM_COMB = 256
TM_COMB_PROJ = 512
TM_COMB_SUB = 256
PLAN_TILE = 2048
TH = 256
HGRN_GROUP = 4
DMA_UNROLL = 8
COMB_SLOTS = 3
VMEM_MB = 48


def _params(n_axes, vmem_mb=VMEM_MB):
    return pltpu.CompilerParams(dimension_semantics=("arbitrary",) * n_axes,
                                vmem_limit_bytes=vmem_mb * 1024 * 1024)


def _const_spec(shape):
    nd = len(shape)
    return pl.BlockSpec(shape, lambda *_: (0,) * nd)


def _rms(x, g):
    return x * lax.rsqrt(jnp.mean(x * x, axis=-1, keepdims=True) + EPS) * g


def _dot(a, b):
    return jnp.dot(a, b, preferred_element_type=F32)


def _dot_nt(a, b):
    return lax.dot_general(a, b, (((1,), (1,)), ((), ())), preferred_element_type=F32)


def _dot_tn(a, b):
    return lax.dot_general(a, b, (((0,), (0,)), ((), ())), preferred_element_type=F32)


def _split_bf16(x):
    hi = x.astype(BF16)
    lo = (x - hi.astype(F32)).astype(BF16)
    return hi, lo


def _pack_rows(ref, val, pair_s):
    m = val.shape[0]
    for j in range(XROWS):
        pair_s[j, pl.ds(0, m, stride=2), :] = val[:, LANES * j:LANES * (j + 1)]
        pair_s[j, pl.ds(1, m, stride=2), :] = val[:, LANES * (j + XROWS):LANES * (j + XROWS + 1)]
        ref[pl.ds(j, m, stride=XROWS), :] = pltpu.bitcast(pair_s[j, pl.ds(0, 2 * m), :].astype(BF16), jnp.uint32)


def _packed_zeros(m):
    return pltpu.bitcast(jnp.zeros((2 * m * XROWS, LANES), BF16), jnp.uint32)


def _unpack_rows(ref, m, pair_s):
    lo, hi = [], []
    for j in range(XROWS):
        pair_s[j, pl.ds(0, 2 * m), :] = pltpu.bitcast(ref[pl.ds(j, m, stride=XROWS), :], BF16).astype(F32)
        lo.append(pair_s[j, pl.ds(0, m, stride=2), :])
        hi.append(pair_s[j, pl.ds(1, m, stride=2), :])
    return jnp.concatenate(lo + hi, axis=1)


def _proj0_lru_kernel(x_ref, g_ref, w_ref, cw_ref, cb_ref, wa_ref, wx_ref, ba_ref, bx_ref, lam_ref,
                      qkv_ref, ya_ref, xbuf, h_ref, a_s, u_s):
    ts = x_ref.shape[0]
    w = ya_ref.shape[1]
    bw = qkv_ref.shape[1] // 3

    @pl.when(pl.program_id(1) == 0)
    def _():
        xbuf[pl.ds(0, SUBLANES), :] = jnp.zeros((SUBLANES, w), F32)
        h_ref[...] = jnp.zeros_like(h_ref)

    hn = _rms(x_ref[...], g_ref[...]).astype(BF16)

    def qkv_cols(c, scale=None):
        z = _dot(hn, w_ref[:, 2 * w + c * bw:2 * w + (c + 1) * bw])
        qkv_ref[:, c * bw:(c + 1) * bw] = (z if scale is None else z * scale).astype(BF16)

    xg = _dot(hn, w_ref[:, :2 * w])

    xbuf[pl.ds(SUBLANES, ts), :] = xg[:, :w]
    nk = cw_ref.shape[0]
    y = cb_ref[...]
    for k in range(nk):
        y = y + cw_ref[nk - 1 - k:nk - k, :] * xbuf[pl.ds(SUBLANES - k, ts), :]
    xbuf[pl.ds(0, SUBLANES), :] = xbuf[pl.ds(ts, SUBLANES), :]

    yb = y.astype(BF16)
    rg = jax.nn.sigmoid(_dot(yb, wa_ref[...]) + ba_ref[...])
    ig = jax.nn.sigmoid(_dot(yb, wx_ref[...]) + bx_ref[...])
    qkv_cols(0, CHUNK ** -0.5 * LOG2E)
    lam = lam_ref[...]
    log_sig = jnp.minimum(lam, 0.0) - jnp.log1p(jnp.exp(-jnp.abs(lam)))
    log_a = RG_C * rg * log_sig
    a = jnp.exp(log_a)
    m = 1.0 - a * a
    u = jnp.where(m > 0.0, m * lax.rsqrt(m), 0.0) * (ig * y)
    qkv_cols(1)

    grp = (ts // SUBLANES, SUBLANES, w)
    a = a.reshape(grp)
    u = u.reshape(grp)
    rowm = lax.broadcasted_iota(jnp.int32, grp, 1)
    for s in (1, 2, 4):
        keep = rowm >= s
        a_sh = jnp.where(keep, pltpu.roll(a, s, 1), 1.0)
        u_sh = jnp.where(keep, pltpu.roll(u, s, 1), 0.0)
        u = a * u_sh + u
        a = a * a_sh
    a_s[...] = a.reshape(ts, w)
    u_s[...] = u.reshape(ts, w)
    qkv_cols(2)

    def group(gi, h):
        off = pl.multiple_of(gi * SUBLANES, SUBLANES)
        hg = a_s[pl.ds(off, SUBLANES), :] * h + u_s[pl.ds(off, SUBLANES), :]
        u_s[pl.ds(off, SUBLANES), :] = hg
        return jnp.broadcast_to(hg[SUBLANES - 1:SUBLANES, :], hg.shape)

    h_ref[...] = lax.fori_loop(0, ts // SUBLANES, group, h_ref[...])

    ga = xg[:, w:]
    gelu = 0.5 * ga * (1.0 + jnp.tanh(0.7978845608028654 * (ga + 0.044715 * (ga * ga * ga))))
    ya_ref[...] = (u_s[...] * gelu).astype(BF16)


def _proj0_lru(x, g, w_in, cw, cb, wa, wx, ba, bx, lam):
    batch, s, d = x.shape
    w = cw.shape[1]
    nqkv = w_in.shape[1] - 2 * w
    ts = min(TS_LRU, s)
    small = [g, w_in, cw, cb, wa, wx, ba, bx, lam]
    blk = lambda width: pl.BlockSpec((None, ts, width), lambda b, j: (b, j, 0))
    qkv, ya = pl.pallas_call(
        _proj0_lru_kernel,
        grid=(batch, s // ts),
        in_specs=[blk(d)] + [_const_spec(a.shape) for a in small],
        out_specs=[blk(nqkv), blk(w)],
        out_shape=[jax.ShapeDtypeStruct((batch, s, nqkv), BF16), jax.ShapeDtypeStruct((batch, s, w), BF16)],
        scratch_shapes=[pltpu.VMEM((ts + SUBLANES, w), F32), pltpu.VMEM((SUBLANES, w), F32),
                        pltpu.VMEM((ts, w), F32), pltpu.VMEM((ts, w), F32)],
        compiler_params=_params(2),
        name="proj0_lru",
    )(x, *small)
    return qkv.reshape(batch * s, nqkv), ya.reshape(batch * s, w)


def _attn_bias_kernel(base_ref, o_ref):
    tq, tk = o_ref.shape
    nkb = tk // tq
    row = base_ref[...]
    full = pltpu.roll(jnp.broadcast_to(row, (tq, tq + tk)), 0, 1, stride=1, stride_axis=0)
    qc = lax.broadcasted_iota(jnp.int32, (tq, tk), 0) // CHUNK
    col = lax.broadcasted_iota(jnp.int32, (tq, tk), 1)
    kc = col // CHUNK
    first_ok = (nkb - 1 - pl.program_id(0)) * tq
    valid = (kc >= qc) & (kc <= qc + LEFT_CHUNKS) & (col >= first_ok)
    o_ref[...] = jnp.where(valid, full[:, tq:] * LOG2E, -1e30)


def _attn_bias(base):
    h = base.shape[0]
    tk = TQ + LEFT_CHUNKS * CHUNK
    return pl.pallas_call(
        _attn_bias_kernel,
        grid=(tk // TQ, h),
        in_specs=[pl.BlockSpec((None, 1, TQ + tk), lambda v, i: (i, 0, 0))],
        out_specs=pl.BlockSpec((None, None, TQ, tk), lambda v, i: (v, i, 0, 0)),
        out_shape=jax.ShapeDtypeStruct((tk // TQ, h, TQ, tk), F32),
        compiler_params=_params(2),
        name="attn_bias",
    )(base)


def _attn_kernel(q_ref, k0_ref, k1_ref, k2_ref, v0_ref, v1_ref, v2_ref, bias_ref, o_ref):
    tq = q_ref.shape[0]
    half_w = LANES // 2
    lo = lax.broadcasted_iota(jnp.int32, (tq, LANES), 1) < half_w
    lo_k = lax.broadcasted_iota(jnp.int32, (3 * tq, LANES), 1) < half_w
    k_refs = (k0_ref, k1_ref, k2_ref)
    v_refs = (v0_ref, v1_ref, v2_ref)
    npair = q_ref.shape[1] // LANES
    for j0 in range(0, npair, ATTN_GROUP):
        pairs = range(j0, min(j0 + ATTN_GROUP, npair))
        qk, vh = {}, {}
        for j in pairs:
            sl = slice(LANES * j, LANES * (j + 1))
            q2 = q_ref[:, sl]
            kcat = jnp.concatenate([r[:, sl] for r in k_refs], axis=0)
            vcat = jnp.concatenate([r[:, sl] for r in v_refs], axis=0)
            one = jnp.ones_like(vcat)
            zero = jnp.zeros_like(q2)
            vh[j] = (jnp.where(lo_k, vcat, one), jnp.where(lo_k, one, vcat))
            qk[j] = [_dot_nt(jnp.where(lo if half == 0 else jnp.logical_not(lo), q2, zero), kcat)
                     for half in range(2)]
        ps = {}
        for j in pairs:
            ps[j] = []
            for half in range(2):
                s = qk[j][half] + bias_ref[2 * j + half]
                ps[j].append(jnp.exp2(s - jnp.max(s, axis=-1, keepdims=True)).astype(BF16))
        for j in pairs:
            pv = [_dot(ps[j][half], vh[j][half]) for half in range(2)]
            num = jnp.where(lo, pv[0], pv[1])
            den = pltpu.roll(jnp.where(lo, pv[1], pv[0]), half_w, 1)
            o_ref[:, LANES * j:LANES * (j + 1)] = (num / den).astype(BF16)


def _attention(qkv, bias, batch):
    t, w3 = qkv.shape
    w = w3 // 3
    s = t // batch
    qkv3 = qkv.reshape(batch, s, w3)
    nvar = bias.shape[0]

    def kv_spec(colblk, back):
        return pl.BlockSpec((None, TQ, w), lambda i, b: (b, jnp.maximum(i - back, 0), colblk))

    out = pl.pallas_call(
        _attn_kernel,
        grid=(s // TQ, batch),
        in_specs=[pl.BlockSpec((None, TQ, w), lambda i, b: (b, i, 0)),
                  kv_spec(1, 2), kv_spec(1, 1), kv_spec(1, 0),
                  kv_spec(2, 2), kv_spec(2, 1), kv_spec(2, 0),
                  pl.BlockSpec((None,) + bias.shape[1:], lambda i, b: (jnp.minimum(i, nvar - 1), 0, 0, 0))],
        out_specs=pl.BlockSpec((None, TQ, w), lambda i, b: (b, i, 0)),
        out_shape=jax.ShapeDtypeStruct((batch, s, w), BF16),
        compiler_params=_params(2),
        name="attention",
    )(qkv3, qkv3, qkv3, qkv3, qkv3, qkv3, qkv3, bias)
    return out.reshape(t, w)


def _proj_route_kernel(a_ref, b_ref, r_ref, wa_ref, wb_ref, g_ref, wt_ref,
                       h_ref, hn_ref, meta_ref, metat_ref, cnt_ref, carry_ref, pair_s):
    tm = a_ref.shape[0]
    ts = min(TM_ROUTE_SUB, tm)

    @pl.when(pl.program_id(0) == 0)
    def _():
        carry_ref[...] = jnp.zeros_like(carry_ref)

    wt = wt_ref[...]
    ridx = lax.broadcasted_iota(jnp.int32, (ROUTE_ROWS, ts), 0).astype(F32)
    r8 = lax.broadcasted_iota(jnp.int32, (SUBLANES, ts), 0)
    rr = lax.broadcasted_iota(jnp.int32, (ts, ts), 0)
    cc = lax.broadcasted_iota(jnp.int32, (ts, ts), 1)
    utri = jnp.where(rr < cc, 1.0, 0.0).astype(BF16)
    neg = -jnp.inf
    far = float(LANES)
    carry = carry_ref[...]

    all_logits = []
    for sub in range(tm // ts):
        rows = pl.ds(sub * ts, ts)
        h = r_ref[rows, :] + (_dot(a_ref[rows, :], wa_ref[...]) + _dot(b_ref[rows, :], wb_ref[...]))
        h_ref[rows, :] = h
        hn = _rms(h, g_ref[...])
        _pack_rows(hn_ref.at[pl.ds(sub * ts * XROWS, ts * XROWS), :], hn, pair_s)

        hi, lo = _split_bf16(hn)
        p_hi = _dot_nt(wt, hi)
        p_lo = _dot_nt(wt[:LANES], lo)
        all_logits.append((p_hi[:ROUTE_ROWS] + p_hi[LANES:LANES + ROUTE_ROWS]) + p_lo[:ROUTE_ROWS])

    for sub, logits in enumerate(all_logits):
        rows = pl.ds(sub * ts, ts)

        def first_max(mask, logits=logits):
            v = jnp.max(jnp.where(mask, logits, neg), axis=0, keepdims=True)
            idx = jnp.min(jnp.where(mask & (logits == v), ridx, far), axis=0, keepdims=True)
            return v, idx

        gmask = ridx < N_GROUPS
        gmax, gidx = first_max(gmask)
        g_gate = 1.0 / jnp.sum(jnp.where(gmask, jnp.exp(logits - gmax), 0.0), axis=0, keepdims=True)
        e_lo = N_GROUPS + EXPERTS_PER_GROUP * gidx
        emask = (ridx >= e_lo) & (ridx < e_lo + EXPERTS_PER_GROUP)
        v1, i1 = first_max(emask)
        v2, i2 = first_max(emask & (ridx != i1))
        tt = jnp.exp(v2 - v1)
        w1 = g_gate / (1.0 + tt)
        w2 = g_gate * tt / (1.0 + tt)

        sel1 = ridx == i1
        sel2 = ridx == i2
        onehot = jnp.where(sel1 | sel2, 1.0, 0.0)
        before = _dot(onehot.astype(BF16), utri) + carry[:, 0:1]
        rank1 = jnp.sum(jnp.where(sel1, before, 0.0), axis=0, keepdims=True)
        rank2 = jnp.sum(jnp.where(sel2, before, 0.0), axis=0, keepdims=True)
        carry = carry + jnp.sum(onehot, axis=1, keepdims=True)

        mt = jnp.zeros((SUBLANES, ts), F32)
        for c, val in enumerate((i1 - N_GROUPS, i2 - N_GROUPS, rank1, rank2, w1, w2)):
            mt = jnp.where(r8 == c, val, mt)
        metat_ref[:, rows] = mt
        meta_ref[rows, :] = jnp.concatenate([mt, jnp.zeros((LANES - SUBLANES, ts), F32)], axis=0).T

    carry_ref[...] = carry
    cnt_ref[...] = carry


def _proj_route(a, acol, b, bcol, resid, wa, wb, g, wt):
    t, d = resid.shape
    kw = wa.shape[0]
    tm = min(TM_ROUTE, t)
    return pl.pallas_call(
        _proj_route_kernel,
        grid=(t // tm,),
        in_specs=[pl.BlockSpec((tm, kw), lambda i: (i, acol)), pl.BlockSpec((tm, kw), lambda i: (i, bcol)),
                  pl.BlockSpec((tm, d), lambda i: (i, 0)),
                  _const_spec(wa.shape), _const_spec(wb.shape), _const_spec(g.shape), _const_spec(wt.shape)],
        out_specs=[pl.BlockSpec((tm, d), lambda i: (i, 0)),
                   pl.BlockSpec((tm * XROWS, LANES), lambda i: (i, 0)),
                   pl.BlockSpec((tm, LANES), lambda i: (i, 0)),
                   pl.BlockSpec((SUBLANES, tm), lambda i: (0, i)),
                   pl.BlockSpec((ROUTE_ROWS, LANES), lambda i: (0, 0))],
        out_shape=[jax.ShapeDtypeStruct((t, d), F32),
                   jax.ShapeDtypeStruct((t * XROWS, LANES), jnp.uint32),
                   jax.ShapeDtypeStruct((t, LANES), F32),
                   jax.ShapeDtypeStruct((SUBLANES, t), F32),
                   jax.ShapeDtypeStruct((ROUTE_ROWS, LANES), F32)],
        scratch_shapes=[pltpu.VMEM((ROUTE_ROWS, LANES), F32),
                        pltpu.VMEM((XROWS, 2 * min(TM_ROUTE_SUB, tm), LANES), F32)],
        compiler_params=_params(1),
        name="proj_route",
    )(a, b, resid, wa, wb, g, wt)


def _plan_kernel(cnt_ref, meta_ref, dest_ref, tab_ref, start_s, *, tb):
    r_n = ROUTE_ROWS
    tab_w = tab_ref.shape[1]

    @pl.when(pl.program_id(0) == 0)
    def _():
        row = lax.broadcasted_iota(jnp.int32, (r_n, LANES), 0)
        lane = lax.broadcasted_iota(jnp.int32, (r_n, LANES), 1)

        def prefix(v):
            s = 1
            while s < r_n:
                v = v + jnp.where(row >= s, pltpu.roll(v, s, 0), 0.0)
                s *= 2
            return v

        counts = cnt_ref[...]
        padded = jnp.floor((counts + (tb - 1)) * (1.0 / tb)) * tb
        pad_end = prefix(padded)
        pad_start = pad_end - padded
        pad_lo = pad_start + counts
        start_s[...] = pad_start
        used = padded > 0.0
        order = prefix(jnp.where(used, 1.0, 0.0)) - 1.0
        nused = order[r_n - 1:r_n, :] + 1.0
        nact = pad_end[r_n - 1:r_n, :] * (1.0 / tb)
        expert = (row - N_GROUPS).astype(F32)
        lane_f = lane.astype(F32)
        used_seq = jnp.sum(jnp.where(used & (order == jnp.minimum(lane_f, nused - 1.0)), expert, 0.0),
                           axis=0, keepdims=True)

        erow = lax.broadcasted_iota(jnp.int32, (r_n, tab_w), 0)
        blk_start = lax.broadcasted_iota(jnp.int32, (r_n, tab_w), 1).astype(F32) * tb
        is_expert = (erow >= N_GROUPS) & (erow < N_GROUPS + N_EXPERTS)
        ended = jnp.where(is_expert & (pad_end[:, 0:1] <= blk_start), 1.0, 0.0)
        blk_expert = jnp.minimum(jnp.sum(ended, axis=0, keepdims=True), N_EXPERTS - 1.0)
        mine = (erow - N_GROUPS).astype(F32) == blk_expert
        blk_order = jnp.sum(jnp.where(mine, order[:, 0:1], 0.0), axis=0, keepdims=True)
        blk_lo = jnp.sum(jnp.where(mine, pad_lo[:, 0:1], 0.0), axis=0, keepdims=True)
        blk_fill = jnp.clip(blk_lo - blk_start[0:1, :], 0.0, float(tb))

        cols = jnp.where(lane == 0, pad_lo, jnp.where(lane == 1, pad_end, 0.0))
        rows_t = jnp.concatenate([cols, jnp.zeros((LANES - r_n, LANES), F32)], axis=0).T
        misc = jnp.where(lane[0:1, :] == 0, nact, jnp.where(lane[0:1, :] == 1, nused, 0.0))
        wide = lambda v: v if tab_w == LANES else jnp.concatenate(
            [v, jnp.zeros((v.shape[0], tab_w - LANES), F32)], axis=1)
        tab = jnp.concatenate([wide(rows_t[0:2, :]), wide(used_seq), wide(misc), blk_order, blk_fill,
                               jnp.zeros((2, tab_w), F32)], axis=0)
        tab_ref[...] = tab.astype(jnp.int32)

    for c in range(dest_ref.shape[1] // LANES):
        sl = slice(LANES * c, LANES * (c + 1))
        for k in range(TOP_K):
            expert = meta_ref[k:k + 1, sl]
            start = jnp.zeros((1, LANES), F32)
            for e in range(N_EXPERTS):
                start = jnp.where(expert == float(e), start_s[N_GROUPS + e:N_GROUPS + e + 1, :], start)
            dest_ref[k:k + 1, sl] = (start + meta_ref[TOP_K + k:TOP_K + k + 1, sl]).astype(jnp.int32)


def _plan(meta_t, cnt, tb):
    t = meta_t.shape[1]
    nblk = -(-(t * TOP_K + N_EXPERTS * (tb - 1)) // tb)
    tile = min(PLAN_TILE, t)
    tab_w = -(-nblk // LANES) * LANES
    dest, tab = pl.pallas_call(
        functools.partial(_plan_kernel, tb=tb),
        grid=(t // tile,),
        in_specs=[_const_spec(cnt.shape), pl.BlockSpec((SUBLANES, tile), lambda i: (0, i))],
        out_specs=[pl.BlockSpec((TOP_K, tile), lambda i: (0, i)), pl.BlockSpec((SUBLANES, tab_w), lambda i: (0, 0))],
        out_shape=[jax.ShapeDtypeStruct((TOP_K, t), jnp.int32), jax.ShapeDtypeStruct((SUBLANES, tab_w), jnp.int32)],
        scratch_shapes=[pltpu.VMEM((ROUTE_ROWS, LANES), F32)],
        compiler_params=_params(1),
        name="plan",
    )(cnt, meta_t)
    experts = slice(N_GROUPS, N_GROUPS + N_EXPERTS)
    return (dest.reshape(-1), tab[0, experts], tab[1, experts], tab[4, :nblk], tab[5, :nblk],
            tab[2, :N_EXPERTS], tab[3, :2], nblk)


def _dispatch_kernel(dest_ref, plo_ref, phi_ref, nact_ref, hn_ref, xs_ref, zero_ref, sem, zsem):
    tm = hn_ref.shape[0] // XROWS
    t = dest_ref.shape[0] // TOP_K
    tb = zero_ref.shape[0] // XROWS
    nblk = xs_ref.shape[0] // zero_ref.shape[0]
    i = pl.program_id(0)

    def zero_copy(row, nrows):
        return pltpu.make_async_copy(
            zero_ref.at[pl.ds(0, nrows * XROWS), :],
            xs_ref.at[pl.ds(pl.multiple_of(row * XROWS, XROWS), nrows * XROWS), :], zsem)

    def zero_fill(start):
        def go(cp):
            cp.start() if start else cp.wait()

        def per_expert(e, c):
            off = plo_ref[e]
            n = phi_ref[e] - off
            bit = tb // 2
            while bit:
                pl.when((n & bit) != 0)(functools.partial(lambda o, b: go(zero_copy(o, b)), off, bit))
                off = off + (n & bit)
                bit //= 2
            return c

        def per_block(b, c):
            go(zero_copy(b * tb, tb))
            return c

        lax.fori_loop(0, N_EXPERTS, per_expert, 0)
        lax.fori_loop(nact_ref[0], nblk, per_block, 0)

    @pl.when(i == 0)
    def _():
        zero_ref[...] = jnp.zeros_like(zero_ref)
        zero_fill(True)

    def copy(r, k):
        d = dest_ref[k * t + i * tm + r]
        return pltpu.make_async_copy(
            hn_ref.at[pl.ds(pl.multiple_of(r * XROWS, XROWS), XROWS), :],
            xs_ref.at[pl.ds(pl.multiple_of(d * XROWS, XROWS), XROWS), :], sem)

    def start(r, c):
        for k in range(TOP_K):
            copy(r, k).start(priority=k)
        return c

    lax.fori_loop(0, tm, start, 0, unroll=DMA_UNROLL)
    for k in range(TOP_K):
        pltpu.make_async_copy(hn_ref, xs_ref.at[pl.ds(0, tm * XROWS), :], sem).wait()

    @pl.when(i == 0)
    def _():
        zero_fill(False)


def _dispatch(dest, pad_lo, pad_hi, nact, hn_slab, nblk, tb):
    t = hn_slab.shape[0] // XROWS
    tm = min(TM_COMB, t)
    return pl.pallas_call(
        _dispatch_kernel,
        grid_spec=pltpu.PrefetchScalarGridSpec(
            num_scalar_prefetch=4,
            grid=(t // tm,),
            in_specs=[pl.BlockSpec((tm * XROWS, LANES), lambda i, *_: (i, 0))],
            out_specs=pl.BlockSpec(memory_space=pl.ANY),
            scratch_shapes=[pltpu.VMEM((tb * XROWS, LANES), jnp.uint32),
                            pltpu.SemaphoreType.DMA(()), pltpu.SemaphoreType.DMA(())]),
        out_shape=jax.ShapeDtypeStruct((nblk * tb * XROWS, LANES), jnp.uint32),
        compiler_params=_params(1),
        name="dispatch",
    )(dest, pad_lo, pad_hi, nact, hn_slab)


def _expert_kernel(ord_ref, fill_ref, useq_ref, nact_ref, xs_ref, wg_ref, wu_ref, wd_ref, ys_ref,
                   xbuf, xsem, wg_f, wu_f, wd_f, wsem, wg_s, wu_s, wd_s, pair_s, *, layer):
    i = pl.program_id(0)
    n = pl.num_programs(0)
    tb = ys_ref.shape[0] // XROWS
    rows = tb * XROWS
    active = i < nact_ref[0]
    order = ord_ref[i]
    changed = (i == 0) | (order != ord_ref[jnp.maximum(i - 1, 0)])
    slot = i % X_SLOTS
    ahead = X_SLOTS - 1

    def w_copies(j):
        e = useq_ref[jnp.minimum(j, useq_ref.shape[0] - 1)]
        ws = j % 2
        return [pltpu.make_async_copy(src.at[layer, e], dst.at[ws], wsem.at[ws])
                for src, dst in ((wg_ref, wg_f), (wu_ref, wu_f), (wd_ref, wd_f))]

    @pl.when(i == 0)
    def _():
        for cp in w_copies(0):
            cp.start()

    def x_copy(step, sl):
        blk = jnp.minimum(step, nact_ref[0] - 1)
        return pltpu.make_async_copy(xs_ref.at[pl.ds(pl.multiple_of(blk * rows, rows), rows), :],
                                     xbuf.at[sl], xsem.at[sl])

    @pl.when(i == 0)
    def _():
        for a in range(ahead):
            x_copy(a, a).start()

    x_copy(i, slot).wait()
    x_copy(i + ahead, (i + ahead) % X_SLOTS).start()

    @pl.when(i == n - 1)
    def _():
        for a in range(1, X_SLOTS):
            x_copy(i + a, (i + a) % X_SLOTS).wait()

    @pl.when(active & changed)
    def _():
        for cp in w_copies(order):
            cp.wait()
        for cp in w_copies(order + 1):
            cp.start()
        ws = order % 2
        wg_s[...] = wg_f[ws].astype(BF16)
        wu_s[...] = wu_f[ws].astype(BF16)
        wd_s[...] = wd_f[ws].astype(BF16)

    @pl.when(i == n - 1)
    def _():
        for cp in w_copies(nact_ref[1]):
            cp.wait()

    def mlp(nrows):
        x = _unpack_rows(xbuf.at[slot, pl.ds(0, nrows * XROWS), :], nrows, pair_s).astype(BF16)
        gate = _dot(x, wg_s[...])
        up = _dot(x, wu_s[...])
        hid = (gate * jax.nn.sigmoid(gate) * up).astype(BF16)
        _pack_rows(ys_ref.at[pl.ds(0, nrows * XROWS), :], _dot(hid, wd_s[...]), pair_s)

    half = tb // 2
    upper_empty = fill_ref[i] <= half

    @pl.when(active & jnp.logical_not(upper_empty))
    def _():
        mlp(tb)

    @pl.when(active & upper_empty)
    def _():
        mlp(half)
        ys_ref[pl.ds(half * XROWS, half * XROWS), :] = _packed_zeros(half)

    @pl.when(jnp.logical_not(active))
    def _():
        ys_ref[...] = _packed_zeros(tb)


def _experts(blk_order, blk_fill, used_seq, nact, xs, wg, wu, wd, layer, tb):
    nblk = blk_order.shape[0]
    _, _, d, de = wg.shape
    anyspec = pl.BlockSpec(memory_space=pl.ANY)
    return pl.pallas_call(
        functools.partial(_expert_kernel, layer=layer),
        grid_spec=pltpu.PrefetchScalarGridSpec(
            num_scalar_prefetch=4,
            grid=(nblk,),
            in_specs=[anyspec, anyspec, anyspec, anyspec],
            out_specs=pl.BlockSpec((tb * XROWS, LANES), lambda i, *_: (i, 0)),
            scratch_shapes=[pltpu.VMEM((X_SLOTS, tb * XROWS, LANES), jnp.uint32),
                            pltpu.SemaphoreType.DMA((X_SLOTS,)),
                            pltpu.VMEM((2, d, de), F32), pltpu.VMEM((2, d, de), F32), pltpu.VMEM((2, de, d), F32),
                            pltpu.SemaphoreType.DMA((2,)),
                            pltpu.VMEM((d, de), BF16), pltpu.VMEM((d, de), BF16), pltpu.VMEM((de, d), BF16),
                            pltpu.VMEM((XROWS, 2 * tb, LANES), F32)]),
        out_shape=jax.ShapeDtypeStruct((nblk * tb * XROWS, LANES), jnp.uint32),
        compiler_params=_params(1),
        name="experts",
    )(blk_order, blk_fill, used_seq, nact, xs, wg, wu, wd)


def _combine(dest_ref, ys_ref, meta_ref, buf_ref, sem_ref, pair_s, tm, nsub=1):
    i = pl.program_id(0)
    n = pl.num_programs(0)
    slot = i % COMB_SLOTS
    ahead = COMB_SLOTS - 1
    t = dest_ref.shape[0] // TOP_K

    def copy(tile, sl, r, k):
        d = dest_ref[k * t + tile * tm + r]
        row = r * XROWS if isinstance(r, int) else pl.multiple_of(r * XROWS, XROWS)
        return pltpu.make_async_copy(
            ys_ref.at[pl.ds(pl.multiple_of(d * XROWS, XROWS), XROWS), :],
            buf_ref.at[sl, k, pl.ds(row, XROWS), :],
            sem_ref.at[sl])

    def start_tile(tile, sl):
        def body(r, c):
            for k in range(TOP_K):
                copy(tile, sl, r, k).start(priority=k)
            return c
        lax.fori_loop(0, tm, body, 0, unroll=DMA_UNROLL)

    def wait_tile(sl):
        for k in range(TOP_K):
            pltpu.make_async_copy(ys_ref.at[pl.ds(0, tm * XROWS), :], buf_ref.at[sl, k], sem_ref.at[sl]).wait()

    @pl.when(i == 0)
    def _():
        for a in range(ahead):
            start_tile(jnp.minimum(a, n - 1), a)

    wait_tile(slot)

    nxt = jnp.minimum(i + ahead, n - 1)
    nslot = (i + ahead) % COMB_SLOTS

    def issue(part, nparts):
        rows = tm // nparts
        for r in range(part * rows, (part + 1) * rows):
            for k in range(TOP_K):
                copy(nxt, nslot, r, k).start(priority=k)

    @pl.when(i == n - 1)
    def _():
        for a in range(1, ahead):
            wait_tile((i + a) % COMB_SLOTS)

    ts = tm // nsub
    ys = []
    for s in range(nsub):
        y = None
        for k in range(TOP_K):
            gate = meta_ref[pl.ds(s * ts, ts), 2 * TOP_K + k:2 * TOP_K + k + 1]
            rows = buf_ref.at[slot, k, pl.ds(s * ts * XROWS, ts * XROWS), :]
            term = _unpack_rows(rows, ts, pair_s.at[k * nsub + s]) * gate
            y = term if y is None else y + term
        ys.append(y)
    return ys, issue


def _combine_drain(buf_ref, sem_ref, ys_ref, tm):
    i = pl.program_id(0)

    @pl.when(i == pl.num_programs(0) - 1)
    def _():
        sl = (i + COMB_SLOTS - 1) % COMB_SLOTS
        for k in range(TOP_K):
            pltpu.make_async_copy(ys_ref.at[pl.ds(0, tm * XROWS), :], buf_ref.at[sl, k], sem_ref.at[sl]).wait()


def _combine_scratch(tm, nsub=1):
    return [pltpu.VMEM((COMB_SLOTS, TOP_K, tm * XROWS, LANES), jnp.uint32), pltpu.SemaphoreType.DMA((COMB_SLOTS,)),
            pltpu.VMEM((TOP_K * nsub, XROWS, 2 * tm // nsub, LANES), F32)]


def _comb_proj1_kernel(dest_ref, h_ref, meta_ref, ys_ref, g_ref, w_ref, lbl_ref,
                       h2_ref, qs_ref, lf_ref, iv_ref, gs_ref, buf_ref, sem_ref, pair_s, *, lb_rows):
    tm, d = h_ref.shape
    nsub = tm // min(TM_COMB_SUB, tm)
    ts = tm // nsub
    ys, issue = _combine(dest_ref, ys_ref, meta_ref, buf_ref, sem_ref, pair_s, tm, nsub)
    lbl = lbl_ref[...]
    ex = jnp.exp(lbl - jnp.max(lbl, axis=0, keepdims=True))
    lb = jnp.sum(ex[:lb_rows], axis=0, keepdims=True) / jnp.sum(ex, axis=0, keepdims=True)

    hns = []
    for s, y in enumerate(ys):
        h2 = h_ref[pl.ds(s * ts, ts), :] + y
        h2_ref[pl.ds(s * ts, ts), :] = h2
        hns.append(_rms(h2, g_ref[...]).astype(BF16))

    ncol = 4
    z = []
    for s, hn in enumerate(hns):
        for c in range(ncol):
            z.append(_dot(hn, w_ref[:, d * c:d * (c + 1)]))
            issue(s * ncol + c, nsub * ncol)
    for s in range(nsub):
        rows = pl.ds(s * ts, ts)
        q, fz, iv, gz = z[s * ncol:(s + 1) * ncol]
        qs_ref[rows, :] = (q * jax.nn.sigmoid(q)).astype(BF16)
        lf_ref[rows, :] = jnp.log(lb + (1.0 - lb) * jax.nn.sigmoid(fz)) * LOG2E
        iv_ref[rows, :] = iv.astype(BF16)
        gs_ref[rows, :] = (gz * jax.nn.sigmoid(gz)).astype(BF16)
    _combine_drain(buf_ref, sem_ref, ys_ref, tm)


def _comb_proj1(dest, h, meta, ys, g, w, lb_logits, lb_rows):
    t, d = h.shape
    tm = min(TM_COMB_PROJ, t)
    nsub = tm // min(TM_COMB_SUB, tm)
    row = lambda i, dd: (i, 0)
    const = lambda shape: pl.BlockSpec(shape, lambda i, dd: (0,) * len(shape))
    return pl.pallas_call(
        functools.partial(_comb_proj1_kernel, lb_rows=lb_rows),
        grid_spec=pltpu.PrefetchScalarGridSpec(
            num_scalar_prefetch=1,
            grid=(t // tm,),
            in_specs=[pl.BlockSpec((tm, d), row), pl.BlockSpec((tm, LANES), row),
                      pl.BlockSpec(memory_space=pl.ANY),
                      const(g.shape), const(w.shape), const(lb_logits.shape)],
            out_specs=[pl.BlockSpec((tm, d), row)] * 5,
            scratch_shapes=_combine_scratch(tm, nsub)),
        out_shape=[jax.ShapeDtypeStruct((t, d), F32), jax.ShapeDtypeStruct((t, d), BF16),
                   jax.ShapeDtypeStruct((t, d), F32), jax.ShapeDtypeStruct((t, d), BF16),
                   jax.ShapeDtypeStruct((t, d), BF16)],
        compiler_params=_params(1),
        name="comb_proj1",
    )(dest, h, meta, ys, g, w, lb_logits)


def _comb_final_kernel(dest_ref, h_ref, meta_ref, ys_ref, g_ref, o_ref, buf_ref, sem_ref, pair_s):
    tm = h_ref.shape[0]
    (y,), issue = _combine(dest_ref, ys_ref, meta_ref, buf_ref, sem_ref, pair_s, tm)
    issue(0, 1)
    o_ref[...] = _rms(h_ref[...] + y, g_ref[...])
    _combine_drain(buf_ref, sem_ref, ys_ref, tm)


def _comb_final(dest, h, meta, ys, g):
    t, d = h.shape
    tm = min(TM_COMB, t)
    row = lambda i, dd: (i, 0)
    return pl.pallas_call(
        _comb_final_kernel,
        grid_spec=pltpu.PrefetchScalarGridSpec(
            num_scalar_prefetch=1,
            grid=(t // tm,),
            in_specs=[pl.BlockSpec((tm, d), row), pl.BlockSpec((tm, LANES), row),
                      pl.BlockSpec(memory_space=pl.ANY),
                      pl.BlockSpec(g.shape, lambda i, dd: (0, 0))],
            out_specs=pl.BlockSpec((tm, d), row),
            scratch_shapes=_combine_scratch(tm)),
        out_shape=jax.ShapeDtypeStruct((t, d), F32),
        compiler_params=_params(1),
        name="comb_final",
    )(dest, h, meta, ys, g)


def _hgrn_kernel(qs_ref, lf_ref, iv_ref, gs_ref, ng_ref, o_ref, st_ref, d1_s, t16_s, *, dk):
    th = qs_ref.shape[0]
    nsub = CHUNK // SUB

    @pl.when(pl.program_id(1) == 0)
    def _():
        st_ref[...] = jnp.zeros_like(st_ref)

    rr = lax.broadcasted_iota(jnp.int32, (th, th), 0)
    cc = lax.broadcasted_iota(jnp.int32, (th, th), 1)
    same_sub = (rr // SUB) == (cc // SUB)
    same_chunk = (rr // CHUNK) == (cc // CHUNK)
    m_diag = same_sub & (cc <= rr)
    dsub = rr // SUB - cc // SUB
    m_off = [same_chunk & (dsub == dd) for dd in range(1, nsub)]

    hi, lo = _split_bf16(lf_ref[...])
    tri = jnp.where(m_diag, 1.0, 0.0).astype(BF16)
    d1 = _dot(tri, hi) + _dot(tri, lo)
    d1_s[...] = d1
    sub3 = (th // SUB, SUB, d1.shape[1])
    t16_s[...] = jnp.broadcast_to(d1.reshape(sub3)[:, SUB - 1:SUB, :], sub3).reshape(th, d1.shape[1])
    rrow = lax.broadcasted_iota(jnp.int32, (th, dk), 0)
    rsub = (rrow // SUB) % nsub
    rchunk = rrow // CHUNK
    nch = th // CHUNK

    def back_rows(dd):
        return [slice(CHUNK * c + SUB * dd, CHUNK * (c + 1)) for c in range(nch)]

    def prepare(hd):
        ls = pl.ds(hd * dk, dk)
        d1 = d1_s[:, ls]
        t16 = t16_s[:, ls]
        q = qs_ref[:, ls].astype(F32)
        kt = 1.0 - jnp.exp2(lf_ref[:, ls])
        suf = t16 - d1
        kx = (kt * jnp.exp2(suf)).astype(BF16)
        qv = [q * jnp.exp2(-suf)]
        acc = d1
        tail = suf
        for dd in range(1, nsub):
            qv.append(jnp.concatenate([q[sl] * jnp.exp2(acc[sl]) for sl in back_rows(dd)], axis=0))
            acc = acc + jnp.where(rsub >= dd, pltpu.roll(t16, SUB * dd, 0), 0.0)
            tail = tail + jnp.where(rsub < nsub - dd, pltpu.roll(t16, th - SUB * dd, 0), 0.0)
        b = acc
        qb = q * jnp.exp2(b)
        kend = kt * jnp.exp2(tail)
        zero = jnp.zeros_like(q)
        kend_x = jnp.concatenate([jnp.where(rchunk == c, kend, zero) for c in range(nch)], axis=1).astype(BF16)
        qb_x = jnp.concatenate([jnp.where(rchunk == c, qb, zero) for c in range(nch)], axis=1).astype(BF16)
        decs = [jnp.exp2(b[CHUNK * (c + 1) - 1:CHUNK * (c + 1), :]) for c in range(nch)]
        return ls, jnp.concatenate(qv, axis=0).astype(BF16), kx, kend_x, qb_x, decs

    def group(gi, carry):
        heads = [gi * HGRN_GROUP + u for u in range(HGRN_GROUP)]
        prep = [prepare(hd) for hd in heads]
        a4s = [_dot_nt(p[1], p[2]) for p in prep]
        incs = [_dot_tn(iv_ref[:, p[0]], p[3]) for p in prep]
        o_intras = []
        for p, a4 in zip(prep, a4s):
            att = jnp.where(m_diag, a4[:th], 0.0)
            at = th
            for dd in range(1, nsub):
                pieces = []
                for sl in back_rows(dd):
                    n = sl.stop - sl.start
                    pieces += [jnp.zeros((SUB * dd, th), F32), a4[at:at + n]]
                    at += n
                att = jnp.where(m_off[dd - 1], jnp.concatenate(pieces, axis=0), att)
            o_intras.append(_dot(att.astype(BF16), iv_ref[:, p[0]]))
        for hd, p, inc, o_intra in zip(heads, prep, incs, o_intras):
            ls, decs = p[0], p[5]
            st = st_ref[hd]
            starts = []
            for c in range(nch):
                starts.append(st)
                st = st * decs[c] + inc[:, dk * c:dk * (c + 1)]
            st_ref[hd] = st
            oh = o_intra + _dot_nt(p[4], jnp.concatenate(starts, axis=1).astype(BF16))
            on = oh * lax.rsqrt(jnp.mean(oh * oh, axis=-1, keepdims=True) + EPS)
            o_ref[:, ls] = (on * ng_ref[:, ls] * gs_ref[:, ls].astype(F32)).astype(BF16)
        return carry

    for gi in range(qs_ref.shape[1] // dk // HGRN_GROUP):
        group(gi, 0)


def _hgrn(qs, lf, iv, gs, ng, batch, dk):
    t, d = qs.shape
    s = t // batch
    th = min(TH, s)
    blk = pl.BlockSpec((None, th, d), lambda b, j: (b, j, 0))
    r3 = lambda a: a.reshape(batch, s, d)
    out = pl.pallas_call(
        functools.partial(_hgrn_kernel, dk=dk),
        grid=(batch, s // th),
        in_specs=[blk, blk, blk, blk, _const_spec(ng.shape)],
        out_specs=blk,
        out_shape=jax.ShapeDtypeStruct((batch, s, d), BF16),
        scratch_shapes=[pltpu.VMEM((d // dk, dk, dk), F32), pltpu.VMEM((th, d), F32), pltpu.VMEM((th, d), F32)],
        compiler_params=_params(2),
        name="hgrn2",
    )(r3(qs), r3(lf), r3(iv), r3(gs), ng)
    return out.reshape(t, d)


def _block_diag(w):
    n, c, dd = w.shape
    eye = jnp.eye(n, dtype=w.dtype)
    return (eye[:, None, :, None] * w[:, :, None, :]).reshape(n * c, n * dd)


def _router_weights(wg, we):
    d = wg.shape[0]
    w = jnp.concatenate([wg, we, jnp.zeros((d, LANES - wg.shape[1] - we.shape[1]), F32)], axis=1).T
    hi = w.astype(BF16)
    return jnp.concatenate([hi, (w - hi.astype(F32)).astype(BF16)], axis=0)


def _moe(layer, meta_t, cnt, hn_slab, moe_w_gate, moe_w_up, moe_w_down):
    dest, pad_lo, pad_hi, blk_order, blk_fill, used_seq, nact, nblk = _plan(meta_t, cnt, TB_EXPERT)
    xs = _dispatch(dest, pad_lo, pad_hi, nact, hn_slab, nblk, TB_EXPERT)
    ys = _experts(blk_order, blk_fill, used_seq, nact, xs, moe_w_gate, moe_w_up, moe_w_down, layer, TB_EXPERT)
    return dest, ys


def kernel(x, norm_mix_g, norm_ffn_g, norm_final_g, ab_w_in, ab_conv_w, ab_conv_b, rg_w_a, rg_b_a, rg_w_x, rg_b_x, rg_lambda, attn_rel_bias, ab_w_out, c_w_in, c_lb_logits, c_norm_g, c_w_out, moe_router_group, moe_router_expert, moe_w_gate, moe_w_up, moe_w_down):
    batch, seq, d = x.shape
    t = batch * seq
    xt = x.reshape(t, d)
    row = lambda v: v.reshape(1, -1)

    aw = ab_conv_w.shape[2]
    qkv, ya = _proj0_lru(x, row(norm_mix_g[0]), ab_w_in[0].astype(BF16), ab_conv_w[0], row(ab_conv_b[0]),
                         _block_diag(rg_w_a[0]).astype(BF16), _block_diag(rg_w_x[0]).astype(BF16),
                         row(rg_b_a[0]), row(rg_b_x[0]), row(rg_lambda[0]))
    tk = TQ + LEFT_CHUNKS * CHUNK
    dist = jnp.clip(tk - jnp.arange(TQ + tk), -REL_CLIP, REL_CLIP) + REL_CLIP
    bias = _attn_bias(attn_rel_bias[0][:, None, dist])
    yb = _attention(qkv, bias, batch)
    wo = ab_w_out[0].astype(BF16)
    wt = _router_weights(moe_router_group[0], moe_router_expert[0])
    h1, hn1, meta1, metat1, cnt1 = _proj_route(ya, 0, yb, 0, xt, wo[:aw], wo[aw:], row(norm_ffn_g[0]), wt)
    dest1, ys1 = _moe(0, metat1, cnt1, hn1, moe_w_gate, moe_w_up, moe_w_down)

    dk = c_norm_g.shape[1] // 8
    h2, qs, lf, iv, gs = _comb_proj1(dest1, h1, meta1, ys1, row(norm_mix_g[1]), c_w_in[0].astype(BF16),
                                     c_lb_logits, 1)
    om = _hgrn(qs, lf, iv, gs, row(c_norm_g[0]), batch, dk)
    wo = c_w_out[0].astype(BF16)
    half = wo.shape[0] // 2
    wt = _router_weights(moe_router_group[1], moe_router_expert[1])
    h3, hn3, meta3, metat3, cnt3 = _proj_route(om, 0, om, 1, h2, wo[:half], wo[half:], row(norm_ffn_g[1]), wt)
    dest3, ys3 = _moe(1, metat3, cnt3, hn3, moe_w_gate, moe_w_up, moe_w_down)

    out = _comb_final(dest3, h3, meta3, ys3, row(norm_final_g))
    return out.reshape(batch, seq, d)
```

```python
import functools

import jax
import jax.numpy as jnp
from jax import lax
from jax.experimental import pallas as pl
from jax.experimental.pallas import tpu as pltpu

F32 = jnp.float32
BF16 = jnp.bfloat16

EPS = 1e-6
LOG2E = 1.4426950408889634
RG_C = 8.0
CHUNK = 64
LEFT_CHUNKS = 8
REL_CLIP = 256
N_GROUPS = 4
EXPERTS_PER_GROUP = 8
N_EXPERTS = N_GROUPS * EXPERTS_PER_GROUP
TOP_K = 2
ROUTE_ROWS = 40

LANES = 128
SUBLANES = 8
XROWS = 4
SUB = 16

TS_LRU = 1024
TQ = 256
ATTN_GROUP = 4
TM_ROUTE = 1024
TM_ROUTE_SUB = 256
TB_EXPERT = 512
EXPERT_PARTS = 4
X_SLOTS = 3
TM_COMB = 256
TM_COMB_PROJ = 512
TM_COMB_SUB = 256
PLAN_TILE = 2048
TH = 256
HGRN_GROUP = 4
DMA_UNROLL = 8
COMB_SLOTS = 3
VMEM_MB = 48


def _params(n_axes, vmem_mb=VMEM_MB):
    return pltpu.CompilerParams(dimension_semantics=("arbitrary",) * n_axes,
                                vmem_limit_bytes=vmem_mb * 1024 * 1024)


def _const_spec(shape):
    nd = len(shape)
    return pl.BlockSpec(shape, lambda *_: (0,) * nd)


def _rms(x, g):
    return x * lax.rsqrt(jnp.mean(x * x, axis=-1, keepdims=True) + EPS) * g


def _dot(a, b):
    return jnp.dot(a, b, preferred_element_type=F32)


def _dot_nt(a, b):
    return lax.dot_general(a, b, (((1,), (1,)), ((), ())), preferred_element_type=F32)


def _dot_tn(a, b):
    return lax.dot_general(a, b, (((0,), (0,)), ((), ())), preferred_element_type=F32)


def _split_bf16(x):
    hi = x.astype(BF16)
    lo = (x - hi.astype(F32)).astype(BF16)
    return hi, lo


def _pack_rows(ref, val, pair_s):
    m = val.shape[0]
    for j in range(XROWS):
        pair_s[j, pl.ds(0, m, stride=2), :] = val[:, LANES * j:LANES * (j + 1)]
        pair_s[j, pl.ds(1, m, stride=2), :] = val[:, LANES * (j + XROWS):LANES * (j + XROWS + 1)]
        ref[pl.ds(j, m, stride=XROWS), :] = pltpu.bitcast(pair_s[j, pl.ds(0, 2 * m), :].astype(BF16), jnp.uint32)


def _packed_zeros(m):
    return pltpu.bitcast(jnp.zeros((2 * m * XROWS, LANES), BF16), jnp.uint32)


def _unpack_rows(ref, m, pair_s):
    lo, hi = [], []
    for j in range(XROWS):
        pair_s[j, pl.ds(0, 2 * m), :] = pltpu.bitcast(ref[pl.ds(j, m, stride=XROWS), :], BF16).astype(F32)
        lo.append(pair_s[j, pl.ds(0, m, stride=2), :])
        hi.append(pair_s[j, pl.ds(1, m, stride=2), :])
    return jnp.concatenate(lo + hi, axis=1)


def _proj0_lru_kernel(x_ref, g_ref, w_ref, cw_ref, cb_ref, wa_ref, wx_ref, ba_ref, bx_ref, lam_ref,
                      qkv_ref, ya_ref, xbuf, h_ref, a_s, u_s):
    ts = x_ref.shape[0]
    w = ya_ref.shape[1]
    bw = qkv_ref.shape[1] // 3

    @pl.when(pl.program_id(1) == 0)
    def _():
        xbuf[pl.ds(0, SUBLANES), :] = jnp.zeros((SUBLANES, w), F32)
        h_ref[...] = jnp.zeros_like(h_ref)

    hn = _rms(x_ref[...], g_ref[...]).astype(BF16)

    def qkv_cols(c, scale=None):
        z = _dot(hn, w_ref[:, 2 * w + c * bw:2 * w + (c + 1) * bw])
        qkv_ref[:, c * bw:(c + 1) * bw] = (z if scale is None else z * scale).astype(BF16)

    xg = _dot(hn, w_ref[:, :2 * w])

    xbuf[pl.ds(SUBLANES, ts), :] = xg[:, :w]
    nk = cw_ref.shape[0]
    y = cb_ref[...]
    for k in range(nk):
        y = y + cw_ref[nk - 1 - k:nk - k, :] * xbuf[pl.ds(SUBLANES - k, ts), :]
    xbuf[pl.ds(0, SUBLANES), :] = xbuf[pl.ds(ts, SUBLANES), :]

    yb = y.astype(BF16)
    rg = jax.nn.sigmoid(_dot(yb, wa_ref[...]) + ba_ref[...])
    ig = jax.nn.sigmoid(_dot(yb, wx_ref[...]) + bx_ref[...])
    qkv_cols(0, CHUNK ** -0.5 * LOG2E)
    lam = lam_ref[...]
    log_sig = jnp.minimum(lam, 0.0) - jnp.log1p(jnp.exp(-jnp.abs(lam)))
    log_a = RG_C * rg * log_sig
    a = jnp.exp(log_a)
    m = 1.0 - a * a
    u = jnp.where(m > 0.0, m * lax.rsqrt(m), 0.0) * (ig * y)
    qkv_cols(1)

    grp = (ts // SUBLANES, SUBLANES, w)
    a = a.reshape(grp)
    u = u.reshape(grp)
    rowm = lax.broadcasted_iota(jnp.int32, grp, 1)
    for s in (1, 2, 4):
        keep = rowm >= s
        a_sh = jnp.where(keep, pltpu.roll(a, s, 1), 1.0)
        u_sh = jnp.where(keep, pltpu.roll(u, s, 1), 0.0)
        u = a * u_sh + u
        a = a * a_sh
    a_s[...] = a.reshape(ts, w)
    u_s[...] = u.reshape(ts, w)
    qkv_cols(2)

    def group(gi, h):
        off = pl.multiple_of(gi * SUBLANES, SUBLANES)
        hg = a_s[pl.ds(off, SUBLANES), :] * h + u_s[pl.ds(off, SUBLANES), :]
        u_s[pl.ds(off, SUBLANES), :] = hg
        return jnp.broadcast_to(hg[SUBLANES - 1:SUBLANES, :], hg.shape)

    h_ref[...] = lax.fori_loop(0, ts // SUBLANES, group, h_ref[...])

    ga = xg[:, w:]
    gelu = 0.5 * ga * (1.0 + jnp.tanh(0.7978845608028654 * (ga + 0.044715 * (ga * ga * ga))))
    ya_ref[...] = (u_s[...] * gelu).astype(BF16)


def _proj0_lru(x, g, w_in, cw, cb, wa, wx, ba, bx, lam):
    batch, s, d = x.shape
    w = cw.shape[1]
    nqkv = w_in.shape[1] - 2 * w
    ts = min(TS_LRU, s)
    small = [g, w_in, cw, cb, wa, wx, ba, bx, lam]
    blk = lambda width: pl.BlockSpec((None, ts, width), lambda b, j: (b, j, 0))
    qkv, ya = pl.pallas_call(
        _proj0_lru_kernel,
        grid=(batch, s // ts),
        in_specs=[blk(d)] + [_const_spec(a.shape) for a in small],
        out_specs=[blk(nqkv), blk(w)],
        out_shape=[jax.ShapeDtypeStruct((batch, s, nqkv), BF16), jax.ShapeDtypeStruct((batch, s, w), BF16)],
        scratch_shapes=[pltpu.VMEM((ts + SUBLANES, w), F32), pltpu.VMEM((SUBLANES, w), F32),
                        pltpu.VMEM((ts, w), F32), pltpu.VMEM((ts, w), F32)],
        compiler_params=_params(2),
        name="proj0_lru",
    )(x, *small)
    return qkv.reshape(batch * s, nqkv), ya.reshape(batch * s, w)


def _attn_bias_kernel(base_ref, o_ref):
    tq, tk = o_ref.shape
    nkb = tk // tq
    row = base_ref[...]
    full = pltpu.roll(jnp.broadcast_to(row, (tq, tq + tk)), 0, 1, stride=1, stride_axis=0)
    qc = lax.broadcasted_iota(jnp.int32, (tq, tk), 0) // CHUNK
    col = lax.broadcasted_iota(jnp.int32, (tq, tk), 1)
    kc = col // CHUNK
    first_ok = (nkb - 1 - pl.program_id(0)) * tq
    valid = (kc >= qc) & (kc <= qc + LEFT_CHUNKS) & (col >= first_ok)
    o_ref[...] = jnp.where(valid, full[:, tq:] * LOG2E, -1e30)


def _attn_bias(base):
    h = base.shape[0]
    tk = TQ + LEFT_CHUNKS * CHUNK
    return pl.pallas_call(
        _attn_bias_kernel,
        grid=(tk // TQ, h),
        in_specs=[pl.BlockSpec((None, 1, TQ + tk), lambda v, i: (i, 0, 0))],
        out_specs=pl.BlockSpec((None, None, TQ, tk), lambda v, i: (v, i, 0, 0)),
        out_shape=jax.ShapeDtypeStruct((tk // TQ, h, TQ, tk), F32),
        compiler_params=_params(2),
        name="attn_bias",
    )(base)


def _attn_kernel(q_ref, k0_ref, k1_ref, k2_ref, v0_ref, v1_ref, v2_ref, bias_ref, o_ref):
    tq = q_ref.shape[0]
    half_w = LANES // 2
    lo = lax.broadcasted_iota(jnp.int32, (tq, LANES), 1) < half_w
    lo_k = lax.broadcasted_iota(jnp.int32, (3 * tq, LANES), 1) < half_w
    k_refs = (k0_ref, k1_ref, k2_ref)
    v_refs = (v0_ref, v1_ref, v2_ref)
    npair = q_ref.shape[1] // LANES
    for j0 in range(0, npair, ATTN_GROUP):
        pairs = range(j0, min(j0 + ATTN_GROUP, npair))
        qk, vh = {}, {}
        for j in pairs:
            sl = slice(LANES * j, LANES * (j + 1))
            q2 = q_ref[:, sl]
            kcat = jnp.concatenate([r[:, sl] for r in k_refs], axis=0)
            vcat = jnp.concatenate([r[:, sl] for r in v_refs], axis=0)
            one = jnp.ones_like(vcat)
            zero = jnp.zeros_like(q2)
            vh[j] = (jnp.where(lo_k, vcat, one), jnp.where(lo_k, one, vcat))
            qk[j] = [_dot_nt(jnp.where(lo if half == 0 else jnp.logical_not(lo), q2, zero), kcat)
                     for half in range(2)]
        ps = {}
        for j in pairs:
            ps[j] = []
            for half in range(2):
                s = qk[j][half] + bias_ref[2 * j + half]
                ps[j].append(jnp.exp2(s - jnp.max(s, axis=-1, keepdims=True)).astype(BF16))
        for j in pairs:
            pv = [_dot(ps[j][half], vh[j][half]) for half in range(2)]
            num = jnp.where(lo, pv[0], pv[1])
            den = pltpu.roll(jnp.where(lo, pv[1], pv[0]), half_w, 1)
            o_ref[:, LANES * j:LANES * (j + 1)] = (num / den).astype(BF16)


def _attention(qkv, bias, batch):
    t, w3 = qkv.shape
    w = w3 // 3
    s = t // batch
    qkv3 = qkv.reshape(batch, s, w3)
    nvar = bias.shape[0]

    def kv_spec(colblk, back):
        return pl.BlockSpec((None, TQ, w), lambda i, b: (b, jnp.maximum(i - back, 0), colblk))

    out = pl.pallas_call(
        _attn_kernel,
        grid=(s // TQ, batch),
        in_specs=[pl.BlockSpec((None, TQ, w), lambda i, b: (b, i, 0)),
                  kv_spec(1, 2), kv_spec(1, 1), kv_spec(1, 0),
                  kv_spec(2, 2), kv_spec(2, 1), kv_spec(2, 0),
                  pl.BlockSpec((None,) + bias.shape[1:], lambda i, b: (jnp.minimum(i, nvar - 1), 0, 0, 0))],
        out_specs=pl.BlockSpec((None, TQ, w), lambda i, b: (b, i, 0)),
        out_shape=jax.ShapeDtypeStruct((batch, s, w), BF16),
        compiler_params=_params(2),
        name="attention",
    )(qkv3, qkv3, qkv3, qkv3, qkv3, qkv3, qkv3, bias)
    return out.reshape(t, w)


def _proj_route_kernel(a_ref, b_ref, r_ref, wa_ref, wb_ref, g_ref, wt_ref,
                       h_ref, hn_ref, meta_ref, metat_ref, cnt_ref, carry_ref, pair_s):
    tm = a_ref.shape[0]
    ts = min(TM_ROUTE_SUB, tm)

    @pl.when(pl.program_id(0) == 0)
    def _():
        carry_ref[...] = jnp.zeros_like(carry_ref)

    wt = wt_ref[...]
    ridx = lax.broadcasted_iota(jnp.int32, (ROUTE_ROWS, ts), 0).astype(F32)
    r8 = lax.broadcasted_iota(jnp.int32, (SUBLANES, ts), 0)
    rr = lax.broadcasted_iota(jnp.int32, (ts, ts), 0)
    cc = lax.broadcasted_iota(jnp.int32, (ts, ts), 1)
    utri = jnp.where(rr < cc, 1.0, 0.0).astype(BF16)
    neg = -jnp.inf
    far = float(LANES)
    carry = carry_ref[...]

    all_logits = []
    for sub in range(tm // ts):
        rows = pl.ds(sub * ts, ts)
        h = r_ref[rows, :] + (_dot(a_ref[rows, :], wa_ref[...]) + _dot(b_ref[rows, :], wb_ref[...]))
        h_ref[rows, :] = h
        hn = _rms(h, g_ref[...])
        _pack_rows(hn_ref.at[pl.ds(sub * ts * XROWS, ts * XROWS), :], hn, pair_s)

        hi, lo = _split_bf16(hn)
        p_hi = _dot_nt(wt, hi)
        p_lo = _dot_nt(wt[:LANES], lo)
        all_logits.append((p_hi[:ROUTE_ROWS] + p_hi[LANES:LANES + ROUTE_ROWS]) + p_lo[:ROUTE_ROWS])

    for sub, logits in enumerate(all_logits):
        rows = pl.ds(sub * ts, ts)

        def first_max(mask, logits=logits):
            v = jnp.max(jnp.where(mask, logits, neg), axis=0, keepdims=True)
            idx = jnp.min(jnp.where(mask & (logits == v), ridx, far), axis=0, keepdims=True)
            return v, idx

        gmask = ridx < N_GROUPS
        gmax, gidx = first_max(gmask)
        g_gate = 1.0 / jnp.sum(jnp.where(gmask, jnp.exp(logits - gmax), 0.0), axis=0, keepdims=True)
        e_lo = N_GROUPS + EXPERTS_PER_GROUP * gidx
        emask = (ridx >= e_lo) & (ridx < e_lo + EXPERTS_PER_GROUP)
        v1, i1 = first_max(emask)
        v2, i2 = first_max(emask & (ridx != i1))
        tt = jnp.exp(v2 - v1)
        w1 = g_gate / (1.0 + tt)
        w2 = g_gate * tt / (1.0 + tt)

        sel1 = ridx == i1
        sel2 = ridx == i2
        onehot = jnp.where(sel1 | sel2, 1.0, 0.0)
        before = _dot(onehot.astype(BF16), utri) + carry[:, 0:1]
        rank1 = jnp.sum(jnp.where(sel1, before, 0.0), axis=0, keepdims=True)
        rank2 = jnp.sum(jnp.where(sel2, before, 0.0), axis=0, keepdims=True)
        carry = carry + jnp.sum(onehot, axis=1, keepdims=True)

        mt = jnp.zeros((SUBLANES, ts), F32)
        for c, val in enumerate((i1 - N_GROUPS, i2 - N_GROUPS, rank1, rank2, w1, w2)):
            mt = jnp.where(r8 == c, val, mt)
        metat_ref[:, rows] = mt
        meta_ref[rows, :] = jnp.concatenate([mt, jnp.zeros((LANES - SUBLANES, ts), F32)], axis=0).T

    carry_ref[...] = carry
    cnt_ref[...] = carry


def _proj_route(a, acol, b, bcol, resid, wa, wb, g, wt):
    t, d = resid.shape
    kw = wa.shape[0]
    tm = min(TM_ROUTE, t)
    return pl.pallas_call(
        _proj_route_kernel,
        grid=(t // tm,),
        in_specs=[pl.BlockSpec((tm, kw), lambda i: (i, acol)), pl.BlockSpec((tm, kw), lambda i: (i, bcol)),
                  pl.BlockSpec((tm, d), lambda i: (i, 0)),
                  _const_spec(wa.shape), _const_spec(wb.shape), _const_spec(g.shape), _const_spec(wt.shape)],
        out_specs=[pl.BlockSpec((tm, d), lambda i: (i, 0)),
                   pl.BlockSpec((tm * XROWS, LANES), lambda i: (i, 0)),
                   pl.BlockSpec((tm, LANES), lambda i: (i, 0)),
                   pl.BlockSpec((SUBLANES, tm), lambda i: (0, i)),
                   pl.BlockSpec((ROUTE_ROWS, LANES), lambda i: (0, 0))],
        out_shape=[jax.ShapeDtypeStruct((t, d), F32),
                   jax.ShapeDtypeStruct((t * XROWS, LANES), jnp.uint32),
                   jax.ShapeDtypeStruct((t, LANES), F32),
                   jax.ShapeDtypeStruct((SUBLANES, t), F32),
                   jax.ShapeDtypeStruct((ROUTE_ROWS, LANES), F32)],
        scratch_shapes=[pltpu.VMEM((ROUTE_ROWS, LANES), F32),
                        pltpu.VMEM((XROWS, 2 * min(TM_ROUTE_SUB, tm), LANES), F32)],
        compiler_params=_params(1),
        name="proj_route",
    )(a, b, resid, wa, wb, g, wt)


def _plan_kernel(cnt_ref, meta_ref, dest_ref, tab_ref, start_s, *, tb):
    r_n = ROUTE_ROWS
    tab_w = tab_ref.shape[1]

    @pl.when(pl.program_id(0) == 0)
    def _():
        row = lax.broadcasted_iota(jnp.int32, (r_n, LANES), 0)
        lane = lax.broadcasted_iota(jnp.int32, (r_n, LANES), 1)

        def prefix(v):
            s = 1
            while s < r_n:
                v = v + jnp.where(row >= s, pltpu.roll(v, s, 0), 0.0)
                s *= 2
            return v

        counts = cnt_ref[...]
        padded = jnp.floor((counts + (tb - 1)) * (1.0 / tb)) * tb
        pad_end = prefix(padded)
        pad_start = pad_end - padded
        pad_lo = pad_start + counts
        start_s[...] = pad_start
        used = padded > 0.0
        order = prefix(jnp.where(used, 1.0, 0.0)) - 1.0
        nused = order[r_n - 1:r_n, :] + 1.0
        nact = pad_end[r_n - 1:r_n, :] * (1.0 / tb)
        expert = (row - N_GROUPS).astype(F32)
        lane_f = lane.astype(F32)
        used_seq = jnp.sum(jnp.where(used & (order == jnp.minimum(lane_f, nused - 1.0)), expert, 0.0),
                           axis=0, keepdims=True)

        erow = lax.broadcasted_iota(jnp.int32, (r_n, tab_w), 0)
        blk_start = lax.broadcasted_iota(jnp.int32, (r_n, tab_w), 1).astype(F32) * tb
        is_expert = (erow >= N_GROUPS) & (erow < N_GROUPS + N_EXPERTS)
        ended = jnp.where(is_expert & (pad_end[:, 0:1] <= blk_start), 1.0, 0.0)
        blk_expert = jnp.minimum(jnp.sum(ended, axis=0, keepdims=True), N_EXPERTS - 1.0)
        mine = (erow - N_GROUPS).astype(F32) == blk_expert
        blk_order = jnp.sum(jnp.where(mine, order[:, 0:1], 0.0), axis=0, keepdims=True)
        blk_lo = jnp.sum(jnp.where(mine, pad_lo[:, 0:1], 0.0), axis=0, keepdims=True)
        blk_fill = jnp.clip(blk_lo - blk_start[0:1, :], 0.0, float(tb))

        cols = jnp.where(lane == 0, pad_lo, jnp.where(lane == 1, pad_end, 0.0))
        rows_t = jnp.concatenate([cols, jnp.zeros((LANES - r_n, LANES), F32)], axis=0).T
        misc = jnp.where(lane[0:1, :] == 0, nact, jnp.where(lane[0:1, :] == 1, nused, 0.0))
        wide = lambda v: v if tab_w == LANES else jnp.concatenate(
            [v, jnp.zeros((v.shape[0], tab_w - LANES), F32)], axis=1)
        tab = jnp.concatenate([wide(rows_t[0:2, :]), wide(used_seq), wide(misc), blk_order, blk_fill,
                               jnp.zeros((2, tab_w), F32)], axis=0)
        tab_ref[...] = tab.astype(jnp.int32)

    for c in range(dest_ref.shape[1] // LANES):
        sl = slice(LANES * c, LANES * (c + 1))
        for k in range(TOP_K):
            expert = meta_ref[k:k + 1, sl]
            start = jnp.zeros((1, LANES), F32)
            for e in range(N_EXPERTS):
                start = jnp.where(expert == float(e), start_s[N_GROUPS + e:N_GROUPS + e + 1, :], start)
            dest_ref[k:k + 1, sl] = (start + meta_ref[TOP_K + k:TOP_K + k + 1, sl]).astype(jnp.int32)


def _plan(meta_t, cnt, tb):
    t = meta_t.shape[1]
    nblk = -(-(t * TOP_K + N_EXPERTS * (tb - 1)) // tb)
    tile = min(PLAN_TILE, t)
    tab_w = -(-nblk // LANES) * LANES
    dest, tab = pl.pallas_call(
        functools.partial(_plan_kernel, tb=tb),
        grid=(t // tile,),
        in_specs=[_const_spec(cnt.shape), pl.BlockSpec((SUBLANES, tile), lambda i: (0, i))],
        out_specs=[pl.BlockSpec((TOP_K, tile), lambda i: (0, i)), pl.BlockSpec((SUBLANES, tab_w), lambda i: (0, 0))],
        out_shape=[jax.ShapeDtypeStruct((TOP_K, t), jnp.int32), jax.ShapeDtypeStruct((SUBLANES, tab_w), jnp.int32)],
        scratch_shapes=[pltpu.VMEM((ROUTE_ROWS, LANES), F32)],
        compiler_params=_params(1),
        name="plan",
    )(cnt, meta_t)
    experts = slice(N_GROUPS, N_GROUPS + N_EXPERTS)
    return (dest.reshape(-1), tab[0, experts], tab[1, experts], tab[4, :nblk], tab[5, :nblk],
            tab[2, :N_EXPERTS], tab[3, :2], nblk)


def _dispatch_kernel(dest_ref, plo_ref, phi_ref, nact_ref, hn_ref, xs_ref, zero_ref, sem, zsem):
    tm = hn_ref.shape[0] // XROWS
    t = dest_ref.shape[0] // TOP_K
    tb = zero_ref.shape[0] // XROWS
    nblk = xs_ref.shape[0] // zero_ref.shape[0]
    i = pl.program_id(0)

    def zero_copy(row, nrows):
        return pltpu.make_async_copy(
            zero_ref.at[pl.ds(0, nrows * XROWS), :],
            xs_ref.at[pl.ds(pl.multiple_of(row * XROWS, XROWS), nrows * XROWS), :], zsem)

    def zero_fill(start):
        def go(cp):
            cp.start() if start else cp.wait()

        def per_expert(e, c):
            off = plo_ref[e]
            n = phi_ref[e] - off
            bit = tb // 2
            while bit:
                pl.when((n & bit) != 0)(functools.partial(lambda o, b: go(zero_copy(o, b)), off, bit))
                off = off + (n & bit)
                bit //= 2
            return c

        def per_block(b, c):
            go(zero_copy(b * tb, tb))
            return c

        lax.fori_loop(0, N_EXPERTS, per_expert, 0)
        lax.fori_loop(nact_ref[0], nblk, per_block, 0)

    @pl.when(i == 0)
    def _():
        zero_ref[...] = jnp.zeros_like(zero_ref)
        zero_fill(True)

    def copy(r, k):
        d = dest_ref[k * t + i * tm + r]
        return pltpu.make_async_copy(
            hn_ref.at[pl.ds(pl.multiple_of(r * XROWS, XROWS), XROWS), :],
            xs_ref.at[pl.ds(pl.multiple_of(d * XROWS, XROWS), XROWS), :], sem)

    def start(r, c):
        for k in range(TOP_K):
            copy(r, k).start(priority=k)
        return c

    lax.fori_loop(0, tm, start, 0, unroll=DMA_UNROLL)
    for k in range(TOP_K):
        pltpu.make_async_copy(hn_ref, xs_ref.at[pl.ds(0, tm * XROWS), :], sem).wait()

    @pl.when(i == 0)
    def _():
        zero_fill(False)


def _dispatch(dest, pad_lo, pad_hi, nact, hn_slab, nblk, tb):
    t = hn_slab.shape[0] // XROWS
    tm = min(TM_COMB, t)
    return pl.pallas_call(
        _dispatch_kernel,
        grid_spec=pltpu.PrefetchScalarGridSpec(
            num_scalar_prefetch=4,
            grid=(t // tm,),
            in_specs=[pl.BlockSpec((tm * XROWS, LANES), lambda i, *_: (i, 0))],
            out_specs=pl.BlockSpec(memory_space=pl.ANY),
            scratch_shapes=[pltpu.VMEM((tb * XROWS, LANES), jnp.uint32),
                            pltpu.SemaphoreType.DMA(()), pltpu.SemaphoreType.DMA(())]),
        out_shape=jax.ShapeDtypeStruct((nblk * tb * XROWS, LANES), jnp.uint32),
        compiler_params=_params(1),
        name="dispatch",
    )(dest, pad_lo, pad_hi, nact, hn_slab)


def _expert_kernel(ord_ref, fill_ref, useq_ref, nact_ref, xs_ref, wg_ref, wu_ref, wd_ref, ys_ref,
                   xbuf, xsem, wg_f, wu_f, wd_f, wsem, wg_s, wu_s, wd_s, pair_s, *, layer):
    i = pl.program_id(0)
    n = pl.num_programs(0)
    tb = ys_ref.shape[0] // XROWS
    rows = tb * XROWS
    active = i < nact_ref[0]
    order = ord_ref[i]
    changed = (i == 0) | (order != ord_ref[jnp.maximum(i - 1, 0)])
    slot = i % X_SLOTS
    ahead = X_SLOTS - 1

    def w_copies(j):
        e = useq_ref[jnp.minimum(j, useq_ref.shape[0] - 1)]
        ws = j % 2
        return [pltpu.make_async_copy(src.at[layer, e], dst.at[ws], wsem.at[ws])
                for src, dst in ((wg_ref, wg_f), (wu_ref, wu_f), (wd_ref, wd_f))]

    @pl.when(i == 0)
    def _():
        for cp in w_copies(0):
            cp.start()

    def x_copy(step, sl):
        blk = jnp.minimum(step, nact_ref[0] - 1)
        return pltpu.make_async_copy(xs_ref.at[pl.ds(pl.multiple_of(blk * rows, rows), rows), :],
                                     xbuf.at[sl], xsem.at[sl])

    @pl.when(i == 0)
    def _():
        for a in range(ahead):
            x_copy(a, a).start()

    x_copy(i, slot).wait()
    x_copy(i + ahead, (i + ahead) % X_SLOTS).start()

    @pl.when(i == n - 1)
    def _():
        for a in range(1, X_SLOTS):
            x_copy(i + a, (i + a) % X_SLOTS).wait()

    @pl.when(active & changed)
    def _():
        for cp in w_copies(order):
            cp.wait()
        for cp in w_copies(order + 1):
            cp.start()
        ws = order % 2
        wg_s[...] = wg_f[ws].astype(BF16)
        wu_s[...] = wu_f[ws].astype(BF16)
        wd_s[...] = wd_f[ws].astype(BF16)

    @pl.when(i == n - 1)
    def _():
        for cp in w_copies(nact_ref[1]):
            cp.wait()

    def mlp(nrows):
        x = _unpack_rows(xbuf.at[slot, pl.ds(0, nrows * XROWS), :], nrows, pair_s).astype(BF16)
        gate = _dot(x, wg_s[...])
        up = _dot(x, wu_s[...])
        hid = (gate * jax.nn.sigmoid(gate) * up).astype(BF16)
        _pack_rows(ys_ref.at[pl.ds(0, nrows * XROWS), :], _dot(hid, wd_s[...]), pair_s)

    part = tb // EXPERT_PARTS
    parts = jnp.clip((fill_ref[i] + part - 1) // part, 1, EXPERT_PARTS)
    for p in range(1, EXPERT_PARTS + 1):
        @pl.when(active & (parts == p))
        def _(p=p):
            mlp(p * part)
            if p < EXPERT_PARTS:
                ys_ref[pl.ds(p * part * XROWS, (tb - p * part) * XROWS), :] = _packed_zeros(tb - p * part)

    @pl.when(jnp.logical_not(active))
    def _():
        ys_ref[...] = _packed_zeros(tb)


def _experts(blk_order, blk_fill, used_seq, nact, xs, wg, wu, wd, layer, tb):
    nblk = blk_order.shape[0]
    _, _, d, de = wg.shape
    anyspec = pl.BlockSpec(memory_space=pl.ANY)
    return pl.pallas_call(
        functools.partial(_expert_kernel, layer=layer),
        grid_spec=pltpu.PrefetchScalarGridSpec(
            num_scalar_prefetch=4,
            grid=(nblk,),
            in_specs=[anyspec, anyspec, anyspec, anyspec],
            out_specs=pl.BlockSpec((tb * XROWS, LANES), lambda i, *_: (i, 0)),
            scratch_shapes=[pltpu.VMEM((X_SLOTS, tb * XROWS, LANES), jnp.uint32),
                            pltpu.SemaphoreType.DMA((X_SLOTS,)),
                            pltpu.VMEM((2, d, de), F32), pltpu.VMEM((2, d, de), F32), pltpu.VMEM((2, de, d), F32),
                            pltpu.SemaphoreType.DMA((2,)),
                            pltpu.VMEM((d, de), BF16), pltpu.VMEM((d, de), BF16), pltpu.VMEM((de, d), BF16),
                            pltpu.VMEM((XROWS, 2 * tb, LANES), F32)]),
        out_shape=jax.ShapeDtypeStruct((nblk * tb * XROWS, LANES), jnp.uint32),
        compiler_params=_params(1),
        name="experts",
    )(blk_order, blk_fill, used_seq, nact, xs, wg, wu, wd)


def _combine(dest_ref, ys_ref, meta_ref, buf_ref, sem_ref, pair_s, tm, nsub=1):
    i = pl.program_id(0)
    n = pl.num_programs(0)
    slot = i % COMB_SLOTS
    ahead = COMB_SLOTS - 1
    t = dest_ref.shape[0] // TOP_K

    def copy(tile, sl, r, k):
        d = dest_ref[k * t + tile * tm + r]
        row = r * XROWS if isinstance(r, int) else pl.multiple_of(r * XROWS, XROWS)
        return pltpu.make_async_copy(
            ys_ref.at[pl.ds(pl.multiple_of(d * XROWS, XROWS), XROWS), :],
            buf_ref.at[sl, k, pl.ds(row, XROWS), :],
            sem_ref.at[sl])

    def start_tile(tile, sl):
        def body(r, c):
            for k in range(TOP_K):
                copy(tile, sl, r, k).start(priority=k)
            return c
        lax.fori_loop(0, tm, body, 0, unroll=DMA_UNROLL)

    def wait_tile(sl):
        for k in range(TOP_K):
            pltpu.make_async_copy(ys_ref.at[pl.ds(0, tm * XROWS), :], buf_ref.at[sl, k], sem_ref.at[sl]).wait()

    @pl.when(i == 0)
    def _():
        for a in range(ahead):
            start_tile(jnp.minimum(a, n - 1), a)

    wait_tile(slot)

    nxt = jnp.minimum(i + ahead, n - 1)
    nslot = (i + ahead) % COMB_SLOTS

    def issue(part, nparts):
        rows = tm // nparts
        for r in range(part * rows, (part + 1) * rows):
            for k in range(TOP_K):
                copy(nxt, nslot, r, k).start(priority=k)

    @pl.when(i == n - 1)
    def _():
        for a in range(1, ahead):
            wait_tile((i + a) % COMB_SLOTS)

    ts = tm // nsub
    ys = []
    for s in range(nsub):
        y = None
        for k in range(TOP_K):
            gate = meta_ref[pl.ds(s * ts, ts), 2 * TOP_K + k:2 * TOP_K + k + 1]
            rows = buf_ref.at[slot, k, pl.ds(s * ts * XROWS, ts * XROWS), :]
            term = _unpack_rows(rows, ts, pair_s.at[k * nsub + s]) * gate
            y = term if y is None else y + term
        ys.append(y)
    return ys, issue


def _combine_drain(buf_ref, sem_ref, ys_ref, tm):
    i = pl.program_id(0)

    @pl.when(i == pl.num_programs(0) - 1)
    def _():
        sl = (i + COMB_SLOTS - 1) % COMB_SLOTS
        for k in range(TOP_K):
            pltpu.make_async_copy(ys_ref.at[pl.ds(0, tm * XROWS), :], buf_ref.at[sl, k], sem_ref.at[sl]).wait()


def _combine_scratch(tm, nsub=1):
    return [pltpu.VMEM((COMB_SLOTS, TOP_K, tm * XROWS, LANES), jnp.uint32), pltpu.SemaphoreType.DMA((COMB_SLOTS,)),
            pltpu.VMEM((TOP_K * nsub, XROWS, 2 * tm // nsub, LANES), F32)]


def _comb_proj1_kernel(dest_ref, h_ref, meta_ref, ys_ref, g_ref, w_ref, lbl_ref,
                       h2_ref, qs_ref, lf_ref, iv_ref, gs_ref, buf_ref, sem_ref, pair_s, *, lb_rows):
    tm, d = h_ref.shape
    nsub = tm // min(TM_COMB_SUB, tm)
    ts = tm // nsub
    ys, issue = _combine(dest_ref, ys_ref, meta_ref, buf_ref, sem_ref, pair_s, tm, nsub)
    lbl = lbl_ref[...]
    ex = jnp.exp(lbl - jnp.max(lbl, axis=0, keepdims=True))
    lb = jnp.sum(ex[:lb_rows], axis=0, keepdims=True) / jnp.sum(ex, axis=0, keepdims=True)

    hns = []
    for s, y in enumerate(ys):
        h2 = h_ref[pl.ds(s * ts, ts), :] + y
        h2_ref[pl.ds(s * ts, ts), :] = h2
        hns.append(_rms(h2, g_ref[...]).astype(BF16))

    ncol = 4
    z = []
    for s, hn in enumerate(hns):
        for c in range(ncol):
            z.append(_dot(hn, w_ref[:, d * c:d * (c + 1)]))
            issue(s * ncol + c, nsub * ncol)
    for s in range(nsub):
        rows = pl.ds(s * ts, ts)
        q, fz, iv, gz = z[s * ncol:(s + 1) * ncol]
        qs_ref[rows, :] = (q * jax.nn.sigmoid(q)).astype(BF16)
        lf_ref[rows, :] = jnp.log(lb + (1.0 - lb) * jax.nn.sigmoid(fz)) * LOG2E
        iv_ref[rows, :] = iv.astype(BF16)
        gs_ref[rows, :] = (gz * jax.nn.sigmoid(gz)).astype(BF16)
    _combine_drain(buf_ref, sem_ref, ys_ref, tm)


def _comb_proj1(dest, h, meta, ys, g, w, lb_logits, lb_rows):
    t, d = h.shape
    tm = min(TM_COMB_PROJ, t)
    nsub = tm // min(TM_COMB_SUB, tm)
    row = lambda i, dd: (i, 0)
    const = lambda shape: pl.BlockSpec(shape, lambda i, dd: (0,) * len(shape))
    return pl.pallas_call(
        functools.partial(_comb_proj1_kernel, lb_rows=lb_rows),
        grid_spec=pltpu.PrefetchScalarGridSpec(
            num_scalar_prefetch=1,
            grid=(t // tm,),
            in_specs=[pl.BlockSpec((tm, d), row), pl.BlockSpec((tm, LANES), row),
                      pl.BlockSpec(memory_space=pl.ANY),
                      const(g.shape), const(w.shape), const(lb_logits.shape)],
            out_specs=[pl.BlockSpec((tm, d), row)] * 5,
            scratch_shapes=_combine_scratch(tm, nsub)),
        out_shape=[jax.ShapeDtypeStruct((t, d), F32), jax.ShapeDtypeStruct((t, d), BF16),
                   jax.ShapeDtypeStruct((t, d), F32), jax.ShapeDtypeStruct((t, d), BF16),
                   jax.ShapeDtypeStruct((t, d), BF16)],
        compiler_params=_params(1),
        name="comb_proj1",
    )(dest, h, meta, ys, g, w, lb_logits)


def _comb_final_kernel(dest_ref, h_ref, meta_ref, ys_ref, g_ref, o_ref, buf_ref, sem_ref, pair_s):
    tm = h_ref.shape[0]
    (y,), issue = _combine(dest_ref, ys_ref, meta_ref, buf_ref, sem_ref, pair_s, tm)
    issue(0, 1)
    o_ref[...] = _rms(h_ref[...] + y, g_ref[...])
    _combine_drain(buf_ref, sem_ref, ys_ref, tm)


def _comb_final(dest, h, meta, ys, g):
    t, d = h.shape
    tm = min(TM_COMB, t)
    row = lambda i, dd: (i, 0)
    return pl.pallas_call(
        _comb_final_kernel,
        grid_spec=pltpu.PrefetchScalarGridSpec(
            num_scalar_prefetch=1,
            grid=(t // tm,),
            in_specs=[pl.BlockSpec((tm, d), row), pl.BlockSpec((tm, LANES), row),
                      pl.BlockSpec(memory_space=pl.ANY),
                      pl.BlockSpec(g.shape, lambda i, dd: (0, 0))],
            out_specs=pl.BlockSpec((tm, d), row),
            scratch_shapes=_combine_scratch(tm)),
        out_shape=jax.ShapeDtypeStruct((t, d), F32),
        compiler_params=_params(1),
        name="comb_final",
    )(dest, h, meta, ys, g)


def _hgrn_kernel(qs_ref, lf_ref, iv_ref, gs_ref, ng_ref, o_ref, st_ref, d1_s, t16_s, *, dk):
    th = qs_ref.shape[0]
    nsub = CHUNK // SUB

    @pl.when(pl.program_id(1) == 0)
    def _():
        st_ref[...] = jnp.zeros_like(st_ref)

    rr = lax.broadcasted_iota(jnp.int32, (th, th), 0)
    cc = lax.broadcasted_iota(jnp.int32, (th, th), 1)
    same_sub = (rr // SUB) == (cc // SUB)
    same_chunk = (rr // CHUNK) == (cc // CHUNK)
    m_diag = same_sub & (cc <= rr)
    dsub = rr // SUB - cc // SUB
    m_off = [same_chunk & (dsub == dd) for dd in range(1, nsub)]

    hi, lo = _split_bf16(lf_ref[...])
    tri = jnp.where(m_diag, 1.0, 0.0).astype(BF16)
    d1 = _dot(tri, hi) + _dot(tri, lo)
    d1_s[...] = d1
    sub3 = (th // SUB, SUB, d1.shape[1])
    t16_s[...] = jnp.broadcast_to(d1.reshape(sub3)[:, SUB - 1:SUB, :], sub3).reshape(th, d1.shape[1])
    rrow = lax.broadcasted_iota(jnp.int32, (th, dk), 0)
    rsub = (rrow // SUB) % nsub
    rchunk = rrow // CHUNK
    nch = th // CHUNK

    def back_rows(dd):
        return [slice(CHUNK * c + SUB * dd, CHUNK * (c + 1)) for c in range(nch)]

    def prepare(hd):
        ls = pl.ds(hd * dk, dk)
        d1 = d1_s[:, ls]
        t16 = t16_s[:, ls]
        q = qs_ref[:, ls].astype(F32)
        kt = 1.0 - jnp.exp2(lf_ref[:, ls])
        suf = t16 - d1
        kx = (kt * jnp.exp2(suf)).astype(BF16)
        qv = [q * jnp.exp2(-suf)]
        acc = d1
        tail = suf
        for dd in range(1, nsub):
            qv.append(jnp.concatenate([q[sl] * jnp.exp2(acc[sl]) for sl in back_rows(dd)], axis=0))
            acc = acc + jnp.where(rsub >= dd, pltpu.roll(t16, SUB * dd, 0), 0.0)
            tail = tail + jnp.where(rsub < nsub - dd, pltpu.roll(t16, th - SUB * dd, 0), 0.0)
        b = acc
        qb = q * jnp.exp2(b)
        kend = kt * jnp.exp2(tail)
        zero = jnp.zeros_like(q)
        kend_x = jnp.concatenate([jnp.where(rchunk == c, kend, zero) for c in range(nch)], axis=1).astype(BF16)
        qb_x = jnp.concatenate([jnp.where(rchunk == c, qb, zero) for c in range(nch)], axis=1).astype(BF16)
        decs = [jnp.exp2(b[CHUNK * (c + 1) - 1:CHUNK * (c + 1), :]) for c in range(nch)]
        return ls, jnp.concatenate(qv, axis=0).astype(BF16), kx, kend_x, qb_x, decs

    def group(gi, carry):
        heads = [gi * HGRN_GROUP + u for u in range(HGRN_GROUP)]
        prep = [prepare(hd) for hd in heads]
        a4s = [_dot_nt(p[1], p[2]) for p in prep]
        incs = [_dot_tn(iv_ref[:, p[0]], p[3]) for p in prep]
        o_intras = []
        for p, a4 in zip(prep, a4s):
            att = jnp.where(m_diag, a4[:th], 0.0)
            at = th
            for dd in range(1, nsub):
                pieces = []
                for sl in back_rows(dd):
                    n = sl.stop - sl.start
                    pieces += [jnp.zeros((SUB * dd, th), F32), a4[at:at + n]]
                    at += n
                att = jnp.where(m_off[dd - 1], jnp.concatenate(pieces, axis=0), att)
            o_intras.append(_dot(att.astype(BF16), iv_ref[:, p[0]]))
        for hd, p, inc, o_intra in zip(heads, prep, incs, o_intras):
            ls, decs = p[0], p[5]
            st = st_ref[hd]
            starts = []
            for c in range(nch):
                starts.append(st)
                st = st * decs[c] + inc[:, dk * c:dk * (c + 1)]
            st_ref[hd] = st
            oh = o_intra + _dot_nt(p[4], jnp.concatenate(starts, axis=1).astype(BF16))
            on = oh * lax.rsqrt(jnp.mean(oh * oh, axis=-1, keepdims=True) + EPS)
            o_ref[:, ls] = (on * ng_ref[:, ls] * gs_ref[:, ls].astype(F32)).astype(BF16)
        return carry

    for gi in range(qs_ref.shape[1] // dk // HGRN_GROUP):
        group(gi, 0)


def _hgrn(qs, lf, iv, gs, ng, batch, dk):
    t, d = qs.shape
    s = t // batch
    th = min(TH, s)
    blk = pl.BlockSpec((None, th, d), lambda b, j: (b, j, 0))
    r3 = lambda a: a.reshape(batch, s, d)
    out = pl.pallas_call(
        functools.partial(_hgrn_kernel, dk=dk),
        grid=(batch, s // th),
        in_specs=[blk, blk, blk, blk, _const_spec(ng.shape)],
        out_specs=blk,
        out_shape=jax.ShapeDtypeStruct((batch, s, d), BF16),
        scratch_shapes=[pltpu.VMEM((d // dk, dk, dk), F32), pltpu.VMEM((th, d), F32), pltpu.VMEM((th, d), F32)],
        compiler_params=_params(2),
        name="hgrn2",
    )(r3(qs), r3(lf), r3(iv), r3(gs), ng)
    return out.reshape(t, d)


def _block_diag(w):
    n, c, dd = w.shape
    eye = jnp.eye(n, dtype=w.dtype)
    return (eye[:, None, :, None] * w[:, :, None, :]).reshape(n * c, n * dd)


def _router_weights(wg, we):
    d = wg.shape[0]
    w = jnp.concatenate([wg, we, jnp.zeros((d, LANES - wg.shape[1] - we.shape[1]), F32)], axis=1).T
    hi = w.astype(BF16)
    return jnp.concatenate([hi, (w - hi.astype(F32)).astype(BF16)], axis=0)


def _moe(layer, meta_t, cnt, hn_slab, moe_w_gate, moe_w_up, moe_w_down):
    dest, pad_lo, pad_hi, blk_order, blk_fill, used_seq, nact, nblk = _plan(meta_t, cnt, TB_EXPERT)
    xs = _dispatch(dest, pad_lo, pad_hi, nact, hn_slab, nblk, TB_EXPERT)
    ys = _experts(blk_order, blk_fill, used_seq, nact, xs, moe_w_gate, moe_w_up, moe_w_down, layer, TB_EXPERT)
    return dest, ys


def kernel(x, norm_mix_g, norm_ffn_g, norm_final_g, ab_w_in, ab_conv_w, ab_conv_b, rg_w_a, rg_b_a, rg_w_x, rg_b_x, rg_lambda, attn_rel_bias, ab_w_out, c_w_in, c_lb_logits, c_norm_g, c_w_out, moe_router_group, moe_router_expert, moe_w_gate, moe_w_up, moe_w_down):
    batch, seq, d = x.shape
    t = batch * seq
    xt = x.reshape(t, d)
    row = lambda v: v.reshape(1, -1)

    aw = ab_conv_w.shape[2]
    qkv, ya = _proj0_lru(x, row(norm_mix_g[0]), ab_w_in[0].astype(BF16), ab_conv_w[0], row(ab_conv_b[0]),
                         _block_diag(rg_w_a[0]).astype(BF16), _block_diag(rg_w_x[0]).astype(BF16),
                         row(rg_b_a[0]), row(rg_b_x[0]), row(rg_lambda[0]))
    tk = TQ + LEFT_CHUNKS * CHUNK
    dist = jnp.clip(tk - jnp.arange(TQ + tk), -REL_CLIP, REL_CLIP) + REL_CLIP
    bias = _attn_bias(attn_rel_bias[0][:, None, dist])
    yb = _attention(qkv, bias, batch)
    wo = ab_w_out[0].astype(BF16)
    wt = _router_weights(moe_router_group[0], moe_router_expert[0])
    h1, hn1, meta1, metat1, cnt1 = _proj_route(ya, 0, yb, 0, xt, wo[:aw], wo[aw:], row(norm_ffn_g[0]), wt)
    dest1, ys1 = _moe(0, metat1, cnt1, hn1, moe_w_gate, moe_w_up, moe_w_down)

    dk = c_norm_g.shape[1] // 8
    h2, qs, lf, iv, gs = _comb_proj1(dest1, h1, meta1, ys1, row(norm_mix_g[1]), c_w_in[0].astype(BF16),
                                     c_lb_logits, 1)
    om = _hgrn(qs, lf, iv, gs, row(c_norm_g[0]), batch, dk)
    wo = c_w_out[0].astype(BF16)
    half = wo.shape[0] // 2
    wt = _router_weights(moe_router_group[1], moe_router_expert[1])
    h3, hn3, meta3, metat3, cnt3 = _proj_route(om, 0, om, 1, h2, wo[:half], wo[half:], row(norm_ffn_g[1]), wt)
    dest3, ys3 = _moe(1, metat3, cnt3, hn3, moe_w_gate, moe_w_up, moe_w_down)

    out = _comb_final(dest3, h3, meta3, ys3, row(norm_final_g))
    return out.reshape(batch, seq, d)
```

```python
import functools

import jax
import jax.numpy as jnp
from jax import lax
from jax.experimental import pallas as pl
from jax.experimental.pallas import tpu as pltpu

F32 = jnp.float32
BF16 = jnp.bfloat16

EPS = 1e-6
LOG2E = 1.4426950408889634
RG_C = 8.0
CHUNK = 64
LEFT_CHUNKS = 8
REL_CLIP = 256
N_GROUPS = 4
EXPERTS_PER_GROUP = 8
N_EXPERTS = N_GROUPS * EXPERTS_PER_GROUP
TOP_K = 2
ROUTE_ROWS = 40

LANES = 128
SUBLANES = 8
XROWS = 4
SUB = 16

TS_LRU = 1024
TQ = 256
ATTN_GROUP = 4
TM_ROUTE = 1024
TM_ROUTE_SUB = 512
TB_EXPERT = 512
EXPERT_PARTS = 4
X_SLOTS = 3
TM_COMB = 256
TM_COMB_PROJ = 512
TM_COMB_SUB = 256
PLAN_TILE = 2048
TH = 256
HGRN_GROUP = 4
DMA_UNROLL = 8
COMB_SLOTS = 3
VMEM_MB = 48


def _params(n_axes, vmem_mb=VMEM_MB):
    return pltpu.CompilerParams(dimension_semantics=("arbitrary",) * n_axes,
                                vmem_limit_bytes=vmem_mb * 1024 * 1024)


def _const_spec(shape):
    nd = len(shape)
    return pl.BlockSpec(shape, lambda *_: (0,) * nd)


def _rms(x, g):
    return x * lax.rsqrt(jnp.mean(x * x, axis=-1, keepdims=True) + EPS) * g


def _dot(a, b):
    return jnp.dot(a, b, preferred_element_type=F32)


def _dot_nt(a, b):
    return lax.dot_general(a, b, (((1,), (1,)), ((), ())), preferred_element_type=F32)


def _dot_tn(a, b):
    return lax.dot_general(a, b, (((0,), (0,)), ((), ())), preferred_element_type=F32)


def _split_bf16(x):
    hi = x.astype(BF16)
    lo = (x - hi.astype(F32)).astype(BF16)
    return hi, lo


def _pack_rows(ref, val, pair_s):
    m = val.shape[0]
    for j in range(XROWS):
        pair_s[j, pl.ds(0, m, stride=2), :] = val[:, LANES * j:LANES * (j + 1)]
        pair_s[j, pl.ds(1, m, stride=2), :] = val[:, LANES * (j + XROWS):LANES * (j + XROWS + 1)]
        ref[pl.ds(j, m, stride=XROWS), :] = pltpu.bitcast(pair_s[j, pl.ds(0, 2 * m), :].astype(BF16), jnp.uint32)


def _packed_zeros(m):
    return pltpu.bitcast(jnp.zeros((2 * m * XROWS, LANES), BF16), jnp.uint32)


def _unpack_rows(ref, m, pair_s):
    lo, hi = [], []
    for j in range(XROWS):
        pair_s[j, pl.ds(0, 2 * m), :] = pltpu.bitcast(ref[pl.ds(j, m, stride=XROWS), :], BF16).astype(F32)
        lo.append(pair_s[j, pl.ds(0, m, stride=2), :])
        hi.append(pair_s[j, pl.ds(1, m, stride=2), :])
    return jnp.concatenate(lo + hi, axis=1)


def _proj0_lru_kernel(x_ref, g_ref, w_ref, cw_ref, cb_ref, wa_ref, wx_ref, ba_ref, bx_ref, lam_ref,
                      qkv_ref, ya_ref, xbuf, h_ref, a_s, u_s):
    ts = x_ref.shape[0]
    w = ya_ref.shape[1]
    bw = qkv_ref.shape[1] // 3

    @pl.when(pl.program_id(1) == 0)
    def _():
        xbuf[pl.ds(0, SUBLANES), :] = jnp.zeros((SUBLANES, w), F32)
        h_ref[...] = jnp.zeros_like(h_ref)

    hn = _rms(x_ref[...], g_ref[...]).astype(BF16)

    def qkv_cols(c, scale=None):
        z = _dot(hn, w_ref[:, 2 * w + c * bw:2 * w + (c + 1) * bw])
        qkv_ref[:, c * bw:(c + 1) * bw] = (z if scale is None else z * scale).astype(BF16)

    xg = _dot(hn, w_ref[:, :2 * w])

    xbuf[pl.ds(SUBLANES, ts), :] = xg[:, :w]
    nk = cw_ref.shape[0]
    y = cb_ref[...]
    for k in range(nk):
        y = y + cw_ref[nk - 1 - k:nk - k, :] * xbuf[pl.ds(SUBLANES - k, ts), :]
    xbuf[pl.ds(0, SUBLANES), :] = xbuf[pl.ds(ts, SUBLANES), :]

    yb = y.astype(BF16)
    rg = jax.nn.sigmoid(_dot(yb, wa_ref[...]) + ba_ref[...])
    ig = jax.nn.sigmoid(_dot(yb, wx_ref[...]) + bx_ref[...])
    qkv_cols(0, CHUNK ** -0.5 * LOG2E)
    lam = lam_ref[...]
    log_sig = jnp.minimum(lam, 0.0) - jnp.log1p(jnp.exp(-jnp.abs(lam)))
    log_a = RG_C * rg * log_sig
    a = jnp.exp(log_a)
    m = 1.0 - a * a
    u = jnp.where(m > 0.0, m * lax.rsqrt(m), 0.0) * (ig * y)
    qkv_cols(1)

    grp = (ts // SUBLANES, SUBLANES, w)
    a = a.reshape(grp)
    u = u.reshape(grp)
    rowm = lax.broadcasted_iota(jnp.int32, grp, 1)
    for s in (1, 2, 4):
        keep = rowm >= s
        a_sh = jnp.where(keep, pltpu.roll(a, s, 1), 1.0)
        u_sh = jnp.where(keep, pltpu.roll(u, s, 1), 0.0)
        u = a * u_sh + u
        a = a * a_sh
    a_s[...] = a.reshape(ts, w)
    u_s[...] = u.reshape(ts, w)
    qkv_cols(2)

    def group(gi, h):
        off = pl.multiple_of(gi * SUBLANES, SUBLANES)
        hg = a_s[pl.ds(off, SUBLANES), :] * h + u_s[pl.ds(off, SUBLANES), :]
        u_s[pl.ds(off, SUBLANES), :] = hg
        return jnp.broadcast_to(hg[SUBLANES - 1:SUBLANES, :], hg.shape)

    h_ref[...] = lax.fori_loop(0, ts // SUBLANES, group, h_ref[...])

    ga = xg[:, w:]
    gelu = 0.5 * ga * (1.0 + jnp.tanh(0.7978845608028654 * (ga + 0.044715 * (ga * ga * ga))))
    ya_ref[...] = (u_s[...] * gelu).astype(BF16)


def _proj0_lru(x, g, w_in, cw, cb, wa, wx, ba, bx, lam):
    batch, s, d = x.shape
    w = cw.shape[1]
    nqkv = w_in.shape[1] - 2 * w
    ts = min(TS_LRU, s)
    small = [g, w_in, cw, cb, wa, wx, ba, bx, lam]
    blk = lambda width: pl.BlockSpec((None, ts, width), lambda b, j: (b, j, 0))
    qkv, ya = pl.pallas_call(
        _proj0_lru_kernel,
        grid=(batch, s // ts),
        in_specs=[blk(d)] + [_const_spec(a.shape) for a in small],
        out_specs=[blk(nqkv), blk(w)],
        out_shape=[jax.ShapeDtypeStruct((batch, s, nqkv), BF16), jax.ShapeDtypeStruct((batch, s, w), BF16)],
        scratch_shapes=[pltpu.VMEM((ts + SUBLANES, w), F32), pltpu.VMEM((SUBLANES, w), F32),
                        pltpu.VMEM((ts, w), F32), pltpu.VMEM((ts, w), F32)],
        compiler_params=_params(2),
        name="proj0_lru",
    )(x, *small)
    return qkv.reshape(batch * s, nqkv), ya.reshape(batch * s, w)


def _attn_bias_kernel(base_ref, o_ref):
    tq, tk = o_ref.shape
    nkb = tk // tq
    row = base_ref[...]
    full = pltpu.roll(jnp.broadcast_to(row, (tq, tq + tk)), 0, 1, stride=1, stride_axis=0)
    qc = lax.broadcasted_iota(jnp.int32, (tq, tk), 0) // CHUNK
    col = lax.broadcasted_iota(jnp.int32, (tq, tk), 1)
    kc = col // CHUNK
    first_ok = (nkb - 1 - pl.program_id(0)) * tq
    valid = (kc >= qc) & (kc <= qc + LEFT_CHUNKS) & (col >= first_ok)
    o_ref[...] = jnp.where(valid, full[:, tq:] * LOG2E, -1e30)


def _attn_bias(base):
    h = base.shape[0]
    tk = TQ + LEFT_CHUNKS * CHUNK
    return pl.pallas_call(
        _attn_bias_kernel,
        grid=(tk // TQ, h),
        in_specs=[pl.BlockSpec((None, 1, TQ + tk), lambda v, i: (i, 0, 0))],
        out_specs=pl.BlockSpec((None, None, TQ, tk), lambda v, i: (v, i, 0, 0)),
        out_shape=jax.ShapeDtypeStruct((tk // TQ, h, TQ, tk), F32),
        compiler_params=_params(2),
        name="attn_bias",
    )(base)


def _attn_kernel(q_ref, k0_ref, k1_ref, k2_ref, v0_ref, v1_ref, v2_ref, bias_ref, o_ref):
    tq = q_ref.shape[0]
    half_w = LANES // 2
    lo = lax.broadcasted_iota(jnp.int32, (tq, LANES), 1) < half_w
    lo_k = lax.broadcasted_iota(jnp.int32, (3 * tq, LANES), 1) < half_w
    k_refs = (k0_ref, k1_ref, k2_ref)
    v_refs = (v0_ref, v1_ref, v2_ref)
    npair = q_ref.shape[1] // LANES
    for j0 in range(0, npair, ATTN_GROUP):
        pairs = range(j0, min(j0 + ATTN_GROUP, npair))
        qk, vh = {}, {}
        for j in pairs:
            sl = slice(LANES * j, LANES * (j + 1))
            q2 = q_ref[:, sl]
            kcat = jnp.concatenate([r[:, sl] for r in k_refs], axis=0)
            vcat = jnp.concatenate([r[:, sl] for r in v_refs], axis=0)
            one = jnp.ones_like(vcat)
            zero = jnp.zeros_like(q2)
            vh[j] = (jnp.where(lo_k, vcat, one), jnp.where(lo_k, one, vcat))
            qk[j] = [_dot_nt(jnp.where(lo if half == 0 else jnp.logical_not(lo), q2, zero), kcat)
                     for half in range(2)]
        ps = {}
        for j in pairs:
            ps[j] = []
            for half in range(2):
                s = qk[j][half] + bias_ref[2 * j + half]
                ps[j].append(jnp.exp2(s - jnp.max(s, axis=-1, keepdims=True)).astype(BF16))
        for j in pairs:
            pv = [_dot(ps[j][half], vh[j][half]) for half in range(2)]
            num = jnp.where(lo, pv[0], pv[1])
            den = pltpu.roll(jnp.where(lo, pv[1], pv[0]), half_w, 1)
            o_ref[:, LANES * j:LANES * (j + 1)] = (num / den).astype(BF16)


def _attention(qkv, bias, batch):
    t, w3 = qkv.shape
    w = w3 // 3
    s = t // batch
    qkv3 = qkv.reshape(batch, s, w3)
    nvar = bias.shape[0]

    def kv_spec(colblk, back):
        return pl.BlockSpec((None, TQ, w), lambda i, b: (b, jnp.maximum(i - back, 0), colblk))

    out = pl.pallas_call(
        _attn_kernel,
        grid=(s // TQ, batch),
        in_specs=[pl.BlockSpec((None, TQ, w), lambda i, b: (b, i, 0)),
                  kv_spec(1, 2), kv_spec(1, 1), kv_spec(1, 0),
                  kv_spec(2, 2), kv_spec(2, 1), kv_spec(2, 0),
                  pl.BlockSpec((None,) + bias.shape[1:], lambda i, b: (jnp.minimum(i, nvar - 1), 0, 0, 0))],
        out_specs=pl.BlockSpec((None, TQ, w), lambda i, b: (b, i, 0)),
        out_shape=jax.ShapeDtypeStruct((batch, s, w), BF16),
        compiler_params=_params(2),
        name="attention",
    )(qkv3, qkv3, qkv3, qkv3, qkv3, qkv3, qkv3, bias)
    return out.reshape(t, w)


def _proj_route_kernel(a_ref, b_ref, r_ref, wa_ref, wb_ref, g_ref, wt_ref,
                       h_ref, hn_ref, meta_ref, metat_ref, cnt_ref, carry_ref, pair_s):
    tm = a_ref.shape[0]
    ts = min(TM_ROUTE_SUB, tm)

    @pl.when(pl.program_id(0) == 0)
    def _():
        carry_ref[...] = jnp.zeros_like(carry_ref)

    wt = wt_ref[...]
    ridx = lax.broadcasted_iota(jnp.int32, (ROUTE_ROWS, ts), 0).astype(F32)
    r8 = lax.broadcasted_iota(jnp.int32, (SUBLANES, ts), 0)
    rr = lax.broadcasted_iota(jnp.int32, (ts, ts), 0)
    cc = lax.broadcasted_iota(jnp.int32, (ts, ts), 1)
    utri = jnp.where(rr < cc, 1.0, 0.0).astype(BF16)
    neg = -jnp.inf
    far = float(LANES)
    carry = carry_ref[...]

    all_logits = []
    for sub in range(tm // ts):
        rows = pl.ds(sub * ts, ts)
        h = r_ref[rows, :] + (_dot(a_ref[rows, :], wa_ref[...]) + _dot(b_ref[rows, :], wb_ref[...]))
        h_ref[rows, :] = h
        hn = _rms(h, g_ref[...])
        _pack_rows(hn_ref.at[pl.ds(sub * ts * XROWS, ts * XROWS), :], hn, pair_s)

        hi, lo = _split_bf16(hn)
        p_hi = _dot_nt(wt, hi)
        p_lo = _dot_nt(wt[:LANES], lo)
        all_logits.append((p_hi[:ROUTE_ROWS] + p_hi[LANES:LANES + ROUTE_ROWS]) + p_lo[:ROUTE_ROWS])

    for sub, logits in enumerate(all_logits):
        rows = pl.ds(sub * ts, ts)

        def first_max(mask, logits=logits):
            v = jnp.max(jnp.where(mask, logits, neg), axis=0, keepdims=True)
            idx = jnp.min(jnp.where(mask & (logits == v), ridx, far), axis=0, keepdims=True)
            return v, idx

        gmask = ridx < N_GROUPS
        gmax, gidx = first_max(gmask)
        g_gate = 1.0 / jnp.sum(jnp.where(gmask, jnp.exp(logits - gmax), 0.0), axis=0, keepdims=True)
        e_lo = N_GROUPS + EXPERTS_PER_GROUP * gidx
        emask = (ridx >= e_lo) & (ridx < e_lo + EXPERTS_PER_GROUP)
        v1, i1 = first_max(emask)
        v2, i2 = first_max(emask & (ridx != i1))
        tt = jnp.exp(v2 - v1)
        w1 = g_gate / (1.0 + tt)
        w2 = g_gate * tt / (1.0 + tt)

        sel1 = ridx == i1
        sel2 = ridx == i2
        onehot = jnp.where(sel1 | sel2, 1.0, 0.0)
        before = _dot(onehot.astype(BF16), utri) + carry[:, 0:1]
        rank1 = jnp.sum(jnp.where(sel1, before, 0.0), axis=0, keepdims=True)
        rank2 = jnp.sum(jnp.where(sel2, before, 0.0), axis=0, keepdims=True)
        carry = carry + jnp.sum(onehot, axis=1, keepdims=True)

        mt = jnp.zeros((SUBLANES, ts), F32)
        for c, val in enumerate((i1 - N_GROUPS, i2 - N_GROUPS, rank1, rank2, w1, w2)):
            mt = jnp.where(r8 == c, val, mt)
        metat_ref[:, rows] = mt
        meta_ref[rows, :] = jnp.concatenate([mt, jnp.zeros((LANES - SUBLANES, ts), F32)], axis=0).T

    carry_ref[...] = carry
    cnt_ref[...] = carry


def _proj_route(a, acol, b, bcol, resid, wa, wb, g, wt):
    t, d = resid.shape
    kw = wa.shape[0]
    tm = min(TM_ROUTE, t)
    return pl.pallas_call(
        _proj_route_kernel,
        grid=(t // tm,),
        in_specs=[pl.BlockSpec((tm, kw), lambda i: (i, acol)), pl.BlockSpec((tm, kw), lambda i: (i, bcol)),
                  pl.BlockSpec((tm, d), lambda i: (i, 0)),
                  _const_spec(wa.shape), _const_spec(wb.shape), _const_spec(g.shape), _const_spec(wt.shape)],
        out_specs=[pl.BlockSpec((tm, d), lambda i: (i, 0)),
                   pl.BlockSpec((tm * XROWS, LANES), lambda i: (i, 0)),
                   pl.BlockSpec((tm, LANES), lambda i: (i, 0)),
                   pl.BlockSpec((SUBLANES, tm), lambda i: (0, i)),
                   pl.BlockSpec((ROUTE_ROWS, LANES), lambda i: (0, 0))],
        out_shape=[jax.ShapeDtypeStruct((t, d), F32),
                   jax.ShapeDtypeStruct((t * XROWS, LANES), jnp.uint32),
                   jax.ShapeDtypeStruct((t, LANES), F32),
                   jax.ShapeDtypeStruct((SUBLANES, t), F32),
                   jax.ShapeDtypeStruct((ROUTE_ROWS, LANES), F32)],
        scratch_shapes=[pltpu.VMEM((ROUTE_ROWS, LANES), F32),
                        pltpu.VMEM((XROWS, 2 * min(TM_ROUTE_SUB, tm), LANES), F32)],
        compiler_params=_params(1),
        name="proj_route",
    )(a, b, resid, wa, wb, g, wt)


def _plan_kernel(cnt_ref, meta_ref, dest_ref, tab_ref, start_s, *, tb):
    r_n = ROUTE_ROWS
    tab_w = tab_ref.shape[1]

    @pl.when(pl.program_id(0) == 0)
    def _():
        row = lax.broadcasted_iota(jnp.int32, (r_n, LANES), 0)
        lane = lax.broadcasted_iota(jnp.int32, (r_n, LANES), 1)

        def prefix(v):
            s = 1
            while s < r_n:
                v = v + jnp.where(row >= s, pltpu.roll(v, s, 0), 0.0)
                s *= 2
            return v

        counts = cnt_ref[...]
        padded = jnp.floor((counts + (tb - 1)) * (1.0 / tb)) * tb
        pad_end = prefix(padded)
        pad_start = pad_end - padded
        pad_lo = pad_start + counts
        start_s[...] = pad_start
        used = padded > 0.0
        order = prefix(jnp.where(used, 1.0, 0.0)) - 1.0
        nused = order[r_n - 1:r_n, :] + 1.0
        nact = pad_end[r_n - 1:r_n, :] * (1.0 / tb)
        expert = (row - N_GROUPS).astype(F32)
        lane_f = lane.astype(F32)
        used_seq = jnp.sum(jnp.where(used & (order == jnp.minimum(lane_f, nused - 1.0)), expert, 0.0),
                           axis=0, keepdims=True)

        erow = lax.broadcasted_iota(jnp.int32, (r_n, tab_w), 0)
        blk_start = lax.broadcasted_iota(jnp.int32, (r_n, tab_w), 1).astype(F32) * tb
        is_expert = (erow >= N_GROUPS) & (erow < N_GROUPS + N_EXPERTS)
        ended = jnp.where(is_expert & (pad_end[:, 0:1] <= blk_start), 1.0, 0.0)
        blk_expert = jnp.minimum(jnp.sum(ended, axis=0, keepdims=True), N_EXPERTS - 1.0)
        mine = (erow - N_GROUPS).astype(F32) == blk_expert
        blk_order = jnp.sum(jnp.where(mine, order[:, 0:1], 0.0), axis=0, keepdims=True)
        blk_lo = jnp.sum(jnp.where(mine, pad_lo[:, 0:1], 0.0), axis=0, keepdims=True)
        blk_fill = jnp.clip(blk_lo - blk_start[0:1, :], 0.0, float(tb))

        cols = jnp.where(lane == 0, pad_lo, jnp.where(lane == 1, pad_end, 0.0))
        rows_t = jnp.concatenate([cols, jnp.zeros((LANES - r_n, LANES), F32)], axis=0).T
        misc = jnp.where(lane[0:1, :] == 0, nact, jnp.where(lane[0:1, :] == 1, nused, 0.0))
        wide = lambda v: v if tab_w == LANES else jnp.concatenate(
            [v, jnp.zeros((v.shape[0], tab_w - LANES), F32)], axis=1)
        tab = jnp.concatenate([wide(rows_t[0:2, :]), wide(used_seq), wide(misc), blk_order, blk_fill,
                               jnp.zeros((2, tab_w), F32)], axis=0)
        tab_ref[...] = tab.astype(jnp.int32)

    for c in range(dest_ref.shape[1] // LANES):
        sl = slice(LANES * c, LANES * (c + 1))
        for k in range(TOP_K):
            expert = meta_ref[k:k + 1, sl]
            start = jnp.zeros((1, LANES), F32)
            for e in range(N_EXPERTS):
                start = jnp.where(expert == float(e), start_s[N_GROUPS + e:N_GROUPS + e + 1, :], start)
            dest_ref[k:k + 1, sl] = (start + meta_ref[TOP_K + k:TOP_K + k + 1, sl]).astype(jnp.int32)


def _plan(meta_t, cnt, tb):
    t = meta_t.shape[1]
    nblk = -(-(t * TOP_K + N_EXPERTS * (tb - 1)) // tb)
    tile = min(PLAN_TILE, t)
    tab_w = -(-nblk // LANES) * LANES
    dest, tab = pl.pallas_call(
        functools.partial(_plan_kernel, tb=tb),
        grid=(t // tile,),
        in_specs=[_const_spec(cnt.shape), pl.BlockSpec((SUBLANES, tile), lambda i: (0, i))],
        out_specs=[pl.BlockSpec((TOP_K, tile), lambda i: (0, i)), pl.BlockSpec((SUBLANES, tab_w), lambda i: (0, 0))],
        out_shape=[jax.ShapeDtypeStruct((TOP_K, t), jnp.int32), jax.ShapeDtypeStruct((SUBLANES, tab_w), jnp.int32)],
        scratch_shapes=[pltpu.VMEM((ROUTE_ROWS, LANES), F32)],
        compiler_params=_params(1),
        name="plan",
    )(cnt, meta_t)
    experts = slice(N_GROUPS, N_GROUPS + N_EXPERTS)
    return (dest.reshape(-1), tab[0, experts], tab[1, experts], tab[4, :nblk], tab[5, :nblk],
            tab[2, :N_EXPERTS], tab[3, :2], nblk)


def _dispatch_kernel(dest_ref, plo_ref, phi_ref, nact_ref, hn_ref, xs_ref, zero_ref, sem, zsem):
    tm = hn_ref.shape[0] // XROWS
    t = dest_ref.shape[0] // TOP_K
    tb = zero_ref.shape[0] // XROWS
    nblk = xs_ref.shape[0] // zero_ref.shape[0]
    i = pl.program_id(0)

    def zero_copy(row, nrows):
        return pltpu.make_async_copy(
            zero_ref.at[pl.ds(0, nrows * XROWS), :],
            xs_ref.at[pl.ds(pl.multiple_of(row * XROWS, XROWS), nrows * XROWS), :], zsem)

    def zero_fill(start):
        def go(cp):
            cp.start() if start else cp.wait()

        def per_expert(e, c):
            off = plo_ref[e]
            n = phi_ref[e] - off
            bit = tb // 2
            while bit:
                pl.when((n & bit) != 0)(functools.partial(lambda o, b: go(zero_copy(o, b)), off, bit))
                off = off + (n & bit)
                bit //= 2
            return c

        def per_block(b, c):
            go(zero_copy(b * tb, tb))
            return c

        lax.fori_loop(0, N_EXPERTS, per_expert, 0)
        lax.fori_loop(nact_ref[0], nblk, per_block, 0)

    @pl.when(i == 0)
    def _():
        zero_ref[...] = jnp.zeros_like(zero_ref)
        zero_fill(True)

    def copy(r, k):
        d = dest_ref[k * t + i * tm + r]
        return pltpu.make_async_copy(
            hn_ref.at[pl.ds(pl.multiple_of(r * XROWS, XROWS), XROWS), :],
            xs_ref.at[pl.ds(pl.multiple_of(d * XROWS, XROWS), XROWS), :], sem)

    def start(r, c):
        for k in range(TOP_K):
            copy(r, k).start(priority=k)
        return c

    lax.fori_loop(0, tm, start, 0, unroll=DMA_UNROLL)
    for k in range(TOP_K):
        pltpu.make_async_copy(hn_ref, xs_ref.at[pl.ds(0, tm * XROWS), :], sem).wait()

    @pl.when(i == 0)
    def _():
        zero_fill(False)


def _dispatch(dest, pad_lo, pad_hi, nact, hn_slab, nblk, tb):
    t = hn_slab.shape[0] // XROWS
    tm = min(TM_COMB, t)
    return pl.pallas_call(
        _dispatch_kernel,
        grid_spec=pltpu.PrefetchScalarGridSpec(
            num_scalar_prefetch=4,
            grid=(t // tm,),
            in_specs=[pl.BlockSpec((tm * XROWS, LANES), lambda i, *_: (i, 0))],
            out_specs=pl.BlockSpec(memory_space=pl.ANY),
            scratch_shapes=[pltpu.VMEM((tb * XROWS, LANES), jnp.uint32),
                            pltpu.SemaphoreType.DMA(()), pltpu.SemaphoreType.DMA(())]),
        out_shape=jax.ShapeDtypeStruct((nblk * tb * XROWS, LANES), jnp.uint32),
        compiler_params=_params(1),
        name="dispatch",
    )(dest, pad_lo, pad_hi, nact, hn_slab)


def _expert_kernel(ord_ref, fill_ref, useq_ref, nact_ref, xs_ref, wg_ref, wu_ref, wd_ref, ys_ref,
                   xbuf, xsem, wg_f, wu_f, wd_f, wsem, wg_s, wu_s, wd_s, pair_s, *, layer):
    i = pl.program_id(0)
    n = pl.num_programs(0)
    tb = ys_ref.shape[0] // XROWS
    rows = tb * XROWS
    active = i < nact_ref[0]
    order = ord_ref[i]
    changed = (i == 0) | (order != ord_ref[jnp.maximum(i - 1, 0)])
    slot = i % X_SLOTS
    ahead = X_SLOTS - 1

    def w_copies(j):
        e = useq_ref[jnp.minimum(j, useq_ref.shape[0] - 1)]
        ws = j % 2
        return [pltpu.make_async_copy(src.at[layer, e], dst.at[ws], wsem.at[ws])
                for src, dst in ((wg_ref, wg_f), (wu_ref, wu_f), (wd_ref, wd_f))]

    @pl.when(i == 0)
    def _():
        for cp in w_copies(0):
            cp.start()

    def x_copy(step, sl):
        blk = jnp.minimum(step, nact_ref[0] - 1)
        return pltpu.make_async_copy(xs_ref.at[pl.ds(pl.multiple_of(blk * rows, rows), rows), :],
                                     xbuf.at[sl], xsem.at[sl])

    @pl.when(i == 0)
    def _():
        for a in range(ahead):
            x_copy(a, a).start()

    x_copy(i, slot).wait()
    x_copy(i + ahead, (i + ahead) % X_SLOTS).start()

    @pl.when(i == n - 1)
    def _():
        for a in range(1, X_SLOTS):
            x_copy(i + a, (i + a) % X_SLOTS).wait()

    @pl.when(active & changed)
    def _():
        for cp in w_copies(order):
            cp.wait()
        for cp in w_copies(order + 1):
            cp.start()
        ws = order % 2
        wg_s[...] = wg_f[ws].astype(BF16)
        wu_s[...] = wu_f[ws].astype(BF16)
        wd_s[...] = wd_f[ws].astype(BF16)

    @pl.when(i == n - 1)
    def _():
        for cp in w_copies(nact_ref[1]):
            cp.wait()

    def mlp(nrows):
        x = _unpack_rows(xbuf.at[slot, pl.ds(0, nrows * XROWS), :], nrows, pair_s).astype(BF16)
        gate = _dot(x, wg_s[...])
        up = _dot(x, wu_s[...])
        hid = (gate * jax.nn.sigmoid(gate) * up).astype(BF16)
        _pack_rows(ys_ref.at[pl.ds(0, nrows * XROWS), :], _dot(hid, wd_s[...]), pair_s)

    part = tb // EXPERT_PARTS
    parts = jnp.clip((fill_ref[i] + part - 1) // part, 1, EXPERT_PARTS)
    for p in range(1, EXPERT_PARTS + 1):
        @pl.when(active & (parts == p))
        def _(p=p):
            mlp(p * part)
            if p < EXPERT_PARTS:
                ys_ref[pl.ds(p * part * XROWS, (tb - p * part) * XROWS), :] = _packed_zeros(tb - p * part)

    @pl.when(jnp.logical_not(active))
    def _():
        ys_ref[...] = _packed_zeros(tb)


def _experts(blk_order, blk_fill, used_seq, nact, xs, wg, wu, wd, layer, tb):
    nblk = blk_order.shape[0]
    _, _, d, de = wg.shape
    anyspec = pl.BlockSpec(memory_space=pl.ANY)
    return pl.pallas_call(
        functools.partial(_expert_kernel, layer=layer),
        grid_spec=pltpu.PrefetchScalarGridSpec(
            num_scalar_prefetch=4,
            grid=(nblk,),
            in_specs=[anyspec, anyspec, anyspec, anyspec],
            out_specs=pl.BlockSpec((tb * XROWS, LANES), lambda i, *_: (i, 0)),
            scratch_shapes=[pltpu.VMEM((X_SLOTS, tb * XROWS, LANES), jnp.uint32),
                            pltpu.SemaphoreType.DMA((X_SLOTS,)),
                            pltpu.VMEM((2, d, de), F32), pltpu.VMEM((2, d, de), F32), pltpu.VMEM((2, de, d), F32),
                            pltpu.SemaphoreType.DMA((2,)),
                            pltpu.VMEM((d, de), BF16), pltpu.VMEM((d, de), BF16), pltpu.VMEM((de, d), BF16),
                            pltpu.VMEM((XROWS, 2 * tb, LANES), F32)]),
        out_shape=jax.ShapeDtypeStruct((nblk * tb * XROWS, LANES), jnp.uint32),
        compiler_params=_params(1),
        name="experts",
    )(blk_order, blk_fill, used_seq, nact, xs, wg, wu, wd)


def _combine(dest_ref, ys_ref, meta_ref, buf_ref, sem_ref, pair_s, tm, nsub=1):
    i = pl.program_id(0)
    n = pl.num_programs(0)
    slot = i % COMB_SLOTS
    ahead = COMB_SLOTS - 1
    t = dest_ref.shape[0] // TOP_K

    def copy(tile, sl, r, k):
        d = dest_ref[k * t + tile * tm + r]
        row = r * XROWS if isinstance(r, int) else pl.multiple_of(r * XROWS, XROWS)
        return pltpu.make_async_copy(
            ys_ref.at[pl.ds(pl.multiple_of(d * XROWS, XROWS), XROWS), :],
            buf_ref.at[sl, k, pl.ds(row, XROWS), :],
            sem_ref.at[sl])

    def start_tile(tile, sl):
        def body(r, c):
            for k in range(TOP_K):
                copy(tile, sl, r, k).start(priority=k)
            return c
        lax.fori_loop(0, tm, body, 0, unroll=DMA_UNROLL)

    def wait_tile(sl):
        for k in range(TOP_K):
            pltpu.make_async_copy(ys_ref.at[pl.ds(0, tm * XROWS), :], buf_ref.at[sl, k], sem_ref.at[sl]).wait()

    @pl.when(i == 0)
    def _():
        for a in range(ahead):
            start_tile(jnp.minimum(a, n - 1), a)

    wait_tile(slot)

    nxt = jnp.minimum(i + ahead, n - 1)
    nslot = (i + ahead) % COMB_SLOTS

    def issue(part, nparts):
        rows = tm // nparts
        for r in range(part * rows, (part + 1) * rows):
            for k in range(TOP_K):
                copy(nxt, nslot, r, k).start(priority=k)

    @pl.when(i == n - 1)
    def _():
        for a in range(1, ahead):
            wait_tile((i + a) % COMB_SLOTS)

    ts = tm // nsub
    ys = []
    for s in range(nsub):
        y = None
        for k in range(TOP_K):
            gate = meta_ref[pl.ds(s * ts, ts), 2 * TOP_K + k:2 * TOP_K + k + 1]
            rows = buf_ref.at[slot, k, pl.ds(s * ts * XROWS, ts * XROWS), :]
            term = _unpack_rows(rows, ts, pair_s.at[k * nsub + s]) * gate
            y = term if y is None else y + term
        ys.append(y)
    return ys, issue


def _combine_drain(buf_ref, sem_ref, ys_ref, tm):
    i = pl.program_id(0)

    @pl.when(i == pl.num_programs(0) - 1)
    def _():
        sl = (i + COMB_SLOTS - 1) % COMB_SLOTS
        for k in range(TOP_K):
            pltpu.make_async_copy(ys_ref.at[pl.ds(0, tm * XROWS), :], buf_ref.at[sl, k], sem_ref.at[sl]).wait()


def _combine_scratch(tm, nsub=1):
    return [pltpu.VMEM((COMB_SLOTS, TOP_K, tm * XROWS, LANES), jnp.uint32), pltpu.SemaphoreType.DMA((COMB_SLOTS,)),
            pltpu.VMEM((TOP_K * nsub, XROWS, 2 * tm // nsub, LANES), F32)]


def _comb_proj1_kernel(dest_ref, h_ref, meta_ref, ys_ref, g_ref, w_ref, lbl_ref,
                       h2_ref, qs_ref, lf_ref, iv_ref, gs_ref, buf_ref, sem_ref, pair_s, *, lb_rows):
    tm, d = h_ref.shape
    nsub = tm // min(TM_COMB_SUB, tm)
    ts = tm // nsub
    ys, issue = _combine(dest_ref, ys_ref, meta_ref, buf_ref, sem_ref, pair_s, tm, nsub)
    lbl = lbl_ref[...]
    ex = jnp.exp(lbl - jnp.max(lbl, axis=0, keepdims=True))
    lb = jnp.sum(ex[:lb_rows], axis=0, keepdims=True) / jnp.sum(ex, axis=0, keepdims=True)

    hns = []
    for s, y in enumerate(ys):
        h2 = h_ref[pl.ds(s * ts, ts), :] + y
        h2_ref[pl.ds(s * ts, ts), :] = h2
        hns.append(_rms(h2, g_ref[...]).astype(BF16))

    ncol = 4
    z = []
    for s, hn in enumerate(hns):
        for c in range(ncol):
            z.append(_dot(hn, w_ref[:, d * c:d * (c + 1)]))
            issue(s * ncol + c, nsub * ncol)
    for s in range(nsub):
        rows = pl.ds(s * ts, ts)
        q, fz, iv, gz = z[s * ncol:(s + 1) * ncol]
        qs_ref[rows, :] = (q * jax.nn.sigmoid(q)).astype(BF16)
        lf_ref[rows, :] = jnp.log(lb + (1.0 - lb) * jax.nn.sigmoid(fz)) * LOG2E
        iv_ref[rows, :] = iv.astype(BF16)
        gs_ref[rows, :] = (gz * jax.nn.sigmoid(gz)).astype(BF16)
    _combine_drain(buf_ref, sem_ref, ys_ref, tm)


def _comb_proj1(dest, h, meta, ys, g, w, lb_logits, lb_rows):
    t, d = h.shape
    tm = min(TM_COMB_PROJ, t)
    nsub = tm // min(TM_COMB_SUB, tm)
    row = lambda i, dd: (i, 0)
    const = lambda shape: pl.BlockSpec(shape, lambda i, dd: (0,) * len(shape))
    return pl.pallas_call(
        functools.partial(_comb_proj1_kernel, lb_rows=lb_rows),
        grid_spec=pltpu.PrefetchScalarGridSpec(
            num_scalar_prefetch=1,
            grid=(t // tm,),
            in_specs=[pl.BlockSpec((tm, d), row), pl.BlockSpec((tm, LANES), row),
                      pl.BlockSpec(memory_space=pl.ANY),
                      const(g.shape), const(w.shape), const(lb_logits.shape)],
            out_specs=[pl.BlockSpec((tm, d), row)] * 5,
            scratch_shapes=_combine_scratch(tm, nsub)),
        out_shape=[jax.ShapeDtypeStruct((t, d), F32), jax.ShapeDtypeStruct((t, d), BF16),
                   jax.ShapeDtypeStruct((t, d), F32), jax.ShapeDtypeStruct((t, d), BF16),
                   jax.ShapeDtypeStruct((t, d), BF16)],
        compiler_params=_params(1),
        name="comb_proj1",
    )(dest, h, meta, ys, g, w, lb_logits)


def _comb_final_kernel(dest_ref, h_ref, meta_ref, ys_ref, g_ref, o_ref, buf_ref, sem_ref, pair_s):
    tm = h_ref.shape[0]
    (y,), issue = _combine(dest_ref, ys_ref, meta_ref, buf_ref, sem_ref, pair_s, tm)
    issue(0, 1)
    o_ref[...] = _rms(h_ref[...] + y, g_ref[...])
    _combine_drain(buf_ref, sem_ref, ys_ref, tm)


def _comb_final(dest, h, meta, ys, g):
    t, d = h.shape
    tm = min(TM_COMB, t)
    row = lambda i, dd: (i, 0)
    return pl.pallas_call(
        _comb_final_kernel,
        grid_spec=pltpu.PrefetchScalarGridSpec(
            num_scalar_prefetch=1,
            grid=(t // tm,),
            in_specs=[pl.BlockSpec((tm, d), row), pl.BlockSpec((tm, LANES), row),
                      pl.BlockSpec(memory_space=pl.ANY),
                      pl.BlockSpec(g.shape, lambda i, dd: (0, 0))],
            out_specs=pl.BlockSpec((tm, d), row),
            scratch_shapes=_combine_scratch(tm)),
        out_shape=jax.ShapeDtypeStruct((t, d), F32),
        compiler_params=_params(1),
        name="comb_final",
    )(dest, h, meta, ys, g)


def _hgrn_kernel(qs_ref, lf_ref, iv_ref, gs_ref, ng_ref, o_ref, st_ref, d1_s, t16_s, *, dk):
    th = qs_ref.shape[0]
    nsub = CHUNK // SUB

    @pl.when(pl.program_id(1) == 0)
    def _():
        st_ref[...] = jnp.zeros_like(st_ref)

    rr = lax.broadcasted_iota(jnp.int32, (th, th), 0)
    cc = lax.broadcasted_iota(jnp.int32, (th, th), 1)
    same_sub = (rr // SUB) == (cc // SUB)
    same_chunk = (rr // CHUNK) == (cc // CHUNK)
    m_diag = same_sub & (cc <= rr)
    dsub = rr // SUB - cc // SUB
    m_off = [same_chunk & (dsub == dd) for dd in range(1, nsub)]

    hi, lo = _split_bf16(lf_ref[...])
    tri = jnp.where(m_diag, 1.0, 0.0).astype(BF16)
    d1 = _dot(tri, hi) + _dot(tri, lo)
    d1_s[...] = d1
    sub3 = (th // SUB, SUB, d1.shape[1])
    t16_s[...] = jnp.broadcast_to(d1.reshape(sub3)[:, SUB - 1:SUB, :], sub3).reshape(th, d1.shape[1])
    rrow = lax.broadcasted_iota(jnp.int32, (th, dk), 0)
    rsub = (rrow // SUB) % nsub
    rchunk = rrow // CHUNK
    nch = th // CHUNK

    def back_rows(dd):
        return [slice(CHUNK * c + SUB * dd, CHUNK * (c + 1)) for c in range(nch)]

    def prepare(hd):
        ls = pl.ds(hd * dk, dk)
        d1 = d1_s[:, ls]
        t16 = t16_s[:, ls]
        q = qs_ref[:, ls].astype(F32)
        kt = 1.0 - jnp.exp2(lf_ref[:, ls])
        suf = t16 - d1
        kx = (kt * jnp.exp2(suf)).astype(BF16)
        qv = [q * jnp.exp2(-suf)]
        acc = d1
        tail = suf
        for dd in range(1, nsub):
            qv.append(jnp.concatenate([q[sl] * jnp.exp2(acc[sl]) for sl in back_rows(dd)], axis=0))
            acc = acc + jnp.where(rsub >= dd, pltpu.roll(t16, SUB * dd, 0), 0.0)
            tail = tail + jnp.where(rsub < nsub - dd, pltpu.roll(t16, th - SUB * dd, 0), 0.0)
        b = acc
        qb = q * jnp.exp2(b)
        kend = kt * jnp.exp2(tail)
        zero = jnp.zeros_like(q)
        kend_x = jnp.concatenate([jnp.where(rchunk == c, kend, zero) for c in range(nch)], axis=1).astype(BF16)
        qb_x = jnp.concatenate([jnp.where(rchunk == c, qb, zero) for c in range(nch)], axis=1).astype(BF16)
        decs = [jnp.exp2(b[CHUNK * (c + 1) - 1:CHUNK * (c + 1), :]) for c in range(nch)]
        return ls, jnp.concatenate(qv, axis=0).astype(BF16), kx, kend_x, qb_x, decs

    def group(gi, carry):
        heads = [gi * HGRN_GROUP + u for u in range(HGRN_GROUP)]
        prep = [prepare(hd) for hd in heads]
        a4s = [_dot_nt(p[1], p[2]) for p in prep]
        incs = [_dot_tn(iv_ref[:, p[0]], p[3]) for p in prep]
        o_intras = []
        for p, a4 in zip(prep, a4s):
            att = jnp.where(m_diag, a4[:th], 0.0)
            at = th
            for dd in range(1, nsub):
                pieces = []
                for sl in back_rows(dd):
                    n = sl.stop - sl.start
                    pieces += [jnp.zeros((SUB * dd, th), F32), a4[at:at + n]]
                    at += n
                att = jnp.where(m_off[dd - 1], jnp.concatenate(pieces, axis=0), att)
            o_intras.append(_dot(att.astype(BF16), iv_ref[:, p[0]]))
        for hd, p, inc, o_intra in zip(heads, prep, incs, o_intras):
            ls, decs = p[0], p[5]
            st = st_ref[hd]
            starts = []
            for c in range(nch):
                starts.append(st)
                st = st * decs[c] + inc[:, dk * c:dk * (c + 1)]
            st_ref[hd] = st
            oh = o_intra + _dot_nt(p[4], jnp.concatenate(starts, axis=1).astype(BF16))
            on = oh * lax.rsqrt(jnp.mean(oh * oh, axis=-1, keepdims=True) + EPS)
            o_ref[:, ls] = (on * ng_ref[:, ls] * gs_ref[:, ls].astype(F32)).astype(BF16)
        return carry

    for gi in range(qs_ref.shape[1] // dk // HGRN_GROUP):
        group(gi, 0)


def _hgrn(qs, lf, iv, gs, ng, batch, dk):
    t, d = qs.shape
    s = t // batch
    th = min(TH, s)
    blk = pl.BlockSpec((None, th, d), lambda b, j: (b, j, 0))
    r3 = lambda a: a.reshape(batch, s, d)
    out = pl.pallas_call(
        functools.partial(_hgrn_kernel, dk=dk),
        grid=(batch, s // th),
        in_specs=[blk, blk, blk, blk, _const_spec(ng.shape)],
        out_specs=blk,
        out_shape=jax.ShapeDtypeStruct((batch, s, d), BF16),
        scratch_shapes=[pltpu.VMEM((d // dk, dk, dk), F32), pltpu.VMEM((th, d), F32), pltpu.VMEM((th, d), F32)],
        compiler_params=_params(2),
        name="hgrn2",
    )(r3(qs), r3(lf), r3(iv), r3(gs), ng)
    return out.reshape(t, d)


def _block_diag(w):
    n, c, dd = w.shape
    eye = jnp.eye(n, dtype=w.dtype)
    return (eye[:, None, :, None] * w[:, :, None, :]).reshape(n * c, n * dd)


def _router_weights(wg, we):
    d = wg.shape[0]
    w = jnp.concatenate([wg, we, jnp.zeros((d, LANES - wg.shape[1] - we.shape[1]), F32)], axis=1).T
    hi = w.astype(BF16)
    return jnp.concatenate([hi, (w - hi.astype(F32)).astype(BF16)], axis=0)


def _moe(layer, meta_t, cnt, hn_slab, moe_w_gate, moe_w_up, moe_w_down):
    dest, pad_lo, pad_hi, blk_order, blk_fill, used_seq, nact, nblk = _plan(meta_t, cnt, TB_EXPERT)
    xs = _dispatch(dest, pad_lo, pad_hi, nact, hn_slab, nblk, TB_EXPERT)
    ys = _experts(blk_order, blk_fill, used_seq, nact, xs, moe_w_gate, moe_w_up, moe_w_down, layer, TB_EXPERT)
    return dest, ys


def kernel(x, norm_mix_g, norm_ffn_g, norm_final_g, ab_w_in, ab_conv_w, ab_conv_b, rg_w_a, rg_b_a, rg_w_x, rg_b_x, rg_lambda, attn_rel_bias, ab_w_out, c_w_in, c_lb_logits, c_norm_g, c_w_out, moe_router_group, moe_router_expert, moe_w_gate, moe_w_up, moe_w_down):
    batch, seq, d = x.shape
    t = batch * seq
    xt = x.reshape(t, d)
    row = lambda v: v.reshape(1, -1)

    aw = ab_conv_w.shape[2]
    qkv, ya = _proj0_lru(x, row(norm_mix_g[0]), ab_w_in[0].astype(BF16), ab_conv_w[0], row(ab_conv_b[0]),
                         _block_diag(rg_w_a[0]).astype(BF16), _block_diag(rg_w_x[0]).astype(BF16),
                         row(rg_b_a[0]), row(rg_b_x[0]), row(rg_lambda[0]))
    tk = TQ + LEFT_CHUNKS * CHUNK
    dist = jnp.clip(tk - jnp.arange(TQ + tk), -REL_CLIP, REL_CLIP) + REL_CLIP
    bias = _attn_bias(attn_rel_bias[0][:, None, dist])
    yb = _attention(qkv, bias, batch)
    wo = ab_w_out[0].astype(BF16)
    wt = _router_weights(moe_router_group[0], moe_router_expert[0])
    h1, hn1, meta1, metat1, cnt1 = _proj_route(ya, 0, yb, 0, xt, wo[:aw], wo[aw:], row(norm_ffn_g[0]), wt)
    dest1, ys1 = _moe(0, metat1, cnt1, hn1, moe_w_gate, moe_w_up, moe_w_down)

    dk = c_norm_g.shape[1] // 8
    h2, qs, lf, iv, gs = _comb_proj1(dest1, h1, meta1, ys1, row(norm_mix_g[1]), c_w_in[0].astype(BF16),
                                     c_lb_logits, 1)
    om = _hgrn(qs, lf, iv, gs, row(c_norm_g[0]), batch, dk)
    wo = c_w_out[0].astype(BF16)
    half = wo.shape[0] // 2
    wt = _router_weights(moe_router_group[1], moe_router_expert[1])
    h3, hn3, meta3, metat3, cnt3 = _proj_route(om, 0, om, 1, h2, wo[:half], wo[half:], row(norm_ffn_g[1]), wt)
    dest3, ys3 = _moe(1, metat3, cnt3, hn3, moe_w_gate, moe_w_up, moe_w_down)

    out = _comb_final(dest3, h3, meta3, ys3, row(norm_final_g))
    return out.reshape(batch, seq, d)
```

```python
import functools

import jax
import jax.numpy as jnp
from jax import lax
from jax.experimental import pallas as pl
from jax.experimental.pallas import tpu as pltpu

F32 = jnp.float32
BF16 = jnp.bfloat16

EPS = 1e-6
LOG2E = 1.4426950408889634
RG_C = 8.0
CHUNK = 64
LEFT_CHUNKS = 8
REL_CLIP = 256
N_GROUPS = 4
EXPERTS_PER_GROUP = 8
N_EXPERTS = N_GROUPS * EXPERTS_PER_GROUP
TOP_K = 2
ROUTE_ROWS = 40

LANES = 128
SUBLANES = 8
XROWS = 4
SUB = 16

TS_LRU = 1024
LRU_TILE = 256
TQ = 256
ATTN_GROUP = 4
TM_ROUTE = 1024
TM_ROUTE_SUB = 512
TB_EXPERT = 512
EXPERT_PARTS = 4
X_SLOTS = 3
TM_COMB = 256
TM_COMB_PROJ = 512
TM_COMB_SUB = 256
PLAN_TILE = 2048
TH = 256
HGRN_GROUP = 4
DMA_UNROLL = 8
COMB_SLOTS = 3
VMEM_MB = 48


def _params(n_axes, vmem_mb=VMEM_MB):
    return pltpu.CompilerParams(dimension_semantics=("arbitrary",) * n_axes,
                                vmem_limit_bytes=vmem_mb * 1024 * 1024)


def _const_spec(shape):
    nd = len(shape)
    return pl.BlockSpec(shape, lambda *_: (0,) * nd)


def _rms(x, g):
    return x * lax.rsqrt(jnp.mean(x * x, axis=-1, keepdims=True) + EPS) * g


def _dot(a, b):
    return jnp.dot(a, b, preferred_element_type=F32)


def _dot_nt(a, b):
    return lax.dot_general(a, b, (((1,), (1,)), ((), ())), preferred_element_type=F32)


def _dot_tn(a, b):
    return lax.dot_general(a, b, (((0,), (0,)), ((), ())), preferred_element_type=F32)


def _split_bf16(x):
    hi = x.astype(BF16)
    lo = (x - hi.astype(F32)).astype(BF16)
    return hi, lo


def _pack_rows(ref, val, pair_s):
    m = val.shape[0]
    for j in range(XROWS):
        pair_s[j, pl.ds(0, m, stride=2), :] = val[:, LANES * j:LANES * (j + 1)]
        pair_s[j, pl.ds(1, m, stride=2), :] = val[:, LANES * (j + XROWS):LANES * (j + XROWS + 1)]
        ref[pl.ds(j, m, stride=XROWS), :] = pltpu.bitcast(pair_s[j, pl.ds(0, 2 * m), :].astype(BF16), jnp.uint32)


def _packed_zeros(m):
    return pltpu.bitcast(jnp.zeros((2 * m * XROWS, LANES), BF16), jnp.uint32)


def _unpack_rows(ref, m, pair_s):
    lo, hi = [], []
    for j in range(XROWS):
        pair_s[j, pl.ds(0, 2 * m), :] = pltpu.bitcast(ref[pl.ds(j, m, stride=XROWS), :], BF16).astype(F32)
        lo.append(pair_s[j, pl.ds(0, m, stride=2), :])
        hi.append(pair_s[j, pl.ds(1, m, stride=2), :])
    return jnp.concatenate(lo + hi, axis=1)


def _proj0_lru_kernel(x_ref, g_ref, w_ref, cw_ref, cb_ref, wa_ref, wx_ref, ba_ref, bx_ref, lam_ref,
                      qkv_ref, ya_ref, xbuf, h_ref, a_s, u_s):
    ts = x_ref.shape[0]
    w = ya_ref.shape[1]
    bw = qkv_ref.shape[1] // 3

    @pl.when(pl.program_id(1) == 0)
    def _():
        xbuf[pl.ds(0, SUBLANES), :] = jnp.zeros((SUBLANES, w), F32)
        h_ref[...] = jnp.zeros_like(h_ref)

    hn = _rms(x_ref[...], g_ref[...]).astype(BF16)

    def qkv_cols(c, scale=None):
        z = _dot(hn, w_ref[:, 2 * w + c * bw:2 * w + (c + 1) * bw])
        qkv_ref[:, c * bw:(c + 1) * bw] = (z if scale is None else z * scale).astype(BF16)

    cwid = min(LRU_TILE, w)
    nk = cw_ref.shape[0]
    xg = _dot(hn, w_ref[:, :2 * w])

    def channel_tile(j):
        cs = slice(j * cwid, (j + 1) * cwid)
        xbuf[pl.ds(SUBLANES, ts), cs] = xg[:, cs]
        y = cb_ref[:, cs]
        for k in range(nk):
            y = y + cw_ref[nk - 1 - k:nk - k, cs] * xbuf[pl.ds(SUBLANES - k, ts), cs]
        xbuf[pl.ds(0, SUBLANES), cs] = xbuf[pl.ds(ts, SUBLANES), cs]

        yb = y.astype(BF16)
        rg = jax.nn.sigmoid(_dot(yb, wa_ref[cs, cs]) + ba_ref[:, cs])
        ig = jax.nn.sigmoid(_dot(yb, wx_ref[cs, cs]) + bx_ref[:, cs])
        lam = lam_ref[:, cs]
        log_sig = jnp.minimum(lam, 0.0) - jnp.log1p(jnp.exp(-jnp.abs(lam)))
        log_a = RG_C * rg * log_sig
        a = jnp.exp(log_a)
        m = 1.0 - a * a
        u = jnp.where(m > 0.0, m * lax.rsqrt(m), 0.0) * (ig * y)

        grp = (ts // SUBLANES, SUBLANES, cwid)
        a = a.reshape(grp)
        u = u.reshape(grp)
        rowm = lax.broadcasted_iota(jnp.int32, grp, 1)
        for s in (1, 2, 4):
            keep = rowm >= s
            a_sh = jnp.where(keep, pltpu.roll(a, s, 1), 1.0)
            u_sh = jnp.where(keep, pltpu.roll(u, s, 1), 0.0)
            u = a * u_sh + u
            a = a * a_sh
        a_s[:, cs] = a.reshape(ts, cwid)
        u_s[:, cs] = u.reshape(ts, cwid)

    for j in range(max(w // cwid, 3)):
        if j < w // cwid:
            channel_tile(j)
        if j < 3:
            qkv_cols(j, CHUNK ** -0.5 * LOG2E if j == 0 else None)

    def group(gi, h):
        off = pl.multiple_of(gi * SUBLANES, SUBLANES)
        hg = a_s[pl.ds(off, SUBLANES), :] * h + u_s[pl.ds(off, SUBLANES), :]
        u_s[pl.ds(off, SUBLANES), :] = hg
        return jnp.broadcast_to(hg[SUBLANES - 1:SUBLANES, :], hg.shape)

    h_ref[...] = lax.fori_loop(0, ts // SUBLANES, group, h_ref[...])

    ga = xg[:, w:]
    gelu =0.5 * ga * (1.0 + jnp.tanh(0.7978845608028654 * (ga + 0.044715 * (ga * ga * ga))))
    ya_ref[...] = (u_s[...] * gelu).astype(BF16)


def _proj0_lru(x, g, w_in, cw, cb, wa, wx, ba, bx, lam):
    batch, s, d = x.shape
    w = cw.shape[1]
    nqkv = w_in.shape[1] - 2 * w
    ts = min(TS_LRU, s)
    small = [g, w_in, cw, cb, wa, wx, ba, bx, lam]
    blk = lambda width: pl.BlockSpec((None, ts, width), lambda b, j: (b, j, 0))
    qkv, ya = pl.pallas_call(
        _proj0_lru_kernel,
        grid=(batch, s // ts),
        in_specs=[blk(d)] + [_const_spec(a.shape) for a in small],
        out_specs=[blk(nqkv), blk(w)],
        out_shape=[jax.ShapeDtypeStruct((batch, s, nqkv), BF16), jax.ShapeDtypeStruct((batch, s, w), BF16)],
        scratch_shapes=[pltpu.VMEM((ts + SUBLANES, w), F32), pltpu.VMEM((SUBLANES, w), F32),
                        pltpu.VMEM((ts, w), F32), pltpu.VMEM((ts, w), F32)],
        compiler_params=_params(2),
        name="proj0_lru",
    )(x, *small)
    return qkv.reshape(batch * s, nqkv), ya.reshape(batch * s, w)


def _attn_bias_kernel(base_ref, o_ref):
    tq, tk = o_ref.shape
    nkb = tk // tq
    row = base_ref[...]
    full = pltpu.roll(jnp.broadcast_to(row, (tq, tq + tk)), 0, 1, stride=1, stride_axis=0)
    qc = lax.broadcasted_iota(jnp.int32, (tq, tk), 0) // CHUNK
    col = lax.broadcasted_iota(jnp.int32, (tq, tk), 1)
    kc = col // CHUNK
    first_ok = (nkb - 1 - pl.program_id(0)) * tq
    valid = (kc >= qc) & (kc <= qc + LEFT_CHUNKS) & (col >= first_ok)
    o_ref[...] = jnp.where(valid, full[:, tq:] * LOG2E, -1e30)


def _attn_bias(base):
    h = base.shape[0]
    tk = TQ + LEFT_CHUNKS * CHUNK
    return pl.pallas_call(
        _attn_bias_kernel,
        grid=(tk // TQ, h),
        in_specs=[pl.BlockSpec((None, 1, TQ + tk), lambda v, i: (i, 0, 0))],
        out_specs=pl.BlockSpec((None, None, TQ, tk), lambda v, i: (v, i, 0, 0)),
        out_shape=jax.ShapeDtypeStruct((tk // TQ, h, TQ, tk), F32),
        compiler_params=_params(2),
        name="attn_bias",
    )(base)


def _attn_kernel(q_ref, k0_ref, k1_ref, k2_ref, v0_ref, v1_ref, v2_ref, bias_ref, o_ref):
    tq = q_ref.shape[0]
    half_w = LANES // 2
    lo = lax.broadcasted_iota(jnp.int32, (tq, LANES), 1) < half_w
    lo_k = lax.broadcasted_iota(jnp.int32, (3 * tq, LANES), 1) < half_w
    k_refs = (k0_ref, k1_ref, k2_ref)
    v_refs = (v0_ref, v1_ref, v2_ref)
    npair = q_ref.shape[1] // LANES
    for j0 in range(0, npair, ATTN_GROUP):
        pairs = range(j0, min(j0 + ATTN_GROUP, npair))
        qk, vh = {}, {}
        for j in pairs:
            sl = slice(LANES * j, LANES * (j + 1))
            q2 = q_ref[:, sl]
            kcat = jnp.concatenate([r[:, sl] for r in k_refs], axis=0)
            vcat = jnp.concatenate([r[:, sl] for r in v_refs], axis=0)
            one = jnp.ones_like(vcat)
            zero = jnp.zeros_like(q2)
            vh[j] = (jnp.where(lo_k, vcat, one), jnp.where(lo_k, one, vcat))
            qk[j] = [_dot_nt(jnp.where(lo if half == 0 else jnp.logical_not(lo), q2, zero), kcat)
                     for half in range(2)]
        ps = {}
        for j in pairs:
            ps[j] = []
            for half in range(2):
                s = qk[j][half] + bias_ref[2 * j + half]
                ps[j].append(jnp.exp2(s - jnp.max(s, axis=-1, keepdims=True)).astype(BF16))
        for j in pairs:
            pv = [_dot(ps[j][half], vh[j][half]) for half in range(2)]
            num = jnp.where(lo, pv[0], pv[1])
            den = pltpu.roll(jnp.where(lo, pv[1], pv[0]), half_w, 1)
            o_ref[:, LANES * j:LANES * (j + 1)] = (num / den).astype(BF16)


def _attention(qkv, bias, batch):
    t, w3 = qkv.shape
    w = w3 // 3
    s = t // batch
    qkv3 = qkv.reshape(batch, s, w3)
    nvar = bias.shape[0]

    def kv_spec(colblk, back):
        return pl.BlockSpec((None, TQ, w), lambda i, b: (b, jnp.maximum(i - back, 0), colblk))

    out = pl.pallas_call(
        _attn_kernel,
        grid=(s // TQ, batch),
        in_specs=[pl.BlockSpec((None, TQ, w), lambda i, b: (b, i, 0)),
                  kv_spec(1, 2), kv_spec(1, 1), kv_spec(1, 0),
                  kv_spec(2, 2), kv_spec(2, 1), kv_spec(2, 0),
                  pl.BlockSpec((None,) + bias.shape[1:], lambda i, b: (jnp.minimum(i, nvar - 1), 0, 0, 0))],
        out_specs=pl.BlockSpec((None, TQ, w), lambda i, b: (b, i, 0)),
        out_shape=jax.ShapeDtypeStruct((batch, s, w), BF16),
        compiler_params=_params(2),
        name="attention",
    )(qkv3, qkv3, qkv3, qkv3, qkv3, qkv3, qkv3, bias)
    return out.reshape(t, w)


def _proj_route_kernel(a_ref, b_ref, r_ref, wa_ref, wb_ref, g_ref, wt_ref,
                       h_ref, hn_ref, meta_ref, metat_ref, cnt_ref, carry_ref, pair_s):
    tm = a_ref.shape[0]
    ts = min(TM_ROUTE_SUB, tm)

    @pl.when(pl.program_id(0) == 0)
    def _():
        carry_ref[...] = jnp.zeros_like(carry_ref)

    wt = wt_ref[...]
    ridx = lax.broadcasted_iota(jnp.int32, (ROUTE_ROWS, ts), 0).astype(F32)
    r8 = lax.broadcasted_iota(jnp.int32, (SUBLANES, ts), 0)
    rr = lax.broadcasted_iota(jnp.int32, (ts, ts), 0)
    cc = lax.broadcasted_iota(jnp.int32, (ts, ts), 1)
    utri = jnp.where(rr < cc, 1.0, 0.0).astype(BF16)
    neg = -jnp.inf
    far = float(LANES)
    carry = carry_ref[...]

    all_logits = []
    for sub in range(tm // ts):
        rows = pl.ds(sub * ts, ts)
        h = r_ref[rows, :] + (_dot(a_ref[rows, :], wa_ref[...]) + _dot(b_ref[rows, :], wb_ref[...]))
        h_ref[rows, :] = h
        hn = _rms(h, g_ref[...])
        _pack_rows(hn_ref.at[pl.ds(sub * ts * XROWS, ts * XROWS), :], hn, pair_s)

        hi, lo = _split_bf16(hn)
        p_hi = _dot_nt(wt, hi)
        p_lo = _dot_nt(wt[:LANES], lo)
        all_logits.append((p_hi[:ROUTE_ROWS] + p_hi[LANES:LANES + ROUTE_ROWS]) + p_lo[:ROUTE_ROWS])

    for sub, logits in enumerate(all_logits):
        rows = pl.ds(sub * ts, ts)

        def first_max(mask, logits=logits):
            v = jnp.max(jnp.where(mask, logits, neg), axis=0, keepdims=True)
            idx = jnp.min(jnp.where(mask & (logits == v), ridx, far), axis=0, keepdims=True)
            return v, idx

        gmask = ridx < N_GROUPS
        gmax, gidx = first_max(gmask)
        g_gate = 1.0 / jnp.sum(jnp.where(gmask, jnp.exp(logits - gmax), 0.0), axis=0, keepdims=True)
        e_lo = N_GROUPS + EXPERTS_PER_GROUP * gidx
        emask = (ridx >= e_lo) & (ridx < e_lo + EXPERTS_PER_GROUP)
        v1, i1 = first_max(emask)
        v2, i2 = first_max(emask & (ridx != i1))
        tt = jnp.exp(v2 - v1)
        w1 = g_gate / (1.0 + tt)
        w2 = g_gate * tt / (1.0 + tt)

        sel1 = ridx == i1
        sel2 = ridx == i2
        onehot = jnp.where(sel1 | sel2, 1.0, 0.0)
        before = _dot(onehot.astype(BF16), utri) + carry[:, 0:1]
        rank1 = jnp.sum(jnp.where(sel1, before, 0.0), axis=0, keepdims=True)
        rank2 = jnp.sum(jnp.where(sel2, before, 0.0), axis=0, keepdims=True)
        carry = carry + jnp.sum(onehot, axis=1, keepdims=True)

        mt = jnp.zeros((SUBLANES, ts), F32)
        for c, val in enumerate((i1 - N_GROUPS, i2 - N_GROUPS, rank1, rank2, w1, w2)):
            mt = jnp.where(r8 == c, val, mt)
        metat_ref[:, rows] = mt
        meta_ref[rows, :] = jnp.concatenate([mt, jnp.zeros((LANES - SUBLANES, ts), F32)], axis=0).T

    carry_ref[...] = carry
    cnt_ref[...] = carry


def _proj_route(a, acol, b, bcol, resid, wa, wb, g, wt):
    t, d = resid.shape
    kw = wa.shape[0]
    tm = min(TM_ROUTE, t)
    return pl.pallas_call(
        _proj_route_kernel,
        grid=(t // tm,),
        in_specs=[pl.BlockSpec((tm, kw), lambda i: (i, acol)), pl.BlockSpec((tm, kw), lambda i: (i, bcol)),
                  pl.BlockSpec((tm, d), lambda i: (i, 0)),
                  _const_spec(wa.shape), _const_spec(wb.shape), _const_spec(g.shape), _const_spec(wt.shape)],
        out_specs=[pl.BlockSpec((tm, d), lambda i: (i, 0)),
                   pl.BlockSpec((tm * XROWS, LANES), lambda i: (i, 0)),
                   pl.BlockSpec((tm, LANES), lambda i: (i, 0)),
                   pl.BlockSpec((SUBLANES, tm), lambda i: (0, i)),
                   pl.BlockSpec((ROUTE_ROWS, LANES), lambda i: (0, 0))],
        out_shape=[jax.ShapeDtypeStruct((t, d), F32),
                   jax.ShapeDtypeStruct((t * XROWS, LANES), jnp.uint32),
                   jax.ShapeDtypeStruct((t, LANES), F32),
                   jax.ShapeDtypeStruct((SUBLANES, t), F32),
                   jax.ShapeDtypeStruct((ROUTE_ROWS, LANES), F32)],
        scratch_shapes=[pltpu.VMEM((ROUTE_ROWS, LANES), F32),
                        pltpu.VMEM((XROWS, 2 * min(TM_ROUTE_SUB, tm), LANES), F32)],
        compiler_params=_params(1),
        name="proj_route",
    )(a, b, resid, wa, wb, g, wt)


def _plan_kernel(cnt_ref, meta_ref, dest_ref, tab_ref, start_s, *, tb):
    r_n = ROUTE_ROWS
    tab_w = tab_ref.shape[1]

    @pl.when(pl.program_id(0) == 0)
    def _():
        row = lax.broadcasted_iota(jnp.int32, (r_n, LANES), 0)
        lane = lax.broadcasted_iota(jnp.int32, (r_n, LANES), 1)

        def prefix(v):
            s = 1
            while s < r_n:
                v = v + jnp.where(row >= s, pltpu.roll(v, s, 0), 0.0)
                s *= 2
            return v

        counts = cnt_ref[...]
        padded = jnp.floor((counts + (tb - 1)) * (1.0 / tb)) * tb
        pad_end = prefix(padded)
        pad_start = pad_end - padded
        pad_lo = pad_start + counts
        start_s[...] = pad_start
        used = padded > 0.0
        order = prefix(jnp.where(used, 1.0, 0.0)) - 1.0
        nused = order[r_n - 1:r_n, :] + 1.0
        nact = pad_end[r_n - 1:r_n, :] * (1.0 / tb)
        expert = (row - N_GROUPS).astype(F32)
        lane_f = lane.astype(F32)
        used_seq = jnp.sum(jnp.where(used & (order == jnp.minimum(lane_f, nused - 1.0)), expert, 0.0),
                           axis=0, keepdims=True)

        erow = lax.broadcasted_iota(jnp.int32, (r_n, tab_w), 0)
        blk_start = lax.broadcasted_iota(jnp.int32, (r_n, tab_w), 1).astype(F32) * tb
        is_expert = (erow >= N_GROUPS) & (erow < N_GROUPS + N_EXPERTS)
        ended = jnp.where(is_expert & (pad_end[:, 0:1] <= blk_start), 1.0, 0.0)
        blk_expert = jnp.minimum(jnp.sum(ended, axis=0, keepdims=True), N_EXPERTS - 1.0)
        mine = (erow - N_GROUPS).astype(F32) == blk_expert
        blk_order = jnp.sum(jnp.where(mine, order[:, 0:1], 0.0), axis=0, keepdims=True)
        blk_lo = jnp.sum(jnp.where(mine, pad_lo[:, 0:1], 0.0), axis=0, keepdims=True)
        blk_fill = jnp.clip(blk_lo - blk_start[0:1, :], 0.0, float(tb))

        cols = jnp.where(lane == 0, pad_lo, jnp.where(lane == 1, pad_end, 0.0))
        rows_t = jnp.concatenate([cols, jnp.zeros((LANES - r_n, LANES), F32)], axis=0).T
        misc = jnp.where(lane[0:1, :] == 0, nact, jnp.where(lane[0:1, :] == 1, nused, 0.0))
        wide = lambda v: v if tab_w == LANES else jnp.concatenate(
            [v, jnp.zeros((v.shape[0], tab_w - LANES), F32)], axis=1)
        tab = jnp.concatenate([wide(rows_t[0:2, :]), wide(used_seq), wide(misc), blk_order, blk_fill,
                               jnp.zeros((2, tab_w), F32)], axis=0)
        tab_ref[...] = tab.astype(jnp.int32)

    for c in range(dest_ref.shape[1] // LANES):
        sl = slice(LANES * c, LANES * (c + 1))
        for k in range(TOP_K):
            expert = meta_ref[k:k + 1, sl]
            start = jnp.zeros((1, LANES), F32)
            for e in range(N_EXPERTS):
                start = jnp.where(expert == float(e), start_s[N_GROUPS + e:N_GROUPS + e + 1, :], start)
            dest_ref[k:k + 1, sl] = (start + meta_ref[TOP_K + k:TOP_K + k + 1, sl]).astype(jnp.int32)


def _plan(meta_t, cnt, tb):
    t = meta_t.shape[1]
    nblk = -(-(t * TOP_K + N_EXPERTS * (tb - 1)) // tb)
    tile = min(PLAN_TILE, t)
    tab_w = -(-nblk // LANES) * LANES
    dest, tab = pl.pallas_call(
        functools.partial(_plan_kernel, tb=tb),
        grid=(t // tile,),
        in_specs=[_const_spec(cnt.shape), pl.BlockSpec((SUBLANES, tile), lambda i: (0, i))],
        out_specs=[pl.BlockSpec((TOP_K, tile), lambda i: (0, i)), pl.BlockSpec((SUBLANES, tab_w), lambda i: (0, 0))],
        out_shape=[jax.ShapeDtypeStruct((TOP_K, t), jnp.int32), jax.ShapeDtypeStruct((SUBLANES, tab_w), jnp.int32)],
        scratch_shapes=[pltpu.VMEM((ROUTE_ROWS, LANES), F32)],
        compiler_params=_params(1),
        name="plan",
    )(cnt, meta_t)
    experts = slice(N_GROUPS, N_GROUPS + N_EXPERTS)
    return (dest.reshape(-1), tab[0, experts], tab[1, experts], tab[4, :nblk], tab[5, :nblk],
            tab[2, :N_EXPERTS], tab[3, :2], nblk)


def _dispatch_kernel(dest_ref, plo_ref, phi_ref, nact_ref, hn_ref, xs_ref, zero_ref, sem, zsem):
    tm = hn_ref.shape[0] // XROWS
    t = dest_ref.shape[0] // TOP_K
    tb = zero_ref.shape[0] // XROWS
    nblk = xs_ref.shape[0] // zero_ref.shape[0]
    i = pl.program_id(0)

    def zero_copy(row, nrows):
        return pltpu.make_async_copy(
            zero_ref.at[pl.ds(0, nrows * XROWS), :],
            xs_ref.at[pl.ds(pl.multiple_of(row * XROWS, XROWS), nrows * XROWS), :], zsem)

    def zero_fill(start):
        def go(cp):
            cp.start() if start else cp.wait()

        def per_expert(e, c):
            off = plo_ref[e]
            n = phi_ref[e] - off
            bit = tb // 2
            while bit:
                pl.when((n & bit) != 0)(functools.partial(lambda o, b: go(zero_copy(o, b)), off, bit))
                off = off + (n & bit)
                bit //= 2
            return c

        def per_block(b, c):
            go(zero_copy(b * tb, tb))
            return c

        lax.fori_loop(0, N_EXPERTS, per_expert, 0)
        lax.fori_loop(nact_ref[0], nblk, per_block, 0)

    @pl.when(i == 0)
    def _():
        zero_ref[...] = jnp.zeros_like(zero_ref)
        zero_fill(True)

    def copy(r, k):
        d = dest_ref[k * t + i * tm + r]
        return pltpu.make_async_copy(
            hn_ref.at[pl.ds(pl.multiple_of(r * XROWS, XROWS), XROWS), :],
            xs_ref.at[pl.ds(pl.multiple_of(d * XROWS, XROWS), XROWS), :], sem)

    def start(r, c):
        for k in range(TOP_K):
            copy(r, k).start(priority=k)
        return c

    lax.fori_loop(0, tm, start, 0, unroll=DMA_UNROLL)
    for k in range(TOP_K):
        pltpu.make_async_copy(hn_ref, xs_ref.at[pl.ds(0, tm * XROWS), :], sem).wait()

    @pl.when(i == 0)
    def _():
        zero_fill(False)


def _dispatch(dest, pad_lo, pad_hi, nact, hn_slab, nblk, tb):
    t = hn_slab.shape[0] // XROWS
    tm = min(TM_COMB, t)
    return pl.pallas_call(
        _dispatch_kernel,
        grid_spec=pltpu.PrefetchScalarGridSpec(
            num_scalar_prefetch=4,
            grid=(t // tm,),
            in_specs=[pl.BlockSpec((tm * XROWS, LANES), lambda i, *_: (i, 0))],
            out_specs=pl.BlockSpec(memory_space=pl.ANY),
            scratch_shapes=[pltpu.VMEM((tb * XROWS, LANES), jnp.uint32),
                            pltpu.SemaphoreType.DMA(()), pltpu.SemaphoreType.DMA(())]),
        out_shape=jax.ShapeDtypeStruct((nblk * tb * XROWS, LANES), jnp.uint32),
        compiler_params=_params(1),
        name="dispatch",
    )(dest, pad_lo, pad_hi, nact, hn_slab)


def _expert_kernel(ord_ref, fill_ref, useq_ref, nact_ref, xs_ref, wg_ref, wu_ref, wd_ref, ys_ref,
                   xbuf, xsem, wg_f, wu_f, wd_f, wsem, wg_s, wu_s, wd_s, pair_s, *, layer):
    i = pl.program_id(0)
    n = pl.num_programs(0)
    tb = ys_ref.shape[0] // XROWS
    rows = tb * XROWS
    active = i < nact_ref[0]
    order = ord_ref[i]
    changed = (i == 0) | (order != ord_ref[jnp.maximum(i - 1, 0)])
    slot = i % X_SLOTS
    ahead = X_SLOTS - 1

    def w_copies(j):
        e = useq_ref[jnp.minimum(j, useq_ref.shape[0] - 1)]
        ws = j % 2
        return [pltpu.make_async_copy(src.at[layer, e], dst.at[ws], wsem.at[ws])
                for src, dst in ((wg_ref, wg_f), (wu_ref, wu_f), (wd_ref, wd_f))]

    @pl.when(i == 0)
    def _():
        for cp in w_copies(0):
            cp.start()

    def x_copy(step, sl):
        blk = jnp.minimum(step, nact_ref[0] - 1)
        return pltpu.make_async_copy(xs_ref.at[pl.ds(pl.multiple_of(blk * rows, rows), rows), :],
                                     xbuf.at[sl], xsem.at[sl])

    @pl.when(i == 0)
    def _():
        for a in range(ahead):
            x_copy(a, a).start()

    x_copy(i, slot).wait()
    x_copy(i + ahead, (i + ahead) % X_SLOTS).start()

    @pl.when(i == n - 1)
    def _():
        for a in range(1, X_SLOTS):
            x_copy(i + a, (i + a) % X_SLOTS).wait()

    @pl.when(active & changed)
    def _():
        for cp in w_copies(order):
            cp.wait()
        for cp in w_copies(order + 1):
            cp.start()
        ws = order % 2
        wg_s[...] = wg_f[ws].astype(BF16)
        wu_s[...] = wu_f[ws].astype(BF16)
        wd_s[...] = wd_f[ws].astype(BF16)

    @pl.when(i == n - 1)
    def _():
        for cp in w_copies(nact_ref[1]):
            cp.wait()

    def mlp(nrows):
        x = _unpack_rows(xbuf.at[slot, pl.ds(0, nrows * XROWS), :], nrows, pair_s).astype(BF16)
        gate = _dot(x, wg_s[...])
        up = _dot(x, wu_s[...])
        hid = (gate * jax.nn.sigmoid(gate) * up).astype(BF16)
        _pack_rows(ys_ref.at[pl.ds(0, nrows * XROWS), :], _dot(hid, wd_s[...]), pair_s)

    part = tb // EXPERT_PARTS
    parts = jnp.clip((fill_ref[i] + part - 1) // part, 1, EXPERT_PARTS)
    for p in range(1, EXPERT_PARTS + 1):
        @pl.when(active & (parts == p))
        def _(p=p):
            mlp(p * part)
            if p < EXPERT_PARTS:
                ys_ref[pl.ds(p * part * XROWS, (tb - p * part) * XROWS), :] = _packed_zeros(tb - p * part)

    @pl.when(jnp.logical_not(active))
    def _():
        ys_ref[...] = _packed_zeros(tb)


def _experts(blk_order, blk_fill, used_seq, nact, xs, wg, wu, wd, layer, tb):
    nblk = blk_order.shape[0]
    _, _, d, de = wg.shape
    anyspec = pl.BlockSpec(memory_space=pl.ANY)
    return pl.pallas_call(
        functools.partial(_expert_kernel, layer=layer),
        grid_spec=pltpu.PrefetchScalarGridSpec(
            num_scalar_prefetch=4,
            grid=(nblk,),
            in_specs=[anyspec, anyspec, anyspec, anyspec],
            out_specs=pl.BlockSpec((tb * XROWS, LANES), lambda i, *_: (i, 0)),
            scratch_shapes=[pltpu.VMEM((X_SLOTS, tb * XROWS, LANES), jnp.uint32),
                            pltpu.SemaphoreType.DMA((X_SLOTS,)),
                            pltpu.VMEM((2, d, de), F32), pltpu.VMEM((2, d, de), F32), pltpu.VMEM((2, de, d), F32),
                            pltpu.SemaphoreType.DMA((2,)),
                            pltpu.VMEM((d, de), BF16), pltpu.VMEM((d, de), BF16), pltpu.VMEM((de, d), BF16),
                            pltpu.VMEM((XROWS, 2 * tb, LANES), F32)]),
        out_shape=jax.ShapeDtypeStruct((nblk * tb * XROWS, LANES), jnp.uint32),
        compiler_params=_params(1),
        name="experts",
    )(blk_order, blk_fill, used_seq, nact, xs, wg, wu, wd)


def _combine(dest_ref, ys_ref, meta_ref, buf_ref, sem_ref, pair_s, tm, nsub=1):
    i = pl.program_id(0)
    n = pl.num_programs(0)
    slot = i % COMB_SLOTS
    ahead = COMB_SLOTS - 1
    t = dest_ref.shape[0] // TOP_K

    def copy(tile, sl, r, k):
        d = dest_ref[k * t + tile * tm + r]
        row = r * XROWS if isinstance(r, int) else pl.multiple_of(r * XROWS, XROWS)
        return pltpu.make_async_copy(
            ys_ref.at[pl.ds(pl.multiple_of(d * XROWS, XROWS), XROWS), :],
            buf_ref.at[sl, k, pl.ds(row, XROWS), :],
            sem_ref.at[sl])

    def start_tile(tile, sl):
        def body(r, c):
            for k in range(TOP_K):
                copy(tile, sl, r, k).start(priority=k)
            return c
        lax.fori_loop(0, tm, body, 0, unroll=DMA_UNROLL)

    def wait_tile(sl):
        for k in range(TOP_K):
            pltpu.make_async_copy(ys_ref.at[pl.ds(0, tm * XROWS), :], buf_ref.at[sl, k], sem_ref.at[sl]).wait()

    @pl.when(i == 0)
    def _():
        for a in range(ahead):
            start_tile(jnp.minimum(a, n - 1), a)

    wait_tile(slot)

    nxt = jnp.minimum(i + ahead, n - 1)
    nslot = (i + ahead) % COMB_SLOTS

    def issue(part, nparts):
        rows = tm // nparts
        for r in range(part * rows, (part + 1) * rows):
            for k in range(TOP_K):
                copy(nxt, nslot, r, k).start(priority=k)

    @pl.when(i == n - 1)
    def _():
        for a in range(1, ahead):
            wait_tile((i + a) % COMB_SLOTS)

    ts = tm // nsub
    ys = []
    for s in range(nsub):
        y = None
        for k in range(TOP_K):
            gate = meta_ref[pl.ds(s * ts, ts), 2 * TOP_K + k:2 * TOP_K + k + 1]
            rows = buf_ref.at[slot, k, pl.ds(s * ts * XROWS, ts * XROWS), :]
            term = _unpack_rows(rows, ts, pair_s.at[k * nsub + s]) * gate
            y = term if y is None else y + term
        ys.append(y)
    return ys, issue


def _combine_drain(buf_ref, sem_ref, ys_ref, tm):
    i = pl.program_id(0)

    @pl.when(i == pl.num_programs(0) - 1)
    def _():
        sl = (i + COMB_SLOTS - 1) % COMB_SLOTS
        for k in range(TOP_K):
            pltpu.make_async_copy(ys_ref.at[pl.ds(0, tm * XROWS), :], buf_ref.at[sl, k], sem_ref.at[sl]).wait()


def _combine_scratch(tm, nsub=1):
    return [pltpu.VMEM((COMB_SLOTS, TOP_K, tm * XROWS, LANES), jnp.uint32), pltpu.SemaphoreType.DMA((COMB_SLOTS,)),
            pltpu.VMEM((TOP_K * nsub, XROWS, 2 * tm // nsub, LANES), F32)]


def _comb_proj1_kernel(dest_ref, h_ref, meta_ref, ys_ref, g_ref, w_ref, lbl_ref,
                       h2_ref, qs_ref, lf_ref, iv_ref, gs_ref, buf_ref, sem_ref, pair_s, *, lb_rows):
    tm, d = h_ref.shape
    nsub = tm // min(TM_COMB_SUB, tm)
    ts = tm // nsub
    ys, issue = _combine(dest_ref, ys_ref, meta_ref, buf_ref, sem_ref, pair_s, tm, nsub)
    lbl = lbl_ref[...]
    ex = jnp.exp(lbl - jnp.max(lbl, axis=0, keepdims=True))
    lb = jnp.sum(ex[:lb_rows], axis=0, keepdims=True) / jnp.sum(ex, axis=0, keepdims=True)

    hns = []
    for s, y in enumerate(ys):
        h2 = h_ref[pl.ds(s * ts, ts), :] + y
        h2_ref[pl.ds(s * ts, ts), :] = h2
        hns.append(_rms(h2, g_ref[...]).astype(BF16))

    ncol = 4
    z = []
    for s, hn in enumerate(hns):
        for c in range(ncol):
            z.append(_dot(hn, w_ref[:, d * c:d * (c + 1)]))
            issue(s * ncol + c, nsub * ncol)
    for s in range(nsub):
        rows = pl.ds(s * ts, ts)
        q, fz, iv, gz = z[s * ncol:(s + 1) * ncol]
        qs_ref[rows, :] = (q * jax.nn.sigmoid(q)).astype(BF16)
        lf_ref[rows, :] = jnp.log(lb + (1.0 - lb) * jax.nn.sigmoid(fz)) * LOG2E
        iv_ref[rows, :] = iv.astype(BF16)
        gs_ref[rows, :] = (gz * jax.nn.sigmoid(gz)).astype(BF16)
    _combine_drain(buf_ref, sem_ref, ys_ref, tm)


def _comb_proj1(dest, h, meta, ys, g, w, lb_logits, lb_rows):
    t, d = h.shape
    tm = min(TM_COMB_PROJ, t)
    nsub = tm // min(TM_COMB_SUB, tm)
    row = lambda i, dd: (i, 0)
    const = lambda shape: pl.BlockSpec(shape, lambda i, dd: (0,) * len(shape))
    return pl.pallas_call(
        functools.partial(_comb_proj1_kernel, lb_rows=lb_rows),
        grid_spec=pltpu.PrefetchScalarGridSpec(
            num_scalar_prefetch=1,
            grid=(t // tm,),
            in_specs=[pl.BlockSpec((tm, d), row), pl.BlockSpec((tm, LANES), row),
                      pl.BlockSpec(memory_space=pl.ANY),
                      const(g.shape), const(w.shape), const(lb_logits.shape)],
            out_specs=[pl.BlockSpec((tm, d), row)] * 5,
            scratch_shapes=_combine_scratch(tm, nsub)),
        out_shape=[jax.ShapeDtypeStruct((t, d), F32), jax.ShapeDtypeStruct((t, d), BF16),
                   jax.ShapeDtypeStruct((t, d), F32), jax.ShapeDtypeStruct((t, d), BF16),
                   jax.ShapeDtypeStruct((t, d), BF16)],
        compiler_params=_params(1),
        name="comb_proj1",
    )(dest, h, meta, ys, g, w, lb_logits)


def _comb_final_kernel(dest_ref, h_ref, meta_ref, ys_ref, g_ref, o_ref, buf_ref, sem_ref, pair_s):
    tm = h_ref.shape[0]
    (y,), issue = _combine(dest_ref, ys_ref, meta_ref, buf_ref, sem_ref, pair_s, tm)
    issue(0, 1)
    o_ref[...] = _rms(h_ref[...] + y, g_ref[...])
    _combine_drain(buf_ref, sem_ref, ys_ref, tm)


def _comb_final(dest, h, meta, ys, g):
    t, d = h.shape
    tm = min(TM_COMB, t)
    row = lambda i, dd: (i, 0)
    return pl.pallas_call(
        _comb_final_kernel,
        grid_spec=pltpu.PrefetchScalarGridSpec(
            num_scalar_prefetch=1,
            grid=(t // tm,),
            in_specs=[pl.BlockSpec((tm, d), row), pl.BlockSpec((tm, LANES), row),
                      pl.BlockSpec(memory_space=pl.ANY),
                      pl.BlockSpec(g.shape, lambda i, dd: (0, 0))],
            out_specs=pl.BlockSpec((tm, d), row),
            scratch_shapes=_combine_scratch(tm)),
        out_shape=jax.ShapeDtypeStruct((t, d), F32),
        compiler_params=_params(1),
        name="comb_final",
    )(dest, h, meta, ys, g)


def _hgrn_kernel(qs_ref, lf_ref, iv_ref, gs_ref, ng_ref, o_ref, st_ref, d1_s, t16_s, *, dk):
    th = qs_ref.shape[0]
    nsub = CHUNK // SUB

    @pl.when(pl.program_id(1) == 0)
    def _():
        st_ref[...] = jnp.zeros_like(st_ref)

    rr = lax.broadcasted_iota(jnp.int32, (th, th), 0)
    cc = lax.broadcasted_iota(jnp.int32, (th, th), 1)
    same_sub = (rr // SUB) == (cc // SUB)
    same_chunk = (rr // CHUNK) == (cc // CHUNK)
    m_diag = same_sub & (cc <= rr)
    dsub = rr // SUB - cc // SUB
    m_off = [same_chunk & (dsub == dd) for dd in range(1, nsub)]

    hi, lo = _split_bf16(lf_ref[...])
    tri = jnp.where(m_diag, 1.0, 0.0).astype(BF16)
    d1 = _dot(tri, hi) + _dot(tri, lo)
    d1_s[...] = d1
    sub3 = (th // SUB, SUB, d1.shape[1])
    t16_s[...] = jnp.broadcast_to(d1.reshape(sub3)[:, SUB - 1:SUB, :], sub3).reshape(th, d1.shape[1])
    rrow = lax.broadcasted_iota(jnp.int32, (th, dk), 0)
    rsub = (rrow // SUB) % nsub
    rchunk = rrow // CHUNK
    nch = th // CHUNK

    def back_rows(dd):
        return [slice(CHUNK * c + SUB * dd, CHUNK * (c + 1)) for c in range(nch)]

    def prepare(hd):
        ls = pl.ds(hd * dk, dk)
        d1 = d1_s[:, ls]
        t16 = t16_s[:, ls]
        q = qs_ref[:, ls].astype(F32)
        kt = 1.0 - jnp.exp2(lf_ref[:, ls])
        suf = t16 - d1
        kx = (kt * jnp.exp2(suf)).astype(BF16)
        qv = [q * jnp.exp2(-suf)]
        acc = d1
        tail = suf
        for dd in range(1, nsub):
            qv.append(jnp.concatenate([q[sl] * jnp.exp2(acc[sl]) for sl in back_rows(dd)], axis=0))
            acc = acc + jnp.where(rsub >= dd, pltpu.roll(t16, SUB * dd, 0), 0.0)
            tail = tail + jnp.where(rsub < nsub - dd, pltpu.roll(t16, th - SUB * dd, 0), 0.0)
        b = acc
        qb = q * jnp.exp2(b)
        kend = kt * jnp.exp2(tail)
        zero = jnp.zeros_like(q)
        kend_x = jnp.concatenate([jnp.where(rchunk == c, kend, zero) for c in range(nch)], axis=1).astype(BF16)
        qb_x = jnp.concatenate([jnp.where(rchunk == c, qb, zero) for c in range(nch)], axis=1).astype(BF16)
        decs = [jnp.exp2(b[CHUNK * (c + 1) - 1:CHUNK * (c + 1), :]) for c in range(nch)]
        return ls, jnp.concatenate(qv, axis=0).astype(BF16), kx, kend_x, qb_x, decs

    def group(gi, carry):
        heads = [gi * HGRN_GROUP + u for u in range(HGRN_GROUP)]
        prep = [prepare(hd) for hd in heads]
        a4s = [_dot_nt(p[1], p[2]) for p in prep]
        incs = [_dot_tn(iv_ref[:, p[0]], p[3]) for p in prep]
        o_intras = []
        for p, a4 in zip(prep, a4s):
            att = jnp.where(m_diag, a4[:th], 0.0)
            at = th
            for dd in range(1, nsub):
                pieces = []
                for sl in back_rows(dd):
                    n = sl.stop - sl.start
                    pieces += [jnp.zeros((SUB * dd, th), F32), a4[at:at + n]]
                    at += n
                att = jnp.where(m_off[dd - 1], jnp.concatenate(pieces, axis=0), att)
            o_intras.append(_dot(att.astype(BF16), iv_ref[:, p[0]]))
        for hd, p, inc, o_intra in zip(heads, prep, incs, o_intras):
            ls, decs = p[0], p[5]
            st = st_ref[hd]
            starts = []
            for c in range(nch):
                starts.append(st)
                st = st * decs[c] + inc[:, dk * c:dk * (c + 1)]
            st_ref[hd] = st
            oh = o_intra + _dot_nt(p[4], jnp.concatenate(starts, axis=1).astype(BF16))
            on = oh * lax.rsqrt(jnp.mean(oh * oh, axis=-1, keepdims=True) + EPS)
            o_ref[:, ls] = (on * ng_ref[:, ls] * gs_ref[:, ls].astype(F32)).astype(BF16)
        return carry

    for gi in range(qs_ref.shape[1] // dk // HGRN_GROUP):
        group(gi, 0)


def _hgrn(qs, lf, iv, gs, ng, batch, dk):
    t, d = qs.shape
    s = t // batch
    th = min(TH, s)
    blk = pl.BlockSpec((None, th, d), lambda b, j: (b, j, 0))
    r3 = lambda a: a.reshape(batch, s, d)
    out = pl.pallas_call(
        functools.partial(_hgrn_kernel, dk=dk),
        grid=(batch, s // th),
        in_specs=[blk, blk, blk, blk, _const_spec(ng.shape)],
        out_specs=blk,
        out_shape=jax.ShapeDtypeStruct((batch, s, d), BF16),
        scratch_shapes=[pltpu.VMEM((d // dk, dk, dk), F32), pltpu.VMEM((th, d), F32), pltpu.VMEM((th, d), F32)],
        compiler_params=_params(2),
        name="hgrn2",
    )(r3(qs), r3(lf), r3(iv), r3(gs), ng)
    return out.reshape(t, d)


def _block_diag(w):
    n, c, dd = w.shape
    eye = jnp.eye(n, dtype=w.dtype)
    return (eye[:, None, :, None] * w[:, :, None, :]).reshape(n * c, n * dd)


def _router_weights(wg, we):
    d = wg.shape[0]
    w = jnp.concatenate([wg, we, jnp.zeros((d, LANES - wg.shape[1] - we.shape[1]), F32)], axis=1).T
    hi = w.astype(BF16)
    return jnp.concatenate([hi, (w - hi.astype(F32)).astype(BF16)], axis=0)


def _moe(layer, meta_t, cnt, hn_slab, moe_w_gate, moe_w_up, moe_w_down):
    dest, pad_lo, pad_hi, blk_order, blk_fill, used_seq, nact, nblk = _plan(meta_t, cnt, TB_EXPERT)
    xs = _dispatch(dest, pad_lo, pad_hi, nact, hn_slab, nblk, TB_EXPERT)
    ys = _experts(blk_order, blk_fill, used_seq, nact, xs, moe_w_gate, moe_w_up, moe_w_down, layer, TB_EXPERT)
    return dest, ys


def kernel(x, norm_mix_g, norm_ffn_g, norm_final_g, ab_w_in, ab_conv_w, ab_conv_b, rg_w_a, rg_b_a, rg_w_x, rg_b_x, rg_lambda, attn_rel_bias, ab_w_out, c_w_in, c_lb_logits, c_norm_g, c_w_out, moe_router_group, moe_router_expert, moe_w_gate, moe_w_up, moe_w_down):
    batch, seq, d = x.shape
    t = batch * seq
    xt = x.reshape(t, d)
    row = lambda v: v.reshape(1, -1)

    aw = ab_conv_w.shape[2]
    assert LRU_TILE % rg_w_a.shape[2] == 0, "gate blocks must not straddle RG-LRU channel tiles"
    qkv, ya = _proj0_lru(x, row(norm_mix_g[0]), ab_w_in[0].astype(BF16), ab_conv_w[0], row(ab_conv_b[0]),
                         _block_diag(rg_w_a[0]).astype(BF16), _block_diag(rg_w_x[0]).astype(BF16),
                         row(rg_b_a[0]), row(rg_b_x[0]), row(rg_lambda[0]))
    tk = TQ + LEFT_CHUNKS * CHUNK
    dist = jnp.clip(tk - jnp.arange(TQ + tk), -REL_CLIP, REL_CLIP) + REL_CLIP
    bias = _attn_bias(attn_rel_bias[0][:, None, dist])
    yb = _attention(qkv, bias, batch)
    wo = ab_w_out[0].astype(BF16)
    wt = _router_weights(moe_router_group[0], moe_router_expert[0])
    h1, hn1, meta1, metat1, cnt1 = _proj_route(ya, 0, yb, 0, xt, wo[:aw], wo[aw:], row(norm_ffn_g[0]), wt)
    dest1, ys1 = _moe(0, metat1, cnt1, hn1, moe_w_gate, moe_w_up, moe_w_down)

    dk = c_norm_g.shape[1] // 8
    h2, qs, lf, iv, gs = _comb_proj1(dest1, h1, meta1, ys1, row(norm_mix_g[1]), c_w_in[0].astype(BF16),
                                     c_lb_logits, 1)
    om = _hgrn(qs, lf, iv, gs, row(c_norm_g[0]), batch, dk)
    wo = c_w_out[0].astype(BF16)
    half = wo.shape[0] // 2
    wt = _router_weights(moe_router_group[1], moe_router_expert[1])
    h3, hn3, meta3, metat3, cnt3 = _proj_route(om, 0, om, 1, h2, wo[:half], wo[half:], row(norm_ffn_g[1]), wt)
    dest3, ys3 = _moe(1, metat3, cnt3, hn3, moe_w_gate, moe_w_up, moe_w_down)

    out = _comb_final(dest3, h3, meta3, ys3, row(norm_final_g))
    return out.reshape(batch, seq, d)
```
